```python
import math
import jax
import jax.numpy as jnp
from jax import lax
import numpy as np

D_MODEL = 1024
BATCH = 8
SEQ = 2048
DEPTH = 2
DEC_BATCH = 128
DEC_SEQ = 4
PAST_LEN = 2048
PAGE_SIZE = 128

NSA_HEADS = 8
NSA_KV_HEADS = 2
NSA_GROUP = NSA_HEADS // NSA_KV_HEADS
HEAD_DIM = 64
NSA_WIDTH = NSA_HEADS * HEAD_DIM
KV_WIDTH = NSA_KV_HEADS * HEAD_DIM
CMP_BLOCK = 32
CMP_STRIDE = 16
CMP_HIDDEN = 128
SEL_BLOCK = 64
SEL_TOP_N = 16
SEL_FORCE = 1e4
WINDOW = 512
WIN_Q_BLOCK = 128
SEL_Q_BLOCK = 64
REL_BUCKETS = 32
REL_MAX_DIST = 128
M_HEADS = 4
M_HEAD_DIM = 128
M_WIDTH = M_HEADS * M_HEAD_DIM
CONV_W = 4
M_CHUNK = 64
D_FF = 2816
N_EXPERTS = 8
TOP_K = 2
MOE_BLOCK = 128
EPS = 1e-6
IN_WIDTHS = (NSA_WIDTH, KV_WIDTH, KV_WIDTH, KV_WIDTH, KV_WIDTH, KV_WIDTH, KV_WIDTH, 3 * NSA_HEADS, M_WIDTH, M_WIDTH, 2 * M_HEADS, D_MODEL, D_MODEL)
IN_COLS = NSA_WIDTH + 6 * KV_WIDTH + 3 * NSA_HEADS + 2 * M_WIDTH + 2 * M_HEADS + 2 * D_MODEL

kernel_name = 'nsa_mlstm_gated_hybrid_step'


def rmsnorm(x, g):
    xf = x.astype(jnp.float32)
    y = xf * lax.rsqrt(jnp.mean(xf * xf, axis=-1, keepdims=True) + EPS)
    return (y * g.astype(jnp.float32)).astype(x.dtype)


def split_cols(z):
    offs = np.cumsum((0,) + IN_WIDTHS)
    return [z[..., int(offs[i]):int(offs[i + 1])] for i in range(len(IN_WIDTHS))]


def rel_bucket(dist):
    d = jnp.maximum(dist, 0)
    exact = REL_BUCKETS // 2
    log_part = exact + (jnp.log(jnp.maximum(d, 1).astype(jnp.float32) / exact)
                        / math.log(REL_MAX_DIST / exact) * (REL_BUCKETS - exact)).astype(jnp.int32)
    return jnp.where(d < exact, d, jnp.minimum(log_part, REL_BUCKETS - 1))


def rel_bias(table, dist):
    return table[rel_bucket(dist)].astype(jnp.float32)


def masked_softmax(s, mask):
    p = jax.nn.softmax(jnp.where(mask, s, -1e30), axis=-1)
    return jnp.where(mask, p, 0.0)


def compress(kv, pe, w1, w2):
    b, t = kv.shape[:2]
    n_chunk = t // CMP_STRIDE
    ch = kv[:, :n_chunk * CMP_STRIDE].reshape(b, n_chunk, CMP_STRIDE, NSA_KV_HEADS, HEAD_DIM)
    blocks = jnp.concatenate([ch[:, :-1], ch[:, 1:]], axis=2) + pe[:, None, :]
    hid = jax.nn.gelu(jnp.einsum('bnlhd,lde->bnhe', blocks, w1))
    return jnp.einsum('bnhe,ed->bnhd', hid, w2)


def cmp_attn(q, kc, vc, q_pos, table):
    nq, nc = q_pos.shape[0], kc.shape[1]
    block_end = jnp.arange(nc) * CMP_STRIDE + (CMP_BLOCK - 1)
    dist = q_pos[:, None] - block_end[None, :]
    bias = jnp.moveaxis(rel_bias(table, dist).reshape(nq, nc, NSA_KV_HEADS, NSA_GROUP), 1, 3)
    s = jnp.einsum('bqhgd,bnhd->bqhgn', q, kc).astype(jnp.float32) + bias
    p = masked_softmax(s, (dist >= 0)[:, None, None, :])
    return jnp.einsum('bqhgn,bnhd->bqhgd', p.astype(vc.dtype), vc), p


def select_blocks(p_cmp, q_pos, n_sel):
    nc = p_cmp.shape[-1]
    c0 = jnp.arange(nc) * CMP_STRIDE
    s0 = jnp.arange(n_sel) * SEL_BLOCK
    ov = jnp.clip(jnp.minimum(c0[:, None] + CMP_BLOCK, s0[None, :] + SEL_BLOCK)
                  - jnp.maximum(c0[:, None], s0[None, :]), 0, CMP_BLOCK).astype(jnp.float32) / CMP_BLOCK
    imp = jnp.einsum('bqhgn,ns->bqhs', p_cmp, ov)
    cur = q_pos // SEL_BLOCK
    j = jnp.arange(n_sel)
    valid = (j[None, :] <= cur[:, None])[:, None, :]
    forced = ((j[None, :] == 0) | (j[None, :] == cur[:, None]) | (j[None, :] == cur[:, None] - 1))[:, None, :]
    score = jnp.where(valid, imp + jnp.where(forced, SEL_FORCE, 0.0), -1.0)
    top_s, idx = lax.top_k(score, min(SEL_TOP_N, n_sel))
    return idx, top_s >= 0.0


def to_blocks(kv):
    b, t = kv.shape[:2]
    n_sel = -(-t // SEL_BLOCK)
    kv = jnp.pad(kv, ((0, 0), (0, n_sel * SEL_BLOCK - t), (0, 0), (0, 0)))
    return kv.reshape(b, n_sel, SEL_BLOCK, NSA_KV_HEADS, HEAD_DIM).transpose(0, 3, 1, 2, 4)


def sel_attn(q, ks_b, vs_b, idx, ok, q_pos, table):
    b = q.shape[0]
    bi = jnp.arange(b)[:, None, None, None]
    hi = jnp.arange(NSA_KV_HEADS)[None, None, :, None]
    kg = ks_b[bi, hi, idx]
    vg = vs_b[bi, hi, idx]
    k_pos = idx[..., None] * SEL_BLOCK + jnp.arange(SEL_BLOCK)
    dist = q_pos[None, :, None, None, None] - k_pos
    mask = (dist >= 0) & ok[..., None]
    table_g = table.reshape(REL_BUCKETS, NSA_KV_HEADS, NSA_GROUP)
    bias = jnp.moveaxis(table_g[rel_bucket(dist), hi[..., None]].astype(jnp.float32), -1, 3)
    s = jnp.einsum('bqhgd,bqhnsd->bqhgns', q, kg).astype(jnp.float32) + bias
    sh = s.shape
    p = masked_softmax(s.reshape(sh[:4] + (-1,)), mask.reshape(mask.shape[:3] + (1, -1)))
    return jnp.einsum('bqhgns,bqhnsd->bqhgd', p.reshape(sh).astype(vg.dtype), vg)


def window_attn(q, k, v, q_pos, k_pos, table):
    nq, nk = q_pos.shape[0], k_pos.shape[0]
    dist = q_pos[:, None] - k_pos[None, :]
    mask = (dist >= 0) & (dist < WINDOW) & (k_pos[None, :] >= 0)
    bias = jnp.moveaxis(rel_bias(table, dist).reshape(nq, nk, NSA_KV_HEADS, NSA_GROUP), 1, 3)
    s = jnp.einsum('bqhgd,bkhd->bqhgk', q, k).astype(jnp.float32) + bias
    p = masked_softmax(s, mask[:, None, None, :])
    return jnp.einsum('bqhgk,bkhd->bqhgd', p.astype(v.dtype), v)


def window_prompt(q, k, v, table):
    b, t = q.shape[:2]
    pad = ((0, 0), (WINDOW, 0), (0, 0), (0, 0))
    kp, vp = jnp.pad(k, pad), jnp.pad(v, pad)
    n_qb = t // WIN_Q_BLOCK
    qb = q.reshape((b, n_qb, WIN_Q_BLOCK) + q.shape[2:])

    def one_block(i):
        start = i * WIN_Q_BLOCK
        qi = lax.dynamic_index_in_dim(qb, i, axis=1, keepdims=False)
        ki = lax.dynamic_slice_in_dim(kp, start, WINDOW + WIN_Q_BLOCK, axis=1)
        vi = lax.dynamic_slice_in_dim(vp, start, WINDOW + WIN_Q_BLOCK, axis=1)
        q_pos = start + jnp.arange(WIN_Q_BLOCK)
        k_pos = start - WINDOW + jnp.arange(WINDOW + WIN_Q_BLOCK)
        return window_attn(qi, ki, vi, q_pos, k_pos, table)

    out = lax.map(one_block, jnp.arange(n_qb))
    return jnp.moveaxis(out, 0, 1).reshape(q.shape)


def nsa_core(q, kc_all, vc_all, ks_all, vs_all, q_pos, table, pe, w1, w2):
    kcc = compress(kc_all, pe[0], w1[0], w2[0])
    vcc = compress(vc_all, pe[1], w1[1], w2[1])
    o_cmp, p_cmp = cmp_attn(q, kcc, vcc, q_pos, table)
    ks_b, vs_b = to_blocks(ks_all), to_blocks(vs_all)
    idx, ok = select_blocks(p_cmp, q_pos, ks_b.shape[2])
    nq = q.shape[1]
    if nq > SEL_Q_BLOCK and nq % SEL_Q_BLOCK == 0:
        n_qb = nq // SEL_Q_BLOCK

        def blk(a):
            return jnp.moveaxis(a.reshape((a.shape[0], n_qb, SEL_Q_BLOCK) + a.shape[2:]), 1, 0)

        o = lax.map(lambda a: sel_attn(a[0], ks_b, vs_b, a[1], a[2], a[3], table),
                    (blk(q), blk(idx), blk(ok), q_pos.reshape(n_qb, SEL_Q_BLOCK)))
        o_sel = jnp.moveaxis(o, 0, 1).reshape(q.shape)
    else:
        o_sel = sel_attn(q, ks_b, vs_b, idx, ok, q_pos, table)
    return o_cmp, o_sel


def mlstm_chunk(state, inp):
    c0, n0, m0 = state
    q, k, v, li, lf = inp
    L = q.shape[1]
    bcum = jnp.cumsum(lf, axis=1)
    log_d = bcum[:, :, None, :] - bcum[:, None, :, :] + li[:, None, :, :]
    causal = jnp.tril(jnp.ones((L, L), dtype=bool))
    log_d = jnp.where(causal[None, :, :, None], log_d, -jnp.inf)
    log_inter = bcum + m0[:, None, :]
    m = jnp.maximum(log_inter, jnp.max(log_d, axis=2))
    dw = jnp.exp(log_d - m[:, :, None, :])
    w_inter = jnp.exp(log_inter - m)
    s = jnp.einsum('bthd,bshd->btsh', q, k) * dw
    num = jnp.einsum('btsh,bshe->bthe', s, v) + w_inter[..., None] * jnp.einsum('bthd,bhde->bthe', q, c0)
    den = jnp.sum(s, axis=2) + w_inter * jnp.einsum('bthd,bhd->bth', q, n0)
    h = num / jnp.maximum(jnp.abs(den), jnp.exp(-m))[..., None]
    b_last = bcum[:, -1]
    log_s = b_last[:, None, :] - bcum + li
    m_new = jnp.maximum(b_last + m0, jnp.max(log_s, axis=1))
    ws = jnp.exp(log_s - m_new[:, None, :])
    wc = jnp.exp(b_last + m0 - m_new)
    c_new = wc[..., None, None] * c0 + jnp.einsum('bsh,bshd,bshe->bhde', ws, k, v)
    n_new = wc[..., None] * n0 + jnp.einsum('bsh,bshd->bhd', ws, k)
    return (c_new, n_new, m_new), h


def mlstm_scan(q, k, v, li, lf, state):
    b, t = q.shape[:2]
    if t % M_CHUNK == 0:
        n_ch = t // M_CHUNK

        def chunks(a):
            return jnp.moveaxis(a.reshape((b, n_ch, M_CHUNK) + a.shape[2:]), 1, 0)

        new_state, h = lax.scan(mlstm_chunk, state, (chunks(q), chunks(k), chunks(v), chunks(li), chunks(lf)))
        h = jnp.moveaxis(h, 0, 1).reshape(v.shape)
    else:
        new_state, h = mlstm_chunk(state, (q, k, v, li, lf))
    return h, new_state


def mlstm_branch(u, o_pre, if_pre, conv_prev, state, conv_w, conv_b, m_qkv, gate_b, norm_g):
    b, t, _ = u.shape
    ext = jnp.concatenate([conv_prev.astype(u.dtype), u], axis=1)
    conv = conv_b
    for j in range(CONV_W):
        conv = conv + ext[:, j:j + t] * conv_w[j]
    uc = jax.nn.silu(conv).reshape(b, t, M_HEADS, M_HEAD_DIM)
    uh = u.reshape(b, t, M_HEADS, M_HEAD_DIM)
    q = jnp.einsum('bthd,hde->bthe', uc, m_qkv[0]).astype(jnp.float32) * (M_HEAD_DIM ** -0.5)
    k = jnp.einsum('bthd,hde->bthe', uc, m_qkv[1]).astype(jnp.float32)
    v = jnp.einsum('bthd,hde->bthe', uh, m_qkv[2]).astype(jnp.float32)
    li = (if_pre[..., :M_HEADS] + gate_b[0]).astype(jnp.float32)
    lf = jax.nn.log_sigmoid((if_pre[..., M_HEADS:] + gate_b[1]).astype(jnp.float32))
    h, new_state = mlstm_scan(q, k, v, li, lf, state)
    hn = h * lax.rsqrt(jnp.mean(h * h, axis=-1, keepdims=True) + EPS)
    out = jax.nn.sigmoid(o_pre.astype(jnp.float32)) * hn.reshape(b, t, M_WIDTH) * norm_g.astype(jnp.float32)
    return out.astype(u.dtype), new_state, ext[:, t:]


def mixer(h, prm, past):
    b, t, _ = h.shape
    table = prm['table']
    (q, kc, vc, ks, vs, kw, vw, g_nsa, u, o_pre, if_pre, g_a, g_b) = split_cols(h @ prm['w_in'])
    q = q.reshape(b, t, NSA_KV_HEADS, NSA_GROUP, HEAD_DIM) * (HEAD_DIM ** -0.5)
    kc, vc, ks, vs, kw, vw = (a.reshape(b, t, NSA_KV_HEADS, HEAD_DIM) for a in (kc, vc, ks, vs, kw, vw))
    new_cmp = jnp.stack([kc, vc], axis=2)
    new_slc = jnp.stack([ks, vs], axis=2)
    win_rows = jnp.stack([kw, vw], axis=2)
    if past is None:
        q_pos = jnp.arange(t)
        kc_all, vc_all, ks_all, vs_all = kc, vc, ks, vs
        o_win = window_prompt(q, kw, vw, table)
        new_win = win_rows[:, t - min(WINDOW, t):]
        conv_prev = jnp.zeros((b, CONV_W - 1, M_WIDTH), u.dtype)
        m_state = (jnp.zeros((b, M_HEADS, M_HEAD_DIM, M_HEAD_DIM), jnp.float32),
                   jnp.zeros((b, M_HEADS, M_HEAD_DIM), jnp.float32),
                   jnp.zeros((b, M_HEADS), jnp.float32))
    else:
        pt = past['page_table']
        past_len = pt.shape[1] * past['cmp'].shape[1]
        pc = past['cmp'][pt].reshape((b, past_len) + past['cmp'].shape[2:])
        psl = past['slc'][pt].reshape((b, past_len) + past['slc'].shape[2:])
        kc_all = jnp.concatenate([pc[:, :, 0], kc], axis=1)
        vc_all = jnp.concatenate([pc[:, :, 1], vc], axis=1)
        ks_all = jnp.concatenate([psl[:, :, 0], ks], axis=1)
        vs_all = jnp.concatenate([psl[:, :, 1], vs], axis=1)
        q_pos = past_len + jnp.arange(t)
        w_buf = past['win'].shape[1]
        win_all = jnp.concatenate([past['win'].astype(win_rows.dtype), win_rows], axis=1)
        k_pos = past_len - w_buf + jnp.arange(w_buf + t)
        o_win = window_attn(q, win_all[:, :, 0], win_all[:, :, 1], q_pos, k_pos, table)
        new_win = win_all[:, t:]
        conv_prev = past['conv']
        m_state = (past['C'].astype(jnp.float32), past['n'].astype(jnp.float32), past['m'].astype(jnp.float32))
    o_cmp, o_sel = nsa_core(q, kc_all, vc_all, ks_all, vs_all, q_pos, table, prm['cmp_pe'], prm['cmp_w1'], prm['cmp_w2'])
    g = jax.nn.sigmoid(g_nsa.reshape(b, t, NSA_KV_HEADS, NSA_GROUP, 3))
    o_nsa = (g[..., 0:1] * o_cmp + g[..., 1:2] * o_sel + g[..., 2:3] * o_win).reshape(b, t, NSA_WIDTH)
    o_m, (c_new, n_new, m_new), conv_new = mlstm_branch(u, o_pre, if_pre, conv_prev, m_state, prm['conv_w'],
                                                        prm['conv_b'], prm['m_qkv'], prm['gate_b'], prm['m_norm'])
    merged = jax.nn.sigmoid(g_a) * (o_nsa @ prm['w_up_a']) + jax.nn.sigmoid(g_b) * (o_m @ prm['w_up_b'])
    return merged @ prm['w_out'], (new_cmp, new_slc, new_win, c_new, n_new, m_new, conv_new)


def swiglu(h, w1, w3, w2):
    return (jax.nn.silu(h @ w1) * (h @ w3)) @ w2


def moe_swiglu(h, router, w1, w3, w2):
    b, t, d = h.shape
    n = b * t
    x = h.reshape(n, d)
    logits = (x @ router).astype(jnp.float32)
    top_val, top_idx = lax.top_k(logits, TOP_K)
    gate = jax.nn.softmax(top_val, axis=-1).reshape(-1)
    e_flat = top_idx.reshape(-1)
    tok = jnp.arange(n * TOP_K) // TOP_K
    order = jnp.argsort(e_flat)
    e_sorted = e_flat[order]
    counts = jnp.bincount(e_flat, length=N_EXPERTS)
    padded = (counts + MOE_BLOCK - 1) // MOE_BLOCK * MOE_BLOCK
    pad_end = jnp.cumsum(padded)
    pad_start = pad_end - padded
    start = jnp.cumsum(counts) - counts
    dest = pad_start[e_sorted] + jnp.arange(n * TOP_K) - start[e_sorted]
    n_blocks = (n * TOP_K + N_EXPERTS * (MOE_BLOCK - 1) + MOE_BLOCK - 1) // MOE_BLOCK
    xd = jnp.zeros((n_blocks * MOE_BLOCK, d), x.dtype).at[dest].set(x[tok[order]])
    blk_e = jnp.minimum(jnp.searchsorted(pad_end, jnp.arange(n_blocks) * MOE_BLOCK, side='right'), N_EXPERTS - 1)
    yd = lax.map(lambda a: swiglu(a[0], w1[a[1]], w3[a[1]], w2[a[1]]),
                 (xd.reshape(n_blocks, MOE_BLOCK, d), blk_e))
    contrib = yd.reshape(-1, d)[dest] * gate[order][:, None].astype(yd.dtype)
    return jax.ops.segment_sum(contrib, tok[order], num_segments=n).reshape(b, t, d)


def channel_mixer(h, l, ffn_w1, ffn_w3, ffn_w2, moe_router, moe_w1, moe_w3, moe_w2):
    i = l // 2
    if l % 2 == 0:
        return swiglu(h, ffn_w1[i], ffn_w3[i], ffn_w2[i])
    return moe_swiglu(h, moe_router[i], moe_w1[i], moe_w3[i], moe_w2[i])


def setup_inputs(seed: int = 0) -> dict:
    key = jax.random.key(seed)
    ks = jax.random.split(key, 40)

    def nrm(i, shape, scale):
        return jax.random.normal(ks[i], shape, jnp.float32) * scale

    n_pages = PAST_LEN // PAGE_SIZE
    n_used = DEC_BATCH * n_pages
    n_phys = n_used + max(1, n_used // 4)
    w_buf = min(WINDOW, PAST_LEN)
    n_dense = (DEPTH + 1) // 2
    n_moe = DEPTH // 2
    kv_row = (2, NSA_KV_HEADS, HEAD_DIM)
    page_table = jax.random.permutation(ks[0], n_phys)[:n_used].reshape(DEC_BATCH, n_pages).astype(jnp.int32)
    gate_bias = jnp.stack([nrm(21, (DEPTH, M_HEADS), 0.1),
                           jnp.linspace(3.0, 6.0, M_HEADS)[None, :] + nrm(33, (DEPTH, M_HEADS), 0.1)], axis=1)
    return {
        'x_prompt': nrm(1, (BATCH, SEQ, D_MODEL), 1.0),
        'x_sample': nrm(2, (DEC_BATCH, DEC_SEQ, D_MODEL), 1.0),
        'cache_cmp_kv': nrm(3, (DEPTH, n_phys, PAGE_SIZE) + kv_row, 1.0),
        'cache_slc_kv': nrm(4, (DEPTH, n_phys, PAGE_SIZE) + kv_row, 1.0),
        'cache_win_kv': nrm(5, (DEPTH, DEC_BATCH, w_buf) + kv_row, 1.0),
        'state_mlstm_C': nrm(6, (DEPTH, DEC_BATCH, M_HEADS, M_HEAD_DIM, M_HEAD_DIM), 0.5),
        'state_mlstm_n': nrm(7, (DEPTH, DEC_BATCH, M_HEADS, M_HEAD_DIM), 0.5),
        'state_mlstm_m': jax.random.uniform(ks[8], (DEPTH, DEC_BATCH, M_HEADS), jnp.float32, 0.0, 2.0),
        'state_mlstm_conv': nrm(9, (DEPTH, DEC_BATCH, CONV_W - 1, M_WIDTH), 1.0),
        'page_table': page_table,
        'rel_bias_table': nrm(10, (REL_BUCKETS, NSA_HEADS), 0.5),
        'norm_mix': 1.0 + nrm(11, (DEPTH, D_MODEL), 0.05),
        'norm_ffn': 1.0 + nrm(12, (DEPTH, D_MODEL), 0.05),
        'norm_final': 1.0 + nrm(13, (D_MODEL,), 0.05),
        'w_in': nrm(14, (DEPTH, D_MODEL, IN_COLS), D_MODEL ** -0.5),
        'cmp_pe': nrm(15, (DEPTH, 2, CMP_BLOCK, HEAD_DIM), 0.1),
        'cmp_w1': nrm(16, (DEPTH, 2, CMP_BLOCK, HEAD_DIM, CMP_HIDDEN), (CMP_BLOCK * HEAD_DIM) ** -0.5),
        'cmp_w2': nrm(17, (DEPTH, 2, CMP_HIDDEN, HEAD_DIM), CMP_HIDDEN ** -0.5),
        'm_conv_w': nrm(18, (DEPTH, CONV_W, M_WIDTH), CONV_W ** -0.5),
        'm_conv_b': nrm(19, (DEPTH, M_WIDTH), 0.01),
        'm_qkv': nrm(20, (DEPTH, 3, M_HEADS, M_HEAD_DIM, M_HEAD_DIM), M_HEAD_DIM ** -0.5),
        'm_gate_bias': gate_bias,
        'm_norm': 1.0 + nrm(22, (DEPTH, M_WIDTH), 0.05),
        'w_up_a': nrm(23, (DEPTH, NSA_WIDTH, D_MODEL), NSA_WIDTH ** -0.5),
        'w_up_b': nrm(24, (DEPTH, M_WIDTH, D_MODEL), M_WIDTH ** -0.5),
        'w_out': nrm(25, (DEPTH, D_MODEL, D_MODEL), D_MODEL ** -0.5),
        'ffn_w1': nrm(26, (n_dense, D_MODEL, D_FF), D_MODEL ** -0.5),
        'ffn_w3': nrm(27, (n_dense, D_MODEL, D_FF), D_MODEL ** -0.5),
        'ffn_w2': nrm(28, (n_dense, D_FF, D_MODEL), D_FF ** -0.5),
        'moe_router': nrm(29, (n_moe, D_MODEL, N_EXPERTS), D_MODEL ** -0.5),
        'moe_w1': nrm(30, (n_moe, N_EXPERTS, D_MODEL, D_FF), D_MODEL ** -0.5),
        'moe_w3': nrm(31, (n_moe, N_EXPERTS, D_MODEL, D_FF), D_MODEL ** -0.5),
        'moe_w2': nrm(32, (n_moe, N_EXPERTS, D_FF, D_MODEL), D_FF ** -0.5),
    }


def reference(x_prompt, x_sample, cache_cmp_kv, cache_slc_kv, cache_win_kv, state_mlstm_C, state_mlstm_n,
              state_mlstm_m, state_mlstm_conv, page_table, rel_bias_table, norm_mix, norm_ffn, norm_final,
              w_in, cmp_pe, cmp_w1, cmp_w2, m_conv_w, m_conv_b, m_qkv, m_gate_bias, m_norm, w_up_a, w_up_b,
              w_out, ffn_w1, ffn_w3, ffn_w2, moe_router, moe_w1, moe_w3, moe_w2):
    xp, xs = x_prompt, x_sample
    prompt_states, sample_states = [], []
    for l in range(DEPTH):
        prm = {'table': rel_bias_table, 'w_in': w_in[l], 'cmp_pe': cmp_pe[l], 'cmp_w1': cmp_w1[l],
               'cmp_w2': cmp_w2[l], 'conv_w': m_conv_w[l], 'conv_b': m_conv_b[l], 'm_qkv': m_qkv[l],
               'gate_b': m_gate_bias[l], 'm_norm': m_norm[l], 'w_up_a': w_up_a[l], 'w_up_b': w_up_b[l],
               'w_out': w_out[l]}
        past = {'cmp': cache_cmp_kv[l], 'slc': cache_slc_kv[l], 'win': cache_win_kv[l],
                'C': state_mlstm_C[l], 'n': state_mlstm_n[l], 'm': state_mlstm_m[l],
                'conv': state_mlstm_conv[l], 'page_table': page_table}
        dp, st_p = mixer(rmsnorm(xp, norm_mix[l]), prm, None)
        ds, st_s = mixer(rmsnorm(xs, norm_mix[l]), prm, past)
        xp = xp + dp
        xs = xs + ds
        xp = xp + channel_mixer(rmsnorm(xp, norm_ffn[l]), l, ffn_w1, ffn_w3, ffn_w2, moe_router, moe_w1, moe_w3, moe_w2)
        xs = xs + channel_mixer(rmsnorm(xs, norm_ffn[l]), l, ffn_w1, ffn_w3, ffn_w2, moe_router, moe_w1, moe_w3, moe_w2)
        prompt_states.append(st_p)
        sample_states.append(st_s)
    y_prompt = rmsnorm(xp, norm_final)
    y_sample = rmsnorm(xs, norm_final)
    ps = [jnp.stack([s[i] for s in prompt_states]) for i in range(7)]
    ss = [jnp.stack([s[i] for s in sample_states]) for i in range(7)]
    return (y_prompt, y_sample, ps[0], ps[1], ps[2], ps[3], ps[4], ps[5], ps[6],
            ss[0], ss[1], ss[2], ss[3], ss[4], ss[5], ss[6])
```

```python
import functools
import math

import jax
import jax.numpy as jnp
import numpy as np
from jax import lax
from jax.experimental import pallas as pl
from jax.experimental.pallas import tpu as pltpu

D_MODEL = 1024
DEPTH = 2
NSA_HEADS = 8
NSA_KV_HEADS = 2
NSA_GROUP = NSA_HEADS // NSA_KV_HEADS
HEAD_DIM = 64
NSA_WIDTH = NSA_HEADS * HEAD_DIM
KV_WIDTH = NSA_KV_HEADS * HEAD_DIM
CMP_BLOCK = 32
CMP_STRIDE = 16
SEL_BLOCK = 64
SEL_TOP_N = 16
SEL_FORCE = 1e4
WINDOW = 512
WIN_Q_BLOCK = 128
REL_BUCKETS = 32
REL_MAX_DIST = 128
M_HEADS = 4
M_HEAD_DIM = 128
M_WIDTH = M_HEADS * M_HEAD_DIM
CONV_W = 4
M_CHUNK = 64
D_FF = 2816
N_EXPERTS = 8
TOP_K = 2
EPS = 1e-6
IN_WIDTHS = (NSA_WIDTH, KV_WIDTH, KV_WIDTH, KV_WIDTH, KV_WIDTH, KV_WIDTH, KV_WIDTH, 3 * NSA_HEADS, M_WIDTH, M_WIDTH,
             2 * M_HEADS, D_MODEL, D_MODEL)

VMEM_LIMIT_V7X = 52 * 1024 * 1024
LANE = 128

F32 = jnp.float32
BF16 = jnp.bfloat16


def _pick_tile(n, cands):
    for c in cands:
        if n % c == 0:
            return c
    return n


def _mm_kernel(*refs, norm, has_res):
    x_ref, g_ref, w_ref = refs[:3]
    res_ref = refs[3] if has_res else None
    o_ref, xs_ref = refs[-2], refs[-1]

    @pl.when(pl.program_id(1) == 0)
    def _():
        x = x_ref[...]
        if norm:
            x = x * lax.rsqrt(jnp.mean(x * x, axis=-1, keepdims=True) + EPS) * g_ref[...]
        xs_ref[...] = x.astype(BF16)

    acc = jnp.dot(xs_ref[...], w_ref[...].astype(BF16), preferred_element_type=F32)
    if has_res:
        acc = acc + res_ref[...]
    o_ref[...] = acc


def _mm(x, w, g=None, res=None):
    m, k = x.shape
    n = w.shape[1]
    tm = _pick_tile(m, (1024, 512, 256, 128))
    tn = _pick_tile(n, (512, 256, 128))
    norm = g is not None
    gg = (g if norm else jnp.ones((k,), F32)).reshape(1, k)
    in_specs = [pl.BlockSpec((tm, k), lambda i, j: (i, 0)),
                pl.BlockSpec((1, k), lambda i, j: (0, 0)),
                pl.BlockSpec((k, tn), lambda i, j: (0, j))]
    args = [x, gg, w]
    if res is not None:
        in_specs.append(pl.BlockSpec((tm, tn), lambda i, j: (i, j)))
        args.append(res)
    return pl.pallas_call(
        functools.partial(_mm_kernel, norm=norm, has_res=res is not None),
        grid=(m // tm, n // tn),
        in_specs=in_specs,
        out_specs=pl.BlockSpec((tm, tn), lambda i, j: (i, j)),
        out_shape=jax.ShapeDtypeStruct((m, n), F32),
        scratch_shapes=[pltpu.VMEM((tm, k), BF16)],
        compiler_params=pltpu.CompilerParams(dimension_semantics=("arbitrary", "arbitrary"),
                                             vmem_limit_bytes=VMEM_LIMIT_V7X),
        name="mm",
    )(*args)


def _ffn_body(x_ref, g_ref, w1_ref, w3_ref, w2_ref, o_ref, xs_ref, acc_ref, *, residual, grouped):
    j = pl.program_id(1)

    @pl.when(j == 0)
    def _():
        x = x_ref[...]
        xn = x * lax.rsqrt(jnp.mean(x * x, axis=-1, keepdims=True) + EPS) * g_ref[...]
        xs_ref[...] = xn.astype(BF16)
        acc_ref[...] = jnp.zeros_like(acc_ref)

    xs = xs_ref[...]
    w1 = w1_ref[0] if grouped else w1_ref[...]
    w3 = w3_ref[0] if grouped else w3_ref[...]
    w2 = w2_ref[0] if grouped else w2_ref[...]
    a = jnp.dot(xs, w1.astype(BF16), preferred_element_type=F32)
    b = jnp.dot(xs, w3.astype(BF16), preferred_element_type=F32)
    h = (a * jax.nn.sigmoid(a) * b).astype(BF16)
    acc_ref[...] += jnp.dot(h, w2.astype(BF16), preferred_element_type=F32)

    @pl.when(j == pl.num_programs(1) - 1)
    def _():
        if residual:
            o_ref[...] = x_ref[...] + acc_ref[...]
        else:
            o_ref[...] = acc_ref[...]


def _ffn_dense_kernel(x_ref, g_ref, w1_ref, w3_ref, w2_ref, o_ref, xs_ref, acc_ref):
    _ffn_body(x_ref, g_ref, w1_ref, w3_ref, w2_ref, o_ref, xs_ref, acc_ref, residual=True, grouped=False)


def _ffn_grouped_kernel(be_ref, x_ref, g_ref, w1_ref, w3_ref, w2_ref, o_ref, xs_ref, acc_ref):
    del be_ref
    _ffn_body(x_ref, g_ref, w1_ref, w3_ref, w2_ref, o_ref, xs_ref, acc_ref, residual=False, grouped=True)


def _ffn_dense(x, g, w1, w3, w2):
    m, d = x.shape
    f = w1.shape[1]
    tm = _pick_tile(m, (1024, 512, 256, 128))
    tf = _pick_tile(f, (256, 128))
    return pl.pallas_call(
        _ffn_dense_kernel,
        grid=(m // tm, f // tf),
        in_specs=[pl.BlockSpec((tm, d), lambda i, j: (i, 0)),
                  pl.BlockSpec((1, d), lambda i, j: (0, 0)),
                  pl.BlockSpec((d, tf), lambda i, j: (0, j)),
                  pl.BlockSpec((d, tf), lambda i, j: (0, j)),
                  pl.BlockSpec((tf, d), lambda i, j: (j, 0))],
        out_specs=pl.BlockSpec((tm, d), lambda i, j: (i, 0)),
        out_shape=jax.ShapeDtypeStruct((m, d), F32),
        scratch_shapes=[pltpu.VMEM((tm, d), BF16), pltpu.VMEM((tm, d), F32)],
        compiler_params=pltpu.CompilerParams(dimension_semantics=("arbitrary", "arbitrary"),
                                             vmem_limit_bytes=VMEM_LIMIT_V7X),
        name="ffn_dense",
    )(x, g.reshape(1, d), w1, w3, w2)


def _ffn_grouped(xd, blk_e, g, w1, w3, w2, tm):
    rows, d = xd.shape
    f = w1.shape[2]
    tf = _pick_tile(f, (256, 128))
    grid_spec = pltpu.PrefetchScalarGridSpec(
        num_scalar_prefetch=1,
        grid=(rows // tm, f // tf),
        in_specs=[pl.BlockSpec((tm, d), lambda i, j, be: (i, 0)),
                  pl.BlockSpec((1, d), lambda i, j, be: (0, 0)),
                  pl.BlockSpec((1, d, tf), lambda i, j, be: (be[i], 0, j)),
                  pl.BlockSpec((1, d, tf), lambda i, j, be: (be[i], 0, j)),
                  pl.BlockSpec((1, tf, d), lambda i, j, be: (be[i], j, 0))],
        out_specs=pl.BlockSpec((tm, d), lambda i, j, be: (i, 0)),
        scratch_shapes=[pltpu.VMEM((tm, d), BF16), pltpu.VMEM((tm, d), F32)],
    )
    return pl.pallas_call(
        _ffn_grouped_kernel,
        grid_spec=grid_spec,
        out_shape=jax.ShapeDtypeStruct((rows, d), F32),
        compiler_params=pltpu.CompilerParams(dimension_semantics=("arbitrary", "arbitrary"),
                                             vmem_limit_bytes=VMEM_LIMIT_V7X),
        name="ffn_grouped",
    )(blk_e, xd, g.reshape(1, d), w1, w3, w2)


def _moe(x, g, router, w1, w3, w2):
    n, d = x.shape
    tm = 1024 if n >= 8192 else 128
    router_p = jnp.pad(router, ((0, 0), (0, LANE - N_EXPERTS)))
    logits = _mm(x, router_p, g=g)[:, :N_EXPERTS]
    top_val, top_idx = lax.top_k(logits, TOP_K)
    gate = jax.nn.softmax(top_val, axis=-1).reshape(-1)
    e_flat = top_idx.reshape(-1)
    n_asg = n * TOP_K
    order = jnp.argsort(e_flat)
    e_sorted = e_flat[order]
    counts = jnp.bincount(e_flat, length=N_EXPERTS)
    padded = (counts + tm - 1) // tm * tm
    pad_end = jnp.cumsum(padded)
    pad_start = pad_end - padded
    start = jnp.cumsum(counts) - counts
    dest = pad_start[e_sorted] + jnp.arange(n_asg) - start[e_sorted]
    n_blocks = n_asg // tm + N_EXPERTS
    src_tok = jnp.zeros((n_blocks * tm,), jnp.int32).at[dest].set((order // TOP_K).astype(jnp.int32))
    pos = jnp.zeros((n_asg,), jnp.int32).at[order].set(dest.astype(jnp.int32))
    blk_e = jnp.minimum(jnp.searchsorted(pad_end, jnp.arange(n_blocks) * tm, side='right'),
                        N_EXPERTS - 1).astype(jnp.int32)
    yd = _ffn_grouped(x[src_tok], blk_e, g, w1, w3, w2, tm)
    contrib = yd[pos] * gate[:, None]
    return x + contrib.reshape(n, TOP_K, d).sum(axis=1)


def rel_bucket(dist):
    d = jnp.maximum(dist, 0)
    exact = REL_BUCKETS // 2
    log_part = exact + (jnp.log(jnp.maximum(d, 1).astype(F32) / exact)
                        / math.log(REL_MAX_DIST / exact) * (REL_BUCKETS - exact)).astype(jnp.int32)
    return jnp.where(d < exact, d, jnp.minimum(log_part, REL_BUCKETS - 1))


def rel_bias(table, dist):
    return table[rel_bucket(dist)].astype(F32)


def masked_softmax(s, mask):
    p = jax.nn.softmax(jnp.where(mask, s, -1e30), axis=-1)
    return jnp.where(mask, p, 0.0)


def compress(kv, pe, w1, w2):
    b, t = kv.shape[:2]
    n_chunk = t // CMP_STRIDE
    ch = kv[:, :n_chunk * CMP_STRIDE].reshape(b, n_chunk, CMP_STRIDE, NSA_KV_HEADS, HEAD_DIM)
    blocks = jnp.concatenate([ch[:, :-1], ch[:, 1:]], axis=2) + pe[:, None, :]
    hid = jax.nn.gelu(jnp.einsum('bnlhd,lde->bnhe', blocks, w1))
    return jnp.einsum('bnhe,ed->bnhd', hid, w2)


def cmp_attn(q, kc, vc, q_pos, table):
    nq, nc = q_pos.shape[0], kc.shape[1]
    block_end = jnp.arange(nc) * CMP_STRIDE + (CMP_BLOCK - 1)
    dist = q_pos[:, None] - block_end[None, :]
    bias = jnp.moveaxis(rel_bias(table, dist).reshape(nq, nc, NSA_KV_HEADS, NSA_GROUP), 1, 3)
    s = jnp.einsum('bqhgd,bnhd->bqhgn', q, kc).astype(F32) + bias
    p = masked_softmax(s, (dist >= 0)[:, None, None, :])
    return jnp.einsum('bqhgn,bnhd->bqhgd', p.astype(vc.dtype), vc), p


def select_mask(p_cmp, q_pos, n_sel):
    nc = p_cmp.shape[-1]
    c0 = jnp.arange(nc) * CMP_STRIDE
    s0 = jnp.arange(n_sel) * SEL_BLOCK
    ov = jnp.clip(jnp.minimum(c0[:, None] + CMP_BLOCK, s0[None, :] + SEL_BLOCK)
                  - jnp.maximum(c0[:, None], s0[None, :]), 0, CMP_BLOCK).astype(F32) / CMP_BLOCK
    imp = jnp.einsum('bqhgn,ns->bqhs', p_cmp, ov, precision=lax.Precision.HIGHEST)
    cur = q_pos // SEL_BLOCK
    j = jnp.arange(n_sel)
    valid = (j[None, :] <= cur[:, None])[:, None, :]
    forced = ((j[None, :] == 0) | (j[None, :] == cur[:, None]) | (j[None, :] == cur[:, None] - 1))[:, None, :]
    score = jnp.where(valid, imp + jnp.where(forced, SEL_FORCE, 0.0), -1.0)
    top_s, idx = lax.top_k(score, min(SEL_TOP_N, n_sel))
    ok = top_s >= 0.0
    return jnp.any((idx[..., None] == j) & ok[..., None], axis=3)


def sel_attn_dense(q, ks, vs, sel, q_pos, table):
    nq, t = q.shape[0], ks.shape[0]
    k_pos = jnp.arange(t)
    dist = q_pos[:, None] - k_pos[None, :]
    bias = jnp.moveaxis(rel_bias(table, dist).reshape(nq, t, NSA_KV_HEADS, NSA_GROUP), 1, 3)
    keymask = jnp.repeat(sel, SEL_BLOCK, axis=-1)[..., :t] & (dist >= 0)[:, None, :]
    s = jnp.einsum('qhgd,khd->qhgk', q, ks).astype(F32) + bias
    p = masked_softmax(s, keymask[:, :, None, :])
    return jnp.einsum('qhgk,khd->qhgd', p, vs)


def window_attn(q, k, v, q_pos, k_pos, table):
    nq, nk = q_pos.shape[0], k_pos.shape[0]
    dist = q_pos[:, None] - k_pos[None, :]
    mask = (dist >= 0) & (dist < WINDOW) & (k_pos[None, :] >= 0)
    bias = jnp.moveaxis(rel_bias(table, dist).reshape(nq, nk, NSA_KV_HEADS, NSA_GROUP), 1, 3)
    s = jnp.einsum('bqhgd,bkhd->bqhgk', q, k).astype(F32) + bias
    p = masked_softmax(s, mask[:, None, None, :])
    return jnp.einsum('bqhgk,bkhd->bqhgd', p.astype(v.dtype), v)


def window_prompt(q, k, v, table):
    b, t = q.shape[:2]
    pad = ((0, 0), (WINDOW, 0), (0, 0), (0, 0))
    kp, vp = jnp.pad(k, pad), jnp.pad(v, pad)
    n_qb = t // WIN_Q_BLOCK
    qb = q.reshape((b, n_qb, WIN_Q_BLOCK) + q.shape[2:])

    def one_block(i):
        start = i * WIN_Q_BLOCK
        qi = lax.dynamic_index_in_dim(qb, i, axis=1, keepdims=False)
        ki = lax.dynamic_slice_in_dim(kp, start, WINDOW + WIN_Q_BLOCK, axis=1)
        vi = lax.dynamic_slice_in_dim(vp, start, WINDOW + WIN_Q_BLOCK, axis=1)
        q_pos = start + jnp.arange(WIN_Q_BLOCK)
        k_pos = start - WINDOW + jnp.arange(WINDOW + WIN_Q_BLOCK)
        return window_attn(qi, ki, vi, q_pos, k_pos, table)

    out = lax.map(one_block, jnp.arange(n_qb))
    return jnp.moveaxis(out, 0, 1).reshape(q.shape)


def nsa_core(q, kc_all, vc_all, ks_all, vs_all, q_pos, table, pe, w1, w2):
    kcc = compress(kc_all, pe[0], w1[0], w2[0])
    vcc = compress(vc_all, pe[1], w1[1], w2[1])
    o_cmp, p_cmp = cmp_attn(q, kcc, vcc, q_pos, table)
    n_sel = -(-ks_all.shape[1] // SEL_BLOCK)
    sel = select_mask(p_cmp, q_pos, n_sel)
    one = lambda a: sel_attn_dense(a[0], a[1], a[2], a[3], q_pos, table)
    batched = (q, ks_all, vs_all, sel)
    o_sel = lax.map(one, batched) if q.shape[1] > SEL_BLOCK else jax.vmap(one)(batched)
    return o_cmp, o_sel


def mlstm_chunk(state, inp):
    c0, n0, m0 = state
    q, k, v, li, lf = inp
    L = q.shape[1]
    bcum = jnp.cumsum(lf, axis=1)
    log_d = bcum[:, :, None, :] - bcum[:, None, :, :] + li[:, None, :, :]
    causal = jnp.tril(jnp.ones((L, L), dtype=bool))
    log_d = jnp.where(causal[None, :, :, None], log_d, -jnp.inf)
    log_inter = bcum + m0[:, None, :]
    m = jnp.maximum(log_inter, jnp.max(log_d, axis=2))
    dw = jnp.exp(log_d - m[:, :, None, :])
    w_inter = jnp.exp(log_inter - m)
    s = jnp.einsum('bthd,bshd->btsh', q, k) * dw
    num = jnp.einsum('btsh,bshe->bthe', s, v) + w_inter[..., None] * jnp.einsum('bthd,bhde->bthe', q, c0)
    den = jnp.sum(s, axis=2) + w_inter * jnp.einsum('bthd,bhd->bth', q, n0)
    h = num / jnp.maximum(jnp.abs(den), jnp.exp(-m))[..., None]
    b_last = bcum[:, -1]
    log_s = b_last[:, None, :] - bcum + li
    m_new = jnp.maximum(b_last + m0, jnp.max(log_s, axis=1))
    ws = jnp.exp(log_s - m_new[:, None, :])
    wc = jnp.exp(b_last + m0 - m_new)
    c_new = wc[..., None, None] * c0 + jnp.einsum('bsh,bshd,bshe->bhde', ws, k, v)
    n_new = wc[..., None] * n0 + jnp.einsum('bsh,bshd->bhd', ws, k)
    return (c_new, n_new, m_new), h


def mlstm_scan(q, k, v, li, lf, state):
    b, t = q.shape[:2]
    if t % M_CHUNK == 0:
        n_ch = t // M_CHUNK

        def chunks(a):
            return jnp.moveaxis(a.reshape((b, n_ch, M_CHUNK) + a.shape[2:]), 1, 0)

        new_state, h = lax.scan(mlstm_chunk, state, (chunks(q), chunks(k), chunks(v), chunks(li), chunks(lf)))
        h = jnp.moveaxis(h, 0, 1).reshape(v.shape)
    else:
        new_state, h = mlstm_chunk(state, (q, k, v, li, lf))
    return h, new_state


def mlstm_branch(u, o_pre, if_pre, conv_prev, state, conv_w, conv_b, m_qkv, gate_b, norm_g):
    b, t, _ = u.shape
    ext = jnp.concatenate([conv_prev.astype(u.dtype), u], axis=1)
    conv = conv_b
    for j in range(CONV_W):
        conv = conv + ext[:, j:j + t] * conv_w[j]
    uc = jax.nn.silu(conv).reshape(b, t, M_HEADS, M_HEAD_DIM)
    uh = u.reshape(b, t, M_HEADS, M_HEAD_DIM)
    q = jnp.einsum('bthd,hde->bthe', uc, m_qkv[0]).astype(F32) * (M_HEAD_DIM ** -0.5)
    k = jnp.einsum('bthd,hde->bthe', uc, m_qkv[1]).astype(F32)
    v = jnp.einsum('bthd,hde->bthe', uh, m_qkv[2]).astype(F32)
    li = (if_pre[..., :M_HEADS] + gate_b[0]).astype(F32)
    lf = jax.nn.log_sigmoid((if_pre[..., M_HEADS:] + gate_b[1]).astype(F32))
    h, new_state = mlstm_scan(q, k, v, li, lf, state)
    hn = h * lax.rsqrt(jnp.mean(h * h, axis=-1, keepdims=True) + EPS)
    out = jax.nn.sigmoid(o_pre.astype(F32)) * hn.reshape(b, t, M_WIDTH) * norm_g.astype(F32)
    return out.astype(u.dtype), new_state, ext[:, t:]


def _split_w_in(w_in):
    offs = np.cumsum((0,) + IN_WIDTHS)
    w_a = w_in[:, :offs[7]]
    w_small = jnp.concatenate([w_in[:, offs[7]:offs[8]], w_in[:, offs[10]:offs[11]]], axis=1)
    w_small = jnp.pad(w_small, ((0, 0), (0, LANE - w_small.shape[1])))
    w_u = w_in[:, offs[8]:offs[10]]
    w_g = w_in[:, offs[11]:]
    return w_a, w_small, w_u, w_g


def mixer(x, norm_g, prm, past):
    b, t, _ = x.shape
    x2 = x.reshape(b * t, D_MODEL)
    table = prm['table']
    w_a, w_small, w_u, w_g = _split_w_in(prm['w_in'])
    z_a = _mm(x2, w_a, g=norm_g).reshape(b, t, -1)
    z_s = _mm(x2, w_small, g=norm_g).reshape(b, t, -1)
    z_u = _mm(x2, w_u, g=norm_g).reshape(b, t, -1)
    z_g = _mm(x2, w_g, g=norm_g)
    q = z_a[..., :NSA_WIDTH]
    new_cmp, new_slc, win_rows = (z_a[..., NSA_WIDTH + 2 * KV_WIDTH * i:NSA_WIDTH + 2 * KV_WIDTH * (i + 1)]
                                  .reshape(b, t, 2, NSA_KV_HEADS, HEAD_DIM) for i in range(3))
    g_nsa, if_pre = z_s[..., :3 * NSA_HEADS], z_s[..., 3 * NSA_HEADS:3 * NSA_HEADS + 2 * M_HEADS]
    u, o_pre = z_u[..., :M_WIDTH], z_u[..., M_WIDTH:]
    g_a, g_b = z_g[:, :D_MODEL], z_g[:, D_MODEL:]
    q = q.reshape(b, t, NSA_KV_HEADS, NSA_GROUP, HEAD_DIM) * (HEAD_DIM ** -0.5)
    kc, vc = new_cmp[:, :, 0], new_cmp[:, :, 1]
    ks, vs = new_slc[:, :, 0], new_slc[:, :, 1]
    kw, vw = win_rows[:, :, 0], win_rows[:, :, 1]
    if past is None:
        q_pos = jnp.arange(t)
        kc_all, vc_all, ks_all, vs_all = kc, vc, ks, vs
        o_win = window_prompt(q, kw, vw, table)
        new_win = win_rows[:, t - min(WINDOW, t):]
        conv_prev = jnp.zeros((b, CONV_W - 1, M_WIDTH), u.dtype)
        m_state = (jnp.zeros((b, M_HEADS, M_HEAD_DIM, M_HEAD_DIM), F32),
                   jnp.zeros((b, M_HEADS, M_HEAD_DIM), F32),
                   jnp.zeros((b, M_HEADS), F32))
    else:
        pt = past['page_table']
        past_len = pt.shape[1] * past['cmp'].shape[1]
        pc = past['cmp'][pt].reshape((b, past_len) + past['cmp'].shape[2:])
        psl = past['slc'][pt].reshape((b, past_len) + past['slc'].shape[2:])
        kc_all = jnp.concatenate([pc[:, :, 0], kc], axis=1)
        vc_all = jnp.concatenate([pc[:, :, 1], vc], axis=1)
        ks_all = jnp.concatenate([psl[:, :, 0], ks], axis=1)
        vs_all = jnp.concatenate([psl[:, :, 1], vs], axis=1)
        q_pos = past_len + jnp.arange(t)
        w_buf = past['win'].shape[1]
        win_all = jnp.concatenate([past['win'].astype(win_rows.dtype), win_rows], axis=1)
        k_pos = past_len - w_buf + jnp.arange(w_buf + t)
        o_win = window_attn(q, win_all[:, :, 0], win_all[:, :, 1], q_pos, k_pos, table)
        new_win = win_all[:, t:]
        conv_prev = past['conv']
        m_state = (past['C'].astype(F32), past['n'].astype(F32), past['m'].astype(F32))
    o_cmp, o_sel = nsa_core(q, kc_all, vc_all, ks_all, vs_all, q_pos, table, prm['cmp_pe'], prm['cmp_w1'],
                            prm['cmp_w2'])
    g = jax.nn.sigmoid(g_nsa.reshape(b, t, NSA_KV_HEADS, NSA_GROUP, 3))
    o_nsa = (g[..., 0:1] * o_cmp + g[..., 1:2] * o_sel + g[..., 2:3] * o_win).reshape(b * t, NSA_WIDTH)
    o_m, (c_new, n_new, m_new), conv_new = mlstm_branch(u, o_pre, if_pre, conv_prev, m_state, prm['conv_w'],
                                                        prm['conv_b'], prm['m_qkv'], prm['gate_b'], prm['m_norm'])
    merged = (jax.nn.sigmoid(g_a) * _mm(o_nsa, prm['w_up_a'])
              + jax.nn.sigmoid(g_b) * _mm(o_m.reshape(b * t, M_WIDTH), prm['w_up_b']))
    y = _mm(merged, prm['w_out'], res=x2).reshape(b, t, D_MODEL)
    return y, (new_cmp, new_slc, new_win, c_new, n_new, m_new, conv_new)


def _channel_mixer(x, g, l, ffn_w1, ffn_w3, ffn_w2, moe_router, moe_w1, moe_w3, moe_w2):
    b, t, d = x.shape
    x2 = x.reshape(b * t, d)
    i = l // 2
    if l % 2 == 0:
        y = _ffn_dense(x2, g, ffn_w1[i], ffn_w3[i], ffn_w2[i])
    else:
        y = _moe(x2, g, moe_router[i], moe_w1[i], moe_w3[i], moe_w2[i])
    return y.reshape(b, t, d)


def _final_norm(x, g):
    xf = x.astype(F32)
    return xf * lax.rsqrt(jnp.mean(xf * xf, axis=-1, keepdims=True) + EPS) * g


def kernel(x_prompt, x_sample, cache_cmp_kv, cache_slc_kv, cache_win_kv, state_mlstm_C, state_mlstm_n,
           state_mlstm_m, state_mlstm_conv, page_table, rel_bias_table, norm_mix, norm_ffn, norm_final,
           w_in, cmp_pe, cmp_w1, cmp_w2, m_conv_w, m_conv_b, m_qkv, m_gate_bias, m_norm, w_up_a, w_up_b,
           w_out, ffn_w1, ffn_w3, ffn_w2, moe_router, moe_w1, moe_w3, moe_w2):
    xp, xs = x_prompt, x_sample
    prompt_states, sample_states = [], []
    for l in range(DEPTH):
        prm = {'table': rel_bias_table, 'w_in': w_in[l], 'cmp_pe': cmp_pe[l], 'cmp_w1': cmp_w1[l],
               'cmp_w2': cmp_w2[l], 'conv_w': m_conv_w[l], 'conv_b': m_conv_b[l], 'm_qkv': m_qkv[l],
               'gate_b': m_gate_bias[l], 'm_norm': m_norm[l], 'w_up_a': w_up_a[l], 'w_up_b': w_up_b[l],
               'w_out': w_out[l]}
        past = {'cmp': cache_cmp_kv[l], 'slc': cache_slc_kv[l], 'win': cache_win_kv[l],
                'C': state_mlstm_C[l], 'n': state_mlstm_n[l], 'm': state_mlstm_m[l],
                'conv': state_mlstm_conv[l], 'page_table': page_table}
        xp, st_p = mixer(xp, norm_mix[l], prm, None)
        xs, st_s = mixer(xs, norm_mix[l], prm, past)
        xp = _channel_mixer(xp, norm_ffn[l], l, ffn_w1, ffn_w3, ffn_w2, moe_router, moe_w1, moe_w3, moe_w2)
        xs = _channel_mixer(xs, norm_ffn[l], l, ffn_w1, ffn_w3, ffn_w2, moe_router, moe_w1, moe_w3, moe_w2)
        prompt_states.append(st_p)
        sample_states.append(st_s)
    y_prompt = _final_norm(xp, norm_final)
    y_sample = _final_norm(xs, norm_final)
    ps = [jnp.stack([s[i] for s in prompt_states]) for i in range(7)]
    ss = [jnp.stack([s[i] for s in sample_states]) for i in range(7)]
    return (y_prompt, y_sample, ps[0], ps[1], ps[2], ps[3], ps[4], ps[5], ps[6],
            ss[0], ss[1], ss[2], ss[3], ss[4], ss[5], ss[6])
```

```python
import functools
import math

import jax
import jax.numpy as jnp
import numpy as np
from jax import lax
from jax.experimental import pallas as pl
from jax.experimental.pallas import tpu as pltpu

D_MODEL = 1024
DEPTH = 2
NSA_HEADS = 8
NSA_KV_HEADS = 2
NSA_GROUP = NSA_HEADS // NSA_KV_HEADS
HEAD_DIM = 64
NSA_WIDTH = NSA_HEADS * HEAD_DIM
KV_WIDTH = NSA_KV_HEADS * HEAD_DIM
CMP_BLOCK = 32
CMP_STRIDE = 16
SEL_BLOCK = 64
SEL_TOP_N = 16
SEL_FORCE = 1e4
WINDOW = 512
WIN_Q_BLOCK = 128
REL_BUCKETS = 32
REL_MAX_DIST = 128
M_HEADS = 4
M_HEAD_DIM = 128
M_WIDTH = M_HEADS * M_HEAD_DIM
CONV_W = 4
M_CHUNK = 64
D_FF = 2816
N_EXPERTS = 8
TOP_K = 2
EPS = 1e-6
IN_WIDTHS = (NSA_WIDTH, KV_WIDTH, KV_WIDTH, KV_WIDTH, KV_WIDTH, KV_WIDTH, KV_WIDTH, 3 * NSA_HEADS, M_WIDTH, M_WIDTH,
             2 * M_HEADS, D_MODEL, D_MODEL)

VMEM_LIMIT_V7X = 52 * 1024 * 1024
LANE = 128

F32 = jnp.float32
BF16 = jnp.bfloat16


def _pick_tile(n, cands):
    for c in cands:
        if n % c == 0:
            return c
    return n


def _mm_kernel(*refs, norm, has_res):
    x_ref, g_ref, w_ref = refs[:3]
    res_ref = refs[3] if has_res else None
    o_ref, xs_ref = refs[-2], refs[-1]

    @pl.when(pl.program_id(1) == 0)
    def _():
        x = x_ref[...]
        if norm:
            x = x * lax.rsqrt(jnp.mean(x * x, axis=-1, keepdims=True) + EPS) * g_ref[...]
        xs_ref[...] = x.astype(BF16)

    acc = jnp.dot(xs_ref[...], w_ref[...].astype(BF16), preferred_element_type=F32)
    if has_res:
        acc = acc + res_ref[...]
    o_ref[...] = acc


def _mm(x, w, g=None, res=None):
    m, k = x.shape
    n = w.shape[1]
    tm = _pick_tile(m, (1024, 512, 256, 128))
    tn = _pick_tile(n, (512, 256, 128))
    norm = g is not None
    gg = (g if norm else jnp.ones((k,), F32)).reshape(1, k)
    in_specs = [pl.BlockSpec((tm, k), lambda i, j: (i, 0)),
                pl.BlockSpec((1, k), lambda i, j: (0, 0)),
                pl.BlockSpec((k, tn), lambda i, j: (0, j))]
    args = [x, gg, w]
    if res is not None:
        in_specs.append(pl.BlockSpec((tm, tn), lambda i, j: (i, j)))
        args.append(res)
    return pl.pallas_call(
        functools.partial(_mm_kernel, norm=norm, has_res=res is not None),
        grid=(m // tm, n // tn),
        in_specs=in_specs,
        out_specs=pl.BlockSpec((tm, tn), lambda i, j: (i, j)),
        out_shape=jax.ShapeDtypeStruct((m, n), F32),
        scratch_shapes=[pltpu.VMEM((tm, k), BF16)],
        compiler_params=pltpu.CompilerParams(dimension_semantics=("arbitrary", "arbitrary"),
                                             vmem_limit_bytes=VMEM_LIMIT_V7X),
        name="mm",
    )(*args)


def _ffn_body(x_ref, g_ref, w1_ref, w3_ref, w2_ref, o_ref, xs_ref, acc_ref, *, residual, grouped):
    j = pl.program_id(1)

    @pl.when(j == 0)
    def _():
        x = x_ref[...]
        xn = x * lax.rsqrt(jnp.mean(x * x, axis=-1, keepdims=True) + EPS) * g_ref[...]
        xs_ref[...] = xn.astype(BF16)
        acc_ref[...] = jnp.zeros_like(acc_ref)

    xs = xs_ref[...]
    w1 = w1_ref[0] if grouped else w1_ref[...]
    w3 = w3_ref[0] if grouped else w3_ref[...]
    w2 = w2_ref[0] if grouped else w2_ref[...]
    a = jnp.dot(xs, w1.astype(BF16), preferred_element_type=F32)
    b = jnp.dot(xs, w3.astype(BF16), preferred_element_type=F32)
    h = (a * jax.nn.sigmoid(a) * b).astype(BF16)
    acc_ref[...] += jnp.dot(h, w2.astype(BF16), preferred_element_type=F32)

    @pl.when(j == pl.num_programs(1) - 1)
    def _():
        if residual:
            o_ref[...] = x_ref[...] + acc_ref[...]
        else:
            o_ref[...] = acc_ref[...]


def _ffn_dense_kernel(x_ref, g_ref, w1_ref, w3_ref, w2_ref, o_ref, xs_ref, acc_ref):
    _ffn_body(x_ref, g_ref, w1_ref, w3_ref, w2_ref, o_ref, xs_ref, acc_ref, residual=True, grouped=False)


def _ffn_grouped_kernel(be_ref, x_ref, g_ref, w1_ref, w3_ref, w2_ref, o_ref, xs_ref, acc_ref):
    del be_ref
    _ffn_body(x_ref, g_ref, w1_ref, w3_ref, w2_ref, o_ref, xs_ref, acc_ref, residual=False, grouped=True)


def _ffn_dense(x, g, w1, w3, w2):
    m, d = x.shape
    f = w1.shape[1]
    tm = _pick_tile(m, (1024, 512, 256, 128))
    tf = _pick_tile(f, (256, 128))
    return pl.pallas_call(
        _ffn_dense_kernel,
        grid=(m // tm, f // tf),
        in_specs=[pl.BlockSpec((tm, d), lambda i, j: (i, 0)),
                  pl.BlockSpec((1, d), lambda i, j: (0, 0)),
                  pl.BlockSpec((d, tf), lambda i, j: (0, j)),
                  pl.BlockSpec((d, tf), lambda i, j: (0, j)),
                  pl.BlockSpec((tf, d), lambda i, j: (j, 0))],
        out_specs=pl.BlockSpec((tm, d), lambda i, j: (i, 0)),
        out_shape=jax.ShapeDtypeStruct((m, d), F32),
        scratch_shapes=[pltpu.VMEM((tm, d), BF16), pltpu.VMEM((tm, d), F32)],
        compiler_params=pltpu.CompilerParams(dimension_semantics=("arbitrary", "arbitrary"),
                                             vmem_limit_bytes=VMEM_LIMIT_V7X),
        name="ffn_dense",
    )(x, g.reshape(1, d), w1, w3, w2)


def _ffn_grouped(xd, blk_e, g, w1, w3, w2, tm):
    rows, d = xd.shape
    f = w1.shape[2]
    tf = _pick_tile(f, (256, 128))
    grid_spec = pltpu.PrefetchScalarGridSpec(
        num_scalar_prefetch=1,
        grid=(rows // tm, f // tf),
        in_specs=[pl.BlockSpec((tm, d), lambda i, j, be: (i, 0)),
                  pl.BlockSpec((1, d), lambda i, j, be: (0, 0)),
                  pl.BlockSpec((1, d, tf), lambda i, j, be: (be[i], 0, j)),
                  pl.BlockSpec((1, d, tf), lambda i, j, be: (be[i], 0, j)),
                  pl.BlockSpec((1, tf, d), lambda i, j, be: (be[i], j, 0))],
        out_specs=pl.BlockSpec((tm, d), lambda i, j, be: (i, 0)),
        scratch_shapes=[pltpu.VMEM((tm, d), BF16), pltpu.VMEM((tm, d), F32)],
    )
    return pl.pallas_call(
        _ffn_grouped_kernel,
        grid_spec=grid_spec,
        out_shape=jax.ShapeDtypeStruct((rows, d), F32),
        compiler_params=pltpu.CompilerParams(dimension_semantics=("arbitrary", "arbitrary"),
                                             vmem_limit_bytes=VMEM_LIMIT_V7X),
        name="ffn_grouped",
    )(blk_e, xd, g.reshape(1, d), w1, w3, w2)


def _moe(x, g, router, w1, w3, w2):
    n, d = x.shape
    tm = 1024 if n >= 8192 else 128
    router_p = jnp.pad(router, ((0, 0), (0, LANE - N_EXPERTS)))
    logits = _mm(x, router_p, g=g)[:, :N_EXPERTS]
    top_val, top_idx = lax.top_k(logits, TOP_K)
    gate = jax.nn.softmax(top_val, axis=-1).reshape(-1)
    e_flat = top_idx.reshape(-1)
    n_asg = n * TOP_K
    order = jnp.argsort(e_flat)
    e_sorted = e_flat[order]
    counts = jnp.bincount(e_flat, length=N_EXPERTS)
    padded = (counts + tm - 1) // tm * tm
    pad_end = jnp.cumsum(padded)
    pad_start = pad_end - padded
    start = jnp.cumsum(counts) - counts
    dest = pad_start[e_sorted] + jnp.arange(n_asg) - start[e_sorted]
    n_blocks = n_asg // tm + N_EXPERTS
    src_tok = jnp.zeros((n_blocks * tm,), jnp.int32).at[dest].set((order // TOP_K).astype(jnp.int32))
    pos = jnp.zeros((n_asg,), jnp.int32).at[order].set(dest.astype(jnp.int32))
    blk_e = jnp.minimum(jnp.searchsorted(pad_end, jnp.arange(n_blocks) * tm, side='right'),
                        N_EXPERTS - 1).astype(jnp.int32)
    yd = _ffn_grouped(x[src_tok], blk_e, g, w1, w3, w2, tm)
    contrib = yd[pos] * gate[:, None]
    return x + contrib.reshape(n, TOP_K, d).sum(axis=1)


N_CHUNK = 128
CHUNK_W = CMP_STRIDE * HEAD_DIM


def _compress_kernel(x_ref, pe_ref, w1_ref, w2_ref, o_ref):
    c = x_ref[0]
    lo = jnp.dot((c + pe_ref[0:1]).astype(BF16), w1_ref[0].astype(BF16), preferred_element_type=F32)
    hi = jnp.dot((c + pe_ref[1:2]).astype(BF16), w1_ref[1].astype(BF16), preferred_element_type=F32)
    hid = jax.nn.gelu(lo + pltpu.roll(hi, N_CHUNK - 1, 0))
    out = jnp.dot(hid.astype(BF16), w2_ref[...].astype(BF16), preferred_element_type=F32)
    row = lax.broadcasted_iota(jnp.int32, out.shape, 0)
    o_ref[0] = jnp.where(row < N_CHUNK - 1, out, 0.0)


def _compress(xc, pe, w1, w2):
    nb = xc.shape[0]
    hidden = w1.shape[-1]
    return pl.pallas_call(
        _compress_kernel,
        grid=(nb,),
        in_specs=[pl.BlockSpec((1, N_CHUNK, CHUNK_W), lambda i: (i, 0, 0)),
                  pl.BlockSpec((2, CHUNK_W), lambda i: (0, 0)),
                  pl.BlockSpec((2, CHUNK_W, hidden), lambda i: (0, 0, 0)),
                  pl.BlockSpec((hidden, HEAD_DIM), lambda i: (0, 0))],
        out_specs=pl.BlockSpec((1, N_CHUNK, HEAD_DIM), lambda i: (i, 0, 0)),
        out_shape=jax.ShapeDtypeStruct((nb, N_CHUNK, HEAD_DIM), F32),
        compiler_params=pltpu.CompilerParams(dimension_semantics=("arbitrary",), vmem_limit_bytes=VMEM_LIMIT_V7X),
        name="compress",
    )(xc, pe.reshape(2, CHUNK_W), w1.reshape(2, CHUNK_W, hidden), w2)


def _compress_heads(kv, pe, w1, w2):
    b = kv.shape[0]
    xc = kv.reshape(b, N_CHUNK, CMP_STRIDE, NSA_KV_HEADS, HEAD_DIM).transpose(0, 3, 1, 2, 4)
    out = _compress(xc.reshape(b * NSA_KV_HEADS, N_CHUNK, CHUNK_W), pe, w1, w2)
    return out.reshape(b, NSA_KV_HEADS, N_CHUNK, HEAD_DIM)


TQ = 128
TK = 128
NEG = -1e30
N_BIAS_TILES = WINDOW // TK + 1


def _nsa_prompt_kernel(q_ref, kct_ref, vcc_ref, kst_ref, vs_ref, kwt_ref, vw_ref, bcmp_ref, btile_ref, gate_ref,
                       ov_ref, exp_ref, o_ref, selneg_ref, m_ref, l_ref, acc_ref):
    i = pl.program_id(2)
    rows = NSA_GROUP * TQ
    q = q_ref[0, 0].reshape(rows, HEAD_DIM)

    s = jnp.dot(q, kct_ref[0, 0], preferred_element_type=F32) + bcmp_ref[0].reshape(rows, N_CHUNK)
    m = jnp.max(s, axis=-1, keepdims=True)
    e = jnp.where(s > 0.1 * NEG, jnp.exp(s - m), 0.0)
    p = e / jnp.maximum(jnp.sum(e, axis=-1, keepdims=True), 1e-30)
    o_cmp = jnp.dot(p.astype(BF16), vcc_ref[0, 0], preferred_element_type=F32)

    p_sum = p[0:TQ] + p[TQ:2 * TQ] + p[2 * TQ:3 * TQ] + p[3 * TQ:4 * TQ]
    imp = jnp.dot(p_sum, ov_ref[...], preferred_element_type=F32, precision=lax.Precision.HIGHEST)
    lane = lax.broadcasted_iota(jnp.int32, (TQ, LANE), 1)
    tok = lax.broadcasted_iota(jnp.int32, (TQ, LANE), 0) + i * TQ
    cur = tok // SEL_BLOCK
    valid = lane <= cur
    forced = (lane == 0) | (lane == cur) | (lane == cur - 1)
    score = jnp.where(valid, imp + jnp.where(forced, SEL_FORCE, 0.0), -1.0)
    rank = jnp.zeros((TQ, LANE), F32)
    for c in range(T_PROMPT // SEL_BLOCK):
        sc = score[:, c:c + 1]
        beats = (sc > score) | ((sc == score) & (lane > c))
        rank = rank + jnp.where(beats, 1.0, 0.0)
    sel = jnp.where(valid & (rank < SEL_TOP_N), 1.0, 0.0).astype(BF16)
    sel_keys = jnp.dot(sel, exp_ref[...], preferred_element_type=F32)
    selneg_ref[...] = (sel_keys - 1.0) * (-NEG)

    def flash(kt_ref, v_ref, j_lo, selected):
        m_ref[...] = jnp.full(m_ref.shape, -jnp.inf, F32)
        l_ref[...] = jnp.zeros(l_ref.shape, F32)
        acc_ref[...] = jnp.zeros(acc_ref.shape, F32)

        def body(j, carry):
            k0 = pl.multiple_of(j * TK, TK)
            d0 = i - j
            sb = jnp.dot(q, kt_ref[0, 0, :, pl.ds(k0, TK)], preferred_element_type=F32)
            sb = sb.reshape(NSA_GROUP, TQ, TK) + btile_ref[0, jnp.minimum(d0, 2) if selected else d0]
            if selected:
                sb = sb + selneg_ref[:, pl.ds(k0, TK)][None]
            sb = sb.reshape(rows, TK)
            m_old = m_ref[...]
            m_new = jnp.maximum(m_old, jnp.max(sb, axis=-1, keepdims=True))
            pt = jnp.exp(sb - m_new)
            alpha = jnp.exp(m_old - m_new)
            l_ref[...] = alpha * l_ref[...] + jnp.sum(pt, axis=-1, keepdims=True)
            acc_ref[...] = alpha * acc_ref[...] + jnp.dot(pt.astype(BF16), v_ref[0, 0, pl.ds(k0, TK), :],
                                                          preferred_element_type=F32)
            m_ref[...] = m_new
            return carry

        lax.fori_loop(j_lo, i + 1, body, 0)
        return acc_ref[...] / l_ref[...]

    o_sel = flash(kst_ref, vs_ref, 0, True)
    o_win = flash(kwt_ref, vw_ref, jnp.maximum(i - (N_BIAS_TILES - 1), 0), False)

    g = jax.nn.sigmoid(gate_ref[0, 0])
    for a in range(NSA_GROUP):
        r = slice(a * TQ, (a + 1) * TQ)
        o_ref[0, 0, a] = (g[:, 3 * a:3 * a + 1] * o_cmp[r] + g[:, 3 * a + 1:3 * a + 2] * o_sel[r]
                          + g[:, 3 * a + 2:3 * a + 3] * o_win[r])


T_PROMPT = 2048


def _bias_lookup(table, dist):
    oh = jax.nn.one_hot(rel_bucket(dist), REL_BUCKETS, dtype=F32)
    return jnp.einsum('...r,rh->...h', oh, table, precision=lax.Precision.HIGHEST)


def _prompt_bias_tables(table):
    t = T_PROMPT
    q_pos = jnp.arange(t)
    block_end = jnp.arange(N_CHUNK) * CMP_STRIDE + (CMP_BLOCK - 1)
    dist = q_pos[:, None] - block_end[None, :]
    ok = (dist >= 0) & (jnp.arange(N_CHUNK)[None, :] < N_CHUNK - 1)
    bcmp = jnp.where(ok[..., None], _bias_lookup(table, dist), NEG)
    bcmp = bcmp.reshape(t, N_CHUNK, NSA_KV_HEADS, NSA_GROUP).transpose(2, 3, 0, 1)
    d0 = jnp.arange(N_BIAS_TILES)[:, None, None]
    dist = d0 * TK + jnp.arange(TQ)[None, :, None] - jnp.arange(TK)[None, None, :]
    ok = (dist >= 0) & (dist < WINDOW)
    bt = jnp.where(ok[..., None], _bias_lookup(table, dist), NEG)
    bt = bt.reshape(N_BIAS_TILES, TQ, TK, NSA_KV_HEADS, NSA_GROUP).transpose(3, 0, 4, 1, 2)
    c0 = jnp.arange(N_CHUNK) * CMP_STRIDE
    s0 = jnp.arange(LANE) * SEL_BLOCK
    ov = jnp.clip(jnp.minimum(c0[:, None] + CMP_BLOCK, s0[None, :] + SEL_BLOCK)
                  - jnp.maximum(c0[:, None], s0[None, :]), 0, CMP_BLOCK).astype(F32) / CMP_BLOCK
    ov = jnp.where((jnp.arange(N_CHUNK)[:, None] < N_CHUNK - 1) & (jnp.arange(LANE)[None, :] < t // SEL_BLOCK), ov, 0.0)
    expand = (jnp.arange(LANE)[:, None] == (jnp.arange(t) // SEL_BLOCK)[None, :]).astype(BF16)
    return bcmp, bt, ov, expand


def _nsa_prompt(q_h, kct, vcc, kst, vs, kwt, vw, gates, tables):
    bcmp, bt, ov, expand = tables
    b, _, _, t, _ = q_h.shape
    rows = NSA_GROUP * TQ
    per_bh = lambda *blk: pl.BlockSpec((1, 1) + blk, lambda bi, h, i: (bi, h) + (0,) * len(blk))
    return pl.pallas_call(
        _nsa_prompt_kernel,
        grid=(b, NSA_KV_HEADS, t // TQ),
        in_specs=[pl.BlockSpec((1, 1, NSA_GROUP, TQ, HEAD_DIM), lambda bi, h, i: (bi, h, 0, i, 0)),
                  per_bh(HEAD_DIM, N_CHUNK), per_bh(N_CHUNK, HEAD_DIM),
                  per_bh(HEAD_DIM, t), per_bh(t, HEAD_DIM), per_bh(HEAD_DIM, t), per_bh(t, HEAD_DIM),
                  pl.BlockSpec((1, NSA_GROUP, TQ, N_CHUNK), lambda bi, h, i: (h, 0, i, 0)),
                  pl.BlockSpec((1, N_BIAS_TILES, NSA_GROUP, TQ, TK), lambda bi, h, i: (h, 0, 0, 0, 0)),
                  pl.BlockSpec((1, 1, TQ, 3 * NSA_GROUP), lambda bi, h, i: (bi, h, i, 0)),
                  pl.BlockSpec((N_CHUNK, LANE), lambda bi, h, i: (0, 0)),
                  pl.BlockSpec((LANE, t), lambda bi, h, i: (0, 0))],
        out_specs=pl.BlockSpec((1, 1, NSA_GROUP, TQ, HEAD_DIM), lambda bi, h, i: (bi, h, 0, i, 0)),
        out_shape=jax.ShapeDtypeStruct((b, NSA_KV_HEADS, NSA_GROUP, t, HEAD_DIM), F32),
        scratch_shapes=[pltpu.VMEM((TQ, t), F32), pltpu.VMEM((rows, 1), F32), pltpu.VMEM((rows, 1), F32),
                        pltpu.VMEM((rows, HEAD_DIM), F32)],
        compiler_params=pltpu.CompilerParams(dimension_semantics=("arbitrary", "arbitrary", "arbitrary"),
                                             vmem_limit_bytes=VMEM_LIMIT_V7X),
        name="nsa_prompt",
    )(q_h, kct, vcc, kst, vs, kwt, vw, bcmp, bt, gates, ov, expand)


def _heads_t(x, b, t):
    return x.astype(BF16).reshape(b, t, NSA_KV_HEADS, HEAD_DIM).transpose(0, 2, 3, 1)


def _heads(x, b, t):
    return x.astype(BF16).reshape(b, t, NSA_KV_HEADS, HEAD_DIM).transpose(0, 2, 1, 3)


def _nsa_prompt_from_proj(z_a, z_s, b, t, tables, pe, w1, w2):
    col = lambda i: z_a[:, NSA_WIDTH + KV_WIDTH * i:NSA_WIDTH + KV_WIDTH * (i + 1)]
    q_h = (z_a[:, :NSA_WIDTH] * (HEAD_DIM ** -0.5)).astype(BF16)
    q_h = q_h.reshape(b, t, NSA_KV_HEADS, NSA_GROUP, HEAD_DIM).transpose(0, 2, 3, 1, 4)
    kcc = _compress_heads(col(0).reshape(b, t, NSA_KV_HEADS, HEAD_DIM), pe[0], w1[0], w2[0])
    vcc = _compress_heads(col(1).reshape(b, t, NSA_KV_HEADS, HEAD_DIM), pe[1], w1[1], w2[1])
    gates = z_s[:, :3 * NSA_HEADS].reshape(b, t, NSA_KV_HEADS, 3 * NSA_GROUP).transpose(0, 2, 1, 3)
    o = _nsa_prompt(q_h, kcc.astype(BF16).transpose(0, 1, 3, 2), vcc.astype(BF16),
                    _heads_t(col(2), b, t), _heads(col(3), b, t), _heads_t(col(4), b, t), _heads(col(5), b, t),
                    gates, tables)
    return o.transpose(0, 3, 1, 2, 4).reshape(b * t, NSA_WIDTH)


def rel_bucket(dist):
    d = jnp.maximum(dist, 0)
    exact = REL_BUCKETS // 2
    log_part = exact + (jnp.log(jnp.maximum(d, 1).astype(F32) / exact)
                        / math.log(REL_MAX_DIST / exact) * (REL_BUCKETS - exact)).astype(jnp.int32)
    return jnp.where(d < exact, d, jnp.minimum(log_part, REL_BUCKETS - 1))


def rel_bias(table, dist):
    return _bias_lookup(table, dist)


def masked_softmax(s, mask):
    p = jax.nn.softmax(jnp.where(mask, s, -1e30), axis=-1)
    return jnp.where(mask, p, 0.0)


def cmp_attn(q, kc, vc, q_pos, table):
    nq, nc = q_pos.shape[0], kc.shape[1]
    block_end = jnp.arange(nc) * CMP_STRIDE + (CMP_BLOCK - 1)
    dist = q_pos[:, None] - block_end[None, :]
    bias = jnp.moveaxis(rel_bias(table, dist).reshape(nq, nc, NSA_KV_HEADS, NSA_GROUP), 1, 3)
    s = jnp.einsum('bqhgd,bnhd->bqhgn', q, kc).astype(F32) + bias
    p = masked_softmax(s, (dist >= 0)[:, None, None, :])
    return jnp.einsum('bqhgn,bnhd->bqhgd', p.astype(vc.dtype), vc), p


def select_mask(p_cmp, q_pos, n_sel):
    nc = p_cmp.shape[-1]
    c0 = jnp.arange(nc) * CMP_STRIDE
    s0 = jnp.arange(n_sel) * SEL_BLOCK
    ov = jnp.clip(jnp.minimum(c0[:, None] + CMP_BLOCK, s0[None, :] + SEL_BLOCK)
                  - jnp.maximum(c0[:, None], s0[None, :]), 0, CMP_BLOCK).astype(F32) / CMP_BLOCK
    imp = jnp.einsum('bqhgn,ns->bqhs', p_cmp, ov, precision=lax.Precision.HIGHEST)
    cur = q_pos // SEL_BLOCK
    j = jnp.arange(n_sel)
    valid = (j[None, :] <= cur[:, None])[:, None, :]
    forced = ((j[None, :] == 0) | (j[None, :] == cur[:, None]) | (j[None, :] == cur[:, None] - 1))[:, None, :]
    score = jnp.where(valid, imp + jnp.where(forced, SEL_FORCE, 0.0), -1.0)
    top_s, idx = lax.top_k(score, min(SEL_TOP_N, n_sel))
    ok = top_s >= 0.0
    return jnp.any((idx[..., None] == j) & ok[..., None], axis=3)


def sel_attn_dense(q, ks, vs, sel, q_pos, table):
    nq, t = q.shape[0], ks.shape[0]
    k_pos = jnp.arange(t)
    dist = q_pos[:, None] - k_pos[None, :]
    bias = jnp.moveaxis(rel_bias(table, dist).reshape(nq, t, NSA_KV_HEADS, NSA_GROUP), 1, 3)
    keymask = jnp.repeat(sel, SEL_BLOCK, axis=-1)[..., :t] & (dist >= 0)[:, None, :]
    s = jnp.einsum('qhgd,khd->qhgk', q, ks).astype(F32) + bias
    p = masked_softmax(s, keymask[:, :, None, :])
    return jnp.einsum('qhgk,khd->qhgd', p, vs)


def window_attn(q, k, v, q_pos, k_pos, table):
    nq, nk = q_pos.shape[0], k_pos.shape[0]
    dist = q_pos[:, None] - k_pos[None, :]
    mask = (dist >= 0) & (dist < WINDOW) & (k_pos[None, :] >= 0)
    bias = jnp.moveaxis(rel_bias(table, dist).reshape(nq, nk, NSA_KV_HEADS, NSA_GROUP), 1, 3)
    s = jnp.einsum('bqhgd,bkhd->bqhgk', q, k).astype(F32) + bias
    p = masked_softmax(s, mask[:, None, None, :])
    return jnp.einsum('bqhgk,bkhd->bqhgd', p.astype(v.dtype), v)


def nsa_core(q, kc_all, vc_all, ks_all, vs_all, q_pos, table, pe, w1, w2):
    n_past = N_CHUNK * CMP_STRIDE
    kcc = _compress_heads(kc_all[:, :n_past], pe[0], w1[0], w2[0])[:, :, :N_CHUNK - 1].transpose(0, 2, 1, 3)
    vcc = _compress_heads(vc_all[:, :n_past], pe[1], w1[1], w2[1])[:, :, :N_CHUNK - 1].transpose(0, 2, 1, 3)
    o_cmp, p_cmp = cmp_attn(q, kcc, vcc, q_pos, table)
    n_sel = -(-ks_all.shape[1] // SEL_BLOCK)
    sel = select_mask(p_cmp, q_pos, n_sel)
    one = lambda a: sel_attn_dense(a[0], a[1], a[2], a[3], q_pos, table)
    batched = (q, ks_all, vs_all, sel)
    o_sel = lax.map(one, batched) if q.shape[1] > SEL_BLOCK else jax.vmap(one)(batched)
    return o_cmp, o_sel


def mlstm_chunk(state, inp):
    c0, n0, m0 = state
    q, k, v, li, lf = inp
    L = q.shape[1]
    bcum = jnp.cumsum(lf, axis=1)
    log_d = bcum[:, :, None, :] - bcum[:, None, :, :] + li[:, None, :, :]
    causal = jnp.tril(jnp.ones((L, L), dtype=bool))
    log_d = jnp.where(causal[None, :, :, None], log_d, -jnp.inf)
    log_inter = bcum + m0[:, None, :]
    m = jnp.maximum(log_inter, jnp.max(log_d, axis=2))
    dw = jnp.exp(log_d - m[:, :, None, :])
    w_inter = jnp.exp(log_inter - m)
    s = jnp.einsum('bthd,bshd->btsh', q, k) * dw
    num = jnp.einsum('btsh,bshe->bthe', s, v) + w_inter[..., None] * jnp.einsum('bthd,bhde->bthe', q, c0)
    den = jnp.sum(s, axis=2) + w_inter * jnp.einsum('bthd,bhd->bth', q, n0)
    h = num / jnp.maximum(jnp.abs(den), jnp.exp(-m))[..., None]
    b_last = bcum[:, -1]
    log_s = b_last[:, None, :] - bcum + li
    m_new = jnp.maximum(b_last + m0, jnp.max(log_s, axis=1))
    ws = jnp.exp(log_s - m_new[:, None, :])
    wc = jnp.exp(b_last + m0 - m_new)
    c_new = wc[..., None, None] * c0 + jnp.einsum('bsh,bshd,bshe->bhde', ws, k, v)
    n_new = wc[..., None] * n0 + jnp.einsum('bsh,bshd->bhd', ws, k)
    return (c_new, n_new, m_new), h


def mlstm_scan(q, k, v, li, lf, state):
    b, t = q.shape[:2]
    if t % M_CHUNK == 0:
        n_ch = t // M_CHUNK

        def chunks(a):
            return jnp.moveaxis(a.reshape((b, n_ch, M_CHUNK) + a.shape[2:]), 1, 0)

        new_state, h = lax.scan(mlstm_chunk, state, (chunks(q), chunks(k), chunks(v), chunks(li), chunks(lf)))
        h = jnp.moveaxis(h, 0, 1).reshape(v.shape)
    else:
        new_state, h = mlstm_chunk(state, (q, k, v, li, lf))
    return h, new_state


def mlstm_branch(u, o_pre, if_pre, conv_prev, state, conv_w, conv_b, m_qkv, gate_b, norm_g):
    b, t, _ = u.shape
    ext = jnp.concatenate([conv_prev.astype(u.dtype), u], axis=1)
    conv = conv_b
    for j in range(CONV_W):
        conv = conv + ext[:, j:j + t] * conv_w[j]
    uc = jax.nn.silu(conv).reshape(b, t, M_HEADS, M_HEAD_DIM)
    uh = u.reshape(b, t, M_HEADS, M_HEAD_DIM)
    q = jnp.einsum('bthd,hde->bthe', uc, m_qkv[0]).astype(F32) * (M_HEAD_DIM ** -0.5)
    k = jnp.einsum('bthd,hde->bthe', uc, m_qkv[1]).astype(F32)
    v = jnp.einsum('bthd,hde->bthe', uh, m_qkv[2]).astype(F32)
    li = (if_pre[..., :M_HEADS] + gate_b[0]).astype(F32)
    lf = jax.nn.log_sigmoid((if_pre[..., M_HEADS:] + gate_b[1]).astype(F32))
    h, new_state = mlstm_scan(q, k, v, li, lf, state)
    hn = h * lax.rsqrt(jnp.mean(h * h, axis=-1, keepdims=True) + EPS)
    out = jax.nn.sigmoid(o_pre.astype(F32)) * hn.reshape(b, t, M_WIDTH) * norm_g.astype(F32)
    return out.astype(u.dtype), new_state, ext[:, t:]


def _split_w_in(w_in):
    offs = np.cumsum((0,) + IN_WIDTHS)
    w_a = w_in[:, :offs[7]]
    w_small = jnp.concatenate([w_in[:, offs[7]:offs[8]], w_in[:, offs[10]:offs[11]]], axis=1)
    w_small = jnp.pad(w_small, ((0, 0), (0, LANE - w_small.shape[1])))
    w_u = w_in[:, offs[8]:offs[10]]
    w_g = w_in[:, offs[11]:]
    return w_a, w_small, w_u, w_g


def mixer(x, norm_g, prm, past):
    b, t, _ = x.shape
    x2 = x.reshape(b * t, D_MODEL)
    table = prm['table']
    w_a, w_small, w_u, w_g = _split_w_in(prm['w_in'])
    z_a = _mm(x2, w_a, g=norm_g).reshape(b, t, -1)
    z_s = _mm(x2, w_small, g=norm_g).reshape(b, t, -1)
    z_u = _mm(x2, w_u, g=norm_g).reshape(b, t, -1)
    z_g = _mm(x2, w_g, g=norm_g)
    q = z_a[..., :NSA_WIDTH]
    new_cmp, new_slc, win_rows = (z_a[..., NSA_WIDTH + 2 * KV_WIDTH * i:NSA_WIDTH + 2 * KV_WIDTH * (i + 1)]
                                  .reshape(b, t, 2, NSA_KV_HEADS, HEAD_DIM) for i in range(3))
    g_nsa, if_pre = z_s[..., :3 * NSA_HEADS], z_s[..., 3 * NSA_HEADS:3 * NSA_HEADS + 2 * M_HEADS]
    u, o_pre = z_u[..., :M_WIDTH], z_u[..., M_WIDTH:]
    g_a, g_b = z_g[:, :D_MODEL], z_g[:, D_MODEL:]
    q = q.reshape(b, t, NSA_KV_HEADS, NSA_GROUP, HEAD_DIM) * (HEAD_DIM ** -0.5)
    kc, vc = new_cmp[:, :, 0], new_cmp[:, :, 1]
    ks, vs = new_slc[:, :, 0], new_slc[:, :, 1]
    kw, vw = win_rows[:, :, 0], win_rows[:, :, 1]
    if past is None:
        o_nsa = _nsa_prompt_from_proj(z_a.reshape(b * t, -1), z_s.reshape(b * t, -1), b, t, prm['prompt_tables'],
                                      prm['cmp_pe'], prm['cmp_w1'], prm['cmp_w2'])
        new_win = win_rows[:, t - min(WINDOW, t):]
        conv_prev = jnp.zeros((b, CONV_W - 1, M_WIDTH), u.dtype)
        m_state = (jnp.zeros((b, M_HEADS, M_HEAD_DIM, M_HEAD_DIM), F32),
                   jnp.zeros((b, M_HEADS, M_HEAD_DIM), F32),
                   jnp.zeros((b, M_HEADS), F32))
    else:
        pt = past['page_table']
        past_len = pt.shape[1] * past['cmp'].shape[1]
        pc = past['cmp'][pt].reshape((b, past_len) + past['cmp'].shape[2:])
        psl = past['slc'][pt].reshape((b, past_len) + past['slc'].shape[2:])
        kc_all = jnp.concatenate([pc[:, :, 0], kc], axis=1)
        vc_all = jnp.concatenate([pc[:, :, 1], vc], axis=1)
        ks_all = jnp.concatenate([psl[:, :, 0], ks], axis=1)
        vs_all = jnp.concatenate([psl[:, :, 1], vs], axis=1)
        q_pos = past_len + jnp.arange(t)
        w_buf = past['win'].shape[1]
        win_all = jnp.concatenate([past['win'].astype(win_rows.dtype), win_rows], axis=1)
        k_pos = past_len - w_buf + jnp.arange(w_buf + t)
        o_win = window_attn(q, win_all[:, :, 0], win_all[:, :, 1], q_pos, k_pos, table)
        new_win = win_all[:, t:]
        conv_prev = past['conv']
        m_state = (past['C'].astype(F32), past['n'].astype(F32), past['m'].astype(F32))
        o_cmp, o_sel = nsa_core(q, kc_all, vc_all, ks_all, vs_all, q_pos, table, prm['cmp_pe'], prm['cmp_w1'],
                                prm['cmp_w2'])
        g = jax.nn.sigmoid(g_nsa.reshape(b, t, NSA_KV_HEADS, NSA_GROUP, 3))
        o_nsa = (g[..., 0:1] * o_cmp + g[..., 1:2] * o_sel + g[..., 2:3] * o_win).reshape(b * t, NSA_WIDTH)
    o_m, (c_new, n_new, m_new), conv_new = mlstm_branch(u, o_pre, if_pre, conv_prev, m_state, prm['conv_w'],
                                                        prm['conv_b'], prm['m_qkv'], prm['gate_b'], prm['m_norm'])
    merged = (jax.nn.sigmoid(g_a) * _mm(o_nsa, prm['w_up_a'])
              + jax.nn.sigmoid(g_b) * _mm(o_m.reshape(b * t, M_WIDTH), prm['w_up_b']))
    y = _mm(merged, prm['w_out'], res=x2).reshape(b, t, D_MODEL)
    return y, (new_cmp, new_slc, new_win, c_new, n_new, m_new, conv_new)


def _channel_mixer(x, g, l, ffn_w1, ffn_w3, ffn_w2, moe_router, moe_w1, moe_w3, moe_w2):
    b, t, d = x.shape
    x2 = x.reshape(b * t, d)
    i = l // 2
    if l % 2 == 0:
        y = _ffn_dense(x2, g, ffn_w1[i], ffn_w3[i], ffn_w2[i])
    else:
        y = _moe(x2, g, moe_router[i], moe_w1[i], moe_w3[i], moe_w2[i])
    return y.reshape(b, t, d)


def _final_norm(x, g):
    xf = x.astype(F32)
    return xf * lax.rsqrt(jnp.mean(xf * xf, axis=-1, keepdims=True) + EPS) * g


def kernel(x_prompt, x_sample, cache_cmp_kv, cache_slc_kv, cache_win_kv, state_mlstm_C, state_mlstm_n,
           state_mlstm_m, state_mlstm_conv, page_table, rel_bias_table, norm_mix, norm_ffn, norm_final,
           w_in, cmp_pe, cmp_w1, cmp_w2, m_conv_w, m_conv_b, m_qkv, m_gate_bias, m_norm, w_up_a, w_up_b,
           w_out, ffn_w1, ffn_w3, ffn_w2, moe_router, moe_w1, moe_w3, moe_w2):
    xp, xs = x_prompt, x_sample
    prompt_states, sample_states = [], []
    prompt_tables = _prompt_bias_tables(rel_bias_table)
    for l in range(DEPTH):
        prm = {'table': rel_bias_table, 'prompt_tables': prompt_tables, 'w_in': w_in[l], 'cmp_pe': cmp_pe[l], 'cmp_w1': cmp_w1[l],
               'cmp_w2': cmp_w2[l], 'conv_w': m_conv_w[l], 'conv_b': m_conv_b[l], 'm_qkv': m_qkv[l],
               'gate_b': m_gate_bias[l], 'm_norm': m_norm[l], 'w_up_a': w_up_a[l], 'w_up_b': w_up_b[l],
               'w_out': w_out[l]}
        past = {'cmp': cache_cmp_kv[l], 'slc': cache_slc_kv[l], 'win': cache_win_kv[l],
                'C': state_mlstm_C[l], 'n': state_mlstm_n[l], 'm': state_mlstm_m[l],
                'conv': state_mlstm_conv[l], 'page_table': page_table}
        xp, st_p = mixer(xp, norm_mix[l], prm, None)
        xs, st_s = mixer(xs, norm_mix[l], prm, past)
        xp = _channel_mixer(xp, norm_ffn[l], l, ffn_w1, ffn_w3, ffn_w2, moe_router, moe_w1, moe_w3, moe_w2)
        xs = _channel_mixer(xs, norm_ffn[l], l, ffn_w1, ffn_w3, ffn_w2, moe_router, moe_w1, moe_w3, moe_w2)
        prompt_states.append(st_p)
        sample_states.append(st_s)
    y_prompt = _final_norm(xp, norm_final)
    y_sample = _final_norm(xs, norm_final)
    ps = [jnp.stack([s[i] for s in prompt_states]) for i in range(7)]
    ss = [jnp.stack([s[i] for s in sample_states]) for i in range(7)]
    return (y_prompt, y_sample, ps[0], ps[1], ps[2], ps[3], ps[4], ps[5], ps[6],
            ss[0], ss[1], ss[2], ss[3], ss[4], ss[5], ss[6])
```

```python
import functools
import math

import jax
import jax.numpy as jnp
import numpy as np
from jax import lax
from jax.experimental import pallas as pl
from jax.experimental.pallas import tpu as pltpu

D_MODEL = 1024
DEPTH = 2
NSA_HEADS = 8
NSA_KV_HEADS = 2
NSA_GROUP = NSA_HEADS // NSA_KV_HEADS
HEAD_DIM = 64
NSA_WIDTH = NSA_HEADS * HEAD_DIM
KV_WIDTH = NSA_KV_HEADS * HEAD_DIM
CMP_BLOCK = 32
CMP_STRIDE = 16
CMP_HIDDEN = 128
SEL_BLOCK = 64
SEL_TOP_N = 16
SEL_FORCE = 1e4
WINDOW = 512
WIN_Q_BLOCK = 128
REL_BUCKETS = 32
REL_MAX_DIST = 128
M_HEADS = 4
M_HEAD_DIM = 128
M_WIDTH = M_HEADS * M_HEAD_DIM
CONV_W = 4
M_CHUNK = 64
D_FF = 2816
N_EXPERTS = 8
TOP_K = 2
EPS = 1e-6
IN_WIDTHS = (NSA_WIDTH, KV_WIDTH, KV_WIDTH, KV_WIDTH, KV_WIDTH, KV_WIDTH, KV_WIDTH, 3 * NSA_HEADS, M_WIDTH, M_WIDTH,
             2 * M_HEADS, D_MODEL, D_MODEL)

VMEM_LIMIT_V7X = 52 * 1024 * 1024
LANE = 128

F32 = jnp.float32
BF16 = jnp.bfloat16


def _split3(x):
    hi = x.astype(BF16)
    r1 = x - hi.astype(F32)
    mid = r1.astype(BF16)
    lo = (r1 - mid.astype(F32)).astype(BF16)
    return hi, mid, lo


def _pick_tile(n, cands):
    for c in cands:
        if n % c == 0:
            return c
    return n


def _mm_kernel(*refs, norm, has_res):
    x_ref, g_ref, w_ref = refs[:3]
    res_ref = refs[3] if has_res else None
    o_ref, xs_ref = refs[-2], refs[-1]

    @pl.when(pl.program_id(1) == 0)
    def _():
        x = x_ref[...]
        if norm:
            x = x * lax.rsqrt(jnp.mean(x * x, axis=-1, keepdims=True) + EPS) * g_ref[...]
        xs_ref[...] = x.astype(BF16)

    acc = jnp.dot(xs_ref[...], w_ref[...].astype(BF16), preferred_element_type=F32)
    if has_res:
        acc = acc + res_ref[...]
    o_ref[...] = acc


def _mm(x, w, g=None, res=None):
    m, k = x.shape
    n = w.shape[1]
    tm = _pick_tile(m, (1024, 512, 256, 128))
    tn = _pick_tile(n, (512, 256, 128))
    norm = g is not None
    gg = (g if norm else jnp.ones((k,), F32)).reshape(1, k)
    in_specs = [pl.BlockSpec((tm, k), lambda i, j: (i, 0)),
                pl.BlockSpec((1, k), lambda i, j: (0, 0)),
                pl.BlockSpec((k, tn), lambda i, j: (0, j))]
    args = [x, gg, w]
    if res is not None:
        in_specs.append(pl.BlockSpec((tm, tn), lambda i, j: (i, j)))
        args.append(res)
    return pl.pallas_call(
        functools.partial(_mm_kernel, norm=norm, has_res=res is not None),
        grid=(m // tm, n // tn),
        in_specs=in_specs,
        out_specs=pl.BlockSpec((tm, tn), lambda i, j: (i, j)),
        out_shape=jax.ShapeDtypeStruct((m, n), F32),
        scratch_shapes=[pltpu.VMEM((tm, k), BF16)],
        compiler_params=pltpu.CompilerParams(dimension_semantics=("arbitrary", "arbitrary"),
                                             vmem_limit_bytes=VMEM_LIMIT_V7X),
        name="mm",
    )(*args)


def _ffn_body(x_ref, g_ref, w1_ref, w3_ref, w2_ref, o_ref, xs_ref, acc_ref, *, residual, grouped):
    j = pl.program_id(1)

    @pl.when(j == 0)
    def _():
        x = x_ref[...]
        xn = x * lax.rsqrt(jnp.mean(x * x, axis=-1, keepdims=True) + EPS) * g_ref[...]
        xs_ref[...] = xn.astype(BF16)
        acc_ref[...] = jnp.zeros_like(acc_ref)

    xs = xs_ref[...]
    w1 = w1_ref[0] if grouped else w1_ref[...]
    w3 = w3_ref[0] if grouped else w3_ref[...]
    w2 = w2_ref[0] if grouped else w2_ref[...]
    a = jnp.dot(xs, w1.astype(BF16), preferred_element_type=F32)
    b = jnp.dot(xs, w3.astype(BF16), preferred_element_type=F32)
    h = (a * jax.nn.sigmoid(a) * b).astype(BF16)
    acc_ref[...] += jnp.dot(h, w2.astype(BF16), preferred_element_type=F32)

    @pl.when(j == pl.num_programs(1) - 1)
    def _():
        if residual:
            o_ref[...] = x_ref[...] + acc_ref[...]
        else:
            o_ref[...] = acc_ref[...]


def _ffn_dense_kernel(x_ref, g_ref, w1_ref, w3_ref, w2_ref, o_ref, xs_ref, acc_ref):
    _ffn_body(x_ref, g_ref, w1_ref, w3_ref, w2_ref, o_ref, xs_ref, acc_ref, residual=True, grouped=False)


def _ffn_grouped_kernel(be_ref, x_ref, g_ref, w1_ref, w3_ref, w2_ref, o_ref, xs_ref, acc_ref):
    del be_ref
    _ffn_body(x_ref, g_ref, w1_ref, w3_ref, w2_ref, o_ref, xs_ref, acc_ref, residual=False, grouped=True)


def _ffn_dense(x, g, w1, w3, w2):
    m, d = x.shape
    f = w1.shape[1]
    tm = _pick_tile(m, (1024, 512, 256, 128))
    tf = _pick_tile(f, (256, 128))
    return pl.pallas_call(
        _ffn_dense_kernel,
        grid=(m // tm, f // tf),
        in_specs=[pl.BlockSpec((tm, d), lambda i, j: (i, 0)),
                  pl.BlockSpec((1, d), lambda i, j: (0, 0)),
                  pl.BlockSpec((d, tf), lambda i, j: (0, j)),
                  pl.BlockSpec((d, tf), lambda i, j: (0, j)),
                  pl.BlockSpec((tf, d), lambda i, j: (j, 0))],
        out_specs=pl.BlockSpec((tm, d), lambda i, j: (i, 0)),
        out_shape=jax.ShapeDtypeStruct((m, d), F32),
        scratch_shapes=[pltpu.VMEM((tm, d), BF16), pltpu.VMEM((tm, d), F32)],
        compiler_params=pltpu.CompilerParams(dimension_semantics=("arbitrary", "arbitrary"),
                                             vmem_limit_bytes=VMEM_LIMIT_V7X),
        name="ffn_dense",
    )(x, g.reshape(1, d), w1, w3, w2)


def _ffn_grouped(xd, blk_e, g, w1, w3, w2, tm):
    rows, d = xd.shape
    f = w1.shape[2]
    tf = _pick_tile(f, (256, 128))
    grid_spec = pltpu.PrefetchScalarGridSpec(
        num_scalar_prefetch=1,
        grid=(rows // tm, f // tf),
        in_specs=[pl.BlockSpec((tm, d), lambda i, j, be: (i, 0)),
                  pl.BlockSpec((1, d), lambda i, j, be: (0, 0)),
                  pl.BlockSpec((1, d, tf), lambda i, j, be: (be[i], 0, j)),
                  pl.BlockSpec((1, d, tf), lambda i, j, be: (be[i], 0, j)),
                  pl.BlockSpec((1, tf, d), lambda i, j, be: (be[i], j, 0))],
        out_specs=pl.BlockSpec((tm, d), lambda i, j, be: (i, 0)),
        scratch_shapes=[pltpu.VMEM((tm, d), BF16), pltpu.VMEM((tm, d), F32)],
    )
    return pl.pallas_call(
        _ffn_grouped_kernel,
        grid_spec=grid_spec,
        out_shape=jax.ShapeDtypeStruct((rows, d), F32),
        compiler_params=pltpu.CompilerParams(dimension_semantics=("arbitrary", "arbitrary"),
                                             vmem_limit_bytes=VMEM_LIMIT_V7X),
        name="ffn_grouped",
    )(blk_e, xd, g.reshape(1, d), w1, w3, w2)


def _moe(x, g, router, w1, w3, w2):
    n, d = x.shape
    tm = 1024 if n >= 8192 else 128
    router_p = jnp.pad(router, ((0, 0), (0, LANE - N_EXPERTS)))
    logits = _mm(x, router_p, g=g)[:, :N_EXPERTS]
    top_val, top_idx = lax.top_k(logits, TOP_K)
    gate = jax.nn.softmax(top_val, axis=-1).reshape(-1)
    e_flat = top_idx.reshape(-1)
    n_asg = n * TOP_K
    order = jnp.argsort(e_flat)
    e_sorted = e_flat[order]
    counts = jnp.bincount(e_flat, length=N_EXPERTS)
    padded = (counts + tm - 1) // tm * tm
    pad_end = jnp.cumsum(padded)
    pad_start = pad_end - padded
    start = jnp.cumsum(counts) - counts
    dest = pad_start[e_sorted] + jnp.arange(n_asg) - start[e_sorted]
    n_blocks = n_asg // tm + N_EXPERTS
    src_tok = jnp.zeros((n_blocks * tm,), jnp.int32).at[dest].set((order // TOP_K).astype(jnp.int32))
    pos = jnp.zeros((n_asg,), jnp.int32).at[order].set(dest.astype(jnp.int32))
    blk_e = jnp.minimum(jnp.searchsorted(pad_end, jnp.arange(n_blocks) * tm, side='right'),
                        N_EXPERTS - 1).astype(jnp.int32)
    yd = _ffn_grouped(x[src_tok], blk_e, g, w1, w3, w2, tm)
    contrib = yd[pos] * gate[:, None]
    return x + contrib.reshape(n, TOP_K, d).sum(axis=1)


N_CHUNK = 128
CHUNK_W = CMP_STRIDE * HEAD_DIM


def _compress_kernel(x_ref, pe_ref, w1_ref, w2_ref, o_ref):
    c = x_ref[0]
    lo = jnp.dot((c + pe_ref[0:1]).astype(BF16), w1_ref[0].astype(BF16), preferred_element_type=F32)
    hi = jnp.dot((c + pe_ref[1:2]).astype(BF16), w1_ref[1].astype(BF16), preferred_element_type=F32)
    hid = jax.nn.gelu(lo + pltpu.roll(hi, N_CHUNK - 1, 0))
    out = jnp.dot(hid.astype(BF16), w2_ref[...].astype(BF16), preferred_element_type=F32)
    row = lax.broadcasted_iota(jnp.int32, out.shape, 0)
    o_ref[0] = jnp.where(row < N_CHUNK - 1, out, 0.0)


def _compress(xc, pe, w1, w2):
    nb = xc.shape[0]
    hidden = w1.shape[-1]
    return pl.pallas_call(
        _compress_kernel,
        grid=(nb,),
        in_specs=[pl.BlockSpec((1, N_CHUNK, CHUNK_W), lambda i: (i, 0, 0)),
                  pl.BlockSpec((2, CHUNK_W), lambda i: (0, 0)),
                  pl.BlockSpec((2, CHUNK_W, hidden), lambda i: (0, 0, 0)),
                  pl.BlockSpec((hidden, HEAD_DIM), lambda i: (0, 0))],
        out_specs=pl.BlockSpec((1, N_CHUNK, HEAD_DIM), lambda i: (i, 0, 0)),
        out_shape=jax.ShapeDtypeStruct((nb, N_CHUNK, HEAD_DIM), F32),
        compiler_params=pltpu.CompilerParams(dimension_semantics=("arbitrary",), vmem_limit_bytes=VMEM_LIMIT_V7X),
        name="compress",
    )(xc, pe.reshape(2, CHUNK_W), w1.reshape(2, CHUNK_W, hidden), w2)


def _compress_heads(kv, pe, w1, w2):
    b = kv.shape[0]
    xc = kv.reshape(b, N_CHUNK, CMP_STRIDE, NSA_KV_HEADS, HEAD_DIM).transpose(0, 3, 1, 2, 4)
    out = _compress(xc.reshape(b * NSA_KV_HEADS, N_CHUNK, CHUNK_W), pe, w1, w2)
    return out.reshape(b, NSA_KV_HEADS, N_CHUNK, HEAD_DIM)


TQ = 128
TK = 128
NEG = -1e30
N_BIAS_TILES = WINDOW // TK + 1


def _nsa_prompt_kernel(q_ref, kct_ref, vcc_ref, kst_ref, vs_ref, kwt_ref, vw_ref, bcmp_ref, btile_ref, gate_ref,
                       ov_ref, exp_ref, o_ref, selneg_ref, m_ref, l_ref, acc_ref):
    i = pl.program_id(2)
    rows = NSA_GROUP * TQ
    q = q_ref[0, 0].reshape(rows, HEAD_DIM)

    s = jnp.dot(q, kct_ref[0, 0], preferred_element_type=F32) + bcmp_ref[0].reshape(rows, N_CHUNK)
    m = jnp.max(s, axis=-1, keepdims=True)
    e = jnp.where(s > 0.1 * NEG, jnp.exp(s - m), 0.0)
    p = e / jnp.maximum(jnp.sum(e, axis=-1, keepdims=True), 1e-30)
    o_cmp = jnp.dot(p.astype(BF16), vcc_ref[0, 0], preferred_element_type=F32)

    p_sum = p[0:TQ] + p[TQ:2 * TQ] + p[2 * TQ:3 * TQ] + p[3 * TQ:4 * TQ]
    imp = sum(jnp.dot(part, ov_ref[...], preferred_element_type=F32) for part in _split3(p_sum))
    lane = lax.broadcasted_iota(jnp.int32, (TQ, LANE), 1)
    tok = lax.broadcasted_iota(jnp.int32, (TQ, LANE), 0) + i * TQ
    cur = tok // SEL_BLOCK
    valid = lane <= cur
    forced = (lane == 0) | (lane == cur) | (lane == cur - 1)
    score = jnp.where(valid, imp + jnp.where(forced, SEL_FORCE, 0.0), -1.0)
    rank = jnp.zeros((TQ, LANE), F32)
    for c in range(T_PROMPT // SEL_BLOCK):
        sc = score[:, c:c + 1]
        beats = (sc > score) | ((sc == score) & (lane > c))
        rank = rank + jnp.where(beats, 1.0, 0.0)
    sel = jnp.where(valid & (rank < SEL_TOP_N), 1.0, 0.0).astype(BF16)
    sel_keys = jnp.dot(sel, exp_ref[...], preferred_element_type=F32)
    selneg_ref[...] = (sel_keys - 1.0) * (-NEG)

    def flash(kt_ref, v_ref, j_lo, selected):
        m_ref[...] = jnp.full(m_ref.shape, -jnp.inf, F32)
        l_ref[...] = jnp.zeros(l_ref.shape, F32)
        acc_ref[...] = jnp.zeros(acc_ref.shape, F32)

        def body(j, carry):
            k0 = pl.multiple_of(j * TK, TK)
            d0 = i - j
            sb = jnp.dot(q, kt_ref[0, 0, :, pl.ds(k0, TK)], preferred_element_type=F32)
            sb = sb.reshape(NSA_GROUP, TQ, TK) + btile_ref[0, jnp.minimum(d0, 2) if selected else d0]
            if selected:
                sb = sb + selneg_ref[:, pl.ds(k0, TK)][None]
            sb = sb.reshape(rows, TK)
            m_old = m_ref[...]
            m_new = jnp.maximum(m_old, jnp.max(sb, axis=-1, keepdims=True))
            pt = jnp.exp(sb - m_new)
            alpha = jnp.exp(m_old - m_new)
            l_ref[...] = alpha * l_ref[...] + jnp.sum(pt, axis=-1, keepdims=True)
            acc_ref[...] = alpha * acc_ref[...] + jnp.dot(pt.astype(BF16), v_ref[0, 0, pl.ds(k0, TK), :],
                                                          preferred_element_type=F32)
            m_ref[...] = m_new
            return carry

        lax.fori_loop(j_lo, i + 1, body, 0)
        return acc_ref[...] / l_ref[...]

    o_sel = flash(kst_ref, vs_ref, 0, True)
    o_win = flash(kwt_ref, vw_ref, jnp.maximum(i - (N_BIAS_TILES - 1), 0), False)

    g = jax.nn.sigmoid(gate_ref[0, 0])
    for a in range(NSA_GROUP):
        r = slice(a * TQ, (a + 1) * TQ)
        o_ref[0, 0, a] = (g[:, 3 * a:3 * a + 1] * o_cmp[r] + g[:, 3 * a + 1:3 * a + 2] * o_sel[r]
                          + g[:, 3 * a + 2:3 * a + 3] * o_win[r])


T_PROMPT = 2048


def _bias_lookup(table, dist):
    oh = jax.nn.one_hot(rel_bucket(dist), REL_BUCKETS, dtype=F32)
    return jnp.einsum('...r,rh->...h', oh, table, precision=lax.Precision.HIGHEST)


def _prompt_bias_tables(table):
    t = T_PROMPT
    q_pos = jnp.arange(t)
    block_end = jnp.arange(N_CHUNK) * CMP_STRIDE + (CMP_BLOCK - 1)
    dist = q_pos[:, None] - block_end[None, :]
    ok = (dist >= 0) & (jnp.arange(N_CHUNK)[None, :] < N_CHUNK - 1)
    bcmp = jnp.where(ok[..., None], _bias_lookup(table, dist), NEG)
    bcmp = bcmp.reshape(t, N_CHUNK, NSA_KV_HEADS, NSA_GROUP).transpose(2, 3, 0, 1)
    d0 = jnp.arange(N_BIAS_TILES)[:, None, None]
    dist = d0 * TK + jnp.arange(TQ)[None, :, None] - jnp.arange(TK)[None, None, :]
    ok = (dist >= 0) & (dist < WINDOW)
    bt = jnp.where(ok[..., None], _bias_lookup(table, dist), NEG)
    bt = bt.reshape(N_BIAS_TILES, TQ, TK, NSA_KV_HEADS, NSA_GROUP).transpose(3, 0, 4, 1, 2)
    c0 = jnp.arange(N_CHUNK) * CMP_STRIDE
    s0 = jnp.arange(LANE) * SEL_BLOCK
    ov = jnp.clip(jnp.minimum(c0[:, None] + CMP_BLOCK, s0[None, :] + SEL_BLOCK)
                  - jnp.maximum(c0[:, None], s0[None, :]), 0, CMP_BLOCK).astype(F32) / CMP_BLOCK
    ov = jnp.where((jnp.arange(N_CHUNK)[:, None] < N_CHUNK - 1) & (jnp.arange(LANE)[None, :] < t // SEL_BLOCK), ov, 0.0)
    expand = (jnp.arange(LANE)[:, None] == (jnp.arange(t) // SEL_BLOCK)[None, :]).astype(BF16)
    return bcmp, bt, ov.astype(BF16), expand


def _nsa_prompt(q_h, kct, vcc, kst, vs, kwt, vw, gates, tables):
    bcmp, bt, ov, expand = tables
    b, _, _, t, _ = q_h.shape
    rows = NSA_GROUP * TQ
    per_bh = lambda *blk: pl.BlockSpec((1, 1) + blk, lambda bi, h, i: (bi, h) + (0,) * len(blk))
    return pl.pallas_call(
        _nsa_prompt_kernel,
        grid=(b, NSA_KV_HEADS, t // TQ),
        in_specs=[pl.BlockSpec((1, 1, NSA_GROUP, TQ, HEAD_DIM), lambda bi, h, i: (bi, h, 0, i, 0)),
                  per_bh(HEAD_DIM, N_CHUNK), per_bh(N_CHUNK, HEAD_DIM),
                  per_bh(HEAD_DIM, t), per_bh(t, HEAD_DIM), per_bh(HEAD_DIM, t), per_bh(t, HEAD_DIM),
                  pl.BlockSpec((1, NSA_GROUP, TQ, N_CHUNK), lambda bi, h, i: (h, 0, i, 0)),
                  pl.BlockSpec((1, N_BIAS_TILES, NSA_GROUP, TQ, TK), lambda bi, h, i: (h, 0, 0, 0, 0)),
                  pl.BlockSpec((1, 1, TQ, 3 * NSA_GROUP), lambda bi, h, i: (bi, h, i, 0)),
                  pl.BlockSpec((N_CHUNK, LANE), lambda bi, h, i: (0, 0)),
                  pl.BlockSpec((LANE, t), lambda bi, h, i: (0, 0))],
        out_specs=pl.BlockSpec((1, 1, NSA_GROUP, TQ, HEAD_DIM), lambda bi, h, i: (bi, h, 0, i, 0)),
        out_shape=jax.ShapeDtypeStruct((b, NSA_KV_HEADS, NSA_GROUP, t, HEAD_DIM), F32),
        scratch_shapes=[pltpu.VMEM((TQ, t), F32), pltpu.VMEM((rows, 1), F32), pltpu.VMEM((rows, 1), F32),
                        pltpu.VMEM((rows, HEAD_DIM), F32)],
        compiler_params=pltpu.CompilerParams(dimension_semantics=("arbitrary", "arbitrary", "arbitrary"),
                                             vmem_limit_bytes=VMEM_LIMIT_V7X),
        name="nsa_prompt",
    )(q_h, kct, vcc, kst, vs, kwt, vw, bcmp, bt, gates, ov, expand)


def _heads_t(x, b, t):
    return x.astype(BF16).reshape(b, t, NSA_KV_HEADS, HEAD_DIM).transpose(0, 2, 3, 1)


def _heads(x, b, t):
    return x.astype(BF16).reshape(b, t, NSA_KV_HEADS, HEAD_DIM).transpose(0, 2, 1, 3)


def _nsa_prompt_from_proj(z_a, z_s, b, t, tables, pe, w1, w2):
    col = lambda i: z_a[:, NSA_WIDTH + KV_WIDTH * i:NSA_WIDTH + KV_WIDTH * (i + 1)]
    q_h = (z_a[:, :NSA_WIDTH] * (HEAD_DIM ** -0.5)).astype(BF16)
    q_h = q_h.reshape(b, t, NSA_KV_HEADS, NSA_GROUP, HEAD_DIM).transpose(0, 2, 3, 1, 4)
    kcc = _compress_heads(col(0).reshape(b, t, NSA_KV_HEADS, HEAD_DIM), pe[0], w1[0], w2[0])
    vcc = _compress_heads(col(1).reshape(b, t, NSA_KV_HEADS, HEAD_DIM), pe[1], w1[1], w2[1])
    gates = z_s[:, :3 * NSA_HEADS].reshape(b, t, NSA_KV_HEADS, 3 * NSA_GROUP).transpose(0, 2, 1, 3)
    o = _nsa_prompt(q_h, kcc.astype(BF16).transpose(0, 1, 3, 2), vcc.astype(BF16),
                    _heads_t(col(2), b, t), _heads(col(3), b, t), _heads_t(col(4), b, t), _heads(col(5), b, t),
                    gates, tables)
    return o.transpose(0, 3, 1, 2, 4).reshape(b * t, NSA_WIDTH)


T_SAMPLE = 4
T_PAD = 8
PAGE = 128
N_PAGES = 16
PAST = N_PAGES * PAGE
N_SEL_SAMPLE = -(-(PAST + T_SAMPLE) // SEL_BLOCK)


def _nsa_sample_kernel(pt_ref, *refs):
    del pt_ref
    cp, sp = refs[:N_PAGES], refs[N_PAGES:2 * N_PAGES]
    (wb_ref, za_ref, zs_ref, pe2_ref, w1_ref, w2_ref, bcmp_ref, bpast_ref, bwin_ref, bnew_ref, ov_ref, exp_ref,
     o_ref, new_ref, q_ref, g_ref, acc_ref, kcat_ref, vcat_ref, s_ref, e_ref, oacc_ref) = refs[2 * N_PAGES:]
    rows = NSA_GROUP * T_PAD

    @pl.when(pl.program_id(0) == 0)
    def _():
        new_ref[...] = jnp.zeros(new_ref.shape, F32)
        q_ref[...] = jnp.zeros(q_ref.shape, F32)
        g_ref[...] = jnp.zeros(g_ref.shape, F32)

    new_ref[0:T_SAMPLE, :] = za_ref[0, :, NSA_WIDTH + 2 * KV_WIDTH:]
    q_ref[0:T_SAMPLE, :] = za_ref[0, :, :NSA_WIDTH] * (HEAD_DIM ** -0.5)
    g_ref[0:T_SAMPLE, :] = zs_ref[0]

    comp = {}
    for kv in range(2):
        acc_ref[...] = jnp.zeros(acc_ref.shape, F32)

        def add_row_offset(l, carry, kv=kv):
            xl = jnp.concatenate([cp[p][0, pl.ds(2 * l + kv, PAGE // CMP_STRIDE, stride=2 * CMP_STRIDE), :]
                                  for p in range(N_PAGES)], axis=0)
            for half in range(2):
                lw = l + CMP_STRIDE * half
                xb = (xl + pe2_ref[kv, pl.ds(lw, 1), :]).astype(BF16)
                xs = jnp.concatenate([xb[:, :HEAD_DIM], xb[:, HEAD_DIM:]], axis=0)
                acc_ref[half] += jnp.dot(xs, w1_ref[kv, lw], preferred_element_type=F32)
            return carry

        lax.fori_loop(0, CMP_STRIDE, add_row_offset, 0)
        for h in range(NSA_KV_HEADS):
            r = slice(h * N_CHUNK, (h + 1) * N_CHUNK)
            hid = jax.nn.gelu(acc_ref[0, r, :] + pltpu.roll(acc_ref[1, r, :], N_CHUNK - 1, 0))
            out = jnp.dot(hid.astype(BF16), w2_ref[kv], preferred_element_type=F32)
            row = lax.broadcasted_iota(jnp.int32, out.shape, 0)
            comp[kv, h] = jnp.where(row < N_CHUNK - 1, out, 0.0).astype(BF16)

    w_buf = wb_ref.shape[1] // 2
    for p in range(N_PAGES):
        kcat_ref[p * PAGE:(p + 1) * PAGE, :] = sp[p][0, pl.ds(0, PAGE, stride=2), :].astype(BF16)
        vcat_ref[p * PAGE:(p + 1) * PAGE, :] = sp[p][0, pl.ds(1, PAGE, stride=2), :].astype(BF16)
    kcat_ref[PAST:, :] = new_ref[:, 0:KV_WIDTH].astype(BF16)
    vcat_ref[PAST:, :] = new_ref[:, KV_WIDTH:2 * KV_WIDTH].astype(BF16)
    kw_buf = wb_ref[0, pl.ds(0, w_buf, stride=2), :].astype(BF16)
    vw_buf = wb_ref[0, pl.ds(1, w_buf, stride=2), :].astype(BF16)
    kw_new, vw_new = (new_ref[:, i * KV_WIDTH:(i + 1) * KV_WIDTH].astype(BF16) for i in (2, 3))
    gates = jax.nn.sigmoid(g_ref[...])
    nt = (((1,), (1,)), ((), ()))
    zeros64 = jnp.zeros((rows, HEAD_DIM), F32)
    pieces = []
    key_chunk = 512

    def attend_window(qp, h):
        s_parts = [lax.dot_general(qp, k, nt, preferred_element_type=F32) + bias
                   for k, bias in ((kw_buf, bwin_ref[h]), (kw_new, bnew_ref[h]))]
        m = jnp.maximum(*[jnp.max(s, axis=-1, keepdims=True) for s in s_parts])
        e_parts = [jnp.exp(s - m) for s in s_parts]
        den = jnp.add(*[jnp.sum(e, axis=-1, keepdims=True) for e in e_parts])
        num = jnp.add(*[jnp.dot(e.astype(BF16), v, preferred_element_type=F32)
                        for e, v in zip(e_parts, (vw_buf, vw_new))])
        return (num / den)[:, h * HEAD_DIM:(h + 1) * HEAD_DIM]

    def attend_selected(qp, h):
        def scores(c, carry):
            ds = pl.ds(pl.multiple_of(c * key_chunk, key_chunk), key_chunk)
            s_ref[:, ds] += lax.dot_general(qp, kcat_ref[ds, :], nt, preferred_element_type=F32)
            return carry

        lax.fori_loop(0, PAST // key_chunk, scores, 0)
        s_ref[:, PAST:] += lax.dot_general(qp, kcat_ref[PAST:, :], nt, preferred_element_type=F32)
        s = s_ref[...]
        e = jnp.exp(s - jnp.max(s, axis=-1, keepdims=True))
        den = jnp.sum(e, axis=-1, keepdims=True)
        e_ref[...] = e.astype(BF16)
        oacc_ref[...] = jnp.dot(e_ref[:, PAST:], vcat_ref[PAST:, :], preferred_element_type=F32)

        def weighted(c, carry):
            ds = pl.ds(pl.multiple_of(c * key_chunk, key_chunk), key_chunk)
            oacc_ref[...] += jnp.dot(e_ref[:, ds], vcat_ref[ds, :], preferred_element_type=F32)
            return carry

        lax.fori_loop(0, PAST // key_chunk, weighted, 0)
        return (oacc_ref[...] / den)[:, h * HEAD_DIM:(h + 1) * HEAD_DIM]

    for h in range(NSA_KV_HEADS):
        q64 = jnp.concatenate([q_ref[:, (h * NSA_GROUP + a) * HEAD_DIM:(h * NSA_GROUP + a + 1) * HEAD_DIM]
                               for a in range(NSA_GROUP)], axis=0)
        qp = jnp.concatenate([q64, zeros64] if h == 0 else [zeros64, q64], axis=1).astype(BF16)
        q64 = q64.astype(BF16)

        s = lax.dot_general(q64, comp[0, h], nt, preferred_element_type=F32) + bcmp_ref[h]
        m = jnp.max(s, axis=-1, keepdims=True)
        e = jnp.where(s > 0.1 * NEG, jnp.exp(s - m), 0.0)
        p = e / jnp.maximum(jnp.sum(e, axis=-1, keepdims=True), 1e-30)
        o_cmp = jnp.dot(p.astype(BF16), comp[1, h], preferred_element_type=F32)

        p_sum = p[0:T_PAD] + p[T_PAD:2 * T_PAD] + p[2 * T_PAD:3 * T_PAD] + p[3 * T_PAD:4 * T_PAD]
        imp = sum(jnp.dot(part, ov_ref[...], preferred_element_type=F32) for part in _split3(p_sum))
        lane = lax.broadcasted_iota(jnp.int32, (T_PAD, LANE), 1)
        cur = (PAST + jnp.minimum(lax.broadcasted_iota(jnp.int32, (T_PAD, LANE), 0), T_SAMPLE - 1)) // SEL_BLOCK
        valid = lane <= cur
        forced = (lane == 0) | (lane == cur) | (lane == cur - 1)
        score = jnp.where(valid, imp + jnp.where(forced, SEL_FORCE, 0.0), -1.0)
        rank = jnp.zeros((T_PAD, LANE), F32)
        for c in range(N_SEL_SAMPLE):
            sc = score[:, c:c + 1]
            beats = (sc > score) | ((sc == score) & (lane > c))
            rank = rank + jnp.where(beats, 1.0, 0.0)
        sel = jnp.where(valid & (rank < SEL_TOP_N), 1.0, 0.0)
        selneg = (jnp.dot(sel.astype(BF16), exp_ref[...], preferred_element_type=F32) - 1.0) * (-NEG)
        selneg_new = (sel[:, PAST // SEL_BLOCK:PAST // SEL_BLOCK + 1] - 1.0) * (-NEG)

        def per_token(bias, tok):
            n = bias.shape[-1]
            return (bias.reshape(NSA_GROUP, T_PAD, n) + tok[None]).reshape(rows, n)

        s_ref[:, :PAST] = per_token(bpast_ref[h], selneg)
        s_ref[:, PAST:] = per_token(bnew_ref[h], jnp.broadcast_to(selneg_new, (T_PAD, LANE)))
        o_sel = attend_selected(qp, h)
        o_win = attend_window(qp, h)

        for a in range(NSA_GROUP):
            r = slice(a * T_PAD, (a + 1) * T_PAD)
            c0 = (h * NSA_GROUP + a) * 3
            pieces.append(gates[:, c0:c0 + 1] * o_cmp[r] + gates[:, c0 + 1:c0 + 2] * o_sel[r]
                          + gates[:, c0 + 2:c0 + 3] * o_win[r])
    o_ref[0] = jnp.concatenate(pieces, axis=1)[0:T_SAMPLE]


def _sample_bias_tables(table):
    tq = jnp.minimum(jnp.arange(T_PAD), T_SAMPLE - 1)
    q_pos = PAST + tq

    def lay(x):
        n = x.shape[1]
        return x.reshape(T_PAD, n, NSA_KV_HEADS, NSA_GROUP).transpose(2, 3, 0, 1).reshape(NSA_KV_HEADS, -1, n)

    block_end = jnp.arange(N_CHUNK) * CMP_STRIDE + (CMP_BLOCK - 1)
    dist = q_pos[:, None] - block_end[None, :]
    ok = (dist >= 0) & (jnp.arange(N_CHUNK)[None, :] < N_CHUNK - 1)
    bcmp = lay(jnp.where(ok[..., None], _bias_lookup(table, dist), NEG))
    dist = q_pos[:, None] - jnp.arange(PAST)[None, :]
    past = _bias_lookup(table, dist)
    bpast = lay(past)
    w_buf = min(WINDOW, PAST)
    bwin = lay(jnp.where((dist < WINDOW)[:, PAST - w_buf:, None], past[:, PAST - w_buf:], NEG))
    j = jnp.arange(LANE)
    dist = tq[:, None] - j[None, :]
    ok = (dist >= 0) & (j[None, :] < T_SAMPLE)
    bnew = lay(jnp.where(ok[..., None], _bias_lookup(table, dist), NEG))
    c0 = jnp.arange(N_CHUNK) * CMP_STRIDE
    s0 = jnp.arange(LANE) * SEL_BLOCK
    ov = jnp.clip(jnp.minimum(c0[:, None] + CMP_BLOCK, s0[None, :] + SEL_BLOCK)
                  - jnp.maximum(c0[:, None], s0[None, :]), 0, CMP_BLOCK).astype(F32) / CMP_BLOCK
    ov = jnp.where((jnp.arange(N_CHUNK)[:, None] < N_CHUNK - 1) & (j[None, :] < N_SEL_SAMPLE), ov, 0.0)
    expand = (j[:, None] == (jnp.arange(PAST) // SEL_BLOCK)[None, :]).astype(BF16)
    return bcmp, bpast, bwin, bnew, ov.astype(BF16), expand


def _nsa_sample(za, zs, cache_cmp, cache_slc, cache_win, page_table, layer, tables, pe, w1, w2):
    b = za.shape[0]
    n_phys = cache_cmp.shape[1]
    cmp_pages = cache_cmp.reshape(DEPTH * n_phys, 2 * PAGE, KV_WIDTH)
    slc_pages = cache_slc.reshape(DEPTH * n_phys, 2 * PAGE, KV_WIDTH)
    w_buf = cache_win.shape[2]
    win = cache_win.reshape(DEPTH * b, 2 * w_buf, KV_WIDTH)
    pt = page_table.reshape(-1).astype(jnp.int32) + layer * n_phys
    bcmp, bpast, bwin, bnew, ov, expand = tables
    pe2 = jnp.concatenate([pe, pe], axis=-1)
    page_spec = lambda p: pl.BlockSpec((1, 2 * PAGE, KV_WIDTH), lambda bi, pt_: (pt_[bi * N_PAGES + p], 0, 0))
    whole = lambda x: pl.BlockSpec(x.shape, lambda bi, pt_: (0,) * x.ndim)
    consts = [pe2, w1.astype(BF16), w2.astype(BF16), bcmp, bpast, bwin, bnew, ov, expand]
    grid_spec = pltpu.PrefetchScalarGridSpec(
        num_scalar_prefetch=1,
        grid=(b,),
        in_specs=[page_spec(p) for p in range(N_PAGES)] * 2
        + [pl.BlockSpec((1, 2 * w_buf, KV_WIDTH), lambda bi, pt_: (layer * b + bi, 0, 0)),
           pl.BlockSpec((1, T_SAMPLE, za.shape[-1]), lambda bi, pt_: (bi, 0, 0)),
           pl.BlockSpec((1, T_SAMPLE, LANE), lambda bi, pt_: (bi, 0, 0))]
        + [whole(x) for x in consts],
        out_specs=pl.BlockSpec((1, T_SAMPLE, NSA_WIDTH), lambda bi, pt_: (bi, 0, 0)),
        scratch_shapes=[pltpu.VMEM((PAGE, 4 * KV_WIDTH), F32), pltpu.VMEM((T_PAD, NSA_WIDTH), F32),
                        pltpu.VMEM((T_PAD, LANE), F32), pltpu.VMEM((2, 2 * N_CHUNK, CMP_HIDDEN), F32),
                        pltpu.VMEM((PAST + PAGE, KV_WIDTH), BF16), pltpu.VMEM((PAST + PAGE, KV_WIDTH), BF16),
                        pltpu.VMEM((NSA_GROUP * T_PAD, PAST + PAGE), F32),
                        pltpu.VMEM((NSA_GROUP * T_PAD, PAST + PAGE), BF16),
                        pltpu.VMEM((NSA_GROUP * T_PAD, KV_WIDTH), F32)],
    )
    return pl.pallas_call(
        _nsa_sample_kernel,
        grid_spec=grid_spec,
        out_shape=jax.ShapeDtypeStruct((b, T_SAMPLE, NSA_WIDTH), F32),
        compiler_params=pltpu.CompilerParams(dimension_semantics=("arbitrary",), vmem_limit_bytes=VMEM_LIMIT_V7X),
        name="nsa_sample",
    )(pt, *([cmp_pages] * N_PAGES), *([slc_pages] * N_PAGES), win, za, zs, *consts)


I_LANE = 3 * NSA_HEADS
F_LANE = I_LANE + M_HEADS
EXT_PAD = 8


def _mlstm_kernel(zu_ref, zs_ref, c0_ref, n0_ref, m0_ref, cprev_ref, convw_ref, convb_ref, wq_ref, wk_ref, wkt_ref,
                  wv_ref, gb_ref, normg_ref, tril_ref, selrow_ref,
                  o_ref, c_out, n_out, m_out, conv_out,
                  ext_ref, zsp_ref, c_sc, n_sc, m_sc, *, rows, rows_pad, n_valid):
    i = pl.program_id(1)
    L = M_CHUNK

    @pl.when(i == 0)
    def _():
        c_sc[...] = c0_ref[0]
        n_sc[...] = n0_ref[0]
        m_sc[...] = m0_ref[0]
        ext_ref[EXT_PAD - (CONV_W - 1):EXT_PAD, :] = cprev_ref[0]

    if rows < rows_pad:
        ext_ref[EXT_PAD:, :] = jnp.zeros((rows_pad, M_WIDTH), F32)
        zsp_ref[...] = jnp.zeros(zsp_ref.shape, F32)
    ext_ref[EXT_PAD:EXT_PAD + rows, :] = zu_ref[0, :, :M_WIDTH]
    zsp_ref[0:rows, :] = zs_ref[0]

    u = ext_ref[EXT_PAD:EXT_PAD + rows_pad, :]
    conv = convb_ref[...] + convw_ref[CONV_W - 1:CONV_W, :] * u
    for j in range(CONV_W - 1):
        lo = EXT_PAD - (CONV_W - 1) + j
        conv = conv + convw_ref[j:j + 1, :] * ext_ref[lo:lo + rows_pad, :]
    uc = (conv * jax.nn.sigmoid(conv)).astype(BF16)
    ub = u.astype(BF16)

    zb = zsp_ref[...] + gb_ref[...]
    bcum = sum(jnp.dot(tril_ref[...], part, preferred_element_type=F32) for part in _split3(jax.nn.log_sigmoid(zb)))
    bcum = pltpu.roll(bcum, LANE - M_HEADS, 1)
    a_col = zb - bcum
    a_row = sum(lax.dot_general(selrow_ref[...], part, (((1,), (1,)), ((), ())), preferred_element_type=F32)
                for part in _split3(a_col))

    tt = lax.broadcasted_iota(jnp.int32, (L, L), 0)
    ss = lax.broadcasted_iota(jnp.int32, (L, L), 1)
    causal = ss <= tt
    tok_col = lax.broadcasted_iota(jnp.int32, (L, 1), 0)
    tok_row = lax.broadcasted_iota(jnp.int32, (1, L), 1)

    for h in range(M_HEADS):
        hs = slice(h * M_HEAD_DIM, (h + 1) * M_HEAD_DIM)
        q_all = jnp.dot(uc[:, hs], wq_ref[h].astype(BF16), preferred_element_type=F32) * (M_HEAD_DIM ** -0.5)
        k_all = jnp.dot(uc[:, hs], wk_ref[h].astype(BF16), preferred_element_type=F32)
        kt_all = lax.dot_general(wkt_ref[h].astype(BF16), uc[:, hs], (((1,), (1,)), ((), ())),
                                 preferred_element_type=F32)
        v_all = jnp.dot(ub[:, hs], wv_ref[h].astype(BF16), preferred_element_type=F32)
        c_st = c_sc[h]
        n_st = n_sc[h:h + 1, :]
        m_st = m_sc[h:h + 1, 0:1]
        for c in range(rows_pad // L):
            r = slice(c * L, (c + 1) * L)
            q, k, kt, v = q_all[r].astype(BF16), k_all[r], kt_all[:, r], v_all[r].astype(BF16)
            b_col = bcum[r, I_LANE + h:I_LANE + h + 1]
            a_c = a_col[r, I_LANE + h:I_LANE + h + 1]
            a_r = a_row[h:h + 1, r]
            log_d = jnp.where(causal, b_col + a_r, NEG)
            m_col = jnp.maximum(b_col + m_st, jnp.max(log_d, axis=-1, keepdims=True))
            dw = jnp.exp(log_d - m_col)
            w_inter = jnp.exp(b_col + m_st - m_col)
            s = lax.dot_general(q, k.astype(BF16), (((1,), (1,)), ((), ())), preferred_element_type=F32) * dw
            num = (jnp.dot(s.astype(BF16), v, preferred_element_type=F32)
                   + w_inter * jnp.dot(q, c_st.astype(BF16), preferred_element_type=F32))
            den = (jnp.sum(s, axis=-1, keepdims=True)
                   + w_inter * jnp.sum(q_all[r] * n_st, axis=-1, keepdims=True))
            hh = num / jnp.maximum(jnp.abs(den), jnp.exp(-m_col))
            hn = hh * lax.rsqrt(jnp.mean(hh * hh, axis=-1, keepdims=True) + EPS)
            o_pre = zu_ref[0, :, M_WIDTH + h * M_HEAD_DIM:M_WIDTH + (h + 1) * M_HEAD_DIM]
            if rows < rows_pad:
                o_ref[0, :, hs] = jax.nn.sigmoid(o_pre) * hn[0:rows] * normg_ref[:, hs]
            else:
                o_ref[0, r, hs] = jax.nn.sigmoid(o_pre[r]) * hn * normg_ref[:, hs]
            b_last = b_col[n_valid - 1:n_valid, :]
            log_s = jnp.where(tok_col < n_valid, b_last + a_c, NEG)
            m_new = jnp.maximum(b_last + m_st, jnp.max(log_s, axis=0, keepdims=True))
            ws_col = jnp.exp(log_s - m_new)
            ws_row = jnp.where(tok_row < n_valid, jnp.exp(b_last + a_r - m_new), 0.0)
            wc = jnp.exp(b_last + m_st - m_new)
            c_st = wc * c_st + jnp.dot((kt * ws_row).astype(BF16), v, preferred_element_type=F32)
            n_st = wc * n_st + jnp.sum(k * ws_col, axis=0, keepdims=True)
            m_st = m_new
        c_sc[h] = c_st
        n_sc[h:h + 1, :] = n_st
        m_sc[h:h + 1, :] = jnp.broadcast_to(m_st, (1, LANE))

    tail = ext_ref[EXT_PAD + rows - (CONV_W - 1):EXT_PAD + rows, :]
    ext_ref[EXT_PAD - (CONV_W - 1):EXT_PAD, :] = tail

    @pl.when(i == pl.num_programs(1) - 1)
    def _():
        c_out[0] = c_sc[...]
        n_out[0] = n_sc[...]
        m_out[0] = m_sc[...]
        conv_out[0] = jnp.zeros(conv_out.shape[1:], F32)
        conv_out[0, 0:CONV_W - 1, :] = tail


def _mlstm(zu, zs, state, conv_prev, conv_w, conv_b, m_qkv, gate_b, norm_g):
    b, t, _ = zu.shape
    rows = min(t, 4 * M_CHUNK)
    rows_pad = -(-rows // M_CHUNK) * M_CHUNK
    n_valid = M_CHUNK if rows == rows_pad else rows
    assert t % rows == 0 and (rows == rows_pad or t == rows)
    c0, n0, m0 = state
    n0p = jnp.pad(n0, ((0, 0), (0, 8 - M_HEADS), (0, 0)))
    m0p = jnp.pad(jnp.broadcast_to(m0[:, :, None], (b, M_HEADS, LANE)), ((0, 0), (0, 8 - M_HEADS), (0, 0)))
    gb = jnp.zeros((1, LANE), F32).at[0, I_LANE:I_LANE + 2 * M_HEADS].set(gate_b.reshape(-1))
    idx = jnp.arange(rows_pad)
    tril = ((idx[:, None] >= idx[None, :]) & (idx[:, None] // M_CHUNK == idx[None, :] // M_CHUNK)).astype(BF16)
    selrow = ((jnp.arange(16)[:, None] + I_LANE == jnp.arange(LANE)[None, :])
              & (jnp.arange(16)[:, None] < M_HEADS)).astype(BF16)
    whole = lambda *shape: pl.BlockSpec(shape, lambda bi, i: (0,) * len(shape))
    per_b = lambda *shape: pl.BlockSpec((1,) + shape, lambda bi, i: (bi,) + (0,) * len(shape))
    out, c_new, n_new, m_new, conv_new = pl.pallas_call(
        functools.partial(_mlstm_kernel, rows=rows, rows_pad=rows_pad, n_valid=n_valid),
        grid=(b, t // rows),
        in_specs=[pl.BlockSpec((1, rows, 2 * M_WIDTH), lambda bi, i: (bi, i, 0)),
                  pl.BlockSpec((1, rows, LANE), lambda bi, i: (bi, i, 0)),
                  per_b(M_HEADS, M_HEAD_DIM, M_HEAD_DIM), per_b(8, M_HEAD_DIM), per_b(8, LANE),
                  per_b(CONV_W - 1, M_WIDTH),
                  whole(CONV_W, M_WIDTH), whole(1, M_WIDTH),
                  whole(M_HEADS, M_HEAD_DIM, M_HEAD_DIM), whole(M_HEADS, M_HEAD_DIM, M_HEAD_DIM),
                  whole(M_HEADS, M_HEAD_DIM, M_HEAD_DIM), whole(M_HEADS, M_HEAD_DIM, M_HEAD_DIM),
                  whole(1, LANE), whole(1, M_WIDTH), whole(rows_pad, rows_pad), whole(16, LANE)],
        out_specs=[pl.BlockSpec((1, rows, M_WIDTH), lambda bi, i: (bi, i, 0)),
                   per_b(M_HEADS, M_HEAD_DIM, M_HEAD_DIM), per_b(8, M_HEAD_DIM), per_b(8, LANE),
                   per_b(8, M_WIDTH)],
        out_shape=[jax.ShapeDtypeStruct((b, t, M_WIDTH), F32),
                   jax.ShapeDtypeStruct((b, M_HEADS, M_HEAD_DIM, M_HEAD_DIM), F32),
                   jax.ShapeDtypeStruct((b, 8, M_HEAD_DIM), F32),
                   jax.ShapeDtypeStruct((b, 8, LANE), F32),
                   jax.ShapeDtypeStruct((b, 8, M_WIDTH), F32)],
        scratch_shapes=[pltpu.VMEM((EXT_PAD + rows_pad, M_WIDTH), F32), pltpu.VMEM((rows_pad, LANE), F32),
                        pltpu.VMEM((M_HEADS, M_HEAD_DIM, M_HEAD_DIM), F32), pltpu.VMEM((8, M_HEAD_DIM), F32),
                        pltpu.VMEM((8, LANE), F32)],
        compiler_params=pltpu.CompilerParams(dimension_semantics=("arbitrary", "arbitrary"),
                                             vmem_limit_bytes=VMEM_LIMIT_V7X),
        name="mlstm",
    )(zu, zs, c0, n0p, m0p, conv_prev, conv_w, conv_b.reshape(1, M_WIDTH), m_qkv[0], m_qkv[1],
      m_qkv[1].transpose(0, 2, 1),
      m_qkv[2], gb, norm_g.reshape(1, M_WIDTH), tril, selrow)
    return out, (c_new, n_new[:, :M_HEADS], m_new[:, :M_HEADS, 0], conv_new[:, :CONV_W - 1])


def rel_bucket(dist):
    d = jnp.maximum(dist, 0)
    exact = REL_BUCKETS // 2
    log_part = exact + (jnp.log(jnp.maximum(d, 1).astype(F32) / exact)
                        / math.log(REL_MAX_DIST / exact) * (REL_BUCKETS - exact)).astype(jnp.int32)
    return jnp.where(d < exact, d, jnp.minimum(log_part, REL_BUCKETS - 1))


def _split_w_in(w_in):
    offs = np.cumsum((0,) + IN_WIDTHS)
    w_a = w_in[:, :offs[7]]
    w_small = jnp.concatenate([w_in[:, offs[7]:offs[8]], w_in[:, offs[10]:offs[11]]], axis=1)
    w_small = jnp.pad(w_small, ((0, 0), (0, LANE - w_small.shape[1])))
    w_u = w_in[:, offs[8]:offs[10]]
    w_g = w_in[:, offs[11]:]
    return w_a, w_small, w_u, w_g


def mixer(x, norm_g, prm, past):
    b, t, _ = x.shape
    x2 = x.reshape(b * t, D_MODEL)
    w_a, w_small, w_u, w_g = _split_w_in(prm['w_in'])
    z_a = _mm(x2, w_a, g=norm_g).reshape(b, t, -1)
    z_s = _mm(x2, w_small, g=norm_g).reshape(b, t, -1)
    z_u = _mm(x2, w_u, g=norm_g).reshape(b, t, -1)
    z_g = _mm(x2, w_g, g=norm_g)
    new_cmp, new_slc, win_rows = (z_a[..., NSA_WIDTH + 2 * KV_WIDTH * i:NSA_WIDTH + 2 * KV_WIDTH * (i + 1)]
                                  .reshape(b, t, 2, NSA_KV_HEADS, HEAD_DIM) for i in range(3))
    g_a, g_b = z_g[:, :D_MODEL], z_g[:, D_MODEL:]
    if past is None:
        o_nsa = _nsa_prompt_from_proj(z_a.reshape(b * t, -1), z_s.reshape(b * t, -1), b, t, prm['prompt_tables'],
                                      prm['cmp_pe'], prm['cmp_w1'], prm['cmp_w2'])
        new_win = win_rows[:, t - min(WINDOW, t):]
        conv_prev = jnp.zeros((b, CONV_W - 1, M_WIDTH), F32)
        m_state = (jnp.zeros((b, M_HEADS, M_HEAD_DIM, M_HEAD_DIM), F32),
                   jnp.zeros((b, M_HEADS, M_HEAD_DIM), F32),
                   jnp.zeros((b, M_HEADS), F32))
    else:
        layer = past['layer']
        assert t == T_SAMPLE and past['page_table'].shape[1] == N_PAGES and past['cmp'].shape[2] == PAGE
        o_nsa = _nsa_sample(z_a, z_s, past['cmp'], past['slc'], past['win'], past['page_table'], layer,
                            prm['sample_tables'], prm['cmp_pe'], prm['cmp_w1'], prm['cmp_w2'])
        o_nsa = o_nsa.reshape(b * t, NSA_WIDTH)
        new_win = jnp.concatenate([past['win'][layer][:, t:], win_rows], axis=1)
        conv_prev = past['conv']
        m_state = (past['C'].astype(F32), past['n'].astype(F32), past['m'].astype(F32))
    o_m, (c_new, n_new, m_new, conv_new) = _mlstm(z_u, z_s, m_state, conv_prev, prm['conv_w'], prm['conv_b'],
                                                  prm['m_qkv'], prm['gate_b'], prm['m_norm'])
    merged = (jax.nn.sigmoid(g_a) * _mm(o_nsa, prm['w_up_a'])
              + jax.nn.sigmoid(g_b) * _mm(o_m.reshape(b * t, M_WIDTH), prm['w_up_b']))
    y = _mm(merged, prm['w_out'], res=x2).reshape(b, t, D_MODEL)
    return y, (new_cmp, new_slc, new_win, c_new, n_new, m_new, conv_new)


def _channel_mixer(x, g, l, ffn_w1, ffn_w3, ffn_w2, moe_router, moe_w1, moe_w3, moe_w2):
    b, t, d = x.shape
    x2 = x.reshape(b * t, d)
    i = l // 2
    if l % 2 == 0:
        y = _ffn_dense(x2, g, ffn_w1[i], ffn_w3[i], ffn_w2[i])
    else:
        y = _moe(x2, g, moe_router[i], moe_w1[i], moe_w3[i], moe_w2[i])
    return y.reshape(b, t, d)


def _final_norm(x, g):
    xf = x.astype(F32)
    return xf * lax.rsqrt(jnp.mean(xf * xf, axis=-1, keepdims=True) + EPS) * g


def kernel(x_prompt, x_sample, cache_cmp_kv, cache_slc_kv, cache_win_kv, state_mlstm_C, state_mlstm_n,
           state_mlstm_m, state_mlstm_conv, page_table, rel_bias_table, norm_mix, norm_ffn, norm_final,
           w_in, cmp_pe, cmp_w1, cmp_w2, m_conv_w, m_conv_b, m_qkv, m_gate_bias, m_norm, w_up_a, w_up_b,
           w_out, ffn_w1, ffn_w3, ffn_w2, moe_router, moe_w1, moe_w3, moe_w2):
    xp, xs = x_prompt, x_sample
    prompt_states, sample_states = [], []
    prompt_tables = _prompt_bias_tables(rel_bias_table)
    sample_tables = _sample_bias_tables(rel_bias_table)
    for l in range(DEPTH):
        prm = {'prompt_tables': prompt_tables, 'sample_tables': sample_tables, 'w_in': w_in[l], 'cmp_pe': cmp_pe[l],
               'cmp_w1': cmp_w1[l], 'cmp_w2': cmp_w2[l], 'conv_w': m_conv_w[l], 'conv_b': m_conv_b[l],
               'm_qkv': m_qkv[l], 'gate_b': m_gate_bias[l], 'm_norm': m_norm[l], 'w_up_a': w_up_a[l],
               'w_up_b': w_up_b[l], 'w_out': w_out[l]}
        past = {'cmp': cache_cmp_kv, 'slc': cache_slc_kv, 'win': cache_win_kv, 'layer': l,
                'C': state_mlstm_C[l], 'n': state_mlstm_n[l], 'm': state_mlstm_m[l],
                'conv': state_mlstm_conv[l], 'page_table': page_table}
        xp, st_p = mixer(xp, norm_mix[l], prm, None)
        xs, st_s = mixer(xs, norm_mix[l], prm, past)
        xp = _channel_mixer(xp, norm_ffn[l], l, ffn_w1, ffn_w3, ffn_w2, moe_router, moe_w1, moe_w3, moe_w2)
        xs = _channel_mixer(xs, norm_ffn[l], l, ffn_w1, ffn_w3, ffn_w2, moe_router, moe_w1, moe_w3, moe_w2)
        prompt_states.append(st_p)
        sample_states.append(st_s)
    y_prompt = _final_norm(xp, norm_final)
    y_sample = _final_norm(xs, norm_final)
    ps = [jnp.stack([s[i] for s in prompt_states]) for i in range(7)]
    ss = [jnp.stack([s[i] for s in sample_states]) for i in range(7)]
    return (y_prompt, y_sample, ps[0], ps[1], ps[2], ps[3], ps[4], ps[5], ps[6],
            ss[0], ss[1], ss[2], ss[3], ss[4], ss[5], ss[6])
```

```python
import functools
import math

import jax
import jax.numpy as jnp
import numpy as np
from jax import lax
from jax.experimental import pallas as pl
from jax.experimental.pallas import tpu as pltpu

D_MODEL = 1024
DEPTH = 2
NSA_HEADS = 8
NSA_KV_HEADS = 2
NSA_GROUP = NSA_HEADS // NSA_KV_HEADS
HEAD_DIM = 64
NSA_WIDTH = NSA_HEADS * HEAD_DIM
KV_WIDTH = NSA_KV_HEADS * HEAD_DIM
CMP_BLOCK = 32
CMP_STRIDE = 16
CMP_HIDDEN = 128
SEL_BLOCK = 64
SEL_TOP_N = 16
SEL_FORCE = 1e4
WINDOW = 512
WIN_Q_BLOCK = 128
REL_BUCKETS = 32
REL_MAX_DIST = 128
M_HEADS = 4
M_HEAD_DIM = 128
M_WIDTH = M_HEADS * M_HEAD_DIM
CONV_W = 4
M_CHUNK = 64
D_FF = 2816
N_EXPERTS = 8
TOP_K = 2
EPS = 1e-6
IN_WIDTHS = (NSA_WIDTH, KV_WIDTH, KV_WIDTH, KV_WIDTH, KV_WIDTH, KV_WIDTH, KV_WIDTH, 3 * NSA_HEADS, M_WIDTH, M_WIDTH,
             2 * M_HEADS, D_MODEL, D_MODEL)

VMEM_LIMIT_V7X = 52 * 1024 * 1024
LANE = 128

F32 = jnp.float32
BF16 = jnp.bfloat16


def _split3(x):
    hi = x.astype(BF16)
    r1 = x - hi.astype(F32)
    mid = r1.astype(BF16)
    lo = (r1 - mid.astype(F32)).astype(BF16)
    return hi, mid, lo


def _pick_tile(n, cands):
    for c in cands:
        if n % c == 0:
            return c
    return n


def _mm_kernel(*refs, norm, has_res):
    x_ref, g_ref, w_ref = refs[:3]
    res_ref = refs[3] if has_res else None
    o_ref, xs_ref = refs[-2], refs[-1]

    @pl.when(pl.program_id(1) == 0)
    def _():
        x = x_ref[...]
        if norm:
            x = x * lax.rsqrt(jnp.mean(x * x, axis=-1, keepdims=True) + EPS) * g_ref[...]
        xs_ref[...] = x.astype(BF16)

    acc = jnp.dot(xs_ref[...], w_ref[...].astype(BF16), preferred_element_type=F32)
    if has_res:
        acc = acc + res_ref[...]
    o_ref[...] = acc


def _mm(x, w, g=None, res=None):
    m, k = x.shape
    n = w.shape[1]
    tm = _pick_tile(m, (1024, 512, 256, 128))
    tn = _pick_tile(n, (512, 256, 128))
    norm = g is not None
    gg = (g if norm else jnp.ones((k,), F32)).reshape(1, k)
    in_specs = [pl.BlockSpec((tm, k), lambda i, j: (i, 0)),
                pl.BlockSpec((1, k), lambda i, j: (0, 0)),
                pl.BlockSpec((k, tn), lambda i, j: (0, j))]
    args = [x, gg, w]
    if res is not None:
        in_specs.append(pl.BlockSpec((tm, tn), lambda i, j: (i, j)))
        args.append(res)
    return pl.pallas_call(
        functools.partial(_mm_kernel, norm=norm, has_res=res is not None),
        grid=(m // tm, n // tn),
        in_specs=in_specs,
        out_specs=pl.BlockSpec((tm, tn), lambda i, j: (i, j)),
        out_shape=jax.ShapeDtypeStruct((m, n), F32),
        scratch_shapes=[pltpu.VMEM((tm, k), BF16)],
        compiler_params=pltpu.CompilerParams(dimension_semantics=("arbitrary", "arbitrary"),
                                             vmem_limit_bytes=VMEM_LIMIT_V7X),
        name="mm",
    )(*args)


def _ffn_body(x_ref, g_ref, w1_ref, w3_ref, w2_ref, o_ref, xs_ref, acc_ref, *, residual, grouped):
    j = pl.program_id(1)

    @pl.when(j == 0)
    def _():
        x = x_ref[...]
        xn = x * lax.rsqrt(jnp.mean(x * x, axis=-1, keepdims=True) + EPS) * g_ref[...]
        xs_ref[...] = xn.astype(BF16)
        acc_ref[...] = jnp.zeros_like(acc_ref)

    xs = xs_ref[...]
    w1 = w1_ref[0] if grouped else w1_ref[...]
    w3 = w3_ref[0] if grouped else w3_ref[...]
    w2 = w2_ref[0] if grouped else w2_ref[...]
    a = jnp.dot(xs, w1.astype(BF16), preferred_element_type=F32)
    b = jnp.dot(xs, w3.astype(BF16), preferred_element_type=F32)
    h = (a * jax.nn.sigmoid(a) * b).astype(BF16)
    acc_ref[...] += jnp.dot(h, w2.astype(BF16), preferred_element_type=F32)

    @pl.when(j == pl.num_programs(1) - 1)
    def _():
        if residual:
            o_ref[...] = x_ref[...] + acc_ref[...]
        else:
            o_ref[...] = acc_ref[...]


def _ffn_dense_kernel(x_ref, g_ref, w1_ref, w3_ref, w2_ref, o_ref, xs_ref, acc_ref):
    _ffn_body(x_ref, g_ref, w1_ref, w3_ref, w2_ref, o_ref, xs_ref, acc_ref, residual=True, grouped=False)


def _ffn_grouped_kernel(be_ref, x_ref, g_ref, w1_ref, w3_ref, w2_ref, o_ref, xs_ref, acc_ref):
    del be_ref
    _ffn_body(x_ref, g_ref, w1_ref, w3_ref, w2_ref, o_ref, xs_ref, acc_ref, residual=False, grouped=True)


def _ffn_dense(x, g, w1, w3, w2):
    m, d = x.shape
    f = w1.shape[1]
    tm = _pick_tile(m, (1024, 512, 256, 128))
    tf = _pick_tile(f, (256, 128))
    return pl.pallas_call(
        _ffn_dense_kernel,
        grid=(m // tm, f // tf),
        in_specs=[pl.BlockSpec((tm, d), lambda i, j: (i, 0)),
                  pl.BlockSpec((1, d), lambda i, j: (0, 0)),
                  pl.BlockSpec((d, tf), lambda i, j: (0, j)),
                  pl.BlockSpec((d, tf), lambda i, j: (0, j)),
                  pl.BlockSpec((tf, d), lambda i, j: (j, 0))],
        out_specs=pl.BlockSpec((tm, d), lambda i, j: (i, 0)),
        out_shape=jax.ShapeDtypeStruct((m, d), F32),
        scratch_shapes=[pltpu.VMEM((tm, d), BF16), pltpu.VMEM((tm, d), F32)],
        compiler_params=pltpu.CompilerParams(dimension_semantics=("arbitrary", "arbitrary"),
                                             vmem_limit_bytes=VMEM_LIMIT_V7X),
        name="ffn_dense",
    )(x, g.reshape(1, d), w1, w3, w2)


def _ffn_grouped(xd, blk_e, g, w1, w3, w2, tm):
    rows, d = xd.shape
    f = w1.shape[2]
    tf = _pick_tile(f, (256, 128))
    grid_spec = pltpu.PrefetchScalarGridSpec(
        num_scalar_prefetch=1,
        grid=(rows // tm, f // tf),
        in_specs=[pl.BlockSpec((tm, d), lambda i, j, be: (i, 0)),
                  pl.BlockSpec((1, d), lambda i, j, be: (0, 0)),
                  pl.BlockSpec((1, d, tf), lambda i, j, be: (be[i], 0, j)),
                  pl.BlockSpec((1, d, tf), lambda i, j, be: (be[i], 0, j)),
                  pl.BlockSpec((1, tf, d), lambda i, j, be: (be[i], j, 0))],
        out_specs=pl.BlockSpec((tm, d), lambda i, j, be: (i, 0)),
        scratch_shapes=[pltpu.VMEM((tm, d), BF16), pltpu.VMEM((tm, d), F32)],
    )
    return pl.pallas_call(
        _ffn_grouped_kernel,
        grid_spec=grid_spec,
        out_shape=jax.ShapeDtypeStruct((rows, d), F32),
        compiler_params=pltpu.CompilerParams(dimension_semantics=("arbitrary", "arbitrary"),
                                             vmem_limit_bytes=VMEM_LIMIT_V7X),
        name="ffn_grouped",
    )(blk_e, xd, g.reshape(1, d), w1, w3, w2)


def _moe(x, g, router, w1, w3, w2):
    n, d = x.shape
    tm = 1024 if n >= 8192 else 128
    router_p = jnp.pad(router, ((0, 0), (0, LANE - N_EXPERTS)))
    logits = _mm(x, router_p, g=g)[:, :N_EXPERTS]
    top_val, top_idx = lax.top_k(logits, TOP_K)
    gate = jax.nn.softmax(top_val, axis=-1).reshape(-1)
    e_flat = top_idx.reshape(-1)
    n_asg = n * TOP_K
    order = jnp.argsort(e_flat)
    e_sorted = e_flat[order]
    counts = jnp.bincount(e_flat, length=N_EXPERTS)
    padded = (counts + tm - 1) // tm * tm
    pad_end = jnp.cumsum(padded)
    pad_start = pad_end - padded
    start = jnp.cumsum(counts) - counts
    dest = pad_start[e_sorted] + jnp.arange(n_asg) - start[e_sorted]
    n_blocks = n_asg // tm + N_EXPERTS
    src_tok = jnp.zeros((n_blocks * tm,), jnp.int32).at[dest].set((order // TOP_K).astype(jnp.int32))
    pos = jnp.zeros((n_asg,), jnp.int32).at[order].set(dest.astype(jnp.int32))
    blk_e = jnp.minimum(jnp.searchsorted(pad_end, jnp.arange(n_blocks) * tm, side='right'),
                        N_EXPERTS - 1).astype(jnp.int32)
    yd = _ffn_grouped(x[src_tok], blk_e, g, w1, w3, w2, tm)
    contrib = yd[pos] * gate[:, None]
    return x + contrib.reshape(n, TOP_K, d).sum(axis=1)


N_CHUNK = 128
CHUNK_W = CMP_STRIDE * HEAD_DIM


def _compress_kernel(x_ref, pe_ref, w1_ref, w2_ref, o_ref):
    c = x_ref[0]
    lo = jnp.dot((c + pe_ref[0:1]).astype(BF16), w1_ref[0].astype(BF16), preferred_element_type=F32)
    hi = jnp.dot((c + pe_ref[1:2]).astype(BF16), w1_ref[1].astype(BF16), preferred_element_type=F32)
    hid = jax.nn.gelu(lo + pltpu.roll(hi, N_CHUNK - 1, 0))
    out = jnp.dot(hid.astype(BF16), w2_ref[...].astype(BF16), preferred_element_type=F32)
    row = lax.broadcasted_iota(jnp.int32, out.shape, 0)
    o_ref[0] = jnp.where(row < N_CHUNK - 1, out, 0.0)


def _compress(xc, pe, w1, w2):
    nb = xc.shape[0]
    hidden = w1.shape[-1]
    return pl.pallas_call(
        _compress_kernel,
        grid=(nb,),
        in_specs=[pl.BlockSpec((1, N_CHUNK, CHUNK_W), lambda i: (i, 0, 0)),
                  pl.BlockSpec((2, CHUNK_W), lambda i: (0, 0)),
                  pl.BlockSpec((2, CHUNK_W, hidden), lambda i: (0, 0, 0)),
                  pl.BlockSpec((hidden, HEAD_DIM), lambda i: (0, 0))],
        out_specs=pl.BlockSpec((1, N_CHUNK, HEAD_DIM), lambda i: (i, 0, 0)),
        out_shape=jax.ShapeDtypeStruct((nb, N_CHUNK, HEAD_DIM), F32),
        compiler_params=pltpu.CompilerParams(dimension_semantics=("arbitrary",), vmem_limit_bytes=VMEM_LIMIT_V7X),
        name="compress",
    )(xc, pe.reshape(2, CHUNK_W), w1.reshape(2, CHUNK_W, hidden), w2)


def _compress_heads(kv, pe, w1, w2):
    b = kv.shape[0]
    xc = kv.reshape(b, N_CHUNK, CMP_STRIDE, NSA_KV_HEADS, HEAD_DIM).transpose(0, 3, 1, 2, 4)
    out = _compress(xc.reshape(b * NSA_KV_HEADS, N_CHUNK, CHUNK_W), pe, w1, w2)
    return out.reshape(b, NSA_KV_HEADS, N_CHUNK, HEAD_DIM)


TQ = 128
TK = 128
NEG = -1e30
N_BIAS_TILES = WINDOW // TK + 1
MASKED_TILE = N_BIAS_TILES
SEL_SUB = 4


def _nsa_prompt_kernel(q_ref, kct_ref, vcc_ref, kst_ref, vs_ref, kwt_ref, vw_ref, bcmp_ref, btile_ref, gate_ref,
                       ov_ref, exp_ref, o_ref, selneg_ref, m_ref, l_ref, acc_ref):
    i = pl.program_id(2)
    rows = NSA_GROUP * TQ
    q = q_ref[0, 0].reshape(rows, HEAD_DIM)

    s = jnp.dot(q, kct_ref[0, 0], preferred_element_type=F32) + bcmp_ref[0].reshape(rows, N_CHUNK)
    m = jnp.max(s, axis=-1, keepdims=True)
    e = jnp.where(s > 0.1 * NEG, jnp.exp(s - m), 0.0)
    p = e / jnp.maximum(jnp.sum(e, axis=-1, keepdims=True), 1e-30)
    o_cmp = jnp.dot(p.astype(BF16), vcc_ref[0, 0], preferred_element_type=F32)

    p_sum = p[0:TQ] + p[TQ:2 * TQ] + p[2 * TQ:3 * TQ] + p[3 * TQ:4 * TQ]
    imp = sum(jnp.dot(part, ov_ref[...], preferred_element_type=F32) for part in _split3(p_sum))
    lane = lax.broadcasted_iota(jnp.int32, (TQ, LANE), 1)
    tok = lax.broadcasted_iota(jnp.int32, (TQ, LANE), 0) + i * TQ
    cur = tok // SEL_BLOCK
    valid = lane <= cur
    forced = (lane == 0) | (lane == cur) | (lane == cur - 1)
    score = jnp.where(valid, imp + jnp.where(forced, SEL_FORCE, 0.0), -1.0)
    rank = jnp.zeros((TQ, LANE), F32)
    for c in range(T_PROMPT // SEL_BLOCK):
        sc = score[:, c:c + 1]
        beats = (sc > score) | ((sc == score) & (lane > c))
        rank = rank + jnp.where(beats, 1.0, 0.0)
    sel = jnp.where(valid & (rank < SEL_TOP_N), 1.0, 0.0).astype(BF16)
    sel_keys = jnp.dot(sel, exp_ref[...], preferred_element_type=F32)
    selneg_ref[...] = (sel_keys - 1.0) * (-NEG)

    def masked_scores(kt_ref, first_tile, n_sub, selected):
        k0 = pl.multiple_of(first_tile * TK, TK)
        s_all = jnp.dot(q, kt_ref[0, 0, :, pl.ds(k0, n_sub * TK)], preferred_element_type=F32)
        pieces = []
        for u in range(n_sub):
            d0 = i - (first_tile + u)
            idx = jnp.where(d0 < 0, MASKED_TILE, jnp.minimum(d0, 2) if selected else d0)
            piece = s_all[:, u * TK:(u + 1) * TK].reshape(NSA_GROUP, TQ, TK) + btile_ref[0, idx]
            if selected:
                piece = piece + selneg_ref[:, pl.ds(pl.multiple_of(k0 + u * TK, TK), TK)][None]
            pieces.append(piece.reshape(rows, TK))
        return pieces, k0

    def row_max(pieces):
        return jnp.max(functools.reduce(jnp.maximum, pieces), axis=-1, keepdims=True)

    def row_sum(pieces):
        return jnp.sum(functools.reduce(jnp.add, pieces), axis=-1, keepdims=True)

    m_ref[...] = jnp.full(m_ref.shape, -jnp.inf, F32)
    l_ref[...] = jnp.zeros(l_ref.shape, F32)
    acc_ref[...] = jnp.zeros(acc_ref.shape, F32)

    def sel_group(gi, carry):
        pieces, k0 = masked_scores(kst_ref, gi * SEL_SUB, SEL_SUB, True)
        m_old = m_ref[...]
        m_new = jnp.maximum(m_old, row_max(pieces))
        pt = [jnp.exp(piece - m_new) for piece in pieces]
        alpha = jnp.exp(m_old - m_new)
        l_ref[...] = alpha * l_ref[...] + row_sum(pt)
        pv = jnp.dot(jnp.concatenate([x.astype(BF16) for x in pt], axis=1), vs_ref[0, 0, pl.ds(k0, SEL_SUB * TK), :],
                     preferred_element_type=F32)
        acc_ref[...] = alpha * acc_ref[...] + pv
        m_ref[...] = m_new
        return carry

    lax.fori_loop(0, i // SEL_SUB + 1, sel_group, 0)
    o_sel = acc_ref[...] / l_ref[...]

    pieces, k0 = masked_scores(kwt_ref, jnp.maximum(i - (N_BIAS_TILES - 1), 0), N_BIAS_TILES, False)
    m_win = row_max(pieces)
    pt = [jnp.exp(piece - m_win) for piece in pieces]
    o_win = jnp.dot(jnp.concatenate([x.astype(BF16) for x in pt], axis=1),
                    vw_ref[0, 0, pl.ds(k0, N_BIAS_TILES * TK), :], preferred_element_type=F32) / row_sum(pt)

    g = jax.nn.sigmoid(gate_ref[0, 0])
    for a in range(NSA_GROUP):
        r = slice(a * TQ, (a + 1) * TQ)
        o_ref[0, 0, a] = (g[:, 3 * a:3 * a + 1] * o_cmp[r] + g[:, 3 * a + 1:3 * a + 2] * o_sel[r]
                          + g[:, 3 * a + 2:3 * a + 3] * o_win[r])


T_PROMPT = 2048


def _bias_lookup(table, dist):
    oh = jax.nn.one_hot(rel_bucket(dist), REL_BUCKETS, dtype=F32)
    return jnp.einsum('...r,rh->...h', oh, table, precision=lax.Precision.HIGHEST)


def _prompt_bias_tables(table):
    t = T_PROMPT
    q_pos = jnp.arange(t)
    block_end = jnp.arange(N_CHUNK) * CMP_STRIDE + (CMP_BLOCK - 1)
    dist = q_pos[:, None] - block_end[None, :]
    ok = (dist >= 0) & (jnp.arange(N_CHUNK)[None, :] < N_CHUNK - 1)
    bcmp = jnp.where(ok[..., None], _bias_lookup(table, dist), NEG)
    bcmp = bcmp.reshape(t, N_CHUNK, NSA_KV_HEADS, NSA_GROUP).transpose(2, 3, 0, 1)
    d0 = jnp.arange(N_BIAS_TILES)[:, None, None]
    dist = d0 * TK + jnp.arange(TQ)[None, :, None] - jnp.arange(TK)[None, None, :]
    ok = (dist >= 0) & (dist < WINDOW)
    bt = jnp.where(ok[..., None], _bias_lookup(table, dist), NEG)
    bt = bt.reshape(N_BIAS_TILES, TQ, TK, NSA_KV_HEADS, NSA_GROUP).transpose(3, 0, 4, 1, 2)
    bt = jnp.concatenate([bt, jnp.full((NSA_KV_HEADS, 1, NSA_GROUP, TQ, TK), NEG, F32)], axis=1)
    c0 = jnp.arange(N_CHUNK) * CMP_STRIDE
    s0 = jnp.arange(LANE) * SEL_BLOCK
    ov = jnp.clip(jnp.minimum(c0[:, None] + CMP_BLOCK, s0[None, :] + SEL_BLOCK)
                  - jnp.maximum(c0[:, None], s0[None, :]), 0, CMP_BLOCK).astype(F32) / CMP_BLOCK
    ov = jnp.where((jnp.arange(N_CHUNK)[:, None] < N_CHUNK - 1) & (jnp.arange(LANE)[None, :] < t // SEL_BLOCK), ov, 0.0)
    expand = (jnp.arange(LANE)[:, None] == (jnp.arange(t) // SEL_BLOCK)[None, :]).astype(BF16)
    return bcmp, bt, ov.astype(BF16), expand


def _nsa_prompt(q_h, kct, vcc, kst, vs, kwt, vw, gates, tables):
    bcmp, bt, ov, expand = tables
    b, _, _, t, _ = q_h.shape
    rows = NSA_GROUP * TQ
    per_bh = lambda *blk: pl.BlockSpec((1, 1) + blk, lambda bi, h, i: (bi, h) + (0,) * len(blk))
    return pl.pallas_call(
        _nsa_prompt_kernel,
        grid=(b, NSA_KV_HEADS, t // TQ),
        in_specs=[pl.BlockSpec((1, 1, NSA_GROUP, TQ, HEAD_DIM), lambda bi, h, i: (bi, h, 0, i, 0)),
                  per_bh(HEAD_DIM, N_CHUNK), per_bh(N_CHUNK, HEAD_DIM),
                  per_bh(HEAD_DIM, t), per_bh(t, HEAD_DIM), per_bh(HEAD_DIM, t), per_bh(t, HEAD_DIM),
                  pl.BlockSpec((1, NSA_GROUP, TQ, N_CHUNK), lambda bi, h, i: (h, 0, i, 0)),
                  pl.BlockSpec((1, N_BIAS_TILES + 1, NSA_GROUP, TQ, TK), lambda bi, h, i: (h, 0, 0, 0, 0)),
                  pl.BlockSpec((1, 1, TQ, 3 * NSA_GROUP), lambda bi, h, i: (bi, h, i, 0)),
                  pl.BlockSpec((N_CHUNK, LANE), lambda bi, h, i: (0, 0)),
                  pl.BlockSpec((LANE, t), lambda bi, h, i: (0, 0))],
        out_specs=pl.BlockSpec((1, 1, NSA_GROUP, TQ, HEAD_DIM), lambda bi, h, i: (bi, h, 0, i, 0)),
        out_shape=jax.ShapeDtypeStruct((b, NSA_KV_HEADS, NSA_GROUP, t, HEAD_DIM), F32),
        scratch_shapes=[pltpu.VMEM((TQ, t), F32), pltpu.VMEM((rows, 1), F32), pltpu.VMEM((rows, 1), F32),
                        pltpu.VMEM((rows, HEAD_DIM), F32)],
        compiler_params=pltpu.CompilerParams(dimension_semantics=("arbitrary", "arbitrary", "arbitrary"),
                                             vmem_limit_bytes=VMEM_LIMIT_V7X),
        name="nsa_prompt",
    )(q_h, kct, vcc, kst, vs, kwt, vw, bcmp, bt, gates, ov, expand)


def _heads_t(x, b, t):
    return x.astype(BF16).reshape(b, t, NSA_KV_HEADS, HEAD_DIM).transpose(0, 2, 3, 1)


def _heads(x, b, t):
    return x.astype(BF16).reshape(b, t, NSA_KV_HEADS, HEAD_DIM).transpose(0, 2, 1, 3)


def _nsa_prompt_from_proj(z_a, z_s, b, t, tables, pe, w1, w2):
    col = lambda i: z_a[:, NSA_WIDTH + KV_WIDTH * i:NSA_WIDTH + KV_WIDTH * (i + 1)]
    q_h = (z_a[:, :NSA_WIDTH] * (HEAD_DIM ** -0.5)).astype(BF16)
    q_h = q_h.reshape(b, t, NSA_KV_HEADS, NSA_GROUP, HEAD_DIM).transpose(0, 2, 3, 1, 4)
    kcc = _compress_heads(col(0).reshape(b, t, NSA_KV_HEADS, HEAD_DIM), pe[0], w1[0], w2[0])
    vcc = _compress_heads(col(1).reshape(b, t, NSA_KV_HEADS, HEAD_DIM), pe[1], w1[1], w2[1])
    gates = z_s[:, :3 * NSA_HEADS].reshape(b, t, NSA_KV_HEADS, 3 * NSA_GROUP).transpose(0, 2, 1, 3)
    o = _nsa_prompt(q_h, kcc.astype(BF16).transpose(0, 1, 3, 2), vcc.astype(BF16),
                    _heads_t(col(2), b, t), _heads(col(3), b, t), _heads_t(col(4), b, t), _heads(col(5), b, t),
                    gates, tables)
    return o.transpose(0, 3, 1, 2, 4).reshape(b * t, NSA_WIDTH)


T_SAMPLE = 4
T_PAD = 8
PAGE = 128
N_PAGES = 16
PAST = N_PAGES * PAGE
N_SEL_SAMPLE = -(-(PAST + T_SAMPLE) // SEL_BLOCK)


def _nsa_sample_kernel(pt_ref, *refs):
    del pt_ref
    cp, sp = refs[:N_PAGES], refs[N_PAGES:2 * N_PAGES]
    (wb_ref, za_ref, zs_ref, pe2_ref, w1_ref, w2_ref, bcmp_ref, bpast_ref, bwin_ref, bnew_ref, ov_ref, exp_ref,
     o_ref, new_ref, q_ref, g_ref, acc_ref, kcat_ref, vcat_ref, s_ref, e_ref, oacc_ref) = refs[2 * N_PAGES:]
    rows = NSA_GROUP * T_PAD

    @pl.when(pl.program_id(0) == 0)
    def _():
        new_ref[...] = jnp.zeros(new_ref.shape, F32)
        q_ref[...] = jnp.zeros(q_ref.shape, F32)
        g_ref[...] = jnp.zeros(g_ref.shape, F32)

    new_ref[0:T_SAMPLE, :] = za_ref[0, :, NSA_WIDTH + 2 * KV_WIDTH:]
    q_ref[0:T_SAMPLE, :] = za_ref[0, :, :NSA_WIDTH] * (HEAD_DIM ** -0.5)
    g_ref[0:T_SAMPLE, :] = zs_ref[0]

    comp = {}
    for kv in range(2):
        acc_ref[...] = jnp.zeros(acc_ref.shape, F32)

        def add_row_offset(l, carry, kv=kv):
            xl = jnp.concatenate([cp[p][0, pl.ds(2 * l + kv, PAGE // CMP_STRIDE, stride=2 * CMP_STRIDE), :]
                                  for p in range(N_PAGES)], axis=0)
            for half in range(2):
                lw = l + CMP_STRIDE * half
                xb = (xl + pe2_ref[kv, pl.ds(lw, 1), :]).astype(BF16)
                xs = jnp.concatenate([xb[:, :HEAD_DIM], xb[:, HEAD_DIM:]], axis=0)
                acc_ref[half] += jnp.dot(xs, w1_ref[kv, lw], preferred_element_type=F32)
            return carry

        lax.fori_loop(0, CMP_STRIDE, add_row_offset, 0)
        for h in range(NSA_KV_HEADS):
            r = slice(h * N_CHUNK, (h + 1) * N_CHUNK)
            hid = jax.nn.gelu(acc_ref[0, r, :] + pltpu.roll(acc_ref[1, r, :], N_CHUNK - 1, 0))
            out = jnp.dot(hid.astype(BF16), w2_ref[kv], preferred_element_type=F32)
            row = lax.broadcasted_iota(jnp.int32, out.shape, 0)
            comp[kv, h] = jnp.where(row < N_CHUNK - 1, out, 0.0).astype(BF16)

    w_buf = wb_ref.shape[1] // 2
    for p in range(N_PAGES):
        kcat_ref[p * PAGE:(p + 1) * PAGE, :] = sp[p][0, pl.ds(0, PAGE, stride=2), :].astype(BF16)
        vcat_ref[p * PAGE:(p + 1) * PAGE, :] = sp[p][0, pl.ds(1, PAGE, stride=2), :].astype(BF16)
    kcat_ref[PAST:, :] = new_ref[:, 0:KV_WIDTH].astype(BF16)
    vcat_ref[PAST:, :] = new_ref[:, KV_WIDTH:2 * KV_WIDTH].astype(BF16)
    kw_buf = wb_ref[0, pl.ds(0, w_buf, stride=2), :].astype(BF16)
    vw_buf = wb_ref[0, pl.ds(1, w_buf, stride=2), :].astype(BF16)
    kw_new, vw_new = (new_ref[:, i * KV_WIDTH:(i + 1) * KV_WIDTH].astype(BF16) for i in (2, 3))
    gates = jax.nn.sigmoid(g_ref[...])
    nt = (((1,), (1,)), ((), ()))
    zeros64 = jnp.zeros((rows, HEAD_DIM), F32)
    pieces = []
    key_chunk = 512

    def attend_window(qp, h):
        s_parts = [lax.dot_general(qp, k, nt, preferred_element_type=F32) + bias
                   for k, bias in ((kw_buf, bwin_ref[h]), (kw_new, bnew_ref[h]))]
        m = jnp.maximum(*[jnp.max(s, axis=-1, keepdims=True) for s in s_parts])
        e_parts = [jnp.exp(s - m) for s in s_parts]
        den = jnp.add(*[jnp.sum(e, axis=-1, keepdims=True) for e in e_parts])
        num = jnp.add(*[jnp.dot(e.astype(BF16), v, preferred_element_type=F32)
                        for e, v in zip(e_parts, (vw_buf, vw_new))])
        return (num / den)[:, h * HEAD_DIM:(h + 1) * HEAD_DIM]

    def attend_selected(qp, h):
        def scores(c, carry):
            ds = pl.ds(pl.multiple_of(c * key_chunk, key_chunk), key_chunk)
            s_ref[:, ds] += lax.dot_general(qp, kcat_ref[ds, :], nt, preferred_element_type=F32)
            return carry

        lax.fori_loop(0, PAST // key_chunk, scores, 0)
        s_ref[:, PAST:] += lax.dot_general(qp, kcat_ref[PAST:, :], nt, preferred_element_type=F32)
        s = s_ref[...]
        e = jnp.exp(s - jnp.max(s, axis=-1, keepdims=True))
        den = jnp.sum(e, axis=-1, keepdims=True)
        e_ref[...] = e.astype(BF16)
        oacc_ref[...] = jnp.dot(e_ref[:, PAST:], vcat_ref[PAST:, :], preferred_element_type=F32)

        def weighted(c, carry):
            ds = pl.ds(pl.multiple_of(c * key_chunk, key_chunk), key_chunk)
            oacc_ref[...] += jnp.dot(e_ref[:, ds], vcat_ref[ds, :], preferred_element_type=F32)
            return carry

        lax.fori_loop(0, PAST // key_chunk, weighted, 0)
        return (oacc_ref[...] / den)[:, h * HEAD_DIM:(h + 1) * HEAD_DIM]

    for h in range(NSA_KV_HEADS):
        q64 = jnp.concatenate([q_ref[:, (h * NSA_GROUP + a) * HEAD_DIM:(h * NSA_GROUP + a + 1) * HEAD_DIM]
                               for a in range(NSA_GROUP)], axis=0)
        qp = jnp.concatenate([q64, zeros64] if h == 0 else [zeros64, q64], axis=1).astype(BF16)
        q64 = q64.astype(BF16)

        s = lax.dot_general(q64, comp[0, h], nt, preferred_element_type=F32) + bcmp_ref[h]
        m = jnp.max(s, axis=-1, keepdims=True)
        e = jnp.where(s > 0.1 * NEG, jnp.exp(s - m), 0.0)
        p = e / jnp.maximum(jnp.sum(e, axis=-1, keepdims=True), 1e-30)
        o_cmp = jnp.dot(p.astype(BF16), comp[1, h], preferred_element_type=F32)

        p_sum = p[0:T_PAD] + p[T_PAD:2 * T_PAD] + p[2 * T_PAD:3 * T_PAD] + p[3 * T_PAD:4 * T_PAD]
        imp = sum(jnp.dot(part, ov_ref[...], preferred_element_type=F32) for part in _split3(p_sum))
        lane = lax.broadcasted_iota(jnp.int32, (T_PAD, LANE), 1)
        cur = (PAST + jnp.minimum(lax.broadcasted_iota(jnp.int32, (T_PAD, LANE), 0), T_SAMPLE - 1)) // SEL_BLOCK
        valid = lane <= cur
        forced = (lane == 0) | (lane == cur) | (lane == cur - 1)
        score = jnp.where(valid, imp + jnp.where(forced, SEL_FORCE, 0.0), -1.0)
        rank = jnp.zeros((T_PAD, LANE), F32)
        for c in range(N_SEL_SAMPLE):
            sc = score[:, c:c + 1]
            beats = (sc > score) | ((sc == score) & (lane > c))
            rank = rank + jnp.where(beats, 1.0, 0.0)
        sel = jnp.where(valid & (rank < SEL_TOP_N), 1.0, 0.0)
        selneg = (jnp.dot(sel.astype(BF16), exp_ref[...], preferred_element_type=F32) - 1.0) * (-NEG)
        selneg_new = (sel[:, PAST // SEL_BLOCK:PAST // SEL_BLOCK + 1] - 1.0) * (-NEG)

        def per_token(bias, tok):
            n = bias.shape[-1]
            return (bias.reshape(NSA_GROUP, T_PAD, n) + tok[None]).reshape(rows, n)

        s_ref[:, :PAST] = per_token(bpast_ref[h], selneg)
        s_ref[:, PAST:] = per_token(bnew_ref[h], jnp.broadcast_to(selneg_new, (T_PAD, LANE)))
        o_sel = attend_selected(qp, h)
        o_win = attend_window(qp, h)

        for a in range(NSA_GROUP):
            r = slice(a * T_PAD, (a + 1) * T_PAD)
            c0 = (h * NSA_GROUP + a) * 3
            pieces.append(gates[:, c0:c0 + 1] * o_cmp[r] + gates[:, c0 + 1:c0 + 2] * o_sel[r]
                          + gates[:, c0 + 2:c0 + 3] * o_win[r])
    o_ref[0] = jnp.concatenate(pieces, axis=1)[0:T_SAMPLE]


def _sample_bias_tables(table):
    tq = jnp.minimum(jnp.arange(T_PAD), T_SAMPLE - 1)
    q_pos = PAST + tq

    def lay(x):
        n = x.shape[1]
        return x.reshape(T_PAD, n, NSA_KV_HEADS, NSA_GROUP).transpose(2, 3, 0, 1).reshape(NSA_KV_HEADS, -1, n)

    block_end = jnp.arange(N_CHUNK) * CMP_STRIDE + (CMP_BLOCK - 1)
    dist = q_pos[:, None] - block_end[None, :]
    ok = (dist >= 0) & (jnp.arange(N_CHUNK)[None, :] < N_CHUNK - 1)
    bcmp = lay(jnp.where(ok[..., None], _bias_lookup(table, dist), NEG))
    dist = q_pos[:, None] - jnp.arange(PAST)[None, :]
    past = _bias_lookup(table, dist)
    bpast = lay(past)
    w_buf = min(WINDOW, PAST)
    bwin = lay(jnp.where((dist < WINDOW)[:, PAST - w_buf:, None], past[:, PAST - w_buf:], NEG))
    j = jnp.arange(LANE)
    dist = tq[:, None] - j[None, :]
    ok = (dist >= 0) & (j[None, :] < T_SAMPLE)
    bnew = lay(jnp.where(ok[..., None], _bias_lookup(table, dist), NEG))
    c0 = jnp.arange(N_CHUNK) * CMP_STRIDE
    s0 = jnp.arange(LANE) * SEL_BLOCK
    ov = jnp.clip(jnp.minimum(c0[:, None] + CMP_BLOCK, s0[None, :] + SEL_BLOCK)
                  - jnp.maximum(c0[:, None], s0[None, :]), 0, CMP_BLOCK).astype(F32) / CMP_BLOCK
    ov = jnp.where((jnp.arange(N_CHUNK)[:, None] < N_CHUNK - 1) & (j[None, :] < N_SEL_SAMPLE), ov, 0.0)
    expand = (j[:, None] == (jnp.arange(PAST) // SEL_BLOCK)[None, :]).astype(BF16)
    return bcmp, bpast, bwin, bnew, ov.astype(BF16), expand


def _nsa_sample(za, zs, cache_cmp, cache_slc, cache_win, page_table, layer, tables, pe, w1, w2):
    b = za.shape[0]
    cmp_pages = cache_cmp[layer][page_table].reshape(b * N_PAGES, 2 * PAGE, KV_WIDTH)
    slc_pages = cache_slc[layer][page_table].reshape(b * N_PAGES, 2 * PAGE, KV_WIDTH)
    w_buf = cache_win.shape[2]
    win = cache_win[layer].reshape(b, 2 * w_buf, KV_WIDTH)
    pt = jnp.arange(b * N_PAGES, dtype=jnp.int32)
    bcmp, bpast, bwin, bnew, ov, expand = tables
    pe2 = jnp.concatenate([pe, pe], axis=-1)
    page_spec = lambda p: pl.BlockSpec((1, 2 * PAGE, KV_WIDTH), lambda bi, pt_: (pt_[bi * N_PAGES + p], 0, 0))
    whole = lambda x: pl.BlockSpec(x.shape, lambda bi, pt_: (0,) * x.ndim)
    consts = [pe2, w1.astype(BF16), w2.astype(BF16), bcmp, bpast, bwin, bnew, ov, expand]
    grid_spec = pltpu.PrefetchScalarGridSpec(
        num_scalar_prefetch=1,
        grid=(b,),
        in_specs=[page_spec(p) for p in range(N_PAGES)] * 2
        + [pl.BlockSpec((1, 2 * w_buf, KV_WIDTH), lambda bi, pt_: (bi, 0, 0)),
           pl.BlockSpec((1, T_SAMPLE, za.shape[-1]), lambda bi, pt_: (bi, 0, 0)),
           pl.BlockSpec((1, T_SAMPLE, LANE), lambda bi, pt_: (bi, 0, 0))]
        + [whole(x) for x in consts],
        out_specs=pl.BlockSpec((1, T_SAMPLE, NSA_WIDTH), lambda bi, pt_: (bi, 0, 0)),
        scratch_shapes=[pltpu.VMEM((PAGE, 4 * KV_WIDTH), F32), pltpu.VMEM((T_PAD, NSA_WIDTH), F32),
                        pltpu.VMEM((T_PAD, LANE), F32), pltpu.VMEM((2, 2 * N_CHUNK, CMP_HIDDEN), F32),
                        pltpu.VMEM((PAST + PAGE, KV_WIDTH), BF16), pltpu.VMEM((PAST + PAGE, KV_WIDTH), BF16),
                        pltpu.VMEM((NSA_GROUP * T_PAD, PAST + PAGE), F32),
                        pltpu.VMEM((NSA_GROUP * T_PAD, PAST + PAGE), BF16),
                        pltpu.VMEM((NSA_GROUP * T_PAD, KV_WIDTH), F32)],
    )
    return pl.pallas_call(
        _nsa_sample_kernel,
        grid_spec=grid_spec,
        out_shape=jax.ShapeDtypeStruct((b, T_SAMPLE, NSA_WIDTH), F32),
        compiler_params=pltpu.CompilerParams(dimension_semantics=("arbitrary",), vmem_limit_bytes=VMEM_LIMIT_V7X),
        name="nsa_sample",
    )(pt, *([cmp_pages] * N_PAGES), *([slc_pages] * N_PAGES), win, za, zs, *consts)


I_LANE = 3 * NSA_HEADS
F_LANE = I_LANE + M_HEADS
EXT_PAD = 8


def _mlstm_kernel(zu_ref, zs_ref, c0_ref, n0_ref, m0_ref, cprev_ref, convw_ref, convb_ref, wq_ref, wk_ref, wkt_ref,
                  wv_ref, gb_ref, normg_ref, tril_ref, selrow_ref,
                  o_ref, c_out, n_out, m_out, conv_out,
                  ext_ref, zsp_ref, c_sc, n_sc, m_sc, *, rows, rows_pad, n_valid):
    i = pl.program_id(1)
    L = M_CHUNK

    @pl.when(i == 0)
    def _():
        c_sc[...] = c0_ref[0]
        n_sc[...] = n0_ref[0]
        m_sc[...] = m0_ref[0]
        ext_ref[EXT_PAD - (CONV_W - 1):EXT_PAD, :] = cprev_ref[0]

    if rows < rows_pad:
        ext_ref[EXT_PAD:, :] = jnp.zeros((rows_pad, M_WIDTH), F32)
        zsp_ref[...] = jnp.zeros(zsp_ref.shape, F32)
    ext_ref[EXT_PAD:EXT_PAD + rows, :] = zu_ref[0, :, :M_WIDTH]
    zsp_ref[0:rows, :] = zs_ref[0]

    u = ext_ref[EXT_PAD:EXT_PAD + rows_pad, :]
    conv = convb_ref[...] + convw_ref[CONV_W - 1:CONV_W, :] * u
    for j in range(CONV_W - 1):
        lo = EXT_PAD - (CONV_W - 1) + j
        conv = conv + convw_ref[j:j + 1, :] * ext_ref[lo:lo + rows_pad, :]
    uc = (conv * jax.nn.sigmoid(conv)).astype(BF16)
    ub = u.astype(BF16)

    zb = zsp_ref[...] + gb_ref[...]
    bcum = sum(jnp.dot(tril_ref[...], part, preferred_element_type=F32) for part in _split3(jax.nn.log_sigmoid(zb)))
    bcum = pltpu.roll(bcum, LANE - M_HEADS, 1)
    a_col = zb - bcum
    a_row = sum(lax.dot_general(selrow_ref[...], part, (((1,), (1,)), ((), ())), preferred_element_type=F32)
                for part in _split3(a_col))

    tt = lax.broadcasted_iota(jnp.int32, (L, L), 0)
    ss = lax.broadcasted_iota(jnp.int32, (L, L), 1)
    causal = ss <= tt
    tok_col = lax.broadcasted_iota(jnp.int32, (L, 1), 0)
    tok_row = lax.broadcasted_iota(jnp.int32, (1, L), 1)

    for h in range(M_HEADS):
        hs = slice(h * M_HEAD_DIM, (h + 1) * M_HEAD_DIM)
        q_all = jnp.dot(uc[:, hs], wq_ref[h].astype(BF16), preferred_element_type=F32) * (M_HEAD_DIM ** -0.5)
        k_all = jnp.dot(uc[:, hs], wk_ref[h].astype(BF16), preferred_element_type=F32)
        kt_all = lax.dot_general(wkt_ref[h].astype(BF16), uc[:, hs], (((1,), (1,)), ((), ())),
                                 preferred_element_type=F32)
        v_all = jnp.dot(ub[:, hs], wv_ref[h].astype(BF16), preferred_element_type=F32)
        c_st = c_sc[h]
        n_st = n_sc[h:h + 1, :]
        m_st = m_sc[h:h + 1, 0:1]
        for c in range(rows_pad // L):
            r = slice(c * L, (c + 1) * L)
            q, k, kt, v = q_all[r].astype(BF16), k_all[r], kt_all[:, r], v_all[r].astype(BF16)
            b_col = bcum[r, I_LANE + h:I_LANE + h + 1]
            a_c = a_col[r, I_LANE + h:I_LANE + h + 1]
            a_r = a_row[h:h + 1, r]
            log_d = jnp.where(causal, b_col + a_r, NEG)
            m_col = jnp.maximum(b_col + m_st, jnp.max(log_d, axis=-1, keepdims=True))
            dw = jnp.exp(log_d - m_col)
            w_inter = jnp.exp(b_col + m_st - m_col)
            s = lax.dot_general(q, k.astype(BF16), (((1,), (1,)), ((), ())), preferred_element_type=F32) * dw
            num = (jnp.dot(s.astype(BF16), v, preferred_element_type=F32)
                   + w_inter * jnp.dot(q, c_st.astype(BF16), preferred_element_type=F32))
            den = (jnp.sum(s, axis=-1, keepdims=True)
                   + w_inter * jnp.sum(q_all[r] * n_st, axis=-1, keepdims=True))
            hh = num / jnp.maximum(jnp.abs(den), jnp.exp(-m_col))
            hn = hh * lax.rsqrt(jnp.mean(hh * hh, axis=-1, keepdims=True) + EPS)
            o_pre = zu_ref[0, :, M_WIDTH + h * M_HEAD_DIM:M_WIDTH + (h + 1) * M_HEAD_DIM]
            if rows < rows_pad:
                o_ref[0, :, hs] = jax.nn.sigmoid(o_pre) * hn[0:rows] * normg_ref[:, hs]
            else:
                o_ref[0, r, hs] = jax.nn.sigmoid(o_pre[r]) * hn * normg_ref[:, hs]
            b_last = b_col[n_valid - 1:n_valid, :]
            log_s = jnp.where(tok_col < n_valid, b_last + a_c, NEG)
            m_new = jnp.maximum(b_last + m_st, jnp.max(log_s, axis=0, keepdims=True))
            ws_col = jnp.exp(log_s - m_new)
            ws_row = jnp.where(tok_row < n_valid, jnp.exp(b_last + a_r - m_new), 0.0)
            wc = jnp.exp(b_last + m_st - m_new)
            c_st = wc * c_st + jnp.dot((kt * ws_row).astype(BF16), v, preferred_element_type=F32)
            n_st = wc * n_st + jnp.sum(k * ws_col, axis=0, keepdims=True)
            m_st = m_new
        c_sc[h] = c_st
        n_sc[h:h + 1, :] = n_st
        m_sc[h:h + 1, :] = jnp.broadcast_to(m_st, (1, LANE))

    tail = ext_ref[EXT_PAD + rows - (CONV_W - 1):EXT_PAD + rows, :]
    ext_ref[EXT_PAD - (CONV_W - 1):EXT_PAD, :] = tail

    @pl.when(i == pl.num_programs(1) - 1)
    def _():
        c_out[0] = c_sc[...]
        n_out[0] = n_sc[...]
        m_out[0] = m_sc[...]
        conv_out[0] = jnp.zeros(conv_out.shape[1:], F32)
        conv_out[0, 0:CONV_W - 1, :] = tail


def _mlstm(zu, zs, state, conv_prev, conv_w, conv_b, m_qkv, gate_b, norm_g):
    b, t, _ = zu.shape
    rows = min(t, 4 * M_CHUNK)
    rows_pad = -(-rows // M_CHUNK) * M_CHUNK
    n_valid = M_CHUNK if rows == rows_pad else rows
    assert t % rows == 0 and (rows == rows_pad or t == rows)
    c0, n0, m0 = state
    n0p = jnp.pad(n0, ((0, 0), (0, 8 - M_HEADS), (0, 0)))
    m0p = jnp.pad(jnp.broadcast_to(m0[:, :, None], (b, M_HEADS, LANE)), ((0, 0), (0, 8 - M_HEADS), (0, 0)))
    gb = jnp.zeros((1, LANE), F32).at[0, I_LANE:I_LANE + 2 * M_HEADS].set(gate_b.reshape(-1))
    idx = jnp.arange(rows_pad)
    tril = ((idx[:, None] >= idx[None, :]) & (idx[:, None] // M_CHUNK == idx[None, :] // M_CHUNK)).astype(BF16)
    selrow = ((jnp.arange(16)[:, None] + I_LANE == jnp.arange(LANE)[None, :])
              & (jnp.arange(16)[:, None] < M_HEADS)).astype(BF16)
    whole = lambda *shape: pl.BlockSpec(shape, lambda bi, i: (0,) * len(shape))
    per_b = lambda *shape: pl.BlockSpec((1,) + shape, lambda bi, i: (bi,) + (0,) * len(shape))
    out, c_new, n_new, m_new, conv_new = pl.pallas_call(
        functools.partial(_mlstm_kernel, rows=rows, rows_pad=rows_pad, n_valid=n_valid),
        grid=(b, t // rows),
        in_specs=[pl.BlockSpec((1, rows, 2 * M_WIDTH), lambda bi, i: (bi, i, 0)),
                  pl.BlockSpec((1, rows, LANE), lambda bi, i: (bi, i, 0)),
                  per_b(M_HEADS, M_HEAD_DIM, M_HEAD_DIM), per_b(8, M_HEAD_DIM), per_b(8, LANE),
                  per_b(CONV_W - 1, M_WIDTH),
                  whole(CONV_W, M_WIDTH), whole(1, M_WIDTH),
                  whole(M_HEADS, M_HEAD_DIM, M_HEAD_DIM), whole(M_HEADS, M_HEAD_DIM, M_HEAD_DIM),
                  whole(M_HEADS, M_HEAD_DIM, M_HEAD_DIM), whole(M_HEADS, M_HEAD_DIM, M_HEAD_DIM),
                  whole(1, LANE), whole(1, M_WIDTH), whole(rows_pad, rows_pad), whole(16, LANE)],
        out_specs=[pl.BlockSpec((1, rows, M_WIDTH), lambda bi, i: (bi, i, 0)),
                   per_b(M_HEADS, M_HEAD_DIM, M_HEAD_DIM), per_b(8, M_HEAD_DIM), per_b(8, LANE),
                   per_b(8, M_WIDTH)],
        out_shape=[jax.ShapeDtypeStruct((b, t, M_WIDTH), F32),
                   jax.ShapeDtypeStruct((b, M_HEADS, M_HEAD_DIM, M_HEAD_DIM), F32),
                   jax.ShapeDtypeStruct((b, 8, M_HEAD_DIM), F32),
                   jax.ShapeDtypeStruct((b, 8, LANE), F32),
                   jax.ShapeDtypeStruct((b, 8, M_WIDTH), F32)],
        scratch_shapes=[pltpu.VMEM((EXT_PAD + rows_pad, M_WIDTH), F32), pltpu.VMEM((rows_pad, LANE), F32),
                        pltpu.VMEM((M_HEADS, M_HEAD_DIM, M_HEAD_DIM), F32), pltpu.VMEM((8, M_HEAD_DIM), F32),
                        pltpu.VMEM((8, LANE), F32)],
        compiler_params=pltpu.CompilerParams(dimension_semantics=("arbitrary", "arbitrary"),
                                             vmem_limit_bytes=VMEM_LIMIT_V7X),
        name="mlstm",
    )(zu, zs, c0, n0p, m0p, conv_prev, conv_w, conv_b.reshape(1, M_WIDTH), m_qkv[0], m_qkv[1],
      m_qkv[1].transpose(0, 2, 1),
      m_qkv[2], gb, norm_g.reshape(1, M_WIDTH), tril, selrow)
    return out, (c_new, n_new[:, :M_HEADS], m_new[:, :M_HEADS, 0], conv_new[:, :CONV_W - 1])


def rel_bucket(dist):
    d = jnp.maximum(dist, 0)
    exact = REL_BUCKETS // 2
    log_part = exact + (jnp.log(jnp.maximum(d, 1).astype(F32) / exact)
                        / math.log(REL_MAX_DIST / exact) * (REL_BUCKETS - exact)).astype(jnp.int32)
    return jnp.where(d < exact, d, jnp.minimum(log_part, REL_BUCKETS - 1))


def _split_w_in(w_in):
    offs = np.cumsum((0,) + IN_WIDTHS)
    w_a = w_in[:, :offs[7]]
    w_small = jnp.concatenate([w_in[:, offs[7]:offs[8]], w_in[:, offs[10]:offs[11]]], axis=1)
    w_small = jnp.pad(w_small, ((0, 0), (0, LANE - w_small.shape[1])))
    w_u = w_in[:, offs[8]:offs[10]]
    w_g = w_in[:, offs[11]:]
    return w_a, w_small, w_u, w_g


def mixer(x, norm_g, prm, past):
    b, t, _ = x.shape
    x2 = x.reshape(b * t, D_MODEL)
    w_a, w_small, w_u, w_g = _split_w_in(prm['w_in'])
    z_a = _mm(x2, w_a, g=norm_g).reshape(b, t, -1)
    z_s = _mm(x2, w_small, g=norm_g).reshape(b, t, -1)
    z_u = _mm(x2, w_u, g=norm_g).reshape(b, t, -1)
    z_g = _mm(x2, w_g, g=norm_g)
    new_cmp, new_slc, win_rows = (z_a[..., NSA_WIDTH + 2 * KV_WIDTH * i:NSA_WIDTH + 2 * KV_WIDTH * (i + 1)]
                                  .reshape(b, t, 2, NSA_KV_HEADS, HEAD_DIM) for i in range(3))
    g_a, g_b = z_g[:, :D_MODEL], z_g[:, D_MODEL:]
    if past is None:
        o_nsa = _nsa_prompt_from_proj(z_a.reshape(b * t, -1), z_s.reshape(b * t, -1), b, t, prm['prompt_tables'],
                                      prm['cmp_pe'], prm['cmp_w1'], prm['cmp_w2'])
        new_win = win_rows[:, t - min(WINDOW, t):]
        conv_prev = jnp.zeros((b, CONV_W - 1, M_WIDTH), F32)
        m_state = (jnp.zeros((b, M_HEADS, M_HEAD_DIM, M_HEAD_DIM), F32),
                   jnp.zeros((b, M_HEADS, M_HEAD_DIM), F32),
                   jnp.zeros((b, M_HEADS), F32))
    else:
        layer = past['layer']
        assert t == T_SAMPLE and past['page_table'].shape[1] == N_PAGES and past['cmp'].shape[2] == PAGE
        o_nsa = _nsa_sample(z_a, z_s, past['cmp'], past['slc'], past['win'], past['page_table'], layer,
                            prm['sample_tables'], prm['cmp_pe'], prm['cmp_w1'], prm['cmp_w2'])
        o_nsa = o_nsa.reshape(b * t, NSA_WIDTH)
        new_win = jnp.concatenate([past['win'][layer][:, t:], win_rows], axis=1)
        conv_prev = past['conv']
        m_state = (past['C'].astype(F32), past['n'].astype(F32), past['m'].astype(F32))
    o_m, (c_new, n_new, m_new, conv_new) = _mlstm(z_u, z_s, m_state, conv_prev, prm['conv_w'], prm['conv_b'],
                                                  prm['m_qkv'], prm['gate_b'], prm['m_norm'])
    merged = (jax.nn.sigmoid(g_a) * _mm(o_nsa, prm['w_up_a'])
              + jax.nn.sigmoid(g_b) * _mm(o_m.reshape(b * t, M_WIDTH), prm['w_up_b']))
    y = _mm(merged, prm['w_out'], res=x2).reshape(b, t, D_MODEL)
    return y, (new_cmp, new_slc, new_win, c_new, n_new, m_new, conv_new)


def _channel_mixer(x, g, l, ffn_w1, ffn_w3, ffn_w2, moe_router, moe_w1, moe_w3, moe_w2):
    b, t, d = x.shape
    x2 = x.reshape(b * t, d)
    i = l // 2
    if l % 2 == 0:
        y = _ffn_dense(x2, g, ffn_w1[i], ffn_w3[i], ffn_w2[i])
    else:
        y = _moe(x2, g, moe_router[i], moe_w1[i], moe_w3[i], moe_w2[i])
    return y.reshape(b, t, d)


def _final_norm(x, g):
    xf = x.astype(F32)
    return xf * lax.rsqrt(jnp.mean(xf * xf, axis=-1, keepdims=True) + EPS) * g


def kernel(x_prompt, x_sample, cache_cmp_kv, cache_slc_kv, cache_win_kv, state_mlstm_C, state_mlstm_n,
           state_mlstm_m, state_mlstm_conv, page_table, rel_bias_table, norm_mix, norm_ffn, norm_final,
           w_in, cmp_pe, cmp_w1, cmp_w2, m_conv_w, m_conv_b, m_qkv, m_gate_bias, m_norm, w_up_a, w_up_b,
           w_out, ffn_w1, ffn_w3, ffn_w2, moe_router, moe_w1, moe_w3, moe_w2):
    xp, xs = x_prompt, x_sample
    prompt_states, sample_states = [], []
    prompt_tables = _prompt_bias_tables(rel_bias_table)
    sample_tables = _sample_bias_tables(rel_bias_table)
    for l in range(DEPTH):
        prm = {'prompt_tables': prompt_tables, 'sample_tables': sample_tables, 'w_in': w_in[l], 'cmp_pe': cmp_pe[l],
               'cmp_w1': cmp_w1[l], 'cmp_w2': cmp_w2[l], 'conv_w': m_conv_w[l], 'conv_b': m_conv_b[l],
               'm_qkv': m_qkv[l], 'gate_b': m_gate_bias[l], 'm_norm': m_norm[l], 'w_up_a': w_up_a[l],
               'w_up_b': w_up_b[l], 'w_out': w_out[l]}
        past = {'cmp': cache_cmp_kv, 'slc': cache_slc_kv, 'win': cache_win_kv, 'layer': l,
                'C': state_mlstm_C[l], 'n': state_mlstm_n[l], 'm': state_mlstm_m[l],
                'conv': state_mlstm_conv[l], 'page_table': page_table}
        xp, st_p = mixer(xp, norm_mix[l], prm, None)
        xs, st_s = mixer(xs, norm_mix[l], prm, past)
        xp = _channel_mixer(xp, norm_ffn[l], l, ffn_w1, ffn_w3, ffn_w2, moe_router, moe_w1, moe_w3, moe_w2)
        xs = _channel_mixer(xs, norm_ffn[l], l, ffn_w1, ffn_w3, ffn_w2, moe_router, moe_w1, moe_w3, moe_w2)
        prompt_states.append(st_p)
        sample_states.append(st_s)
    y_prompt = _final_norm(xp, norm_final)
    y_sample = _final_norm(xs, norm_final)
    ps = [jnp.stack([s[i] for s in prompt_states]) for i in range(7)]
    ss = [jnp.stack([s[i] for s in sample_states]) for i in range(7)]
    return (y_prompt, y_sample, ps[0], ps[1], ps[2], ps[3], ps[4], ps[5], ps[6],
            ss[0], ss[1], ss[2], ss[3], ss[4], ss[5], ss[6])
```

```python
import functools
import math

import jax
import jax.numpy as jnp
import numpy as np
from jax import lax
from jax.experimental import pallas as pl
from jax.experimental.pallas import tpu as pltpu

D_MODEL = 1024
DEPTH = 2
NSA_HEADS = 8
NSA_KV_HEADS = 2
NSA_GROUP = NSA_HEADS // NSA_KV_HEADS
HEAD_DIM = 64
NSA_WIDTH = NSA_HEADS * HEAD_DIM
KV_WIDTH = NSA_KV_HEADS * HEAD_DIM
CMP_BLOCK = 32
CMP_STRIDE = 16
CMP_HIDDEN = 128
SEL_BLOCK = 64
SEL_TOP_N = 16
SEL_FORCE = 1e4
WINDOW = 512
WIN_Q_BLOCK = 128
REL_BUCKETS = 32
REL_MAX_DIST = 128
M_HEADS = 4
M_HEAD_DIM = 128
M_WIDTH = M_HEADS * M_HEAD_DIM
CONV_W = 4
M_CHUNK = 64
D_FF = 2816
N_EXPERTS = 8
TOP_K = 2
EPS = 1e-6
IN_WIDTHS = (NSA_WIDTH, KV_WIDTH, KV_WIDTH, KV_WIDTH, KV_WIDTH, KV_WIDTH, KV_WIDTH, 3 * NSA_HEADS, M_WIDTH, M_WIDTH,
             2 * M_HEADS, D_MODEL, D_MODEL)

VMEM_LIMIT_V7X = 52 * 1024 * 1024
LANE = 128

F32 = jnp.float32
BF16 = jnp.bfloat16


def _split3(x):
    hi = x.astype(BF16)
    r1 = x - hi.astype(F32)
    mid = r1.astype(BF16)
    lo = (r1 - mid.astype(F32)).astype(BF16)
    return hi, mid, lo


def _pick_tile(n, cands):
    for c in cands:
        if n % c == 0:
            return c
    return n


def _mm_kernel(*refs, norm, has_res):
    x_ref, g_ref, w_ref = refs[:3]
    res_ref = refs[3] if has_res else None
    o_ref, xs_ref = refs[-2], refs[-1]

    @pl.when(pl.program_id(1) == 0)
    def _():
        x = x_ref[...]
        if norm:
            x = x * lax.rsqrt(jnp.mean(x * x, axis=-1, keepdims=True) + EPS) * g_ref[...]
        xs_ref[...] = x.astype(BF16)

    acc = jnp.dot(xs_ref[...], w_ref[...].astype(BF16), preferred_element_type=F32)
    if has_res:
        acc = acc + res_ref[...]
    o_ref[...] = acc


def _mm(x, w, g=None, res=None):
    m, k = x.shape
    n = w.shape[1]
    tm = _pick_tile(m, (1024, 512, 256, 128))
    tn = _pick_tile(n, (512, 256, 128))
    norm = g is not None
    gg = (g if norm else jnp.ones((k,), F32)).reshape(1, k)
    in_specs = [pl.BlockSpec((tm, k), lambda i, j: (i, 0)),
                pl.BlockSpec((1, k), lambda i, j: (0, 0)),
                pl.BlockSpec((k, tn), lambda i, j: (0, j))]
    args = [x, gg, w]
    if res is not None:
        in_specs.append(pl.BlockSpec((tm, tn), lambda i, j: (i, j)))
        args.append(res)
    return pl.pallas_call(
        functools.partial(_mm_kernel, norm=norm, has_res=res is not None),
        grid=(m // tm, n // tn),
        in_specs=in_specs,
        out_specs=pl.BlockSpec((tm, tn), lambda i, j: (i, j)),
        out_shape=jax.ShapeDtypeStruct((m, n), F32),
        scratch_shapes=[pltpu.VMEM((tm, k), BF16)],
        compiler_params=pltpu.CompilerParams(dimension_semantics=("arbitrary", "arbitrary"),
                                             vmem_limit_bytes=VMEM_LIMIT_V7X),
        name="mm",
    )(*args)


def _proj_kernel(x_ref, g_ref, *refs):
    n_out = len(refs) // 2
    x = x_ref[...]
    xn = (x * lax.rsqrt(jnp.mean(x * x, axis=-1, keepdims=True) + EPS) * g_ref[...]).astype(BF16)
    for w_ref, o_ref in zip(refs[:n_out], refs[n_out:]):
        o_ref[...] = jnp.dot(xn, w_ref[...], preferred_element_type=F32)


def _proj(x, g, weights):
    m, k = x.shape
    tm = _pick_tile(m, (256, 128))
    row = lambda n: pl.BlockSpec((tm, n), lambda i: (i, 0))
    whole = lambda a: pl.BlockSpec(a.shape, lambda i: (0,) * a.ndim)
    return pl.pallas_call(
        _proj_kernel,
        grid=(m // tm,),
        in_specs=[row(k), whole(g.reshape(1, k))] + [whole(w) for w in weights],
        out_specs=[row(w.shape[1]) for w in weights],
        out_shape=[jax.ShapeDtypeStruct((m, w.shape[1]), F32) for w in weights],
        compiler_params=pltpu.CompilerParams(dimension_semantics=("arbitrary",), vmem_limit_bytes=VMEM_LIMIT_V7X),
        name="proj",
    )(x, g.reshape(1, k), *weights)


def _merge_out_kernel(x_ref, oa_ref, ob_ref, zg_ref, wa_ref, wb_ref, wo_ref, o_ref):
    d = x_ref.shape[-1]
    up_a = jnp.dot(oa_ref[...].astype(BF16), wa_ref[...], preferred_element_type=F32)
    up_b = jnp.dot(ob_ref[...].astype(BF16), wb_ref[...], preferred_element_type=F32)
    merged = jax.nn.sigmoid(zg_ref[:, :d]) * up_a + jax.nn.sigmoid(zg_ref[:, d:]) * up_b
    o_ref[...] = x_ref[...] + jnp.dot(merged.astype(BF16), wo_ref[...], preferred_element_type=F32)


def _merge_out(x, o_a, o_b, z_g, w_up_a, w_up_b, w_out):
    m, d = x.shape
    tm = _pick_tile(m, (512, 256, 128))
    row = lambda a: pl.BlockSpec((tm, a.shape[1]), lambda i: (i, 0))
    whole = lambda a: pl.BlockSpec(a.shape, lambda i: (0,) * a.ndim)
    return pl.pallas_call(
        _merge_out_kernel,
        grid=(m // tm,),
        in_specs=[row(x), row(o_a), row(o_b), row(z_g), whole(w_up_a), whole(w_up_b), whole(w_out)],
        out_specs=row(x),
        out_shape=jax.ShapeDtypeStruct((m, d), F32),
        compiler_params=pltpu.CompilerParams(dimension_semantics=("arbitrary",), vmem_limit_bytes=VMEM_LIMIT_V7X),
        name="merge_out",
    )(x, o_a, o_b, z_g, w_up_a, w_up_b, w_out)


def _ffn_body(x_ref, g_ref, w1_ref, w3_ref, w2_ref, o_ref, xs_ref, acc_ref, *, residual, grouped):
    j = pl.program_id(1)

    @pl.when(j == 0)
    def _():
        x = x_ref[...]
        xn = x * lax.rsqrt(jnp.mean(x * x, axis=-1, keepdims=True) + EPS) * g_ref[...]
        xs_ref[...] = xn.astype(BF16)
        acc_ref[...] = jnp.zeros_like(acc_ref)

    xs = xs_ref[...]
    w1 = w1_ref[0] if grouped else w1_ref[...]
    w3 = w3_ref[0] if grouped else w3_ref[...]
    w2 = w2_ref[0] if grouped else w2_ref[...]
    a = jnp.dot(xs, w1.astype(BF16), preferred_element_type=F32)
    b = jnp.dot(xs, w3.astype(BF16), preferred_element_type=F32)
    h = (a * jax.nn.sigmoid(a) * b).astype(BF16)
    acc_ref[...] += jnp.dot(h, w2.astype(BF16), preferred_element_type=F32)

    @pl.when(j == pl.num_programs(1) - 1)
    def _():
        if residual:
            o_ref[...] = x_ref[...] + acc_ref[...]
        else:
            o_ref[...] = acc_ref[...]


def _ffn_dense_kernel(x_ref, g_ref, w1_ref, w3_ref, w2_ref, o_ref, xs_ref, acc_ref):
    _ffn_body(x_ref, g_ref, w1_ref, w3_ref, w2_ref, o_ref, xs_ref, acc_ref, residual=True, grouped=False)


def _ffn_grouped_kernel(be_ref, x_ref, g_ref, w1_ref, w3_ref, w2_ref, o_ref, xs_ref, acc_ref):
    del be_ref
    _ffn_body(x_ref, g_ref, w1_ref, w3_ref, w2_ref, o_ref, xs_ref, acc_ref, residual=False, grouped=True)


def _ffn_dense(x, g, w1, w3, w2):
    m, d = x.shape
    f = w1.shape[1]
    tm = _pick_tile(m, (1024, 512, 256, 128))
    tf = _pick_tile(f, (256, 128))
    return pl.pallas_call(
        _ffn_dense_kernel,
        grid=(m // tm, f // tf),
        in_specs=[pl.BlockSpec((tm, d), lambda i, j: (i, 0)),
                  pl.BlockSpec((1, d), lambda i, j: (0, 0)),
                  pl.BlockSpec((d, tf), lambda i, j: (0, j)),
                  pl.BlockSpec((d, tf), lambda i, j: (0, j)),
                  pl.BlockSpec((tf, d), lambda i, j: (j, 0))],
        out_specs=pl.BlockSpec((tm, d), lambda i, j: (i, 0)),
        out_shape=jax.ShapeDtypeStruct((m, d), F32),
        scratch_shapes=[pltpu.VMEM((tm, d), BF16), pltpu.VMEM((tm, d), F32)],
        compiler_params=pltpu.CompilerParams(dimension_semantics=("arbitrary", "arbitrary"),
                                             vmem_limit_bytes=VMEM_LIMIT_V7X),
        name="ffn_dense",
    )(x, g.reshape(1, d), w1, w3, w2)


def _ffn_grouped(xd, blk_e, g, w1, w3, w2, tm):
    rows, d = xd.shape
    f = w1.shape[2]
    tf = _pick_tile(f, (256, 128))
    grid_spec = pltpu.PrefetchScalarGridSpec(
        num_scalar_prefetch=1,
        grid=(rows // tm, f // tf),
        in_specs=[pl.BlockSpec((tm, d), lambda i, j, be: (i, 0)),
                  pl.BlockSpec((1, d), lambda i, j, be: (0, 0)),
                  pl.BlockSpec((1, d, tf), lambda i, j, be: (be[i], 0, j)),
                  pl.BlockSpec((1, d, tf), lambda i, j, be: (be[i], 0, j)),
                  pl.BlockSpec((1, tf, d), lambda i, j, be: (be[i], j, 0))],
        out_specs=pl.BlockSpec((tm, d), lambda i, j, be: (i, 0)),
        scratch_shapes=[pltpu.VMEM((tm, d), BF16), pltpu.VMEM((tm, d), F32)],
    )
    return pl.pallas_call(
        _ffn_grouped_kernel,
        grid_spec=grid_spec,
        out_shape=jax.ShapeDtypeStruct((rows, d), F32),
        compiler_params=pltpu.CompilerParams(dimension_semantics=("arbitrary", "arbitrary"),
                                             vmem_limit_bytes=VMEM_LIMIT_V7X),
        name="ffn_grouped",
    )(blk_e, xd, g.reshape(1, d), w1, w3, w2)


def _moe(x, g, router, w1, w3, w2):
    n, d = x.shape
    tm = 1024 if n >= 8192 else 128
    router_p = jnp.pad(router, ((0, 0), (0, LANE - N_EXPERTS)))
    logits = _mm(x, router_p, g=g)[:, :N_EXPERTS]
    top_val, top_idx = lax.top_k(logits, TOP_K)
    gate = jax.nn.softmax(top_val, axis=-1).reshape(-1)
    e_flat = top_idx.reshape(-1)
    n_asg = n * TOP_K
    order = jnp.argsort(e_flat)
    e_sorted = e_flat[order]
    counts = jnp.bincount(e_flat, length=N_EXPERTS)
    padded = (counts + tm - 1) // tm * tm
    pad_end = jnp.cumsum(padded)
    pad_start = pad_end - padded
    start = jnp.cumsum(counts) - counts
    dest = pad_start[e_sorted] + jnp.arange(n_asg) - start[e_sorted]
    n_blocks = n_asg // tm + N_EXPERTS
    src_tok = jnp.zeros((n_blocks * tm,), jnp.int32).at[dest].set((order // TOP_K).astype(jnp.int32))
    pos = jnp.zeros((n_asg,), jnp.int32).at[order].set(dest.astype(jnp.int32))
    blk_e = jnp.minimum(jnp.searchsorted(pad_end, jnp.arange(n_blocks) * tm, side='right'),
                        N_EXPERTS - 1).astype(jnp.int32)
    yd = _ffn_grouped(x[src_tok], blk_e, g, w1, w3, w2, tm)
    contrib = yd[pos] * gate[:, None]
    return x + contrib.reshape(n, TOP_K, d).sum(axis=1)


N_CHUNK = 128
CHUNK_W = CMP_STRIDE * HEAD_DIM


def _compress_kernel(x_ref, pe_ref, w1_ref, w2_ref, o_ref):
    c = x_ref[0]
    lo = jnp.dot((c + pe_ref[0:1]).astype(BF16), w1_ref[0].astype(BF16), preferred_element_type=F32)
    hi = jnp.dot((c + pe_ref[1:2]).astype(BF16), w1_ref[1].astype(BF16), preferred_element_type=F32)
    hid = jax.nn.gelu(lo + pltpu.roll(hi, N_CHUNK - 1, 0))
    out = jnp.dot(hid.astype(BF16), w2_ref[...].astype(BF16), preferred_element_type=F32)
    row = lax.broadcasted_iota(jnp.int32, out.shape, 0)
    o_ref[0] = jnp.where(row < N_CHUNK - 1, out, 0.0)


def _compress(xc, pe, w1, w2):
    nb = xc.shape[0]
    hidden = w1.shape[-1]
    return pl.pallas_call(
        _compress_kernel,
        grid=(nb,),
        in_specs=[pl.BlockSpec((1, N_CHUNK, CHUNK_W), lambda i: (i, 0, 0)),
                  pl.BlockSpec((2, CHUNK_W), lambda i: (0, 0)),
                  pl.BlockSpec((2, CHUNK_W, hidden), lambda i: (0, 0, 0)),
                  pl.BlockSpec((hidden, HEAD_DIM), lambda i: (0, 0))],
        out_specs=pl.BlockSpec((1, N_CHUNK, HEAD_DIM), lambda i: (i, 0, 0)),
        out_shape=jax.ShapeDtypeStruct((nb, N_CHUNK, HEAD_DIM), F32),
        compiler_params=pltpu.CompilerParams(dimension_semantics=("arbitrary",), vmem_limit_bytes=VMEM_LIMIT_V7X),
        name="compress",
    )(xc, pe.reshape(2, CHUNK_W), w1.reshape(2, CHUNK_W, hidden), w2)


def _compress_heads(kv, pe, w1, w2):
    b = kv.shape[0]
    xc = kv.reshape(b, N_CHUNK, CMP_STRIDE, NSA_KV_HEADS, HEAD_DIM).transpose(0, 3, 1, 2, 4)
    out = _compress(xc.reshape(b * NSA_KV_HEADS, N_CHUNK, CHUNK_W), pe, w1, w2)
    return out.reshape(b, NSA_KV_HEADS, N_CHUNK, HEAD_DIM)


TQ = 128
TK = 128
NEG = -1e30
N_BIAS_TILES = WINDOW // TK + 1
MASKED_TILE = N_BIAS_TILES
SEL_SUB = 4


def _nsa_prompt_kernel(q_ref, kct_ref, vcc_ref, kst_ref, vs_ref, kwt_ref, vw_ref, bcmp_ref, btile_ref, gate_ref,
                       ov_ref, exp_ref, o_ref, selneg_ref, m_ref, l_ref, acc_ref):
    i = pl.program_id(2)
    rows = NSA_GROUP * TQ
    q = q_ref[0, 0].reshape(rows, HEAD_DIM)

    s = jnp.dot(q, kct_ref[0, 0], preferred_element_type=F32) + bcmp_ref[0].reshape(rows, N_CHUNK)
    m = jnp.max(s, axis=-1, keepdims=True)
    e = jnp.where(s > 0.1 * NEG, jnp.exp(s - m), 0.0)
    p = e / jnp.maximum(jnp.sum(e, axis=-1, keepdims=True), 1e-30)
    o_cmp = jnp.dot(p.astype(BF16), vcc_ref[0, 0], preferred_element_type=F32)

    p_sum = p[0:TQ] + p[TQ:2 * TQ] + p[2 * TQ:3 * TQ] + p[3 * TQ:4 * TQ]
    imp = sum(jnp.dot(part, ov_ref[...], preferred_element_type=F32) for part in _split3(p_sum))
    lane = lax.broadcasted_iota(jnp.int32, (TQ, LANE), 1)
    tok = lax.broadcasted_iota(jnp.int32, (TQ, LANE), 0) + i * TQ
    cur = tok // SEL_BLOCK
    valid = lane <= cur
    forced = (lane == 0) | (lane == cur) | (lane == cur - 1)
    score = jnp.where(valid, imp + jnp.where(forced, SEL_FORCE, 0.0), -1.0)
    rank = jnp.zeros((TQ, LANE), F32)
    for c in range(T_PROMPT // SEL_BLOCK):
        sc = score[:, c:c + 1]
        beats = (sc > score) | ((sc == score) & (lane > c))
        rank = rank + jnp.where(beats, 1.0, 0.0)
    sel = jnp.where(valid & (rank < SEL_TOP_N), 1.0, 0.0).astype(BF16)
    sel_keys = jnp.dot(sel, exp_ref[...], preferred_element_type=F32)
    selneg_ref[...] = (sel_keys - 1.0) * (-NEG)

    def masked_scores(kt_ref, first_tile, n_sub, selected):
        k0 = pl.multiple_of(first_tile * TK, TK)
        s_all = jnp.dot(q, kt_ref[0, 0, :, pl.ds(k0, n_sub * TK)], preferred_element_type=F32)
        pieces = []
        for u in range(n_sub):
            d0 = i - (first_tile + u)
            idx = jnp.where(d0 < 0, MASKED_TILE, jnp.minimum(d0, 2) if selected else d0)
            piece = s_all[:, u * TK:(u + 1) * TK].reshape(NSA_GROUP, TQ, TK) + btile_ref[0, idx]
            if selected:
                piece = piece + selneg_ref[:, pl.ds(pl.multiple_of(k0 + u * TK, TK), TK)][None]
            pieces.append(piece.reshape(rows, TK))
        return pieces, k0

    def row_max(pieces):
        return jnp.max(functools.reduce(jnp.maximum, pieces), axis=-1, keepdims=True)

    def row_sum(pieces):
        return jnp.sum(functools.reduce(jnp.add, pieces), axis=-1, keepdims=True)

    m_ref[...] = jnp.full(m_ref.shape, -jnp.inf, F32)
    l_ref[...] = jnp.zeros(l_ref.shape, F32)
    acc_ref[...] = jnp.zeros(acc_ref.shape, F32)

    def sel_group(gi, carry):
        pieces, k0 = masked_scores(kst_ref, gi * SEL_SUB, SEL_SUB, True)
        m_old = m_ref[...]
        m_new = jnp.maximum(m_old, row_max(pieces))
        pt = [jnp.exp(piece - m_new) for piece in pieces]
        alpha = jnp.exp(m_old - m_new)
        l_ref[...] = alpha * l_ref[...] + row_sum(pt)
        pv = jnp.dot(jnp.concatenate([x.astype(BF16) for x in pt], axis=1), vs_ref[0, 0, pl.ds(k0, SEL_SUB * TK), :],
                     preferred_element_type=F32)
        acc_ref[...] = alpha * acc_ref[...] + pv
        m_ref[...] = m_new
        return carry

    lax.fori_loop(0, i // SEL_SUB + 1, sel_group, 0)
    o_sel = acc_ref[...] / l_ref[...]

    pieces, k0 = masked_scores(kwt_ref, jnp.maximum(i - (N_BIAS_TILES - 1), 0), N_BIAS_TILES, False)
    m_win = row_max(pieces)
    pt = [jnp.exp(piece - m_win) for piece in pieces]
    o_win = jnp.dot(jnp.concatenate([x.astype(BF16) for x in pt], axis=1),
                    vw_ref[0, 0, pl.ds(k0, N_BIAS_TILES * TK), :], preferred_element_type=F32) / row_sum(pt)

    g = jax.nn.sigmoid(gate_ref[0, 0])
    for a in range(NSA_GROUP):
        r = slice(a * TQ, (a + 1) * TQ)
        o_ref[0, 0, a] = (g[:, 3 * a:3 * a + 1] * o_cmp[r] + g[:, 3 * a + 1:3 * a + 2] * o_sel[r]
                          + g[:, 3 * a + 2:3 * a + 3] * o_win[r])


T_PROMPT = 2048


def _bias_lookup(table, dist):
    oh = jax.nn.one_hot(rel_bucket(dist), REL_BUCKETS, dtype=F32)
    return jnp.einsum('...r,rh->...h', oh, table, precision=lax.Precision.HIGHEST)


def _prompt_bias_tables(table):
    t = T_PROMPT
    q_pos = jnp.arange(t)
    block_end = jnp.arange(N_CHUNK) * CMP_STRIDE + (CMP_BLOCK - 1)
    dist = q_pos[:, None] - block_end[None, :]
    ok = (dist >= 0) & (jnp.arange(N_CHUNK)[None, :] < N_CHUNK - 1)
    bcmp = jnp.where(ok[..., None], _bias_lookup(table, dist), NEG)
    bcmp = bcmp.reshape(t, N_CHUNK, NSA_KV_HEADS, NSA_GROUP).transpose(2, 3, 0, 1)
    d0 = jnp.arange(N_BIAS_TILES)[:, None, None]
    dist = d0 * TK + jnp.arange(TQ)[None, :, None] - jnp.arange(TK)[None, None, :]
    ok = (dist >= 0) & (dist < WINDOW)
    bt = jnp.where(ok[..., None], _bias_lookup(table, dist), NEG)
    bt = bt.reshape(N_BIAS_TILES, TQ, TK, NSA_KV_HEADS, NSA_GROUP).transpose(3, 0, 4, 1, 2)
    bt = jnp.concatenate([bt, jnp.full((NSA_KV_HEADS, 1, NSA_GROUP, TQ, TK), NEG, F32)], axis=1)
    c0 = jnp.arange(N_CHUNK) * CMP_STRIDE
    s0 = jnp.arange(LANE) * SEL_BLOCK
    ov = jnp.clip(jnp.minimum(c0[:, None] + CMP_BLOCK, s0[None, :] + SEL_BLOCK)
                  - jnp.maximum(c0[:, None], s0[None, :]), 0, CMP_BLOCK).astype(F32) / CMP_BLOCK
    ov = jnp.where((jnp.arange(N_CHUNK)[:, None] < N_CHUNK - 1) & (jnp.arange(LANE)[None, :] < t // SEL_BLOCK), ov, 0.0)
    expand = (jnp.arange(LANE)[:, None] == (jnp.arange(t) // SEL_BLOCK)[None, :]).astype(BF16)
    return bcmp, bt, ov.astype(BF16), expand


def _nsa_prompt(q_h, kct, vcc, kst, vs, kwt, vw, gates, tables):
    bcmp, bt, ov, expand = tables
    b, _, _, t, _ = q_h.shape
    rows = NSA_GROUP * TQ
    per_bh = lambda *blk: pl.BlockSpec((1, 1) + blk, lambda bi, h, i: (bi, h) + (0,) * len(blk))
    return pl.pallas_call(
        _nsa_prompt_kernel,
        grid=(b, NSA_KV_HEADS, t // TQ),
        in_specs=[pl.BlockSpec((1, 1, NSA_GROUP, TQ, HEAD_DIM), lambda bi, h, i: (bi, h, 0, i, 0)),
                  per_bh(HEAD_DIM, N_CHUNK), per_bh(N_CHUNK, HEAD_DIM),
                  per_bh(HEAD_DIM, t), per_bh(t, HEAD_DIM), per_bh(HEAD_DIM, t), per_bh(t, HEAD_DIM),
                  pl.BlockSpec((1, NSA_GROUP, TQ, N_CHUNK), lambda bi, h, i: (h, 0, i, 0)),
                  pl.BlockSpec((1, N_BIAS_TILES + 1, NSA_GROUP, TQ, TK), lambda bi, h, i: (h, 0, 0, 0, 0)),
                  pl.BlockSpec((1, 1, TQ, 3 * NSA_GROUP), lambda bi, h, i: (bi, h, i, 0)),
                  pl.BlockSpec((N_CHUNK, LANE), lambda bi, h, i: (0, 0)),
                  pl.BlockSpec((LANE, t), lambda bi, h, i: (0, 0))],
        out_specs=pl.BlockSpec((1, 1, NSA_GROUP, TQ, HEAD_DIM), lambda bi, h, i: (bi, h, 0, i, 0)),
        out_shape=jax.ShapeDtypeStruct((b, NSA_KV_HEADS, NSA_GROUP, t, HEAD_DIM), F32),
        scratch_shapes=[pltpu.VMEM((TQ, t), F32), pltpu.VMEM((rows, 1), F32), pltpu.VMEM((rows, 1), F32),
                        pltpu.VMEM((rows, HEAD_DIM), F32)],
        compiler_params=pltpu.CompilerParams(dimension_semantics=("arbitrary", "arbitrary", "arbitrary"),
                                             vmem_limit_bytes=VMEM_LIMIT_V7X),
        name="nsa_prompt",
    )(q_h, kct, vcc, kst, vs, kwt, vw, bcmp, bt, gates, ov, expand)


def _heads_t(x, b, t):
    return x.astype(BF16).reshape(b, t, NSA_KV_HEADS, HEAD_DIM).transpose(0, 2, 3, 1)


def _heads(x, b, t):
    return x.astype(BF16).reshape(b, t, NSA_KV_HEADS, HEAD_DIM).transpose(0, 2, 1, 3)


def _nsa_prompt_from_proj(z_a, z_s, b, t, tables, pe, w1, w2):
    col = lambda i: z_a[:, NSA_WIDTH + KV_WIDTH * i:NSA_WIDTH + KV_WIDTH * (i + 1)]
    q_h = (z_a[:, :NSA_WIDTH] * (HEAD_DIM ** -0.5)).astype(BF16)
    q_h = q_h.reshape(b, t, NSA_KV_HEADS, NSA_GROUP, HEAD_DIM).transpose(0, 2, 3, 1, 4)
    kcc = _compress_heads(col(0).reshape(b, t, NSA_KV_HEADS, HEAD_DIM), pe[0], w1[0], w2[0])
    vcc = _compress_heads(col(1).reshape(b, t, NSA_KV_HEADS, HEAD_DIM), pe[1], w1[1], w2[1])
    gates = z_s[:, :3 * NSA_HEADS].reshape(b, t, NSA_KV_HEADS, 3 * NSA_GROUP).transpose(0, 2, 1, 3)
    o = _nsa_prompt(q_h, kcc.astype(BF16).transpose(0, 1, 3, 2), vcc.astype(BF16),
                    _heads_t(col(2), b, t), _heads(col(3), b, t), _heads_t(col(4), b, t), _heads(col(5), b, t),
                    gates, tables)
    return o.transpose(0, 3, 1, 2, 4).reshape(b * t, NSA_WIDTH)


T_SAMPLE = 4
T_PAD = 8
PAGE = 128
N_PAGES = 16
PAST = N_PAGES * PAGE
N_SEL_SAMPLE = -(-(PAST + T_SAMPLE) // SEL_BLOCK)


def _nsa_sample_kernel(pt_ref, *refs):
    del pt_ref
    cp, sp = refs[:N_PAGES], refs[N_PAGES:2 * N_PAGES]
    (wb_ref, za_ref, zs_ref, pe2_ref, w1_ref, w2_ref, bcmp_ref, bpast_ref, bwin_ref, bnew_ref, ov_ref, exp_ref,
     o_ref, new_ref, q_ref, g_ref, xcat_ref, kcat_ref, vcat_ref, s_ref, e_ref, oacc_ref) = refs[2 * N_PAGES:]
    rows = NSA_GROUP * T_PAD

    @pl.when(pl.program_id(0) == 0)
    def _():
        new_ref[...] = jnp.zeros(new_ref.shape, F32)
        q_ref[...] = jnp.zeros(q_ref.shape, F32)
        g_ref[...] = jnp.zeros(g_ref.shape, F32)

    new_ref[0:T_SAMPLE, :] = za_ref[0, :, NSA_WIDTH + 2 * KV_WIDTH:]
    q_ref[0:T_SAMPLE, :] = za_ref[0, :, :NSA_WIDTH] * (HEAD_DIM ** -0.5)
    g_ref[0:T_SAMPLE, :] = zs_ref[0]

    comp = []
    for kv in range(2):
        def place_row_offset(l, carry, kv=kv):
            xl = jnp.concatenate([cp[p][0, pl.ds(2 * l + kv, PAGE // CMP_STRIDE, stride=2 * CMP_STRIDE), :]
                                  for p in range(N_PAGES)], axis=0)
            lanes = pl.ds(pl.multiple_of(l * KV_WIDTH, KV_WIDTH), KV_WIDTH)
            for half in range(2):
                xcat_ref[half, :, lanes] = (xl + pe2_ref[kv, pl.ds(l + CMP_STRIDE * half, 1), :]).astype(BF16)
            return carry

        lax.fori_loop(0, CMP_STRIDE, place_row_offset, 0)
        lo = jnp.dot(xcat_ref[0], w1_ref[kv, 0], preferred_element_type=F32)
        hi = jnp.dot(xcat_ref[1], w1_ref[kv, 1], preferred_element_type=F32)
        hid = jax.nn.gelu(lo + pltpu.roll(hi, N_CHUNK - 1, 0))
        out = jnp.dot(hid.astype(BF16), w2_ref[kv], preferred_element_type=F32)
        row = lax.broadcasted_iota(jnp.int32, out.shape, 0)
        comp.append(jnp.where(row < N_CHUNK - 1, out, 0.0).astype(BF16))

    w_buf = wb_ref.shape[1] // 2
    for p in range(N_PAGES):
        kcat_ref[p * PAGE:(p + 1) * PAGE, :] = sp[p][0, pl.ds(0, PAGE, stride=2), :].astype(BF16)
        vcat_ref[p * PAGE:(p + 1) * PAGE, :] = sp[p][0, pl.ds(1, PAGE, stride=2), :].astype(BF16)
    kcat_ref[PAST:, :] = new_ref[:, 0:KV_WIDTH].astype(BF16)
    vcat_ref[PAST:, :] = new_ref[:, KV_WIDTH:2 * KV_WIDTH].astype(BF16)
    kw_buf = wb_ref[0, pl.ds(0, w_buf, stride=2), :].astype(BF16)
    vw_buf = wb_ref[0, pl.ds(1, w_buf, stride=2), :].astype(BF16)
    kw_new, vw_new = (new_ref[:, i * KV_WIDTH:(i + 1) * KV_WIDTH].astype(BF16) for i in (2, 3))
    gates = jax.nn.sigmoid(g_ref[...])
    nt = (((1,), (1,)), ((), ()))
    zeros64 = jnp.zeros((rows, HEAD_DIM), F32)
    pieces = []
    key_chunk = 512

    def attend_window(qp, h):
        s_parts = [lax.dot_general(qp, k, nt, preferred_element_type=F32) + bias
                   for k, bias in ((kw_buf, bwin_ref[h]), (kw_new, bnew_ref[h]))]
        m = jnp.maximum(*[jnp.max(s, axis=-1, keepdims=True) for s in s_parts])
        e_parts = [jnp.exp(s - m) for s in s_parts]
        den = jnp.add(*[jnp.sum(e, axis=-1, keepdims=True) for e in e_parts])
        num = jnp.add(*[jnp.dot(e.astype(BF16), v, preferred_element_type=F32)
                        for e, v in zip(e_parts, (vw_buf, vw_new))])
        return (num / den)[:, h * HEAD_DIM:(h + 1) * HEAD_DIM]

    def attend_selected(qp, h):
        def scores(c, carry):
            ds = pl.ds(pl.multiple_of(c * key_chunk, key_chunk), key_chunk)
            s_ref[:, ds] += lax.dot_general(qp, kcat_ref[ds, :], nt, preferred_element_type=F32)
            return carry

        lax.fori_loop(0, PAST // key_chunk, scores, 0)
        s_ref[:, PAST:] += lax.dot_general(qp, kcat_ref[PAST:, :], nt, preferred_element_type=F32)
        s = s_ref[...]
        e = jnp.exp(s - jnp.max(s, axis=-1, keepdims=True))
        den = jnp.sum(e, axis=-1, keepdims=True)
        e_ref[...] = e.astype(BF16)
        oacc_ref[...] = jnp.dot(e_ref[:, PAST:], vcat_ref[PAST:, :], preferred_element_type=F32)

        def weighted(c, carry):
            ds = pl.ds(pl.multiple_of(c * key_chunk, key_chunk), key_chunk)
            oacc_ref[...] += jnp.dot(e_ref[:, ds], vcat_ref[ds, :], preferred_element_type=F32)
            return carry

        lax.fori_loop(0, PAST // key_chunk, weighted, 0)
        return (oacc_ref[...] / den)[:, h * HEAD_DIM:(h + 1) * HEAD_DIM]

    for h in range(NSA_KV_HEADS):
        q64 = jnp.concatenate([q_ref[:, (h * NSA_GROUP + a) * HEAD_DIM:(h * NSA_GROUP + a + 1) * HEAD_DIM]
                               for a in range(NSA_GROUP)], axis=0)
        qp = jnp.concatenate([q64, zeros64] if h == 0 else [zeros64, q64], axis=1).astype(BF16)

        s = lax.dot_general(qp, comp[0], nt, preferred_element_type=F32) + bcmp_ref[h]
        m = jnp.max(s, axis=-1, keepdims=True)
        e = jnp.where(s > 0.1 * NEG, jnp.exp(s - m), 0.0)
        p = e / jnp.maximum(jnp.sum(e, axis=-1, keepdims=True), 1e-30)
        o_cmp = jnp.dot(p.astype(BF16), comp[1], preferred_element_type=F32)[:, h * HEAD_DIM:(h + 1) * HEAD_DIM]

        p_sum = p[0:T_PAD] + p[T_PAD:2 * T_PAD] + p[2 * T_PAD:3 * T_PAD] + p[3 * T_PAD:4 * T_PAD]
        imp = sum(jnp.dot(part, ov_ref[...], preferred_element_type=F32) for part in _split3(p_sum))
        lane = lax.broadcasted_iota(jnp.int32, (T_PAD, LANE), 1)
        cur = (PAST + jnp.minimum(lax.broadcasted_iota(jnp.int32, (T_PAD, LANE), 0), T_SAMPLE - 1)) // SEL_BLOCK
        valid = lane <= cur
        forced = (lane == 0) | (lane == cur) | (lane == cur - 1)
        score = jnp.where(valid, imp + jnp.where(forced, SEL_FORCE, 0.0), -1.0)
        rank = jnp.zeros((T_PAD, LANE), F32)
        for c in range(N_SEL_SAMPLE):
            sc = score[:, c:c + 1]
            beats = (sc > score) | ((sc == score) & (lane > c))
            rank = rank + jnp.where(beats, 1.0, 0.0)
        sel = jnp.where(valid & (rank < SEL_TOP_N), 1.0, 0.0)
        selneg = (jnp.dot(sel.astype(BF16), exp_ref[...], preferred_element_type=F32) - 1.0) * (-NEG)
        selneg_new = (sel[:, PAST // SEL_BLOCK:PAST // SEL_BLOCK + 1] - 1.0) * (-NEG)

        def per_token(bias, tok):
            n = bias.shape[-1]
            return (bias.reshape(NSA_GROUP, T_PAD, n) + tok[None]).reshape(rows, n)

        s_ref[:, :PAST] = per_token(bpast_ref[h], selneg)
        s_ref[:, PAST:] = per_token(bnew_ref[h], jnp.broadcast_to(selneg_new, (T_PAD, LANE)))
        o_sel = attend_selected(qp, h)
        o_win = attend_window(qp, h)

        for a in range(NSA_GROUP):
            r = slice(a * T_PAD, (a + 1) * T_PAD)
            c0 = (h * NSA_GROUP + a) * 3
            pieces.append(gates[:, c0:c0 + 1] * o_cmp[r] + gates[:, c0 + 1:c0 + 2] * o_sel[r]
                          + gates[:, c0 + 2:c0 + 3] * o_win[r])
    o_ref[0] = jnp.concatenate(pieces, axis=1)[0:T_SAMPLE]


def _sample_bias_tables(table):
    tq = jnp.minimum(jnp.arange(T_PAD), T_SAMPLE - 1)
    q_pos = PAST + tq

    def lay(x):
        n = x.shape[1]
        return x.reshape(T_PAD, n, NSA_KV_HEADS, NSA_GROUP).transpose(2, 3, 0, 1).reshape(NSA_KV_HEADS, -1, n)

    block_end = jnp.arange(N_CHUNK) * CMP_STRIDE + (CMP_BLOCK - 1)
    dist = q_pos[:, None] - block_end[None, :]
    ok = (dist >= 0) & (jnp.arange(N_CHUNK)[None, :] < N_CHUNK - 1)
    bcmp = lay(jnp.where(ok[..., None], _bias_lookup(table, dist), NEG))
    dist = q_pos[:, None] - jnp.arange(PAST)[None, :]
    past = _bias_lookup(table, dist)
    bpast = lay(past)
    w_buf = min(WINDOW, PAST)
    bwin = lay(jnp.where((dist < WINDOW)[:, PAST - w_buf:, None], past[:, PAST - w_buf:], NEG))
    j = jnp.arange(LANE)
    dist = tq[:, None] - j[None, :]
    ok = (dist >= 0) & (j[None, :] < T_SAMPLE)
    bnew = lay(jnp.where(ok[..., None], _bias_lookup(table, dist), NEG))
    c0 = jnp.arange(N_CHUNK) * CMP_STRIDE
    s0 = jnp.arange(LANE) * SEL_BLOCK
    ov = jnp.clip(jnp.minimum(c0[:, None] + CMP_BLOCK, s0[None, :] + SEL_BLOCK)
                  - jnp.maximum(c0[:, None], s0[None, :]), 0, CMP_BLOCK).astype(F32) / CMP_BLOCK
    ov = jnp.where((jnp.arange(N_CHUNK)[:, None] < N_CHUNK - 1) & (j[None, :] < N_SEL_SAMPLE), ov, 0.0)
    expand = (j[:, None] == (jnp.arange(PAST) // SEL_BLOCK)[None, :]).astype(BF16)
    return bcmp, bpast, bwin, bnew, ov.astype(BF16), expand


def _nsa_sample(za, zs, cache_cmp, cache_slc, cache_win, page_table, layer, tables, pe, w1, w2):
    b = za.shape[0]
    cmp_pages = cache_cmp[layer][page_table].reshape(b * N_PAGES, 2 * PAGE, KV_WIDTH)
    slc_pages = cache_slc[layer][page_table].reshape(b * N_PAGES, 2 * PAGE, KV_WIDTH)
    w_buf = cache_win.shape[2]
    win = cache_win[layer].reshape(b, 2 * w_buf, KV_WIDTH)
    pt = jnp.arange(b * N_PAGES, dtype=jnp.int32)
    bcmp, bpast, bwin, bnew, ov, expand = tables
    pe2 = jnp.concatenate([pe, pe], axis=-1)
    page_spec = lambda p: pl.BlockSpec((1, 2 * PAGE, KV_WIDTH), lambda bi, pt_: (pt_[bi * N_PAGES + p], 0, 0))
    whole = lambda x: pl.BlockSpec(x.shape, lambda bi, pt_: (0,) * x.ndim)
    eye = jnp.eye(NSA_KV_HEADS, dtype=F32)
    w1h = w1.reshape(2, 2, CMP_STRIDE, HEAD_DIM, CMP_HIDDEN)
    w1_bd = jnp.einsum('khlde,ab->khladbe', w1h, eye).reshape(2, 2, CMP_STRIDE * KV_WIDTH, NSA_KV_HEADS * CMP_HIDDEN)
    w2_bd = jnp.einsum('ked,ab->kaebd', w2, eye).reshape(2, NSA_KV_HEADS * CMP_HIDDEN, KV_WIDTH)
    consts = [pe2, w1_bd.astype(BF16), w2_bd.astype(BF16), bcmp, bpast, bwin, bnew, ov, expand]
    grid_spec = pltpu.PrefetchScalarGridSpec(
        num_scalar_prefetch=1,
        grid=(b,),
        in_specs=[page_spec(p) for p in range(N_PAGES)] * 2
        + [pl.BlockSpec((1, 2 * w_buf, KV_WIDTH), lambda bi, pt_: (bi, 0, 0)),
           pl.BlockSpec((1, T_SAMPLE, za.shape[-1]), lambda bi, pt_: (bi, 0, 0)),
           pl.BlockSpec((1, T_SAMPLE, LANE), lambda bi, pt_: (bi, 0, 0))]
        + [whole(x) for x in consts],
        out_specs=pl.BlockSpec((1, T_SAMPLE, NSA_WIDTH), lambda bi, pt_: (bi, 0, 0)),
        scratch_shapes=[pltpu.VMEM((PAGE, 4 * KV_WIDTH), F32), pltpu.VMEM((T_PAD, NSA_WIDTH), F32),
                        pltpu.VMEM((T_PAD, LANE), F32), pltpu.VMEM((2, N_CHUNK, CMP_STRIDE * KV_WIDTH), BF16),
                        pltpu.VMEM((PAST + PAGE, KV_WIDTH), BF16), pltpu.VMEM((PAST + PAGE, KV_WIDTH), BF16),
                        pltpu.VMEM((NSA_GROUP * T_PAD, PAST + PAGE), F32),
                        pltpu.VMEM((NSA_GROUP * T_PAD, PAST + PAGE), BF16),
                        pltpu.VMEM((NSA_GROUP * T_PAD, KV_WIDTH), F32)],
    )
    return pl.pallas_call(
        _nsa_sample_kernel,
        grid_spec=grid_spec,
        out_shape=jax.ShapeDtypeStruct((b, T_SAMPLE, NSA_WIDTH), F32),
        compiler_params=pltpu.CompilerParams(dimension_semantics=("arbitrary",), vmem_limit_bytes=VMEM_LIMIT_V7X),
        name="nsa_sample",
    )(pt, *([cmp_pages] * N_PAGES), *([slc_pages] * N_PAGES), win, za, zs, *consts)


I_LANE = 3 * NSA_HEADS
F_LANE = I_LANE + M_HEADS
EXT_PAD = 8


def _mlstm_kernel(zu_ref, zs_ref, c0_ref, n0_ref, m0_ref, cprev_ref, convw_ref, convb_ref, wq_ref, wk_ref, wkt_ref,
                  wv_ref, gb_ref, normg_ref, tril_ref, selrow_ref,
                  o_ref, c_out, n_out, m_out, conv_out,
                  ext_ref, zsp_ref, c_sc, n_sc, m_sc, *, rows, rows_pad, n_valid):
    i = pl.program_id(1)
    L = M_CHUNK

    @pl.when(i == 0)
    def _():
        c_sc[...] = c0_ref[0]
        n_sc[...] = n0_ref[0]
        m_sc[...] = m0_ref[0]
        ext_ref[EXT_PAD - (CONV_W - 1):EXT_PAD, :] = cprev_ref[0]

    if rows < rows_pad:
        ext_ref[EXT_PAD:, :] = jnp.zeros((rows_pad, M_WIDTH), F32)
        zsp_ref[...] = jnp.zeros(zsp_ref.shape, F32)
    ext_ref[EXT_PAD:EXT_PAD + rows, :] = zu_ref[0, :, :M_WIDTH]
    zsp_ref[0:rows, :] = zs_ref[0]

    u = ext_ref[EXT_PAD:EXT_PAD + rows_pad, :]
    conv = convb_ref[...] + convw_ref[CONV_W - 1:CONV_W, :] * u
    for j in range(CONV_W - 1):
        lo = EXT_PAD - (CONV_W - 1) + j
        conv = conv + convw_ref[j:j + 1, :] * ext_ref[lo:lo + rows_pad, :]
    uc = (conv * jax.nn.sigmoid(conv)).astype(BF16)
    ub = u.astype(BF16)

    zb = zsp_ref[...] + gb_ref[...]
    bcum = sum(jnp.dot(tril_ref[...], part, preferred_element_type=F32) for part in _split3(jax.nn.log_sigmoid(zb)))
    bcum = pltpu.roll(bcum, LANE - M_HEADS, 1)
    a_col = zb - bcum
    a_row = sum(lax.dot_general(selrow_ref[...], part, (((1,), (1,)), ((), ())), preferred_element_type=F32)
                for part in _split3(a_col))

    tt = lax.broadcasted_iota(jnp.int32, (L, L), 0)
    ss = lax.broadcasted_iota(jnp.int32, (L, L), 1)
    causal = ss <= tt
    tok_col = lax.broadcasted_iota(jnp.int32, (L, 1), 0)
    tok_row = lax.broadcasted_iota(jnp.int32, (1, L), 1)

    for h in range(M_HEADS):
        hs = slice(h * M_HEAD_DIM, (h + 1) * M_HEAD_DIM)
        q_all = jnp.dot(uc[:, hs], wq_ref[h].astype(BF16), preferred_element_type=F32) * (M_HEAD_DIM ** -0.5)
        k_all = jnp.dot(uc[:, hs], wk_ref[h].astype(BF16), preferred_element_type=F32)
        kt_all = lax.dot_general(wkt_ref[h].astype(BF16), uc[:, hs], (((1,), (1,)), ((), ())),
                                 preferred_element_type=F32)
        v_all = jnp.dot(ub[:, hs], wv_ref[h].astype(BF16), preferred_element_type=F32)
        c_st = c_sc[h]
        n_st = n_sc[h:h + 1, :]
        m_st = m_sc[h:h + 1, 0:1]
        for c in range(rows_pad // L):
            r = slice(c * L, (c + 1) * L)
            q, k, kt, v = q_all[r].astype(BF16), k_all[r], kt_all[:, r], v_all[r].astype(BF16)
            b_col = bcum[r, I_LANE + h:I_LANE + h + 1]
            a_c = a_col[r, I_LANE + h:I_LANE + h + 1]
            a_r = a_row[h:h + 1, r]
            log_d = jnp.where(causal, b_col + a_r, NEG)
            m_col = jnp.maximum(b_col + m_st, jnp.max(log_d, axis=-1, keepdims=True))
            dw = jnp.exp(log_d - m_col)
            w_inter = jnp.exp(b_col + m_st - m_col)
            s = lax.dot_general(q, k.astype(BF16), (((1,), (1,)), ((), ())), preferred_element_type=F32) * dw
            num = (jnp.dot(s.astype(BF16), v, preferred_element_type=F32)
                   + w_inter * jnp.dot(q, c_st.astype(BF16), preferred_element_type=F32))
            den = (jnp.sum(s, axis=-1, keepdims=True)
                   + w_inter * jnp.sum(q_all[r] * n_st, axis=-1, keepdims=True))
            hh = num / jnp.maximum(jnp.abs(den), jnp.exp(-m_col))
            hn = hh * lax.rsqrt(jnp.mean(hh * hh, axis=-1, keepdims=True) + EPS)
            o_pre = zu_ref[0, :, M_WIDTH + h * M_HEAD_DIM:M_WIDTH + (h + 1) * M_HEAD_DIM]
            if rows < rows_pad:
                o_ref[0, :, hs] = jax.nn.sigmoid(o_pre) * hn[0:rows] * normg_ref[:, hs]
            else:
                o_ref[0, r, hs] = jax.nn.sigmoid(o_pre[r]) * hn * normg_ref[:, hs]
            b_last = b_col[n_valid - 1:n_valid, :]
            log_s = jnp.where(tok_col < n_valid, b_last + a_c, NEG)
            m_new = jnp.maximum(b_last + m_st, jnp.max(log_s, axis=0, keepdims=True))
            ws_col = jnp.exp(log_s - m_new)
            ws_row = jnp.where(tok_row < n_valid, jnp.exp(b_last + a_r - m_new), 0.0)
            wc = jnp.exp(b_last + m_st - m_new)
            c_st = wc * c_st + jnp.dot((kt * ws_row).astype(BF16), v, preferred_element_type=F32)
            n_st = wc * n_st + jnp.sum(k * ws_col, axis=0, keepdims=True)
            m_st = m_new
        c_sc[h] = c_st
        n_sc[h:h + 1, :] = n_st
        m_sc[h:h + 1, :] = jnp.broadcast_to(m_st, (1, LANE))

    tail = ext_ref[EXT_PAD + rows - (CONV_W - 1):EXT_PAD + rows, :]
    ext_ref[EXT_PAD - (CONV_W - 1):EXT_PAD, :] = tail

    @pl.when(i == pl.num_programs(1) - 1)
    def _():
        c_out[0] = c_sc[...]
        n_out[0] = n_sc[...]
        m_out[0] = m_sc[...]
        conv_out[0] = jnp.zeros(conv_out.shape[1:], F32)
        conv_out[0, 0:CONV_W - 1, :] = tail


def _mlstm(zu, zs, state, conv_prev, conv_w, conv_b, m_qkv, gate_b, norm_g):
    b, t, _ = zu.shape
    rows = min(t, 4 * M_CHUNK)
    rows_pad = -(-rows // M_CHUNK) * M_CHUNK
    n_valid = M_CHUNK if rows == rows_pad else rows
    assert t % rows == 0 and (rows == rows_pad or t == rows)
    c0, n0, m0 = state
    n0p = jnp.pad(n0, ((0, 0), (0, 8 - M_HEADS), (0, 0)))
    m0p = jnp.pad(jnp.broadcast_to(m0[:, :, None], (b, M_HEADS, LANE)), ((0, 0), (0, 8 - M_HEADS), (0, 0)))
    gb = jnp.zeros((1, LANE), F32).at[0, I_LANE:I_LANE + 2 * M_HEADS].set(gate_b.reshape(-1))
    idx = jnp.arange(rows_pad)
    tril = ((idx[:, None] >= idx[None, :]) & (idx[:, None] // M_CHUNK == idx[None, :] // M_CHUNK)).astype(BF16)
    selrow = ((jnp.arange(16)[:, None] + I_LANE == jnp.arange(LANE)[None, :])
              & (jnp.arange(16)[:, None] < M_HEADS)).astype(BF16)
    whole = lambda *shape: pl.BlockSpec(shape, lambda bi, i: (0,) * len(shape))
    per_b = lambda *shape: pl.BlockSpec((1,) + shape, lambda bi, i: (bi,) + (0,) * len(shape))
    out, c_new, n_new, m_new, conv_new = pl.pallas_call(
        functools.partial(_mlstm_kernel, rows=rows, rows_pad=rows_pad, n_valid=n_valid),
        grid=(b, t // rows),
        in_specs=[pl.BlockSpec((1, rows, 2 * M_WIDTH), lambda bi, i: (bi, i, 0)),
                  pl.BlockSpec((1, rows, LANE), lambda bi, i: (bi, i, 0)),
                  per_b(M_HEADS, M_HEAD_DIM, M_HEAD_DIM), per_b(8, M_HEAD_DIM), per_b(8, LANE),
                  per_b(CONV_W - 1, M_WIDTH),
                  whole(CONV_W, M_WIDTH), whole(1, M_WIDTH),
                  whole(M_HEADS, M_HEAD_DIM, M_HEAD_DIM), whole(M_HEADS, M_HEAD_DIM, M_HEAD_DIM),
                  whole(M_HEADS, M_HEAD_DIM, M_HEAD_DIM), whole(M_HEADS, M_HEAD_DIM, M_HEAD_DIM),
                  whole(1, LANE), whole(1, M_WIDTH), whole(rows_pad, rows_pad), whole(16, LANE)],
        out_specs=[pl.BlockSpec((1, rows, M_WIDTH), lambda bi, i: (bi, i, 0)),
                   per_b(M_HEADS, M_HEAD_DIM, M_HEAD_DIM), per_b(8, M_HEAD_DIM), per_b(8, LANE),
                   per_b(8, M_WIDTH)],
        out_shape=[jax.ShapeDtypeStruct((b, t, M_WIDTH), F32),
                   jax.ShapeDtypeStruct((b, M_HEADS, M_HEAD_DIM, M_HEAD_DIM), F32),
                   jax.ShapeDtypeStruct((b, 8, M_HEAD_DIM), F32),
                   jax.ShapeDtypeStruct((b, 8, LANE), F32),
                   jax.ShapeDtypeStruct((b, 8, M_WIDTH), F32)],
        scratch_shapes=[pltpu.VMEM((EXT_PAD + rows_pad, M_WIDTH), F32), pltpu.VMEM((rows_pad, LANE), F32),
                        pltpu.VMEM((M_HEADS, M_HEAD_DIM, M_HEAD_DIM), F32), pltpu.VMEM((8, M_HEAD_DIM), F32),
                        pltpu.VMEM((8, LANE), F32)],
        compiler_params=pltpu.CompilerParams(dimension_semantics=("arbitrary", "arbitrary"),
                                             vmem_limit_bytes=VMEM_LIMIT_V7X),
        name="mlstm",
    )(zu, zs, c0, n0p, m0p, conv_prev, conv_w, conv_b.reshape(1, M_WIDTH), m_qkv[0], m_qkv[1],
      m_qkv[1].transpose(0, 2, 1),
      m_qkv[2], gb, norm_g.reshape(1, M_WIDTH), tril, selrow)
    return out, (c_new, n_new[:, :M_HEADS], m_new[:, :M_HEADS, 0], conv_new[:, :CONV_W - 1])


def rel_bucket(dist):
    d = jnp.maximum(dist, 0)
    exact = REL_BUCKETS // 2
    log_part = exact + (jnp.log(jnp.maximum(d, 1).astype(F32) / exact)
                        / math.log(REL_MAX_DIST / exact) * (REL_BUCKETS - exact)).astype(jnp.int32)
    return jnp.where(d < exact, d, jnp.minimum(log_part, REL_BUCKETS - 1))


def _split_w_in(w_in):
    offs = np.cumsum((0,) + IN_WIDTHS)
    w_a = w_in[:, :offs[7]]
    w_small = jnp.concatenate([w_in[:, offs[7]:offs[8]], w_in[:, offs[10]:offs[11]]], axis=1)
    w_small = jnp.pad(w_small, ((0, 0), (0, LANE - w_small.shape[1])))
    w_u = w_in[:, offs[8]:offs[10]]
    w_g = w_in[:, offs[11]:]
    return w_a, w_small, w_u, w_g


def mixer(x, norm_g, prm, past):
    b, t, _ = x.shape
    x2 = x.reshape(b * t, D_MODEL)
    z_a, z_s, z_u, z_g = _proj(x2, norm_g, [w.astype(BF16) for w in _split_w_in(prm['w_in'])])
    z_a, z_s, z_u = (z.reshape(b, t, -1) for z in (z_a, z_s, z_u))
    new_cmp, new_slc, win_rows = (z_a[..., NSA_WIDTH + 2 * KV_WIDTH * i:NSA_WIDTH + 2 * KV_WIDTH * (i + 1)]
                                  .reshape(b, t, 2, NSA_KV_HEADS, HEAD_DIM) for i in range(3))
    if past is None:
        o_nsa = _nsa_prompt_from_proj(z_a.reshape(b * t, -1), z_s.reshape(b * t, -1), b, t, prm['prompt_tables'],
                                      prm['cmp_pe'], prm['cmp_w1'], prm['cmp_w2'])
        new_win = win_rows[:, t - min(WINDOW, t):]
        conv_prev = jnp.zeros((b, CONV_W - 1, M_WIDTH), F32)
        m_state = (jnp.zeros((b, M_HEADS, M_HEAD_DIM, M_HEAD_DIM), F32),
                   jnp.zeros((b, M_HEADS, M_HEAD_DIM), F32),
                   jnp.zeros((b, M_HEADS), F32))
    else:
        layer = past['layer']
        assert t == T_SAMPLE and past['page_table'].shape[1] == N_PAGES and past['cmp'].shape[2] == PAGE
        o_nsa = _nsa_sample(z_a, z_s, past['cmp'], past['slc'], past['win'], past['page_table'], layer,
                            prm['sample_tables'], prm['cmp_pe'], prm['cmp_w1'], prm['cmp_w2'])
        o_nsa = o_nsa.reshape(b * t, NSA_WIDTH)
        new_win = jnp.concatenate([past['win'][layer][:, t:], win_rows], axis=1)
        conv_prev = past['conv']
        m_state = (past['C'].astype(F32), past['n'].astype(F32), past['m'].astype(F32))
    o_m, (c_new, n_new, m_new, conv_new) = _mlstm(z_u, z_s, m_state, conv_prev, prm['conv_w'], prm['conv_b'],
                                                  prm['m_qkv'], prm['gate_b'], prm['m_norm'])
    y = _merge_out(x2, o_nsa, o_m.reshape(b * t, M_WIDTH), z_g, prm['w_up_a'].astype(BF16),
                   prm['w_up_b'].astype(BF16), prm['w_out'].astype(BF16)).reshape(b, t, D_MODEL)
    return y, (new_cmp, new_slc, new_win, c_new, n_new, m_new, conv_new)


def _channel_mixer(x, g, l, ffn_w1, ffn_w3, ffn_w2, moe_router, moe_w1, moe_w3, moe_w2):
    b, t, d = x.shape
    x2 = x.reshape(b * t, d)
    i = l // 2
    if l % 2 == 0:
        y = _ffn_dense(x2, g, ffn_w1[i], ffn_w3[i], ffn_w2[i])
    else:
        y = _moe(x2, g, moe_router[i], moe_w1[i], moe_w3[i], moe_w2[i])
    return y.reshape(b, t, d)


def _final_norm(x, g):
    xf = x.astype(F32)
    return xf * lax.rsqrt(jnp.mean(xf * xf, axis=-1, keepdims=True) + EPS) * g


def kernel(x_prompt, x_sample, cache_cmp_kv, cache_slc_kv, cache_win_kv, state_mlstm_C, state_mlstm_n,
           state_mlstm_m, state_mlstm_conv, page_table, rel_bias_table, norm_mix, norm_ffn, norm_final,
           w_in, cmp_pe, cmp_w1, cmp_w2, m_conv_w, m_conv_b, m_qkv, m_gate_bias, m_norm, w_up_a, w_up_b,
           w_out, ffn_w1, ffn_w3, ffn_w2, moe_router, moe_w1, moe_w3, moe_w2):
    xp, xs = x_prompt, x_sample
    prompt_states, sample_states = [], []
    prompt_tables = _prompt_bias_tables(rel_bias_table)
    sample_tables = _sample_bias_tables(rel_bias_table)
    for l in range(DEPTH):
        prm = {'prompt_tables': prompt_tables, 'sample_tables': sample_tables, 'w_in': w_in[l], 'cmp_pe': cmp_pe[l],
               'cmp_w1': cmp_w1[l], 'cmp_w2': cmp_w2[l], 'conv_w': m_conv_w[l], 'conv_b': m_conv_b[l],
               'm_qkv': m_qkv[l], 'gate_b': m_gate_bias[l], 'm_norm': m_norm[l], 'w_up_a': w_up_a[l],
               'w_up_b': w_up_b[l], 'w_out': w_out[l]}
        past = {'cmp': cache_cmp_kv, 'slc': cache_slc_kv, 'win': cache_win_kv, 'layer': l,
                'C': state_mlstm_C[l], 'n': state_mlstm_n[l], 'm': state_mlstm_m[l],
                'conv': state_mlstm_conv[l], 'page_table': page_table}
        xp, st_p = mixer(xp, norm_mix[l], prm, None)
        xs, st_s = mixer(xs, norm_mix[l], prm, past)
        xp = _channel_mixer(xp, norm_ffn[l], l, ffn_w1, ffn_w3, ffn_w2, moe_router, moe_w1, moe_w3, moe_w2)
        xs = _channel_mixer(xs, norm_ffn[l], l, ffn_w1, ffn_w3, ffn_w2, moe_router, moe_w1, moe_w3, moe_w2)
        prompt_states.append(st_p)
        sample_states.append(st_s)
    y_prompt = _final_norm(xp, norm_final)
    y_sample = _final_norm(xs, norm_final)
    ps = [jnp.stack([s[i] for s in prompt_states]) for i in range(7)]
    ss = [jnp.stack([s[i] for s in sample_states]) for i in range(7)]
    return (y_prompt, y_sample, ps[0], ps[1], ps[2], ps[3], ps[4], ps[5], ps[6],
            ss[0], ss[1], ss[2], ss[3], ss[4], ss[5], ss[6])
```

```python
import functools
import math

import jax
import jax.numpy as jnp
import numpy as np
from jax import lax
from jax.experimental import pallas as pl
from jax.experimental.pallas import tpu as pltpu

D_MODEL = 1024
DEPTH = 2
NSA_HEADS = 8
NSA_KV_HEADS = 2
NSA_GROUP = NSA_HEADS // NSA_KV_HEADS
HEAD_DIM = 64
NSA_WIDTH = NSA_HEADS * HEAD_DIM
KV_WIDTH = NSA_KV_HEADS * HEAD_DIM
CMP_BLOCK = 32
CMP_STRIDE = 16
CMP_HIDDEN = 128
SEL_BLOCK = 64
SEL_TOP_N = 16
SEL_FORCE = 1e4
WINDOW = 512
WIN_Q_BLOCK = 128
REL_BUCKETS = 32
REL_MAX_DIST = 128
M_HEADS = 4
M_HEAD_DIM = 128
M_WIDTH = M_HEADS * M_HEAD_DIM
CONV_W = 4
M_CHUNK = 64
D_FF = 2816
N_EXPERTS = 8
TOP_K = 2
EPS = 1e-6
IN_WIDTHS = (NSA_WIDTH, KV_WIDTH, KV_WIDTH, KV_WIDTH, KV_WIDTH, KV_WIDTH, KV_WIDTH, 3 * NSA_HEADS, M_WIDTH, M_WIDTH,
             2 * M_HEADS, D_MODEL, D_MODEL)

VMEM_LIMIT_V7X = 52 * 1024 * 1024
LANE = 128

F32 = jnp.float32
BF16 = jnp.bfloat16


def _split3(x):
    hi = x.astype(BF16)
    r1 = x - hi.astype(F32)
    mid = r1.astype(BF16)
    lo = (r1 - mid.astype(F32)).astype(BF16)
    return hi, mid, lo


def _pick_tile(n, cands):
    for c in cands:
        if n % c == 0:
            return c
    return n


def _mm_kernel(*refs, norm, has_res):
    x_ref, g_ref, w_ref = refs[:3]
    res_ref = refs[3] if has_res else None
    o_ref, xs_ref = refs[-2], refs[-1]

    @pl.when(pl.program_id(1) == 0)
    def _():
        x = x_ref[...]
        if norm:
            x = x * lax.rsqrt(jnp.mean(x * x, axis=-1, keepdims=True) + EPS) * g_ref[...]
        xs_ref[...] = x.astype(BF16)

    acc = jnp.dot(xs_ref[...], w_ref[...].astype(BF16), preferred_element_type=F32)
    if has_res:
        acc = acc + res_ref[...]
    o_ref[...] = acc


def _mm(x, w, g=None, res=None):
    m, k = x.shape
    n = w.shape[1]
    tm = _pick_tile(m, (1024, 512, 256, 128))
    tn = _pick_tile(n, (512, 256, 128))
    norm = g is not None
    gg = (g if norm else jnp.ones((k,), F32)).reshape(1, k)
    in_specs = [pl.BlockSpec((tm, k), lambda i, j: (i, 0)),
                pl.BlockSpec((1, k), lambda i, j: (0, 0)),
                pl.BlockSpec((k, tn), lambda i, j: (0, j))]
    args = [x, gg, w]
    if res is not None:
        in_specs.append(pl.BlockSpec((tm, tn), lambda i, j: (i, j)))
        args.append(res)
    return pl.pallas_call(
        functools.partial(_mm_kernel, norm=norm, has_res=res is not None),
        grid=(m // tm, n // tn),
        in_specs=in_specs,
        out_specs=pl.BlockSpec((tm, tn), lambda i, j: (i, j)),
        out_shape=jax.ShapeDtypeStruct((m, n), F32),
        scratch_shapes=[pltpu.VMEM((tm, k), BF16)],
        compiler_params=pltpu.CompilerParams(dimension_semantics=("arbitrary", "arbitrary"),
                                             vmem_limit_bytes=VMEM_LIMIT_V7X),
        name="mm",
    )(*args)


def _proj_kernel(x_ref, g_ref, *refs):
    n_out = len(refs) // 2
    x = x_ref[...]
    xn = (x * lax.rsqrt(jnp.mean(x * x, axis=-1, keepdims=True) + EPS) * g_ref[...]).astype(BF16)
    for w_ref, o_ref in zip(refs[:n_out], refs[n_out:]):
        o_ref[...] = jnp.dot(xn, w_ref[...], preferred_element_type=F32)


def _proj(x, g, weights):
    m, k = x.shape
    tm = _pick_tile(m, (256, 128))
    row = lambda n: pl.BlockSpec((tm, n), lambda i: (i, 0))
    whole = lambda a: pl.BlockSpec(a.shape, lambda i: (0,) * a.ndim)
    return pl.pallas_call(
        _proj_kernel,
        grid=(m // tm,),
        in_specs=[row(k), whole(g.reshape(1, k))] + [whole(w) for w in weights],
        out_specs=[row(w.shape[1]) for w in weights],
        out_shape=[jax.ShapeDtypeStruct((m, w.shape[1]), F32) for w in weights],
        compiler_params=pltpu.CompilerParams(dimension_semantics=("arbitrary",), vmem_limit_bytes=VMEM_LIMIT_V7X),
        name="proj",
    )(x, g.reshape(1, k), *weights)


def _merge_out_kernel(x_ref, oa_ref, ob_ref, zg_ref, wa_ref, wb_ref, wo_ref, o_ref):
    d = x_ref.shape[-1]
    up_a = jnp.dot(oa_ref[...].astype(BF16), wa_ref[...], preferred_element_type=F32)
    up_b = jnp.dot(ob_ref[...].astype(BF16), wb_ref[...], preferred_element_type=F32)
    merged = jax.nn.sigmoid(zg_ref[:, :d]) * up_a + jax.nn.sigmoid(zg_ref[:, d:]) * up_b
    o_ref[...] = x_ref[...] + jnp.dot(merged.astype(BF16), wo_ref[...], preferred_element_type=F32)


def _merge_out(x, o_a, o_b, z_g, w_up_a, w_up_b, w_out):
    m, d = x.shape
    tm = _pick_tile(m, (512, 256, 128))
    row = lambda a: pl.BlockSpec((tm, a.shape[1]), lambda i: (i, 0))
    whole = lambda a: pl.BlockSpec(a.shape, lambda i: (0,) * a.ndim)
    return pl.pallas_call(
        _merge_out_kernel,
        grid=(m // tm,),
        in_specs=[row(x), row(o_a), row(o_b), row(z_g), whole(w_up_a), whole(w_up_b), whole(w_out)],
        out_specs=row(x),
        out_shape=jax.ShapeDtypeStruct((m, d), F32),
        compiler_params=pltpu.CompilerParams(dimension_semantics=("arbitrary",), vmem_limit_bytes=VMEM_LIMIT_V7X),
        name="merge_out",
    )(x, o_a, o_b, z_g, w_up_a, w_up_b, w_out)


def _ffn_body(x_ref, g_ref, w1_ref, w3_ref, w2_ref, o_ref, xs_ref, acc_ref, *, residual, grouped):
    j = pl.program_id(1)

    @pl.when(j == 0)
    def _():
        x = x_ref[...]
        xn = x * lax.rsqrt(jnp.mean(x * x, axis=-1, keepdims=True) + EPS) * g_ref[...]
        xs_ref[...] = xn.astype(BF16)
        acc_ref[...] = jnp.zeros_like(acc_ref)

    xs = xs_ref[...]
    w1 = w1_ref[0] if grouped else w1_ref[...]
    w3 = w3_ref[0] if grouped else w3_ref[...]
    w2 = w2_ref[0] if grouped else w2_ref[...]
    a = jnp.dot(xs, w1.astype(BF16), preferred_element_type=F32)
    b = jnp.dot(xs, w3.astype(BF16), preferred_element_type=F32)
    h = (a * jax.nn.sigmoid(a) * b).astype(BF16)
    acc_ref[...] += jnp.dot(h, w2.astype(BF16), preferred_element_type=F32)

    @pl.when(j == pl.num_programs(1) - 1)
    def _():
        if residual:
            o_ref[...] = x_ref[...] + acc_ref[...]
        else:
            o_ref[...] = acc_ref[...]


def _ffn_dense_kernel(x_ref, g_ref, w1_ref, w3_ref, w2_ref, o_ref, xs_ref, acc_ref):
    _ffn_body(x_ref, g_ref, w1_ref, w3_ref, w2_ref, o_ref, xs_ref, acc_ref, residual=True, grouped=False)


def _ffn_grouped_kernel(be_ref, x_ref, g_ref, w1_ref, w3_ref, w2_ref, o_ref, xs_ref, acc_ref):
    del be_ref
    _ffn_body(x_ref, g_ref, w1_ref, w3_ref, w2_ref, o_ref, xs_ref, acc_ref, residual=False, grouped=True)


def _ffn_dense(x, g, w1, w3, w2):
    m, d = x.shape
    f = w1.shape[1]
    tm = _pick_tile(m, (1024, 512, 256, 128))
    tf = _pick_tile(f, (256, 128))
    return pl.pallas_call(
        _ffn_dense_kernel,
        grid=(m // tm, f // tf),
        in_specs=[pl.BlockSpec((tm, d), lambda i, j: (i, 0)),
                  pl.BlockSpec((1, d), lambda i, j: (0, 0)),
                  pl.BlockSpec((d, tf), lambda i, j: (0, j)),
                  pl.BlockSpec((d, tf), lambda i, j: (0, j)),
                  pl.BlockSpec((tf, d), lambda i, j: (j, 0))],
        out_specs=pl.BlockSpec((tm, d), lambda i, j: (i, 0)),
        out_shape=jax.ShapeDtypeStruct((m, d), F32),
        scratch_shapes=[pltpu.VMEM((tm, d), BF16), pltpu.VMEM((tm, d), F32)],
        compiler_params=pltpu.CompilerParams(dimension_semantics=("arbitrary", "arbitrary"),
                                             vmem_limit_bytes=VMEM_LIMIT_V7X),
        name="ffn_dense",
    )(x, g.reshape(1, d), w1, w3, w2)


def _ffn_grouped(xd, blk_e, g, w1, w3, w2, tm):
    rows, d = xd.shape
    f = w1.shape[2]
    tf = _pick_tile(f, (256, 128))
    grid_spec = pltpu.PrefetchScalarGridSpec(
        num_scalar_prefetch=1,
        grid=(rows // tm, f // tf),
        in_specs=[pl.BlockSpec((tm, d), lambda i, j, be: (i, 0)),
                  pl.BlockSpec((1, d), lambda i, j, be: (0, 0)),
                  pl.BlockSpec((1, d, tf), lambda i, j, be: (be[i], 0, j)),
                  pl.BlockSpec((1, d, tf), lambda i, j, be: (be[i], 0, j)),
                  pl.BlockSpec((1, tf, d), lambda i, j, be: (be[i], j, 0))],
        out_specs=pl.BlockSpec((tm, d), lambda i, j, be: (i, 0)),
        scratch_shapes=[pltpu.VMEM((tm, d), BF16), pltpu.VMEM((tm, d), F32)],
    )
    return pl.pallas_call(
        _ffn_grouped_kernel,
        grid_spec=grid_spec,
        out_shape=jax.ShapeDtypeStruct((rows, d), F32),
        compiler_params=pltpu.CompilerParams(dimension_semantics=("arbitrary", "arbitrary"),
                                             vmem_limit_bytes=VMEM_LIMIT_V7X),
        name="ffn_grouped",
    )(blk_e, xd, g.reshape(1, d), w1, w3, w2)


def _moe(x, g, router, w1, w3, w2):
    n, d = x.shape
    tm = 1024 if n >= 8192 else 128
    router_p = jnp.pad(router, ((0, 0), (0, LANE - N_EXPERTS)))
    logits = _mm(x, router_p, g=g)[:, :N_EXPERTS]
    top_val, top_idx = lax.top_k(logits, TOP_K)
    gate = jax.nn.softmax(top_val, axis=-1).reshape(-1)
    e_flat = top_idx.reshape(-1)
    n_asg = n * TOP_K
    order = jnp.argsort(e_flat)
    e_sorted = e_flat[order]
    counts = jnp.bincount(e_flat, length=N_EXPERTS)
    padded = (counts + tm - 1) // tm * tm
    pad_end = jnp.cumsum(padded)
    pad_start = pad_end - padded
    start = jnp.cumsum(counts) - counts
    dest = pad_start[e_sorted] + jnp.arange(n_asg) - start[e_sorted]
    n_blocks = n_asg // tm + N_EXPERTS
    src_tok = jnp.zeros((n_blocks * tm,), jnp.int32).at[dest].set((order // TOP_K).astype(jnp.int32))
    pos = jnp.zeros((n_asg,), jnp.int32).at[order].set(dest.astype(jnp.int32))
    blk_e = jnp.minimum(jnp.searchsorted(pad_end, jnp.arange(n_blocks) * tm, side='right'),
                        N_EXPERTS - 1).astype(jnp.int32)
    yd = _ffn_grouped(x[src_tok], blk_e, g, w1, w3, w2, tm)
    contrib = yd[pos] * gate[:, None]
    return x + contrib.reshape(n, TOP_K, d).sum(axis=1)


N_CHUNK = 128
CHUNK_W = CMP_STRIDE * HEAD_DIM


def _compress_kernel(x_ref, pe_ref, w1_ref, w2_ref, o_ref):
    c = x_ref[0]
    lo = jnp.dot((c + pe_ref[0:1]).astype(BF16), w1_ref[0].astype(BF16), preferred_element_type=F32)
    hi = jnp.dot((c + pe_ref[1:2]).astype(BF16), w1_ref[1].astype(BF16), preferred_element_type=F32)
    hid = jax.nn.gelu(lo + pltpu.roll(hi, N_CHUNK - 1, 0))
    out = jnp.dot(hid.astype(BF16), w2_ref[...].astype(BF16), preferred_element_type=F32)
    row = lax.broadcasted_iota(jnp.int32, out.shape, 0)
    o_ref[0] = jnp.where(row < N_CHUNK - 1, out, 0.0)


def _compress(xc, pe, w1, w2):
    nb = xc.shape[0]
    hidden = w1.shape[-1]
    return pl.pallas_call(
        _compress_kernel,
        grid=(nb,),
        in_specs=[pl.BlockSpec((1, N_CHUNK, CHUNK_W), lambda i: (i, 0, 0)),
                  pl.BlockSpec((2, CHUNK_W), lambda i: (0, 0)),
                  pl.BlockSpec((2, CHUNK_W, hidden), lambda i: (0, 0, 0)),
                  pl.BlockSpec((hidden, HEAD_DIM), lambda i: (0, 0))],
        out_specs=pl.BlockSpec((1, N_CHUNK, HEAD_DIM), lambda i: (i, 0, 0)),
        out_shape=jax.ShapeDtypeStruct((nb, N_CHUNK, HEAD_DIM), F32),
        compiler_params=pltpu.CompilerParams(dimension_semantics=("arbitrary",), vmem_limit_bytes=VMEM_LIMIT_V7X),
        name="compress",
    )(xc, pe.reshape(2, CHUNK_W), w1.reshape(2, CHUNK_W, hidden), w2)


def _compress_heads(kv, pe, w1, w2):
    b = kv.shape[0]
    xc = kv.reshape(b, N_CHUNK, CMP_STRIDE, NSA_KV_HEADS, HEAD_DIM).transpose(0, 3, 1, 2, 4)
    out = _compress(xc.reshape(b * NSA_KV_HEADS, N_CHUNK, CHUNK_W), pe, w1, w2)
    return out.reshape(b, NSA_KV_HEADS, N_CHUNK, HEAD_DIM)


TQ = 128
TK = 128
NEG = -1e30
N_BIAS_TILES = WINDOW // TK + 1
MASKED_TILE = N_BIAS_TILES
SEL_SUB = 4


def _nsa_prompt_kernel(q_ref, kct_ref, vcc_ref, kst_ref, vs_ref, kwt_ref, vw_ref, bcmp_ref, btile_ref, gate_ref,
                       ov_ref, exp_ref, o_ref, selneg_ref, m_ref, l_ref, acc_ref):
    i = pl.program_id(2)
    rows = NSA_GROUP * TQ
    qt = (q_ref[...] * (HEAD_DIM ** -0.5)).astype(BF16)
    q = jnp.concatenate([qt[:, a * HEAD_DIM:(a + 1) * HEAD_DIM] for a in range(NSA_GROUP)], axis=0)

    s = jnp.dot(q, kct_ref[0, 0], preferred_element_type=F32) + bcmp_ref[0].reshape(rows, N_CHUNK)
    m = jnp.max(s, axis=-1, keepdims=True)
    e = jnp.where(s > 0.1 * NEG, jnp.exp(s - m), 0.0)
    p = e / jnp.maximum(jnp.sum(e, axis=-1, keepdims=True), 1e-30)
    o_cmp = jnp.dot(p.astype(BF16), vcc_ref[0, 0], preferred_element_type=F32)

    p_sum = p[0:TQ] + p[TQ:2 * TQ] + p[2 * TQ:3 * TQ] + p[3 * TQ:4 * TQ]
    n_sel = T_PROMPT // SEL_BLOCK
    imp_t = sum(lax.dot_general(ov_ref[...], part, (((1,), (1,)), ((), ())), preferred_element_type=F32)
                for part in _split3(p_sum))[0:n_sel]
    blk = lax.broadcasted_iota(jnp.int32, (n_sel, TQ), 0)
    cur = (lax.broadcasted_iota(jnp.int32, (n_sel, TQ), 1) + i * TQ) // SEL_BLOCK
    valid = blk <= cur
    forced = (blk == 0) | (blk == cur) | (blk == cur - 1)
    score = jnp.where(valid, imp_t + jnp.where(forced, SEL_FORCE, 0.0), -1.0)
    rank = jnp.zeros((n_sel, TQ), F32)
    for c in range(n_sel):
        sc = score[c:c + 1, :]
        beats = (sc > score) | ((sc == score) & (blk > c))
        rank = rank + jnp.where(beats, 1.0, 0.0)
    sel_t = jnp.where(valid & (rank < SEL_TOP_N), 1.0, 0.0)
    sel = jnp.concatenate([sel_t, jnp.zeros((LANE - n_sel, TQ), F32)], axis=0).T.astype(BF16)
    sel_keys = jnp.dot(sel, exp_ref[...], preferred_element_type=F32)
    selneg_ref[...] = (sel_keys - 1.0) * (-NEG)

    def masked_scores(kt_ref, first_tile, n_sub, selected):
        k0 = pl.multiple_of(first_tile * TK, TK)
        s_all = jnp.dot(q, kt_ref[0, 0, :, pl.ds(k0, n_sub * TK)], preferred_element_type=F32)
        pieces = []
        for u in range(n_sub):
            d0 = i - (first_tile + u)
            idx = jnp.where(d0 < 0, MASKED_TILE, jnp.minimum(d0, 2) if selected else d0)
            piece = s_all[:, u * TK:(u + 1) * TK].reshape(NSA_GROUP, TQ, TK) + btile_ref[0, idx]
            if selected:
                piece = piece + selneg_ref[:, pl.ds(pl.multiple_of(k0 + u * TK, TK), TK)][None]
            pieces.append(piece.reshape(rows, TK))
        return pieces, k0

    def row_max(pieces):
        return jnp.max(functools.reduce(jnp.maximum, pieces), axis=-1, keepdims=True)

    def row_sum(pieces):
        return jnp.sum(functools.reduce(jnp.add, pieces), axis=-1, keepdims=True)

    m_ref[...] = jnp.full(m_ref.shape, -jnp.inf, F32)
    l_ref[...] = jnp.zeros(l_ref.shape, F32)
    acc_ref[...] = jnp.zeros(acc_ref.shape, F32)

    def sel_group(gi, carry):
        pieces, k0 = masked_scores(kst_ref, gi * SEL_SUB, SEL_SUB, True)
        m_old = m_ref[...]
        m_new = jnp.maximum(m_old, row_max(pieces))
        pt = [jnp.exp(piece - m_new) for piece in pieces]
        alpha = jnp.exp(m_old - m_new)
        l_ref[...] = alpha * l_ref[...] + row_sum(pt)
        pv = jnp.dot(jnp.concatenate([x.astype(BF16) for x in pt], axis=1), vs_ref[0, 0, pl.ds(k0, SEL_SUB * TK), :],
                     preferred_element_type=F32)
        acc_ref[...] = alpha * acc_ref[...] + pv
        m_ref[...] = m_new
        return carry

    lax.fori_loop(0, i // SEL_SUB + 1, sel_group, 0)
    o_sel = acc_ref[...] / l_ref[...]

    pieces, k0 = masked_scores(kwt_ref, jnp.maximum(i - (N_BIAS_TILES - 1), 0), N_BIAS_TILES, False)
    m_win = row_max(pieces)
    pt = [jnp.exp(piece - m_win) for piece in pieces]
    o_win = jnp.dot(jnp.concatenate([x.astype(BF16) for x in pt], axis=1),
                    vw_ref[0, 0, pl.ds(k0, N_BIAS_TILES * TK), :], preferred_element_type=F32) / row_sum(pt)

    g = jax.nn.sigmoid(gate_ref[0, 0])
    pieces = []
    for a in range(NSA_GROUP):
        r = slice(a * TQ, (a + 1) * TQ)
        pieces.append(g[:, 3 * a:3 * a + 1] * o_cmp[r] + g[:, 3 * a + 1:3 * a + 2] * o_sel[r]
                      + g[:, 3 * a + 2:3 * a + 3] * o_win[r])
    o_ref[...] = jnp.concatenate(pieces, axis=1)


T_PROMPT = 2048


def _bias_lookup(table, dist):
    oh = jax.nn.one_hot(rel_bucket(dist), REL_BUCKETS, dtype=F32)
    return jnp.einsum('...r,rh->...h', oh, table, precision=lax.Precision.HIGHEST)


def _prompt_bias_tables(table):
    t = T_PROMPT
    q_pos = jnp.arange(t)
    block_end = jnp.arange(N_CHUNK) * CMP_STRIDE + (CMP_BLOCK - 1)
    dist = q_pos[:, None] - block_end[None, :]
    ok = (dist >= 0) & (jnp.arange(N_CHUNK)[None, :] < N_CHUNK - 1)
    bcmp = jnp.where(ok[..., None], _bias_lookup(table, dist), NEG)
    bcmp = bcmp.reshape(t, N_CHUNK, NSA_KV_HEADS, NSA_GROUP).transpose(2, 3, 0, 1)
    d0 = jnp.arange(N_BIAS_TILES)[:, None, None]
    dist = d0 * TK + jnp.arange(TQ)[None, :, None] - jnp.arange(TK)[None, None, :]
    ok = (dist >= 0) & (dist < WINDOW)
    bt = jnp.where(ok[..., None], _bias_lookup(table, dist), NEG)
    bt = bt.reshape(N_BIAS_TILES, TQ, TK, NSA_KV_HEADS, NSA_GROUP).transpose(3, 0, 4, 1, 2)
    bt = jnp.concatenate([bt, jnp.full((NSA_KV_HEADS, 1, NSA_GROUP, TQ, TK), NEG, F32)], axis=1)
    c0 = jnp.arange(N_CHUNK) * CMP_STRIDE
    s0 = jnp.arange(LANE) * SEL_BLOCK
    ov = jnp.clip(jnp.minimum(c0[:, None] + CMP_BLOCK, s0[None, :] + SEL_BLOCK)
                  - jnp.maximum(c0[:, None], s0[None, :]), 0, CMP_BLOCK).astype(F32) / CMP_BLOCK
    ov = jnp.where((jnp.arange(N_CHUNK)[:, None] < N_CHUNK - 1) & (jnp.arange(LANE)[None, :] < t // SEL_BLOCK), ov, 0.0)
    expand = (jnp.arange(LANE)[:, None] == (jnp.arange(t) // SEL_BLOCK)[None, :]).astype(BF16)
    return bcmp, bt, ov.T.astype(BF16), expand


def _nsa_prompt(z_a, kct, vcc, kst, vs, kwt, vw, gates, tables):
    bcmp, bt, ov, expand = tables
    b, _, _, t = kst.shape
    rows = NSA_GROUP * TQ
    n_q = t // TQ
    q_cols = NSA_GROUP * HEAD_DIM
    per_bh = lambda *blk: pl.BlockSpec((1, 1) + blk, lambda bi, h, i: (bi, h) + (0,) * len(blk))
    return pl.pallas_call(
        _nsa_prompt_kernel,
        grid=(b, NSA_KV_HEADS, n_q),
        in_specs=[pl.BlockSpec((TQ, q_cols), lambda bi, h, i: (bi * n_q + i, h)),
                  per_bh(HEAD_DIM, N_CHUNK), per_bh(N_CHUNK, HEAD_DIM),
                  per_bh(HEAD_DIM, t), per_bh(t, HEAD_DIM), per_bh(HEAD_DIM, t), per_bh(t, HEAD_DIM),
                  pl.BlockSpec((1, NSA_GROUP, TQ, N_CHUNK), lambda bi, h, i: (h, 0, i, 0)),
                  pl.BlockSpec((1, N_BIAS_TILES + 1, NSA_GROUP, TQ, TK), lambda bi, h, i: (h, 0, 0, 0, 0)),
                  pl.BlockSpec((1, 1, TQ, 3 * NSA_GROUP), lambda bi, h, i: (bi, h, i, 0)),
                  pl.BlockSpec((N_CHUNK, LANE), lambda bi, h, i: (0, 0)),
                  pl.BlockSpec((LANE, t), lambda bi, h, i: (0, 0))],
        out_specs=pl.BlockSpec((TQ, q_cols), lambda bi, h, i: (bi * n_q + i, h)),
        out_shape=jax.ShapeDtypeStruct((b * t, NSA_WIDTH), F32),
        scratch_shapes=[pltpu.VMEM((TQ, t), F32), pltpu.VMEM((rows, 1), F32), pltpu.VMEM((rows, 1), F32),
                        pltpu.VMEM((rows, HEAD_DIM), F32)],
        compiler_params=pltpu.CompilerParams(dimension_semantics=("arbitrary", "arbitrary", "arbitrary"),
                                             vmem_limit_bytes=VMEM_LIMIT_V7X),
        name="nsa_prompt",
    )(z_a, kct, vcc, kst, vs, kwt, vw, bcmp, bt, gates, ov, expand)


def _heads_t(x, b, t):
    return x.astype(BF16).reshape(b, t, NSA_KV_HEADS, HEAD_DIM).transpose(0, 2, 3, 1)


def _heads(x, b, t):
    return x.astype(BF16).reshape(b, t, NSA_KV_HEADS, HEAD_DIM).transpose(0, 2, 1, 3)


def _nsa_prompt_from_proj(z_a, z_s, b, t, tables, pe, w1, w2):
    col = lambda i: z_a[:, NSA_WIDTH + KV_WIDTH * i:NSA_WIDTH + KV_WIDTH * (i + 1)]
    kcc = _compress_heads(col(0).reshape(b, t, NSA_KV_HEADS, HEAD_DIM), pe[0], w1[0], w2[0])
    vcc = _compress_heads(col(1).reshape(b, t, NSA_KV_HEADS, HEAD_DIM), pe[1], w1[1], w2[1])
    gates = z_s[:, :3 * NSA_HEADS].reshape(b, t, NSA_KV_HEADS, 3 * NSA_GROUP).transpose(0, 2, 1, 3)
    return _nsa_prompt(z_a, kcc.astype(BF16).transpose(0, 1, 3, 2), vcc.astype(BF16),
                       _heads_t(col(2), b, t), _heads(col(3), b, t), _heads_t(col(4), b, t), _heads(col(5), b, t),
                       gates, tables)


T_SAMPLE = 4
T_PAD = 8
PAGE = 128
N_PAGES = 16
PAST = N_PAGES * PAGE
N_SEL_SAMPLE = -(-(PAST + T_SAMPLE) // SEL_BLOCK)


def _nsa_sample_kernel(pt_cmp_ref, pt_slc_ref, *refs):
    del pt_cmp_ref, pt_slc_ref
    cp, sp = refs[:N_PAGES], refs[N_PAGES:2 * N_PAGES]
    (wb_ref, za_ref, zs_ref, pe2_ref, w1_ref, w2_ref, bcmp_ref, bpast_ref, bwin_ref, bnew_ref, ov_ref, exp_ref,
     o_ref, new_ref, q_ref, g_ref, xcat_ref, kcat_ref, vcat_ref, s_ref, e_ref, oacc_ref) = refs[2 * N_PAGES:]
    rows = NSA_GROUP * T_PAD

    @pl.when(pl.program_id(0) == 0)
    def _():
        new_ref[...] = jnp.zeros(new_ref.shape, F32)
        q_ref[...] = jnp.zeros(q_ref.shape, F32)
        g_ref[...] = jnp.zeros(g_ref.shape, F32)

    new_ref[0:T_SAMPLE, :] = za_ref[0, :, NSA_WIDTH + 2 * KV_WIDTH:]
    q_ref[0:T_SAMPLE, :] = za_ref[0, :, :NSA_WIDTH] * (HEAD_DIM ** -0.5)
    g_ref[0:T_SAMPLE, :] = zs_ref[0]

    comp = []
    for kv in range(2):
        def place_row_offset(l, carry, kv=kv):
            xl = jnp.concatenate([cp[p][0, pl.ds(2 * l + kv, PAGE // CMP_STRIDE, stride=2 * CMP_STRIDE), :]
                                  for p in range(N_PAGES)], axis=0)
            lanes = pl.ds(pl.multiple_of(l * KV_WIDTH, KV_WIDTH), KV_WIDTH)
            for half in range(2):
                xcat_ref[half, :, lanes] = (xl + pe2_ref[kv, pl.ds(l + CMP_STRIDE * half, 1), :]).astype(BF16)
            return carry

        lax.fori_loop(0, CMP_STRIDE, place_row_offset, 0)
        lo = jnp.dot(xcat_ref[0], w1_ref[kv, 0], preferred_element_type=F32)
        hi = jnp.dot(xcat_ref[1], w1_ref[kv, 1], preferred_element_type=F32)
        hid = jax.nn.gelu(lo + pltpu.roll(hi, N_CHUNK - 1, 0))
        out = jnp.dot(hid.astype(BF16), w2_ref[kv], preferred_element_type=F32)
        row = lax.broadcasted_iota(jnp.int32, out.shape, 0)
        comp.append(jnp.where(row < N_CHUNK - 1, out, 0.0).astype(BF16))

    for p in range(N_PAGES):
        kcat_ref[:, p * PAGE:(p + 1) * PAGE] = sp[p][0, 0:KV_WIDTH, :].astype(BF16)
        vcat_ref[:, p * PAGE:(p + 1) * PAGE] = sp[p][0, KV_WIDTH:, :].astype(BF16)
    kw_buf = wb_ref[0, 0:KV_WIDTH, :].astype(BF16)
    vw_buf = wb_ref[0, KV_WIDTH:, :].astype(BF16)
    ks_new, vs_new, kw_new, vw_new = (new_ref[:, i * KV_WIDTH:(i + 1) * KV_WIDTH].astype(BF16) for i in range(4))
    gates = jax.nn.sigmoid(g_ref[...])
    nt = (((1,), (1,)), ((), ()))
    zeros64 = jnp.zeros((rows, HEAD_DIM), F32)
    pieces = []
    key_chunk = 512

    def attend_window(qp, h):
        s_parts = [jnp.dot(qp, kw_buf, preferred_element_type=F32) + bwin_ref[h],
                   lax.dot_general(qp, kw_new, nt, preferred_element_type=F32) + bnew_ref[h]]
        m = jnp.maximum(*[jnp.max(s, axis=-1, keepdims=True) for s in s_parts])
        e_parts = [jnp.exp(s - m) for s in s_parts]
        den = jnp.add(*[jnp.sum(e, axis=-1, keepdims=True) for e in e_parts])
        num = (lax.dot_general(e_parts[0].astype(BF16), vw_buf, nt, preferred_element_type=F32)
               + jnp.dot(e_parts[1].astype(BF16), vw_new, preferred_element_type=F32))
        return (num / den)[:, h * HEAD_DIM:(h + 1) * HEAD_DIM]

    def attend_selected(qp, h):
        def scores(c, carry):
            ds = pl.ds(pl.multiple_of(c * key_chunk, key_chunk), key_chunk)
            s_ref[:, ds] += jnp.dot(qp, kcat_ref[:, ds], preferred_element_type=F32)
            return carry

        lax.fori_loop(0, PAST // key_chunk, scores, 0)
        s_ref[:, PAST:] += lax.dot_general(qp, ks_new, nt, preferred_element_type=F32)
        s = s_ref[...]
        e = jnp.exp(s - jnp.max(s, axis=-1, keepdims=True))
        den = jnp.sum(e, axis=-1, keepdims=True)
        e_ref[...] = e.astype(BF16)
        oacc_ref[...] = jnp.dot(e_ref[:, PAST:], vs_new, preferred_element_type=F32)

        def weighted(c, carry):
            ds = pl.ds(pl.multiple_of(c * key_chunk, key_chunk), key_chunk)
            oacc_ref[...] += lax.dot_general(e_ref[:, ds], vcat_ref[:, ds], nt, preferred_element_type=F32)
            return carry

        lax.fori_loop(0, PAST // key_chunk, weighted, 0)
        return (oacc_ref[...] / den)[:, h * HEAD_DIM:(h + 1) * HEAD_DIM]

    for h in range(NSA_KV_HEADS):
        q64 = jnp.concatenate([q_ref[:, (h * NSA_GROUP + a) * HEAD_DIM:(h * NSA_GROUP + a + 1) * HEAD_DIM]
                               for a in range(NSA_GROUP)], axis=0)
        qp = jnp.concatenate([q64, zeros64] if h == 0 else [zeros64, q64], axis=1).astype(BF16)

        s = lax.dot_general(qp, comp[0], nt, preferred_element_type=F32) + bcmp_ref[h]
        m = jnp.max(s, axis=-1, keepdims=True)
        e = jnp.where(s > 0.1 * NEG, jnp.exp(s - m), 0.0)
        p = e / jnp.maximum(jnp.sum(e, axis=-1, keepdims=True), 1e-30)
        o_cmp = jnp.dot(p.astype(BF16), comp[1], preferred_element_type=F32)[:, h * HEAD_DIM:(h + 1) * HEAD_DIM]

        p_sum = p[0:T_PAD] + p[T_PAD:2 * T_PAD] + p[2 * T_PAD:3 * T_PAD] + p[3 * T_PAD:4 * T_PAD]
        imp = sum(jnp.dot(part, ov_ref[...], preferred_element_type=F32) for part in _split3(p_sum))
        lane = lax.broadcasted_iota(jnp.int32, (T_PAD, LANE), 1)
        cur = (PAST + jnp.minimum(lax.broadcasted_iota(jnp.int32, (T_PAD, LANE), 0), T_SAMPLE - 1)) // SEL_BLOCK
        valid = lane <= cur
        forced = (lane == 0) | (lane == cur) | (lane == cur - 1)
        score = jnp.where(valid, imp + jnp.where(forced, SEL_FORCE, 0.0), -1.0)
        rank = jnp.zeros((T_PAD, LANE), F32)
        for c in range(N_SEL_SAMPLE):
            sc = score[:, c:c + 1]
            beats = (sc > score) | ((sc == score) & (lane > c))
            rank = rank + jnp.where(beats, 1.0, 0.0)
        sel = jnp.where(valid & (rank < SEL_TOP_N), 1.0, 0.0)
        selneg = (jnp.dot(sel.astype(BF16), exp_ref[...], preferred_element_type=F32) - 1.0) * (-NEG)
        selneg_new = (sel[:, PAST // SEL_BLOCK:PAST // SEL_BLOCK + 1] - 1.0) * (-NEG)

        def per_token(bias, tok):
            n = bias.shape[-1]
            return (bias.reshape(NSA_GROUP, T_PAD, n) + tok[None]).reshape(rows, n)

        s_ref[:, :PAST] = per_token(bpast_ref[h], selneg)
        s_ref[:, PAST:] = per_token(bnew_ref[h], jnp.broadcast_to(selneg_new, (T_PAD, LANE)))
        o_sel = attend_selected(qp, h)
        o_win = attend_window(qp, h)

        for a in range(NSA_GROUP):
            r = slice(a * T_PAD, (a + 1) * T_PAD)
            c0 = (h * NSA_GROUP + a) * 3
            pieces.append(gates[:, c0:c0 + 1] * o_cmp[r] + gates[:, c0 + 1:c0 + 2] * o_sel[r]
                          + gates[:, c0 + 2:c0 + 3] * o_win[r])
    o_ref[0] = jnp.concatenate(pieces, axis=1)[0:T_SAMPLE]


def _sample_bias_tables(table):
    tq = jnp.minimum(jnp.arange(T_PAD), T_SAMPLE - 1)
    q_pos = PAST + tq

    def lay(x):
        n = x.shape[1]
        return x.reshape(T_PAD, n, NSA_KV_HEADS, NSA_GROUP).transpose(2, 3, 0, 1).reshape(NSA_KV_HEADS, -1, n)

    block_end = jnp.arange(N_CHUNK) * CMP_STRIDE + (CMP_BLOCK - 1)
    dist = q_pos[:, None] - block_end[None, :]
    ok = (dist >= 0) & (jnp.arange(N_CHUNK)[None, :] < N_CHUNK - 1)
    bcmp = lay(jnp.where(ok[..., None], _bias_lookup(table, dist), NEG))
    dist = q_pos[:, None] - jnp.arange(PAST)[None, :]
    past = _bias_lookup(table, dist)
    bpast = lay(past)
    w_buf = min(WINDOW, PAST)
    bwin = lay(jnp.where((dist < WINDOW)[:, PAST - w_buf:, None], past[:, PAST - w_buf:], NEG))
    j = jnp.arange(LANE)
    dist = tq[:, None] - j[None, :]
    ok = (dist >= 0) & (j[None, :] < T_SAMPLE)
    bnew = lay(jnp.where(ok[..., None], _bias_lookup(table, dist), NEG))
    c0 = jnp.arange(N_CHUNK) * CMP_STRIDE
    s0 = jnp.arange(LANE) * SEL_BLOCK
    ov = jnp.clip(jnp.minimum(c0[:, None] + CMP_BLOCK, s0[None, :] + SEL_BLOCK)
                  - jnp.maximum(c0[:, None], s0[None, :]), 0, CMP_BLOCK).astype(F32) / CMP_BLOCK
    ov = jnp.where((jnp.arange(N_CHUNK)[:, None] < N_CHUNK - 1) & (j[None, :] < N_SEL_SAMPLE), ov, 0.0)
    expand = (j[:, None] == (jnp.arange(PAST) // SEL_BLOCK)[None, :]).astype(BF16)
    return bcmp, bpast, bwin, bnew, ov.astype(BF16), expand


def _nsa_sample(za, zs, cache_cmp, cache_slc, cache_win, page_table, layer, tables, pe, w1, w2):
    b = za.shape[0]
    n_phys = cache_cmp.shape[1]
    cmp_pages = cache_cmp[layer][page_table].reshape(b * N_PAGES, 2 * PAGE, KV_WIDTH)
    pt_cmp = jnp.arange(b * N_PAGES, dtype=jnp.int32)
    slc_pages = jnp.transpose(cache_slc, (0, 1, 3, 4, 5, 2)).reshape(DEPTH * n_phys, 2 * KV_WIDTH, PAGE)
    pt_slc = page_table.reshape(-1).astype(jnp.int32) + layer * n_phys
    w_buf = cache_win.shape[2]
    win = jnp.transpose(cache_win, (0, 1, 3, 4, 5, 2)).reshape(DEPTH * b, 2 * KV_WIDTH, w_buf)
    bcmp, bpast, bwin, bnew, ov, expand = tables
    pe2 = jnp.concatenate([pe, pe], axis=-1)
    cmp_spec = lambda p: pl.BlockSpec((1, 2 * PAGE, KV_WIDTH), lambda bi, ptc, pts: (ptc[bi * N_PAGES + p], 0, 0))
    slc_spec = lambda p: pl.BlockSpec((1, 2 * KV_WIDTH, PAGE), lambda bi, ptc, pts: (pts[bi * N_PAGES + p], 0, 0))
    whole = lambda x: pl.BlockSpec(x.shape, lambda bi, ptc, pts: (0,) * x.ndim)
    eye = jnp.eye(NSA_KV_HEADS, dtype=F32)
    w1h = w1.reshape(2, 2, CMP_STRIDE, HEAD_DIM, CMP_HIDDEN)
    w1_bd = jnp.einsum('khlde,ab->khladbe', w1h, eye).reshape(2, 2, CMP_STRIDE * KV_WIDTH, NSA_KV_HEADS * CMP_HIDDEN)
    w2_bd = jnp.einsum('ked,ab->kaebd', w2, eye).reshape(2, NSA_KV_HEADS * CMP_HIDDEN, KV_WIDTH)
    consts = [pe2, w1_bd.astype(BF16), w2_bd.astype(BF16), bcmp, bpast, bwin, bnew, ov, expand]
    grid_spec = pltpu.PrefetchScalarGridSpec(
        num_scalar_prefetch=2,
        grid=(b,),
        in_specs=[cmp_spec(p) for p in range(N_PAGES)] + [slc_spec(p) for p in range(N_PAGES)]
        + [pl.BlockSpec((1, 2 * KV_WIDTH, w_buf), lambda bi, ptc, pts: (layer * b + bi, 0, 0)),
           pl.BlockSpec((1, T_SAMPLE, za.shape[-1]), lambda bi, ptc, pts: (bi, 0, 0)),
           pl.BlockSpec((1, T_SAMPLE, LANE), lambda bi, ptc, pts: (bi, 0, 0))]
        + [whole(x) for x in consts],
        out_specs=pl.BlockSpec((1, T_SAMPLE, NSA_WIDTH), lambda bi, ptc, pts: (bi, 0, 0)),
        scratch_shapes=[pltpu.VMEM((PAGE, 4 * KV_WIDTH), F32), pltpu.VMEM((T_PAD, NSA_WIDTH), F32),
                        pltpu.VMEM((T_PAD, LANE), F32), pltpu.VMEM((2, N_CHUNK, CMP_STRIDE * KV_WIDTH), BF16),
                        pltpu.VMEM((KV_WIDTH, PAST), BF16), pltpu.VMEM((KV_WIDTH, PAST), BF16),
                        pltpu.VMEM((NSA_GROUP * T_PAD, PAST + PAGE), F32),
                        pltpu.VMEM((NSA_GROUP * T_PAD, PAST + PAGE), BF16),
                        pltpu.VMEM((NSA_GROUP * T_PAD, KV_WIDTH), F32)],
    )
    return pl.pallas_call(
        _nsa_sample_kernel,
        grid_spec=grid_spec,
        out_shape=jax.ShapeDtypeStruct((b, T_SAMPLE, NSA_WIDTH), F32),
        compiler_params=pltpu.CompilerParams(dimension_semantics=("arbitrary",), vmem_limit_bytes=VMEM_LIMIT_V7X),
        name="nsa_sample",
    )(pt_cmp, pt_slc, *([cmp_pages] * N_PAGES), *([slc_pages] * N_PAGES), win, za, zs, *consts)


I_LANE = 3 * NSA_HEADS
F_LANE = I_LANE + M_HEADS
EXT_PAD = 8


def _mlstm_kernel(zu_ref, zs_ref, c0_ref, n0_ref, m0_ref, cprev_ref, convw_ref, convb_ref, wq_ref, wk_ref, wkt_ref,
                  wv_ref, gb_ref, normg_ref, tril_ref, selrow_ref,
                  o_ref, c_out, n_out, m_out, conv_out,
                  ext_ref, zsp_ref, c_sc, n_sc, m_sc, *, rows, rows_pad, n_valid):
    i = pl.program_id(1)
    L = M_CHUNK

    @pl.when(i == 0)
    def _():
        c_sc[...] = c0_ref[0]
        n_sc[...] = n0_ref[0]
        m_sc[...] = m0_ref[0]
        ext_ref[EXT_PAD - (CONV_W - 1):EXT_PAD, :] = cprev_ref[0]

    if rows < rows_pad:
        ext_ref[EXT_PAD:, :] = jnp.zeros((rows_pad, M_WIDTH), F32)
        zsp_ref[...] = jnp.zeros(zsp_ref.shape, F32)
    ext_ref[EXT_PAD:EXT_PAD + rows, :] = zu_ref[0, :, :M_WIDTH]
    zsp_ref[0:rows, :] = zs_ref[0]

    u = ext_ref[EXT_PAD:EXT_PAD + rows_pad, :]
    conv = convb_ref[...] + convw_ref[CONV_W - 1:CONV_W, :] * u
    for j in range(CONV_W - 1):
        lo = EXT_PAD - (CONV_W - 1) + j
        conv = conv + convw_ref[j:j + 1, :] * ext_ref[lo:lo + rows_pad, :]
    uc = (conv * jax.nn.sigmoid(conv)).astype(BF16)
    ub = u.astype(BF16)

    zb = zsp_ref[...] + gb_ref[...]
    bcum = sum(jnp.dot(tril_ref[...], part, preferred_element_type=F32) for part in _split3(jax.nn.log_sigmoid(zb)))
    bcum = pltpu.roll(bcum, LANE - M_HEADS, 1)
    a_col = zb - bcum
    a_row = sum(lax.dot_general(selrow_ref[...], part, (((1,), (1,)), ((), ())), preferred_element_type=F32)
                for part in _split3(a_col))

    tt = lax.broadcasted_iota(jnp.int32, (L, L), 0)
    ss = lax.broadcasted_iota(jnp.int32, (L, L), 1)
    causal = ss <= tt
    tok_col = lax.broadcasted_iota(jnp.int32, (L, 1), 0)
    tok_row = lax.broadcasted_iota(jnp.int32, (1, L), 1)

    for h in range(M_HEADS):
        hs = slice(h * M_HEAD_DIM, (h + 1) * M_HEAD_DIM)
        q_all = jnp.dot(uc[:, hs], wq_ref[h].astype(BF16), preferred_element_type=F32) * (M_HEAD_DIM ** -0.5)
        k_all = jnp.dot(uc[:, hs], wk_ref[h].astype(BF16), preferred_element_type=F32)
        kt_all = lax.dot_general(wkt_ref[h].astype(BF16), uc[:, hs], (((1,), (1,)), ((), ())),
                                 preferred_element_type=F32)
        v_all = jnp.dot(ub[:, hs], wv_ref[h].astype(BF16), preferred_element_type=F32)
        c_st = c_sc[h]
        n_st = n_sc[h:h + 1, :]
        m_st = m_sc[h:h + 1, 0:1]
        for c in range(rows_pad // L):
            r = slice(c * L, (c + 1) * L)
            q, k, kt, v = q_all[r].astype(BF16), k_all[r], kt_all[:, r], v_all[r].astype(BF16)
            b_col = bcum[r, I_LANE + h:I_LANE + h + 1]
            a_c = a_col[r, I_LANE + h:I_LANE + h + 1]
            a_r = a_row[h:h + 1, r]
            log_d = jnp.where(causal, b_col + a_r, NEG)
            m_col = jnp.maximum(b_col + m_st, jnp.max(log_d, axis=-1, keepdims=True))
            dw = jnp.exp(log_d - m_col)
            w_inter = jnp.exp(b_col + m_st - m_col)
            s = lax.dot_general(q, k.astype(BF16), (((1,), (1,)), ((), ())), preferred_element_type=F32) * dw
            num = (jnp.dot(s.astype(BF16), v, preferred_element_type=F32)
                   + w_inter * jnp.dot(q, c_st.astype(BF16), preferred_element_type=F32))
            den = (jnp.sum(s, axis=-1, keepdims=True)
                   + w_inter * jnp.sum(q_all[r] * n_st, axis=-1, keepdims=True))
            hh = num / jnp.maximum(jnp.abs(den), jnp.exp(-m_col))
            hn = hh * lax.rsqrt(jnp.mean(hh * hh, axis=-1, keepdims=True) + EPS)
            o_pre = zu_ref[0, :, M_WIDTH + h * M_HEAD_DIM:M_WIDTH + (h + 1) * M_HEAD_DIM]
            if rows < rows_pad:
                o_ref[0, :, hs] = jax.nn.sigmoid(o_pre) * hn[0:rows] * normg_ref[:, hs]
            else:
                o_ref[0, r, hs] = jax.nn.sigmoid(o_pre[r]) * hn * normg_ref[:, hs]
            b_last = b_col[n_valid - 1:n_valid, :]
            log_s = jnp.where(tok_col < n_valid, b_last + a_c, NEG)
            m_new = jnp.maximum(b_last + m_st, jnp.max(log_s, axis=0, keepdims=True))
            ws_col = jnp.exp(log_s - m_new)
            ws_row = jnp.where(tok_row < n_valid, jnp.exp(b_last + a_r - m_new), 0.0)
            wc = jnp.exp(b_last + m_st - m_new)
            c_st = wc * c_st + jnp.dot((kt * ws_row).astype(BF16), v, preferred_element_type=F32)
            n_st = wc * n_st + jnp.sum(k * ws_col, axis=0, keepdims=True)
            m_st = m_new
        c_sc[h] = c_st
        n_sc[h:h + 1, :] = n_st
        m_sc[h:h + 1, :] = jnp.broadcast_to(m_st, (1, LANE))

    tail = ext_ref[EXT_PAD + rows - (CONV_W - 1):EXT_PAD + rows, :]
    ext_ref[EXT_PAD - (CONV_W - 1):EXT_PAD, :] = tail

    @pl.when(i == pl.num_programs(1) - 1)
    def _():
        c_out[0] = c_sc[...]
        n_out[0] = n_sc[...]
        m_out[0] = m_sc[...]
        conv_out[0] = jnp.zeros(conv_out.shape[1:], F32)
        conv_out[0, 0:CONV_W - 1, :] = tail


def _mlstm(zu, zs, state, conv_prev, conv_w, conv_b, m_qkv, gate_b, norm_g):
    b, t, _ = zu.shape
    rows = min(t, 4 * M_CHUNK)
    rows_pad = -(-rows // M_CHUNK) * M_CHUNK
    n_valid = M_CHUNK if rows == rows_pad else rows
    assert t % rows == 0 and (rows == rows_pad or t == rows)
    c0, n0, m0 = state
    n0p = jnp.pad(n0, ((0, 0), (0, 8 - M_HEADS), (0, 0)))
    m0p = jnp.pad(jnp.broadcast_to(m0[:, :, None], (b, M_HEADS, LANE)), ((0, 0), (0, 8 - M_HEADS), (0, 0)))
    gb = jnp.zeros((1, LANE), F32).at[0, I_LANE:I_LANE + 2 * M_HEADS].set(gate_b.reshape(-1))
    idx = jnp.arange(rows_pad)
    tril = ((idx[:, None] >= idx[None, :]) & (idx[:, None] // M_CHUNK == idx[None, :] // M_CHUNK)).astype(BF16)
    selrow = ((jnp.arange(16)[:, None] + I_LANE == jnp.arange(LANE)[None, :])
              & (jnp.arange(16)[:, None] < M_HEADS)).astype(BF16)
    whole = lambda *shape: pl.BlockSpec(shape, lambda bi, i: (0,) * len(shape))
    per_b = lambda *shape: pl.BlockSpec((1,) + shape, lambda bi, i: (bi,) + (0,) * len(shape))
    out, c_new, n_new, m_new, conv_new = pl.pallas_call(
        functools.partial(_mlstm_kernel, rows=rows, rows_pad=rows_pad, n_valid=n_valid),
        grid=(b, t // rows),
        in_specs=[pl.BlockSpec((1, rows, 2 * M_WIDTH), lambda bi, i: (bi, i, 0)),
                  pl.BlockSpec((1, rows, LANE), lambda bi, i: (bi, i, 0)),
                  per_b(M_HEADS, M_HEAD_DIM, M_HEAD_DIM), per_b(8, M_HEAD_DIM), per_b(8, LANE),
                  per_b(CONV_W - 1, M_WIDTH),
                  whole(CONV_W, M_WIDTH), whole(1, M_WIDTH),
                  whole(M_HEADS, M_HEAD_DIM, M_HEAD_DIM), whole(M_HEADS, M_HEAD_DIM, M_HEAD_DIM),
                  whole(M_HEADS, M_HEAD_DIM, M_HEAD_DIM), whole(M_HEADS, M_HEAD_DIM, M_HEAD_DIM),
                  whole(1, LANE), whole(1, M_WIDTH), whole(rows_pad, rows_pad), whole(16, LANE)],
        out_specs=[pl.BlockSpec((1, rows, M_WIDTH), lambda bi, i: (bi, i, 0)),
                   per_b(M_HEADS, M_HEAD_DIM, M_HEAD_DIM), per_b(8, M_HEAD_DIM), per_b(8, LANE),
                   per_b(8, M_WIDTH)],
        out_shape=[jax.ShapeDtypeStruct((b, t, M_WIDTH), F32),
                   jax.ShapeDtypeStruct((b, M_HEADS, M_HEAD_DIM, M_HEAD_DIM), F32),
                   jax.ShapeDtypeStruct((b, 8, M_HEAD_DIM), F32),
                   jax.ShapeDtypeStruct((b, 8, LANE), F32),
                   jax.ShapeDtypeStruct((b, 8, M_WIDTH), F32)],
        scratch_shapes=[pltpu.VMEM((EXT_PAD + rows_pad, M_WIDTH), F32), pltpu.VMEM((rows_pad, LANE), F32),
                        pltpu.VMEM((M_HEADS, M_HEAD_DIM, M_HEAD_DIM), F32), pltpu.VMEM((8, M_HEAD_DIM), F32),
                        pltpu.VMEM((8, LANE), F32)],
        compiler_params=pltpu.CompilerParams(dimension_semantics=("arbitrary", "arbitrary"),
                                             vmem_limit_bytes=VMEM_LIMIT_V7X),
        name="mlstm",
    )(zu, zs, c0, n0p, m0p, conv_prev, conv_w, conv_b.reshape(1, M_WIDTH), m_qkv[0], m_qkv[1],
      m_qkv[1].transpose(0, 2, 1),
      m_qkv[2], gb, norm_g.reshape(1, M_WIDTH), tril, selrow)
    return out, (c_new, n_new[:, :M_HEADS], m_new[:, :M_HEADS, 0], conv_new[:, :CONV_W - 1])


def rel_bucket(dist):
    d = jnp.maximum(dist, 0)
    exact = REL_BUCKETS // 2
    log_part = exact + (jnp.log(jnp.maximum(d, 1).astype(F32) / exact)
                        / math.log(REL_MAX_DIST / exact) * (REL_BUCKETS - exact)).astype(jnp.int32)
    return jnp.where(d < exact, d, jnp.minimum(log_part, REL_BUCKETS - 1))


def _split_w_in(w_in):
    offs = np.cumsum((0,) + IN_WIDTHS)
    w_a = w_in[:, :offs[7]]
    w_small = jnp.concatenate([w_in[:, offs[7]:offs[8]], w_in[:, offs[10]:offs[11]]], axis=1)
    w_small = jnp.pad(w_small, ((0, 0), (0, LANE - w_small.shape[1])))
    w_u = w_in[:, offs[8]:offs[10]]
    w_g = w_in[:, offs[11]:]
    return w_a, w_small, w_u, w_g


def mixer(x, norm_g, prm, past):
    b, t, _ = x.shape
    x2 = x.reshape(b * t, D_MODEL)
    z_a, z_s, z_u, z_g = _proj(x2, norm_g, [w.astype(BF16) for w in _split_w_in(prm['w_in'])])
    z_a, z_s, z_u = (z.reshape(b, t, -1) for z in (z_a, z_s, z_u))
    new_cmp, new_slc, win_rows = (z_a[..., NSA_WIDTH + 2 * KV_WIDTH * i:NSA_WIDTH + 2 * KV_WIDTH * (i + 1)]
                                  .reshape(b, t, 2, NSA_KV_HEADS, HEAD_DIM) for i in range(3))
    if past is None:
        o_nsa = _nsa_prompt_from_proj(z_a.reshape(b * t, -1), z_s.reshape(b * t, -1), b, t, prm['prompt_tables'],
                                      prm['cmp_pe'], prm['cmp_w1'], prm['cmp_w2'])
        new_win = win_rows[:, t - min(WINDOW, t):]
        conv_prev = jnp.zeros((b, CONV_W - 1, M_WIDTH), F32)
        m_state = (jnp.zeros((b, M_HEADS, M_HEAD_DIM, M_HEAD_DIM), F32),
                   jnp.zeros((b, M_HEADS, M_HEAD_DIM), F32),
                   jnp.zeros((b, M_HEADS), F32))
    else:
        layer = past['layer']
        assert t == T_SAMPLE and past['page_table'].shape[1] == N_PAGES and past['cmp'].shape[2] == PAGE
        o_nsa = _nsa_sample(z_a, z_s, past['cmp'], past['slc'], past['win'], past['page_table'], layer,
                            prm['sample_tables'], prm['cmp_pe'], prm['cmp_w1'], prm['cmp_w2'])
        o_nsa = o_nsa.reshape(b * t, NSA_WIDTH)
        new_win = jnp.concatenate([past['win'][layer][:, t:], win_rows], axis=1)
        conv_prev = past['conv']
        m_state = (past['C'].astype(F32), past['n'].astype(F32), past['m'].astype(F32))
    o_m, (c_new, n_new, m_new, conv_new) = _mlstm(z_u, z_s, m_state, conv_prev, prm['conv_w'], prm['conv_b'],
                                                  prm['m_qkv'], prm['gate_b'], prm['m_norm'])
    y = _merge_out(x2, o_nsa, o_m.reshape(b * t, M_WIDTH), z_g, prm['w_up_a'].astype(BF16),
                   prm['w_up_b'].astype(BF16), prm['w_out'].astype(BF16)).reshape(b, t, D_MODEL)
    return y, (new_cmp, new_slc, new_win, c_new, n_new, m_new, conv_new)


def _channel_mixer(x, g, l, ffn_w1, ffn_w3, ffn_w2, moe_router, moe_w1, moe_w3, moe_w2):
    b, t, d = x.shape
    x2 = x.reshape(b * t, d)
    i = l // 2
    if l % 2 == 0:
        y = _ffn_dense(x2, g, ffn_w1[i], ffn_w3[i], ffn_w2[i])
    else:
        y = _moe(x2, g, moe_router[i], moe_w1[i], moe_w3[i], moe_w2[i])
    return y.reshape(b, t, d)


def _final_norm(x, g):
    xf = x.astype(F32)
    return xf * lax.rsqrt(jnp.mean(xf * xf, axis=-1, keepdims=True) + EPS) * g


def kernel(x_prompt, x_sample, cache_cmp_kv, cache_slc_kv, cache_win_kv, state_mlstm_C, state_mlstm_n,
           state_mlstm_m, state_mlstm_conv, page_table, rel_bias_table, norm_mix, norm_ffn, norm_final,
           w_in, cmp_pe, cmp_w1, cmp_w2, m_conv_w, m_conv_b, m_qkv, m_gate_bias, m_norm, w_up_a, w_up_b,
           w_out, ffn_w1, ffn_w3, ffn_w2, moe_router, moe_w1, moe_w3, moe_w2):
    xp, xs = x_prompt, x_sample
    prompt_states, sample_states = [], []
    prompt_tables = _prompt_bias_tables(rel_bias_table)
    sample_tables = _sample_bias_tables(rel_bias_table)
    for l in range(DEPTH):
        prm = {'prompt_tables': prompt_tables, 'sample_tables': sample_tables, 'w_in': w_in[l], 'cmp_pe': cmp_pe[l],
               'cmp_w1': cmp_w1[l], 'cmp_w2': cmp_w2[l], 'conv_w': m_conv_w[l], 'conv_b': m_conv_b[l],
               'm_qkv': m_qkv[l], 'gate_b': m_gate_bias[l], 'm_norm': m_norm[l], 'w_up_a': w_up_a[l],
               'w_up_b': w_up_b[l], 'w_out': w_out[l]}
        past = {'cmp': cache_cmp_kv, 'slc': cache_slc_kv, 'win': cache_win_kv, 'layer': l,
                'C': state_mlstm_C[l], 'n': state_mlstm_n[l], 'm': state_mlstm_m[l],
                'conv': state_mlstm_conv[l], 'page_table': page_table}
        xp, st_p = mixer(xp, norm_mix[l], prm, None)
        xs, st_s = mixer(xs, norm_mix[l], prm, past)
        xp = _channel_mixer(xp, norm_ffn[l], l, ffn_w1, ffn_w3, ffn_w2, moe_router, moe_w1, moe_w3, moe_w2)
        xs = _channel_mixer(xs, norm_ffn[l], l, ffn_w1, ffn_w3, ffn_w2, moe_router, moe_w1, moe_w3, moe_w2)
        prompt_states.append(st_p)
        sample_states.append(st_s)
    y_prompt = _final_norm(xp, norm_final)
    y_sample = _final_norm(xs, norm_final)
    ps = [jnp.stack([s[i] for s in prompt_states]) for i in range(7)]
    ss = [jnp.stack([s[i] for s in sample_states]) for i in range(7)]
    return (y_prompt, y_sample, ps[0], ps[1], ps[2], ps[3], ps[4], ps[5], ps[6],
            ss[0], ss[1], ss[2], ss[3], ss[4], ss[5], ss[6])
```

```python
import functools
import math

import jax
import jax.numpy as jnp
import numpy as np
from jax import lax
from jax.experimental import pallas as pl
from jax.experimental.pallas import tpu as pltpu

D_MODEL = 1024
DEPTH = 2
NSA_HEADS = 8
NSA_KV_HEADS = 2
NSA_GROUP = NSA_HEADS // NSA_KV_HEADS
HEAD_DIM = 64
NSA_WIDTH = NSA_HEADS * HEAD_DIM
KV_WIDTH = NSA_KV_HEADS * HEAD_DIM
CMP_BLOCK = 32
CMP_STRIDE = 16
CMP_HIDDEN = 128
SEL_BLOCK = 64
SEL_TOP_N = 16
SEL_FORCE = 1e4
WINDOW = 512
WIN_Q_BLOCK = 128
REL_BUCKETS = 32
REL_MAX_DIST = 128
M_HEADS = 4
M_HEAD_DIM = 128
M_WIDTH = M_HEADS * M_HEAD_DIM
CONV_W = 4
M_CHUNK = 64
D_FF = 2816
N_EXPERTS = 8
TOP_K = 2
EPS = 1e-6
IN_WIDTHS = (NSA_WIDTH, KV_WIDTH, KV_WIDTH, KV_WIDTH, KV_WIDTH, KV_WIDTH, KV_WIDTH, 3 * NSA_HEADS, M_WIDTH, M_WIDTH,
             2 * M_HEADS, D_MODEL, D_MODEL)

VMEM_LIMIT_V7X = 52 * 1024 * 1024
LANE = 128

F32 = jnp.float32
BF16 = jnp.bfloat16


def _split3(x):
    hi = x.astype(BF16)
    r1 = x - hi.astype(F32)
    mid = r1.astype(BF16)
    lo = (r1 - mid.astype(F32)).astype(BF16)
    return hi, mid, lo


def _pick_tile(n, cands):
    for c in cands:
        if n % c == 0:
            return c
    return n


def _mm_kernel(*refs, norm, has_res):
    x_ref, g_ref, w_ref = refs[:3]
    res_ref = refs[3] if has_res else None
    o_ref, xs_ref = refs[-2], refs[-1]

    @pl.when(pl.program_id(1) == 0)
    def _():
        x = x_ref[...]
        if norm:
            x = x * lax.rsqrt(jnp.mean(x * x, axis=-1, keepdims=True) + EPS) * g_ref[...]
        xs_ref[...] = x.astype(BF16)

    acc = jnp.dot(xs_ref[...], w_ref[...].astype(BF16), preferred_element_type=F32)
    if has_res:
        acc = acc + res_ref[...]
    o_ref[...] = acc


def _mm(x, w, g=None, res=None):
    m, k = x.shape
    n = w.shape[1]
    tm = _pick_tile(m, (1024, 512, 256, 128))
    tn = _pick_tile(n, (512, 256, 128))
    norm = g is not None
    gg = (g if norm else jnp.ones((k,), F32)).reshape(1, k)
    in_specs = [pl.BlockSpec((tm, k), lambda i, j: (i, 0)),
                pl.BlockSpec((1, k), lambda i, j: (0, 0)),
                pl.BlockSpec((k, tn), lambda i, j: (0, j))]
    args = [x, gg, w]
    if res is not None:
        in_specs.append(pl.BlockSpec((tm, tn), lambda i, j: (i, j)))
        args.append(res)
    return pl.pallas_call(
        functools.partial(_mm_kernel, norm=norm, has_res=res is not None),
        grid=(m // tm, n // tn),
        in_specs=in_specs,
        out_specs=pl.BlockSpec((tm, tn), lambda i, j: (i, j)),
        out_shape=jax.ShapeDtypeStruct((m, n), F32),
        scratch_shapes=[pltpu.VMEM((tm, k), BF16)],
        compiler_params=pltpu.CompilerParams(dimension_semantics=("arbitrary", "arbitrary"),
                                             vmem_limit_bytes=VMEM_LIMIT_V7X),
        name="mm",
    )(*args)


def _proj_kernel(x_ref, g_ref, *refs):
    n_out = len(refs) // 2
    x = x_ref[...]
    xn = (x * lax.rsqrt(jnp.mean(x * x, axis=-1, keepdims=True) + EPS) * g_ref[...]).astype(BF16)
    for w_ref, o_ref in zip(refs[:n_out], refs[n_out:]):
        o_ref[...] = jnp.dot(xn, w_ref[...], preferred_element_type=F32)


def _proj(x, g, weights):
    m, k = x.shape
    tm = _pick_tile(m, (256, 128))
    row = lambda n: pl.BlockSpec((tm, n), lambda i: (i, 0))
    whole = lambda a: pl.BlockSpec(a.shape, lambda i: (0,) * a.ndim)
    return pl.pallas_call(
        _proj_kernel,
        grid=(m // tm,),
        in_specs=[row(k), whole(g.reshape(1, k))] + [whole(w) for w in weights],
        out_specs=[row(w.shape[1]) for w in weights],
        out_shape=[jax.ShapeDtypeStruct((m, w.shape[1]), F32) for w in weights],
        compiler_params=pltpu.CompilerParams(dimension_semantics=("arbitrary",), vmem_limit_bytes=VMEM_LIMIT_V7X),
        name="proj",
    )(x, g.reshape(1, k), *weights)


def _merge_out_kernel(x_ref, oa_ref, ob_ref, zg_ref, wa_ref, wb_ref, wo_ref, o_ref):
    d = x_ref.shape[-1]
    up_a = jnp.dot(oa_ref[...].astype(BF16), wa_ref[...], preferred_element_type=F32)
    up_b = jnp.dot(ob_ref[...].astype(BF16), wb_ref[...], preferred_element_type=F32)
    merged = jax.nn.sigmoid(zg_ref[:, :d]) * up_a + jax.nn.sigmoid(zg_ref[:, d:]) * up_b
    o_ref[...] = x_ref[...] + jnp.dot(merged.astype(BF16), wo_ref[...], preferred_element_type=F32)


def _merge_out(x, o_a, o_b, z_g, w_up_a, w_up_b, w_out):
    m, d = x.shape
    tm = _pick_tile(m, (512, 256, 128))
    row = lambda a: pl.BlockSpec((tm, a.shape[1]), lambda i: (i, 0))
    whole = lambda a: pl.BlockSpec(a.shape, lambda i: (0,) * a.ndim)
    return pl.pallas_call(
        _merge_out_kernel,
        grid=(m // tm,),
        in_specs=[row(x), row(o_a), row(o_b), row(z_g), whole(w_up_a), whole(w_up_b), whole(w_out)],
        out_specs=row(x),
        out_shape=jax.ShapeDtypeStruct((m, d), F32),
        compiler_params=pltpu.CompilerParams(dimension_semantics=("arbitrary",), vmem_limit_bytes=VMEM_LIMIT_V7X),
        name="merge_out",
    )(x, o_a, o_b, z_g, w_up_a, w_up_b, w_out)


def _ffn_body(x_ref, g_ref, w1_ref, w3_ref, w2_ref, o_ref, xs_ref, acc_ref, *, residual, grouped):
    j = pl.program_id(1)

    @pl.when(j == 0)
    def _():
        x = x_ref[...]
        xn = x * lax.rsqrt(jnp.mean(x * x, axis=-1, keepdims=True) + EPS) * g_ref[...]
        xs_ref[...] = xn.astype(BF16)
        acc_ref[...] = jnp.zeros_like(acc_ref)

    xs = xs_ref[...]
    w1 = w1_ref[0] if grouped else w1_ref[...]
    w3 = w3_ref[0] if grouped else w3_ref[...]
    w2 = w2_ref[0] if grouped else w2_ref[...]
    a = jnp.dot(xs, w1.astype(BF16), preferred_element_type=F32)
    b = jnp.dot(xs, w3.astype(BF16), preferred_element_type=F32)
    h = (a * jax.nn.sigmoid(a) * b).astype(BF16)
    acc_ref[...] += jnp.dot(h, w2.astype(BF16), preferred_element_type=F32)

    @pl.when(j == pl.num_programs(1) - 1)
    def _():
        if residual:
            o_ref[...] = x_ref[...] + acc_ref[...]
        else:
            o_ref[...] = acc_ref[...]


def _ffn_dense_kernel(x_ref, g_ref, w1_ref, w3_ref, w2_ref, o_ref, xs_ref, acc_ref):
    _ffn_body(x_ref, g_ref, w1_ref, w3_ref, w2_ref, o_ref, xs_ref, acc_ref, residual=True, grouped=False)


def _ffn_grouped_kernel(be_ref, x_ref, g_ref, w1_ref, w3_ref, w2_ref, o_ref, xs_ref, acc_ref):
    del be_ref
    _ffn_body(x_ref, g_ref, w1_ref, w3_ref, w2_ref, o_ref, xs_ref, acc_ref, residual=False, grouped=True)


def _ffn_dense(x, g, w1, w3, w2):
    m, d = x.shape
    f = w1.shape[1]
    tm = _pick_tile(m, (1024, 512, 256, 128))
    tf = _pick_tile(f, (256, 128))
    return pl.pallas_call(
        _ffn_dense_kernel,
        grid=(m // tm, f // tf),
        in_specs=[pl.BlockSpec((tm, d), lambda i, j: (i, 0)),
                  pl.BlockSpec((1, d), lambda i, j: (0, 0)),
                  pl.BlockSpec((d, tf), lambda i, j: (0, j)),
                  pl.BlockSpec((d, tf), lambda i, j: (0, j)),
                  pl.BlockSpec((tf, d), lambda i, j: (j, 0))],
        out_specs=pl.BlockSpec((tm, d), lambda i, j: (i, 0)),
        out_shape=jax.ShapeDtypeStruct((m, d), F32),
        scratch_shapes=[pltpu.VMEM((tm, d), BF16), pltpu.VMEM((tm, d), F32)],
        compiler_params=pltpu.CompilerParams(dimension_semantics=("arbitrary", "arbitrary"),
                                             vmem_limit_bytes=VMEM_LIMIT_V7X),
        name="ffn_dense",
    )(x, g.reshape(1, d), w1, w3, w2)


def _ffn_grouped(xd, blk_e, g, w1, w3, w2, tm):
    rows, d = xd.shape
    f = w1.shape[2]
    tf = _pick_tile(f, (256, 128))
    grid_spec = pltpu.PrefetchScalarGridSpec(
        num_scalar_prefetch=1,
        grid=(rows // tm, f // tf),
        in_specs=[pl.BlockSpec((tm, d), lambda i, j, be: (i, 0)),
                  pl.BlockSpec((1, d), lambda i, j, be: (0, 0)),
                  pl.BlockSpec((1, d, tf), lambda i, j, be: (be[i], 0, j)),
                  pl.BlockSpec((1, d, tf), lambda i, j, be: (be[i], 0, j)),
                  pl.BlockSpec((1, tf, d), lambda i, j, be: (be[i], j, 0))],
        out_specs=pl.BlockSpec((tm, d), lambda i, j, be: (i, 0)),
        scratch_shapes=[pltpu.VMEM((tm, d), BF16), pltpu.VMEM((tm, d), F32)],
    )
    return pl.pallas_call(
        _ffn_grouped_kernel,
        grid_spec=grid_spec,
        out_shape=jax.ShapeDtypeStruct((rows, d), F32),
        compiler_params=pltpu.CompilerParams(dimension_semantics=("arbitrary", "arbitrary"),
                                             vmem_limit_bytes=VMEM_LIMIT_V7X),
        name="ffn_grouped",
    )(blk_e, xd, g.reshape(1, d), w1, w3, w2)


def _moe(x, g, router, w1, w3, w2):
    n, d = x.shape
    tm = 1024 if n >= 8192 else 128
    router_p = jnp.pad(router, ((0, 0), (0, LANE - N_EXPERTS)))
    logits = _mm(x, router_p, g=g)[:, :N_EXPERTS]
    top_val, top_idx = lax.top_k(logits, TOP_K)
    gate = jax.nn.softmax(top_val, axis=-1).reshape(-1)
    e_flat = top_idx.reshape(-1)
    n_asg = n * TOP_K
    order = jnp.argsort(e_flat)
    onehot = (e_flat[:, None] == jnp.arange(N_EXPERTS, dtype=e_flat.dtype)[None, :]).astype(jnp.int32)
    csum = jnp.cumsum(onehot, axis=0)
    counts = csum[-1]
    rank = jnp.sum(onehot * csum, axis=1) - 1
    padded = (counts + tm - 1) // tm * tm
    pad_end = jnp.cumsum(padded)
    pad_start = pad_end - padded
    start = jnp.cumsum(counts) - counts
    pos = (pad_start[e_flat] + rank).astype(jnp.int32)
    n_blocks = n_asg // tm + N_EXPERTS
    blk_e = jnp.minimum(jnp.searchsorted(pad_end, jnp.arange(n_blocks) * tm, side='right'),
                        N_EXPERTS - 1).astype(jnp.int32)
    e_row = jnp.repeat(blk_e, tm)
    in_expert = jnp.arange(n_blocks * tm) - pad_start[e_row]
    src_asg = order[jnp.clip(start[e_row] + in_expert, 0, n_asg - 1)]
    src_tok = jnp.where(in_expert < counts[e_row], src_asg // TOP_K, 0).astype(jnp.int32)
    yd = _ffn_grouped(x[src_tok], blk_e, g, w1, w3, w2, tm)
    contrib = yd[pos] * gate[:, None]
    return x + contrib.reshape(n, TOP_K, d).sum(axis=1)


N_CHUNK = 128
CHUNK_W = CMP_STRIDE * HEAD_DIM


def _compress_kernel(x_ref, pe_ref, w1_ref, w2_ref, o_ref):
    c = x_ref[0]
    lo = jnp.dot((c + pe_ref[0:1]).astype(BF16), w1_ref[0].astype(BF16), preferred_element_type=F32)
    hi = jnp.dot((c + pe_ref[1:2]).astype(BF16), w1_ref[1].astype(BF16), preferred_element_type=F32)
    hid = jax.nn.gelu(lo + pltpu.roll(hi, N_CHUNK - 1, 0))
    out = jnp.dot(hid.astype(BF16), w2_ref[...].astype(BF16), preferred_element_type=F32)
    row = lax.broadcasted_iota(jnp.int32, out.shape, 0)
    o_ref[0] = jnp.where(row < N_CHUNK - 1, out, 0.0)


def _compress(xc, pe, w1, w2):
    nb = xc.shape[0]
    hidden = w1.shape[-1]
    return pl.pallas_call(
        _compress_kernel,
        grid=(nb,),
        in_specs=[pl.BlockSpec((1, N_CHUNK, CHUNK_W), lambda i: (i, 0, 0)),
                  pl.BlockSpec((2, CHUNK_W), lambda i: (0, 0)),
                  pl.BlockSpec((2, CHUNK_W, hidden), lambda i: (0, 0, 0)),
                  pl.BlockSpec((hidden, HEAD_DIM), lambda i: (0, 0))],
        out_specs=pl.BlockSpec((1, N_CHUNK, HEAD_DIM), lambda i: (i, 0, 0)),
        out_shape=jax.ShapeDtypeStruct((nb, N_CHUNK, HEAD_DIM), F32),
        compiler_params=pltpu.CompilerParams(dimension_semantics=("arbitrary",), vmem_limit_bytes=VMEM_LIMIT_V7X),
        name="compress",
    )(xc, pe.reshape(2, CHUNK_W), w1.reshape(2, CHUNK_W, hidden), w2)


def _compress_heads(kv, pe, w1, w2):
    b = kv.shape[0]
    xc = kv.reshape(b, N_CHUNK, CMP_STRIDE, NSA_KV_HEADS, HEAD_DIM).transpose(0, 3, 1, 2, 4)
    out = _compress(xc.reshape(b * NSA_KV_HEADS, N_CHUNK, CHUNK_W), pe, w1, w2)
    return out.reshape(b, NSA_KV_HEADS, N_CHUNK, HEAD_DIM)


TQ = 128
TK = 128
NEG = -1e30
N_BIAS_TILES = WINDOW // TK + 1
MASKED_TILE = N_BIAS_TILES
SEL_SUB = 4


def _nsa_prompt_kernel(q_ref, kct_ref, vcc_ref, kst_ref, vs_ref, kwt_ref, vw_ref, bcmp_ref, btile_ref, gate_ref,
                       ov_ref, exp_ref, o_ref, selneg_ref, m_ref, l_ref, acc_ref):
    i = pl.program_id(2)
    rows = NSA_GROUP * TQ
    qt = (q_ref[...] * (HEAD_DIM ** -0.5)).astype(BF16)
    q = jnp.concatenate([qt[:, a * HEAD_DIM:(a + 1) * HEAD_DIM] for a in range(NSA_GROUP)], axis=0)

    s = jnp.dot(q, kct_ref[0, 0], preferred_element_type=F32) + bcmp_ref[0].reshape(rows, N_CHUNK)
    m = jnp.max(s, axis=-1, keepdims=True)
    e = jnp.where(s > 0.1 * NEG, jnp.exp(s - m), 0.0)
    p = e / jnp.maximum(jnp.sum(e, axis=-1, keepdims=True), 1e-30)
    o_cmp = jnp.dot(p.astype(BF16), vcc_ref[0, 0], preferred_element_type=F32)

    p_sum = p[0:TQ] + p[TQ:2 * TQ] + p[2 * TQ:3 * TQ] + p[3 * TQ:4 * TQ]
    n_sel = T_PROMPT // SEL_BLOCK
    imp_t = sum(lax.dot_general(ov_ref[...], part, (((1,), (1,)), ((), ())), preferred_element_type=F32)
                for part in _split3(p_sum))[0:n_sel]
    blk = lax.broadcasted_iota(jnp.int32, (n_sel, TQ), 0)
    cur = (lax.broadcasted_iota(jnp.int32, (n_sel, TQ), 1) + i * TQ) // SEL_BLOCK
    valid = blk <= cur
    forced = (blk == 0) | (blk == cur) | (blk == cur - 1)
    score = jnp.where(valid, imp_t + jnp.where(forced, SEL_FORCE, 0.0), -1.0)
    rank = jnp.zeros((n_sel, TQ), F32)
    for c in range(n_sel):
        sc = score[c:c + 1, :]
        beats = (sc > score) | ((sc == score) & (blk > c))
        rank = rank + jnp.where(beats, 1.0, 0.0)
    sel_t = jnp.where(valid & (rank < SEL_TOP_N), 1.0, 0.0)
    sel = jnp.concatenate([sel_t, jnp.zeros((LANE - n_sel, TQ), F32)], axis=0).T.astype(BF16)
    sel_keys = jnp.dot(sel, exp_ref[...], preferred_element_type=F32)
    selneg_ref[...] = (sel_keys - 1.0) * (-NEG)

    def masked_scores(kt_ref, first_tile, n_sub, selected):
        k0 = pl.multiple_of(first_tile * TK, TK)
        s_all = jnp.dot(q, kt_ref[0, 0, :, pl.ds(k0, n_sub * TK)], preferred_element_type=F32)
        pieces = []
        for u in range(n_sub):
            d0 = i - (first_tile + u)
            idx = jnp.where(d0 < 0, MASKED_TILE, jnp.minimum(d0, 2) if selected else d0)
            piece = s_all[:, u * TK:(u + 1) * TK].reshape(NSA_GROUP, TQ, TK) + btile_ref[0, idx]
            if selected:
                piece = piece + selneg_ref[:, pl.ds(pl.multiple_of(k0 + u * TK, TK), TK)][None]
            pieces.append(piece.reshape(rows, TK))
        return pieces, k0

    def row_max(pieces):
        return jnp.max(functools.reduce(jnp.maximum, pieces), axis=-1, keepdims=True)

    def row_sum(pieces):
        return jnp.sum(functools.reduce(jnp.add, pieces), axis=-1, keepdims=True)

    m_ref[...] = jnp.full(m_ref.shape, -jnp.inf, F32)
    l_ref[...] = jnp.zeros(l_ref.shape, F32)
    acc_ref[...] = jnp.zeros(acc_ref.shape, F32)

    def sel_group(gi, carry):
        pieces, k0 = masked_scores(kst_ref, gi * SEL_SUB, SEL_SUB, True)
        m_old = m_ref[...]
        m_new = jnp.maximum(m_old, row_max(pieces))
        pt = [jnp.exp(piece - m_new) for piece in pieces]
        alpha = jnp.exp(m_old - m_new)
        l_ref[...] = alpha * l_ref[...] + row_sum(pt)
        pv = jnp.dot(jnp.concatenate([x.astype(BF16) for x in pt], axis=1), vs_ref[0, 0, pl.ds(k0, SEL_SUB * TK), :],
                     preferred_element_type=F32)
        acc_ref[...] = alpha * acc_ref[...] + pv
        m_ref[...] = m_new
        return carry

    lax.fori_loop(0, i // SEL_SUB + 1, sel_group, 0)
    o_sel = acc_ref[...] / l_ref[...]

    pieces, k0 = masked_scores(kwt_ref, jnp.maximum(i - (N_BIAS_TILES - 1), 0), N_BIAS_TILES, False)
    m_win = row_max(pieces)
    pt = [jnp.exp(piece - m_win) for piece in pieces]
    o_win = jnp.dot(jnp.concatenate([x.astype(BF16) for x in pt], axis=1),
                    vw_ref[0, 0, pl.ds(k0, N_BIAS_TILES * TK), :], preferred_element_type=F32) / row_sum(pt)

    g = jax.nn.sigmoid(gate_ref[0, 0])
    pieces = []
    for a in range(NSA_GROUP):
        r = slice(a * TQ, (a + 1) * TQ)
        pieces.append(g[:, 3 * a:3 * a + 1] * o_cmp[r] + g[:, 3 * a + 1:3 * a + 2] * o_sel[r]
                      + g[:, 3 * a + 2:3 * a + 3] * o_win[r])
    o_ref[...] = jnp.concatenate(pieces, axis=1)


T_PROMPT = 2048


def _bias_lookup(table, dist):
    oh = jax.nn.one_hot(rel_bucket(dist), REL_BUCKETS, dtype=F32)
    return jnp.einsum('...r,rh->...h', oh, table, precision=lax.Precision.HIGHEST)


def _prompt_bias_tables(table):
    t = T_PROMPT
    q_pos = jnp.arange(t)
    block_end = jnp.arange(N_CHUNK) * CMP_STRIDE + (CMP_BLOCK - 1)
    dist = q_pos[:, None] - block_end[None, :]
    ok = (dist >= 0) & (jnp.arange(N_CHUNK)[None, :] < N_CHUNK - 1)
    bcmp = jnp.where(ok[..., None], _bias_lookup(table, dist), NEG)
    bcmp = bcmp.reshape(t, N_CHUNK, NSA_KV_HEADS, NSA_GROUP).transpose(2, 3, 0, 1)
    d0 = jnp.arange(N_BIAS_TILES)[:, None, None]
    dist = d0 * TK + jnp.arange(TQ)[None, :, None] - jnp.arange(TK)[None, None, :]
    ok = (dist >= 0) & (dist < WINDOW)
    bt = jnp.where(ok[..., None], _bias_lookup(table, dist), NEG)
    bt = bt.reshape(N_BIAS_TILES, TQ, TK, NSA_KV_HEADS, NSA_GROUP).transpose(3, 0, 4, 1, 2)
    bt = jnp.concatenate([bt, jnp.full((NSA_KV_HEADS, 1, NSA_GROUP, TQ, TK), NEG, F32)], axis=1)
    c0 = jnp.arange(N_CHUNK) * CMP_STRIDE
    s0 = jnp.arange(LANE) * SEL_BLOCK
    ov = jnp.clip(jnp.minimum(c0[:, None] + CMP_BLOCK, s0[None, :] + SEL_BLOCK)
                  - jnp.maximum(c0[:, None], s0[None, :]), 0, CMP_BLOCK).astype(F32) / CMP_BLOCK
    ov = jnp.where((jnp.arange(N_CHUNK)[:, None] < N_CHUNK - 1) & (jnp.arange(LANE)[None, :] < t // SEL_BLOCK), ov, 0.0)
    expand = (jnp.arange(LANE)[:, None] == (jnp.arange(t) // SEL_BLOCK)[None, :]).astype(BF16)
    return bcmp, bt, ov.T.astype(BF16), expand


def _nsa_prompt(z_a, kct, vcc, kst, vs, kwt, vw, gates, tables):
    bcmp, bt, ov, expand = tables
    b, _, _, t = kst.shape
    rows = NSA_GROUP * TQ
    n_q = t // TQ
    q_cols = NSA_GROUP * HEAD_DIM
    per_bh = lambda *blk: pl.BlockSpec((1, 1) + blk, lambda bi, h, i: (bi, h) + (0,) * len(blk))
    return pl.pallas_call(
        _nsa_prompt_kernel,
        grid=(b, NSA_KV_HEADS, n_q),
        in_specs=[pl.BlockSpec((TQ, q_cols), lambda bi, h, i: (bi * n_q + i, h)),
                  per_bh(HEAD_DIM, N_CHUNK), per_bh(N_CHUNK, HEAD_DIM),
                  per_bh(HEAD_DIM, t), per_bh(t, HEAD_DIM), per_bh(HEAD_DIM, t), per_bh(t, HEAD_DIM),
                  pl.BlockSpec((1, NSA_GROUP, TQ, N_CHUNK), lambda bi, h, i: (h, 0, i, 0)),
                  pl.BlockSpec((1, N_BIAS_TILES + 1, NSA_GROUP, TQ, TK), lambda bi, h, i: (h, 0, 0, 0, 0)),
                  pl.BlockSpec((1, 1, TQ, 3 * NSA_GROUP), lambda bi, h, i: (bi, h, i, 0)),
                  pl.BlockSpec((N_CHUNK, LANE), lambda bi, h, i: (0, 0)),
                  pl.BlockSpec((LANE, t), lambda bi, h, i: (0, 0))],
        out_specs=pl.BlockSpec((TQ, q_cols), lambda bi, h, i: (bi * n_q + i, h)),
        out_shape=jax.ShapeDtypeStruct((b * t, NSA_WIDTH), F32),
        scratch_shapes=[pltpu.VMEM((TQ, t), F32), pltpu.VMEM((rows, 1), F32), pltpu.VMEM((rows, 1), F32),
                        pltpu.VMEM((rows, HEAD_DIM), F32)],
        compiler_params=pltpu.CompilerParams(dimension_semantics=("arbitrary", "arbitrary", "arbitrary"),
                                             vmem_limit_bytes=VMEM_LIMIT_V7X),
        name="nsa_prompt",
    )(z_a, kct, vcc, kst, vs, kwt, vw, bcmp, bt, gates, ov, expand)


def _heads_t(x, b, t):
    return x.astype(BF16).reshape(b, t, NSA_KV_HEADS, HEAD_DIM).transpose(0, 2, 3, 1)


def _heads(x, b, t):
    return x.astype(BF16).reshape(b, t, NSA_KV_HEADS, HEAD_DIM).transpose(0, 2, 1, 3)


def _nsa_prompt_from_proj(z_a, z_s, b, t, tables, pe, w1, w2):
    col = lambda i: z_a[:, NSA_WIDTH + KV_WIDTH * i:NSA_WIDTH + KV_WIDTH * (i + 1)]
    kcc = _compress_heads(col(0).reshape(b, t, NSA_KV_HEADS, HEAD_DIM), pe[0], w1[0], w2[0])
    vcc = _compress_heads(col(1).reshape(b, t, NSA_KV_HEADS, HEAD_DIM), pe[1], w1[1], w2[1])
    gates = z_s[:, :3 * NSA_HEADS].reshape(b, t, NSA_KV_HEADS, 3 * NSA_GROUP).transpose(0, 2, 1, 3)
    return _nsa_prompt(z_a, kcc.astype(BF16).transpose(0, 1, 3, 2), vcc.astype(BF16),
                       _heads_t(col(2), b, t), _heads(col(3), b, t), _heads_t(col(4), b, t), _heads(col(5), b, t),
                       gates, tables)


T_SAMPLE = 4
T_PAD = 8
PAGE = 128
N_PAGES = 16
PAST = N_PAGES * PAGE
N_SEL_SAMPLE = -(-(PAST + T_SAMPLE) // SEL_BLOCK)


def _nsa_sample_kernel(pt_ref, *refs):
    del pt_ref
    cp, sp = refs[:N_PAGES], refs[N_PAGES:2 * N_PAGES]
    (wb_ref, za_ref, zs_ref, pe2_ref, w1_ref, w2_ref, bcmp_ref, bpast_ref, bwin_ref, bnew_ref, ov_ref, exp_ref,
     o_ref, new_ref, q_ref, g_ref, xcat_ref, kcat_ref, vcat_ref, s_ref, e_ref, oacc_ref,
     rowpg_ref) = refs[2 * N_PAGES:]
    rows = NSA_GROUP * T_PAD

    @pl.when(pl.program_id(0) == 0)
    def _():
        new_ref[...] = jnp.zeros(new_ref.shape, F32)
        q_ref[...] = jnp.zeros(q_ref.shape, F32)
        g_ref[...] = jnp.zeros(g_ref.shape, F32)

    new_ref[0:T_SAMPLE, :] = za_ref[0, :, NSA_WIDTH + 2 * KV_WIDTH:]
    q_ref[0:T_SAMPLE, :] = za_ref[0, :, :NSA_WIDTH] * (HEAD_DIM ** -0.5)
    g_ref[0:T_SAMPLE, :] = zs_ref[0]

    for p in range(N_PAGES):
        for kv in range(2):
            rowpg_ref[kv, p * PAGE:(p + 1) * PAGE, :] = cp[p][0, kv * KV_WIDTH:(kv + 1) * KV_WIDTH, :].T
    comp = []
    for kv in range(2):
        def place_row_offset(l, carry, kv=kv):
            xl = rowpg_ref[kv, pl.ds(l, N_CHUNK, stride=CMP_STRIDE), :]
            lanes = pl.ds(pl.multiple_of(l * KV_WIDTH, KV_WIDTH), KV_WIDTH)
            for half in range(2):
                xcat_ref[half, :, lanes] = (xl + pe2_ref[kv, pl.ds(l + CMP_STRIDE * half, 1), :]).astype(BF16)
            return carry

        lax.fori_loop(0, CMP_STRIDE, place_row_offset, 0)
        lo = jnp.dot(xcat_ref[0], w1_ref[kv, 0], preferred_element_type=F32)
        hi = jnp.dot(xcat_ref[1], w1_ref[kv, 1], preferred_element_type=F32)
        hid = jax.nn.gelu(lo + pltpu.roll(hi, N_CHUNK - 1, 0))
        out = jnp.dot(hid.astype(BF16), w2_ref[kv], preferred_element_type=F32)
        row = lax.broadcasted_iota(jnp.int32, out.shape, 0)
        comp.append(jnp.where(row < N_CHUNK - 1, out, 0.0).astype(BF16))

    for p in range(N_PAGES):
        kcat_ref[:, p * PAGE:(p + 1) * PAGE] = sp[p][0, 0:KV_WIDTH, :].astype(BF16)
        vcat_ref[:, p * PAGE:(p + 1) * PAGE] = sp[p][0, KV_WIDTH:, :].astype(BF16)
    kw_buf = wb_ref[0, 0:KV_WIDTH, :].astype(BF16)
    vw_buf = wb_ref[0, KV_WIDTH:, :].astype(BF16)
    ks_new, vs_new, kw_new, vw_new = (new_ref[:, i * KV_WIDTH:(i + 1) * KV_WIDTH].astype(BF16) for i in range(4))
    gates = jax.nn.sigmoid(g_ref[...])
    nt = (((1,), (1,)), ((), ()))
    zeros64 = jnp.zeros((rows, HEAD_DIM), F32)
    pieces = []
    key_chunk = 512

    def attend_window(qp, h):
        s_parts = [jnp.dot(qp, kw_buf, preferred_element_type=F32) + bwin_ref[h],
                   lax.dot_general(qp, kw_new, nt, preferred_element_type=F32) + bnew_ref[h]]
        m = jnp.maximum(*[jnp.max(s, axis=-1, keepdims=True) for s in s_parts])
        e_parts = [jnp.exp(s - m) for s in s_parts]
        den = jnp.add(*[jnp.sum(e, axis=-1, keepdims=True) for e in e_parts])
        num = (lax.dot_general(e_parts[0].astype(BF16), vw_buf, nt, preferred_element_type=F32)
               + jnp.dot(e_parts[1].astype(BF16), vw_new, preferred_element_type=F32))
        return (num / den)[:, h * HEAD_DIM:(h + 1) * HEAD_DIM]

    def attend_selected(qp, h):
        def scores(c, carry):
            ds = pl.ds(pl.multiple_of(c * key_chunk, key_chunk), key_chunk)
            s_ref[:, ds] += jnp.dot(qp, kcat_ref[:, ds], preferred_element_type=F32)
            return carry

        lax.fori_loop(0, PAST // key_chunk, scores, 0)
        s_ref[:, PAST:] += lax.dot_general(qp, ks_new, nt, preferred_element_type=F32)
        s = s_ref[...]
        e = jnp.exp(s - jnp.max(s, axis=-1, keepdims=True))
        den = jnp.sum(e, axis=-1, keepdims=True)
        e_ref[...] = e.astype(BF16)
        oacc_ref[...] = jnp.dot(e_ref[:, PAST:], vs_new, preferred_element_type=F32)

        def weighted(c, carry):
            ds = pl.ds(pl.multiple_of(c * key_chunk, key_chunk), key_chunk)
            oacc_ref[...] += lax.dot_general(e_ref[:, ds], vcat_ref[:, ds], nt, preferred_element_type=F32)
            return carry

        lax.fori_loop(0, PAST // key_chunk, weighted, 0)
        return (oacc_ref[...] / den)[:, h * HEAD_DIM:(h + 1) * HEAD_DIM]

    for h in range(NSA_KV_HEADS):
        q64 = jnp.concatenate([q_ref[:, (h * NSA_GROUP + a) * HEAD_DIM:(h * NSA_GROUP + a + 1) * HEAD_DIM]
                               for a in range(NSA_GROUP)], axis=0)
        qp = jnp.concatenate([q64, zeros64] if h == 0 else [zeros64, q64], axis=1).astype(BF16)

        s = lax.dot_general(qp, comp[0], nt, preferred_element_type=F32) + bcmp_ref[h]
        m = jnp.max(s, axis=-1, keepdims=True)
        e = jnp.where(s > 0.1 * NEG, jnp.exp(s - m), 0.0)
        p = e / jnp.maximum(jnp.sum(e, axis=-1, keepdims=True), 1e-30)
        o_cmp = jnp.dot(p.astype(BF16), comp[1], preferred_element_type=F32)[:, h * HEAD_DIM:(h + 1) * HEAD_DIM]

        p_sum = p[0:T_PAD] + p[T_PAD:2 * T_PAD] + p[2 * T_PAD:3 * T_PAD] + p[3 * T_PAD:4 * T_PAD]
        imp = sum(jnp.dot(part, ov_ref[...], preferred_element_type=F32) for part in _split3(p_sum))
        lane = lax.broadcasted_iota(jnp.int32, (T_PAD, LANE), 1)
        cur = (PAST + jnp.minimum(lax.broadcasted_iota(jnp.int32, (T_PAD, LANE), 0), T_SAMPLE - 1)) // SEL_BLOCK
        valid = lane <= cur
        forced = (lane == 0) | (lane == cur) | (lane == cur - 1)
        score = jnp.where(valid, imp + jnp.where(forced, SEL_FORCE, 0.0), -1.0)
        rank = jnp.zeros((T_PAD, LANE), F32)
        for c in range(N_SEL_SAMPLE):
            sc = score[:, c:c + 1]
            beats = (sc > score) | ((sc == score) & (lane > c))
            rank = rank + jnp.where(beats, 1.0, 0.0)
        sel = jnp.where(valid & (rank < SEL_TOP_N), 1.0, 0.0)
        selneg = (jnp.dot(sel.astype(BF16), exp_ref[...], preferred_element_type=F32) - 1.0) * (-NEG)
        selneg_new = (sel[:, PAST // SEL_BLOCK:PAST // SEL_BLOCK + 1] - 1.0) * (-NEG)

        def per_token(bias, tok):
            n = bias.shape[-1]
            return (bias.reshape(NSA_GROUP, T_PAD, n) + tok[None]).reshape(rows, n)

        s_ref[:, :PAST] = per_token(bpast_ref[h], selneg)
        s_ref[:, PAST:] = per_token(bnew_ref[h], jnp.broadcast_to(selneg_new, (T_PAD, LANE)))
        o_sel = attend_selected(qp, h)
        o_win = attend_window(qp, h)

        for a in range(NSA_GROUP):
            r = slice(a * T_PAD, (a + 1) * T_PAD)
            c0 = (h * NSA_GROUP + a) * 3
            pieces.append(gates[:, c0:c0 + 1] * o_cmp[r] + gates[:, c0 + 1:c0 + 2] * o_sel[r]
                          + gates[:, c0 + 2:c0 + 3] * o_win[r])
    o_ref[0] = jnp.concatenate(pieces, axis=1)[0:T_SAMPLE]


def _sample_bias_tables(table):
    tq = jnp.minimum(jnp.arange(T_PAD), T_SAMPLE - 1)
    q_pos = PAST + tq

    def lay(x):
        n = x.shape[1]
        return x.reshape(T_PAD, n, NSA_KV_HEADS, NSA_GROUP).transpose(2, 3, 0, 1).reshape(NSA_KV_HEADS, -1, n)

    block_end = jnp.arange(N_CHUNK) * CMP_STRIDE + (CMP_BLOCK - 1)
    dist = q_pos[:, None] - block_end[None, :]
    ok = (dist >= 0) & (jnp.arange(N_CHUNK)[None, :] < N_CHUNK - 1)
    bcmp = lay(jnp.where(ok[..., None], _bias_lookup(table, dist), NEG))
    dist = q_pos[:, None] - jnp.arange(PAST)[None, :]
    past = _bias_lookup(table, dist)
    bpast = lay(past)
    w_buf = min(WINDOW, PAST)
    bwin = lay(jnp.where((dist < WINDOW)[:, PAST - w_buf:, None], past[:, PAST - w_buf:], NEG))
    j = jnp.arange(LANE)
    dist = tq[:, None] - j[None, :]
    ok = (dist >= 0) & (j[None, :] < T_SAMPLE)
    bnew = lay(jnp.where(ok[..., None], _bias_lookup(table, dist), NEG))
    c0 = jnp.arange(N_CHUNK) * CMP_STRIDE
    s0 = jnp.arange(LANE) * SEL_BLOCK
    ov = jnp.clip(jnp.minimum(c0[:, None] + CMP_BLOCK, s0[None, :] + SEL_BLOCK)
                  - jnp.maximum(c0[:, None], s0[None, :]), 0, CMP_BLOCK).astype(F32) / CMP_BLOCK
    ov = jnp.where((jnp.arange(N_CHUNK)[:, None] < N_CHUNK - 1) & (j[None, :] < N_SEL_SAMPLE), ov, 0.0)
    expand = (j[:, None] == (jnp.arange(PAST) // SEL_BLOCK)[None, :]).astype(BF16)
    return bcmp, bpast, bwin, bnew, ov.astype(BF16), expand


def _nsa_sample(za, zs, cache_cmp, cache_slc, cache_win, page_table, layer, tables, pe, w1, w2):
    b = za.shape[0]
    n_phys = cache_cmp.shape[1]
    cmp_pages = jnp.transpose(cache_cmp, (0, 1, 3, 4, 5, 2)).reshape(DEPTH * n_phys, 2 * KV_WIDTH, PAGE)
    slc_pages = jnp.transpose(cache_slc, (0, 1, 3, 4, 5, 2)).reshape(DEPTH * n_phys, 2 * KV_WIDTH, PAGE)
    pt = page_table.reshape(-1).astype(jnp.int32) + layer * n_phys
    w_buf = cache_win.shape[2]
    win = jnp.transpose(cache_win, (0, 1, 3, 4, 5, 2)).reshape(DEPTH * b, 2 * KV_WIDTH, w_buf)
    bcmp, bpast, bwin, bnew, ov, expand = tables
    pe2 = jnp.concatenate([pe, pe], axis=-1)
    page_spec = lambda p: pl.BlockSpec((1, 2 * KV_WIDTH, PAGE), lambda bi, pt_: (pt_[bi * N_PAGES + p], 0, 0))
    whole = lambda x: pl.BlockSpec(x.shape, lambda bi, pt_: (0,) * x.ndim)
    eye = jnp.eye(NSA_KV_HEADS, dtype=F32)
    w1h = w1.reshape(2, 2, CMP_STRIDE, HEAD_DIM, CMP_HIDDEN)
    w1_bd = jnp.einsum('khlde,ab->khladbe', w1h, eye).reshape(2, 2, CMP_STRIDE * KV_WIDTH, NSA_KV_HEADS * CMP_HIDDEN)
    w2_bd = jnp.einsum('ked,ab->kaebd', w2, eye).reshape(2, NSA_KV_HEADS * CMP_HIDDEN, KV_WIDTH)
    consts = [pe2, w1_bd.astype(BF16), w2_bd.astype(BF16), bcmp, bpast, bwin, bnew, ov, expand]
    grid_spec = pltpu.PrefetchScalarGridSpec(
        num_scalar_prefetch=1,
        grid=(b,),
        in_specs=[page_spec(p) for p in range(N_PAGES)] * 2
        + [pl.BlockSpec((1, 2 * KV_WIDTH, w_buf), lambda bi, pt_: (layer * b + bi, 0, 0)),
           pl.BlockSpec((1, T_SAMPLE, za.shape[-1]), lambda bi, pt_: (bi, 0, 0)),
           pl.BlockSpec((1, T_SAMPLE, LANE), lambda bi, pt_: (bi, 0, 0))]
        + [whole(x) for x in consts],
        out_specs=pl.BlockSpec((1, T_SAMPLE, NSA_WIDTH), lambda bi, pt_: (bi, 0, 0)),
        scratch_shapes=[pltpu.VMEM((PAGE, 4 * KV_WIDTH), F32), pltpu.VMEM((T_PAD, NSA_WIDTH), F32),
                        pltpu.VMEM((T_PAD, LANE), F32), pltpu.VMEM((2, N_CHUNK, CMP_STRIDE * KV_WIDTH), BF16),
                        pltpu.VMEM((KV_WIDTH, PAST), BF16), pltpu.VMEM((KV_WIDTH, PAST), BF16),
                        pltpu.VMEM((NSA_GROUP * T_PAD, PAST + PAGE), F32),
                        pltpu.VMEM((NSA_GROUP * T_PAD, PAST + PAGE), BF16),
                        pltpu.VMEM((NSA_GROUP * T_PAD, KV_WIDTH), F32), pltpu.VMEM((2, PAST, KV_WIDTH), F32)],
    )
    return pl.pallas_call(
        _nsa_sample_kernel,
        grid_spec=grid_spec,
        out_shape=jax.ShapeDtypeStruct((b, T_SAMPLE, NSA_WIDTH), F32),
        compiler_params=pltpu.CompilerParams(dimension_semantics=("arbitrary",), vmem_limit_bytes=VMEM_LIMIT_V7X),
        name="nsa_sample",
    )(pt, *([cmp_pages] * N_PAGES), *([slc_pages] * N_PAGES), win, za, zs, *consts)


I_LANE = 3 * NSA_HEADS
F_LANE = I_LANE + M_HEADS
EXT_PAD = 8


def _mlstm_kernel(zu_ref, zs_ref, c0_ref, n0_ref, m0_ref, cprev_ref, convw_ref, convb_ref, wq_ref, wk_ref, wkt_ref,
                  wv_ref, gb_ref, normg_ref, tril_ref, selrow_ref,
                  o_ref, c_out, n_out, m_out, conv_out,
                  ext_ref, zsp_ref, c_sc, n_sc, m_sc, *, rows, rows_pad, n_valid):
    i = pl.program_id(1)
    L = M_CHUNK

    @pl.when(i == 0)
    def _():
        c_sc[...] = c0_ref[0]
        n_sc[...] = n0_ref[0]
        m_sc[...] = m0_ref[0]
        ext_ref[EXT_PAD - (CONV_W - 1):EXT_PAD, :] = cprev_ref[0]

    if rows < rows_pad:
        ext_ref[EXT_PAD:, :] = jnp.zeros((rows_pad, M_WIDTH), F32)
        zsp_ref[...] = jnp.zeros(zsp_ref.shape, F32)
    ext_ref[EXT_PAD:EXT_PAD + rows, :] = zu_ref[0, :, :M_WIDTH]
    zsp_ref[0:rows, :] = zs_ref[0]

    u = ext_ref[EXT_PAD:EXT_PAD + rows_pad, :]
    conv = convb_ref[...] + convw_ref[CONV_W - 1:CONV_W, :] * u
    for j in range(CONV_W - 1):
        lo = EXT_PAD - (CONV_W - 1) + j
        conv = conv + convw_ref[j:j + 1, :] * ext_ref[lo:lo + rows_pad, :]
    uc = (conv * jax.nn.sigmoid(conv)).astype(BF16)
    ub = u.astype(BF16)

    zb = zsp_ref[...] + gb_ref[...]
    bcum = sum(jnp.dot(tril_ref[...], part, preferred_element_type=F32) for part in _split3(jax.nn.log_sigmoid(zb)))
    bcum = pltpu.roll(bcum, LANE - M_HEADS, 1)
    a_col = zb - bcum
    a_row = sum(lax.dot_general(selrow_ref[...], part, (((1,), (1,)), ((), ())), preferred_element_type=F32)
                for part in _split3(a_col))

    tt = lax.broadcasted_iota(jnp.int32, (L, L), 0)
    ss = lax.broadcasted_iota(jnp.int32, (L, L), 1)
    causal = ss <= tt
    tok_col = lax.broadcasted_iota(jnp.int32, (L, 1), 0)
    tok_row = lax.broadcasted_iota(jnp.int32, (1, L), 1)

    for h in range(M_HEADS):
        hs = slice(h * M_HEAD_DIM, (h + 1) * M_HEAD_DIM)
        q_all = jnp.dot(uc[:, hs], wq_ref[h].astype(BF16), preferred_element_type=F32) * (M_HEAD_DIM ** -0.5)
        k_all = jnp.dot(uc[:, hs], wk_ref[h].astype(BF16), preferred_element_type=F32)
        kt_all = lax.dot_general(wkt_ref[h].astype(BF16), uc[:, hs], (((1,), (1,)), ((), ())),
                                 preferred_element_type=F32)
        v_all = jnp.dot(ub[:, hs], wv_ref[h].astype(BF16), preferred_element_type=F32)
        c_st = c_sc[h]
        n_st = n_sc[h:h + 1, :]
        m_st = m_sc[h:h + 1, 0:1]
        for c in range(rows_pad // L):
            r = slice(c * L, (c + 1) * L)
            q, k, kt, v = q_all[r].astype(BF16), k_all[r], kt_all[:, r], v_all[r].astype(BF16)
            b_col = bcum[r, I_LANE + h:I_LANE + h + 1]
            a_c = a_col[r, I_LANE + h:I_LANE + h + 1]
            a_r = a_row[h:h + 1, r]
            log_d = jnp.where(causal, b_col + a_r, NEG)
            m_col = jnp.maximum(b_col + m_st, jnp.max(log_d, axis=-1, keepdims=True))
            dw = jnp.exp(log_d - m_col)
            w_inter = jnp.exp(b_col + m_st - m_col)
            s = lax.dot_general(q, k.astype(BF16), (((1,), (1,)), ((), ())), preferred_element_type=F32) * dw
            num = (jnp.dot(s.astype(BF16), v, preferred_element_type=F32)
                   + w_inter * jnp.dot(q, c_st.astype(BF16), preferred_element_type=F32))
            den = (jnp.sum(s, axis=-1, keepdims=True)
                   + w_inter * jnp.sum(q_all[r] * n_st, axis=-1, keepdims=True))
            hh = num / jnp.maximum(jnp.abs(den), jnp.exp(-m_col))
            hn = hh * lax.rsqrt(jnp.mean(hh * hh, axis=-1, keepdims=True) + EPS)
            o_pre = zu_ref[0, :, M_WIDTH + h * M_HEAD_DIM:M_WIDTH + (h + 1) * M_HEAD_DIM]
            if rows < rows_pad:
                o_ref[0, :, hs] = jax.nn.sigmoid(o_pre) * hn[0:rows] * normg_ref[:, hs]
            else:
                o_ref[0, r, hs] = jax.nn.sigmoid(o_pre[r]) * hn * normg_ref[:, hs]
            b_last = b_col[n_valid - 1:n_valid, :]
            log_s = jnp.where(tok_col < n_valid, b_last + a_c, NEG)
            m_new = jnp.maximum(b_last + m_st, jnp.max(log_s, axis=0, keepdims=True))
            ws_col = jnp.exp(log_s - m_new)
            ws_row = jnp.where(tok_row < n_valid, jnp.exp(b_last + a_r - m_new), 0.0)
            wc = jnp.exp(b_last + m_st - m_new)
            c_st = wc * c_st + jnp.dot((kt * ws_row).astype(BF16), v, preferred_element_type=F32)
            n_st = wc * n_st + jnp.sum(k * ws_col, axis=0, keepdims=True)
            m_st = m_new
        c_sc[h] = c_st
        n_sc[h:h + 1, :] = n_st
        m_sc[h:h + 1, :] = jnp.broadcast_to(m_st, (1, LANE))

    tail = ext_ref[EXT_PAD + rows - (CONV_W - 1):EXT_PAD + rows, :]
    ext_ref[EXT_PAD - (CONV_W - 1):EXT_PAD, :] = tail

    @pl.when(i == pl.num_programs(1) - 1)
    def _():
        c_out[0] = c_sc[...]
        n_out[0] = n_sc[...]
        m_out[0] = m_sc[...]
        conv_out[0] = jnp.zeros(conv_out.shape[1:], F32)
        conv_out[0, 0:CONV_W - 1, :] = tail


def _mlstm(zu, zs, state, conv_prev, conv_w, conv_b, m_qkv, gate_b, norm_g):
    b, t, _ = zu.shape
    rows = min(t, 4 * M_CHUNK)
    rows_pad = -(-rows // M_CHUNK) * M_CHUNK
    n_valid = M_CHUNK if rows == rows_pad else rows
    assert t % rows == 0 and (rows == rows_pad or t == rows)
    c0, n0, m0 = state
    n0p = jnp.pad(n0, ((0, 0), (0, 8 - M_HEADS), (0, 0)))
    m0p = jnp.pad(jnp.broadcast_to(m0[:, :, None], (b, M_HEADS, LANE)), ((0, 0), (0, 8 - M_HEADS), (0, 0)))
    gb = jnp.zeros((1, LANE), F32).at[0, I_LANE:I_LANE + 2 * M_HEADS].set(gate_b.reshape(-1))
    idx = jnp.arange(rows_pad)
    tril = ((idx[:, None] >= idx[None, :]) & (idx[:, None] // M_CHUNK == idx[None, :] // M_CHUNK)).astype(BF16)
    selrow = ((jnp.arange(16)[:, None] + I_LANE == jnp.arange(LANE)[None, :])
              & (jnp.arange(16)[:, None] < M_HEADS)).astype(BF16)
    whole = lambda *shape: pl.BlockSpec(shape, lambda bi, i: (0,) * len(shape))
    per_b = lambda *shape: pl.BlockSpec((1,) + shape, lambda bi, i: (bi,) + (0,) * len(shape))
    out, c_new, n_new, m_new, conv_new = pl.pallas_call(
        functools.partial(_mlstm_kernel, rows=rows, rows_pad=rows_pad, n_valid=n_valid),
        grid=(b, t // rows),
        in_specs=[pl.BlockSpec((1, rows, 2 * M_WIDTH), lambda bi, i: (bi, i, 0)),
                  pl.BlockSpec((1, rows, LANE), lambda bi, i: (bi, i, 0)),
                  per_b(M_HEADS, M_HEAD_DIM, M_HEAD_DIM), per_b(8, M_HEAD_DIM), per_b(8, LANE),
                  per_b(CONV_W - 1, M_WIDTH),
                  whole(CONV_W, M_WIDTH), whole(1, M_WIDTH),
                  whole(M_HEADS, M_HEAD_DIM, M_HEAD_DIM), whole(M_HEADS, M_HEAD_DIM, M_HEAD_DIM),
                  whole(M_HEADS, M_HEAD_DIM, M_HEAD_DIM), whole(M_HEADS, M_HEAD_DIM, M_HEAD_DIM),
                  whole(1, LANE), whole(1, M_WIDTH), whole(rows_pad, rows_pad), whole(16, LANE)],
        out_specs=[pl.BlockSpec((1, rows, M_WIDTH), lambda bi, i: (bi, i, 0)),
                   per_b(M_HEADS, M_HEAD_DIM, M_HEAD_DIM), per_b(8, M_HEAD_DIM), per_b(8, LANE),
                   per_b(8, M_WIDTH)],
        out_shape=[jax.ShapeDtypeStruct((b, t, M_WIDTH), F32),
                   jax.ShapeDtypeStruct((b, M_HEADS, M_HEAD_DIM, M_HEAD_DIM), F32),
                   jax.ShapeDtypeStruct((b, 8, M_HEAD_DIM), F32),
                   jax.ShapeDtypeStruct((b, 8, LANE), F32),
                   jax.ShapeDtypeStruct((b, 8, M_WIDTH), F32)],
        scratch_shapes=[pltpu.VMEM((EXT_PAD + rows_pad, M_WIDTH), F32), pltpu.VMEM((rows_pad, LANE), F32),
                        pltpu.VMEM((M_HEADS, M_HEAD_DIM, M_HEAD_DIM), F32), pltpu.VMEM((8, M_HEAD_DIM), F32),
                        pltpu.VMEM((8, LANE), F32)],
        compiler_params=pltpu.CompilerParams(dimension_semantics=("arbitrary", "arbitrary"),
                                             vmem_limit_bytes=VMEM_LIMIT_V7X),
        name="mlstm",
    )(zu, zs, c0, n0p, m0p, conv_prev, conv_w, conv_b.reshape(1, M_WIDTH), m_qkv[0], m_qkv[1],
      m_qkv[1].transpose(0, 2, 1),
      m_qkv[2], gb, norm_g.reshape(1, M_WIDTH), tril, selrow)
    return out, (c_new, n_new[:, :M_HEADS], m_new[:, :M_HEADS, 0], conv_new[:, :CONV_W - 1])


def rel_bucket(dist):
    d = jnp.maximum(dist, 0)
    exact = REL_BUCKETS // 2
    log_part = exact + (jnp.log(jnp.maximum(d, 1).astype(F32) / exact)
                        / math.log(REL_MAX_DIST / exact) * (REL_BUCKETS - exact)).astype(jnp.int32)
    return jnp.where(d < exact, d, jnp.minimum(log_part, REL_BUCKETS - 1))


def _split_w_in(w_in):
    offs = np.cumsum((0,) + IN_WIDTHS)
    w_a = w_in[:, :offs[7]]
    w_small = jnp.concatenate([w_in[:, offs[7]:offs[8]], w_in[:, offs[10]:offs[11]]], axis=1)
    w_small = jnp.pad(w_small, ((0, 0), (0, LANE - w_small.shape[1])))
    w_u = w_in[:, offs[8]:offs[10]]
    w_g = w_in[:, offs[11]:]
    return w_a, w_small, w_u, w_g


def mixer(x, norm_g, prm, past):
    b, t, _ = x.shape
    x2 = x.reshape(b * t, D_MODEL)
    z_a, z_s, z_u, z_g = _proj(x2, norm_g, [w.astype(BF16) for w in _split_w_in(prm['w_in'])])
    z_a, z_s, z_u = (z.reshape(b, t, -1) for z in (z_a, z_s, z_u))
    new_cmp, new_slc, win_rows = (z_a[..., NSA_WIDTH + 2 * KV_WIDTH * i:NSA_WIDTH + 2 * KV_WIDTH * (i + 1)]
                                  .reshape(b, t, 2, NSA_KV_HEADS, HEAD_DIM) for i in range(3))
    if past is None:
        o_nsa = _nsa_prompt_from_proj(z_a.reshape(b * t, -1), z_s.reshape(b * t, -1), b, t, prm['prompt_tables'],
                                      prm['cmp_pe'], prm['cmp_w1'], prm['cmp_w2'])
        new_win = win_rows[:, t - min(WINDOW, t):]
        conv_prev = jnp.zeros((b, CONV_W - 1, M_WIDTH), F32)
        m_state = (jnp.zeros((b, M_HEADS, M_HEAD_DIM, M_HEAD_DIM), F32),
                   jnp.zeros((b, M_HEADS, M_HEAD_DIM), F32),
                   jnp.zeros((b, M_HEADS), F32))
    else:
        layer = past['layer']
        assert t == T_SAMPLE and past['page_table'].shape[1] == N_PAGES and past['cmp'].shape[2] == PAGE
        o_nsa = _nsa_sample(z_a, z_s, past['cmp'], past['slc'], past['win'], past['page_table'], layer,
                            prm['sample_tables'], prm['cmp_pe'], prm['cmp_w1'], prm['cmp_w2'])
        o_nsa = o_nsa.reshape(b * t, NSA_WIDTH)
        new_win = jnp.concatenate([past['win'][layer][:, t:], win_rows], axis=1)
        conv_prev = past['conv']
        m_state = (past['C'].astype(F32), past['n'].astype(F32), past['m'].astype(F32))
    o_m, (c_new, n_new, m_new, conv_new) = _mlstm(z_u, z_s, m_state, conv_prev, prm['conv_w'], prm['conv_b'],
                                                  prm['m_qkv'], prm['gate_b'], prm['m_norm'])
    y = _merge_out(x2, o_nsa, o_m.reshape(b * t, M_WIDTH), z_g, prm['w_up_a'].astype(BF16),
                   prm['w_up_b'].astype(BF16), prm['w_out'].astype(BF16)).reshape(b, t, D_MODEL)
    return y, (new_cmp, new_slc, new_win, c_new, n_new, m_new, conv_new)


def _channel_mixer(x, g, l, ffn_w1, ffn_w3, ffn_w2, moe_router, moe_w1, moe_w3, moe_w2):
    b, t, d = x.shape
    x2 = x.reshape(b * t, d)
    i = l // 2
    if l % 2 == 0:
        y = _ffn_dense(x2, g, ffn_w1[i], ffn_w3[i], ffn_w2[i])
    else:
        y = _moe(x2, g, moe_router[i], moe_w1[i], moe_w3[i], moe_w2[i])
    return y.reshape(b, t, d)


def _final_norm(x, g):
    xf = x.astype(F32)
    return xf * lax.rsqrt(jnp.mean(xf * xf, axis=-1, keepdims=True) + EPS) * g


def kernel(x_prompt, x_sample, cache_cmp_kv, cache_slc_kv, cache_win_kv, state_mlstm_C, state_mlstm_n,
           state_mlstm_m, state_mlstm_conv, page_table, rel_bias_table, norm_mix, norm_ffn, norm_final,
           w_in, cmp_pe, cmp_w1, cmp_w2, m_conv_w, m_conv_b, m_qkv, m_gate_bias, m_norm, w_up_a, w_up_b,
           w_out, ffn_w1, ffn_w3, ffn_w2, moe_router, moe_w1, moe_w3, moe_w2):
    xp, xs = x_prompt, x_sample
    prompt_states, sample_states = [], []
    prompt_tables = _prompt_bias_tables(rel_bias_table)
    sample_tables = _sample_bias_tables(rel_bias_table)
    for l in range(DEPTH):
        prm = {'prompt_tables': prompt_tables, 'sample_tables': sample_tables, 'w_in': w_in[l], 'cmp_pe': cmp_pe[l],
               'cmp_w1': cmp_w1[l], 'cmp_w2': cmp_w2[l], 'conv_w': m_conv_w[l], 'conv_b': m_conv_b[l],
               'm_qkv': m_qkv[l], 'gate_b': m_gate_bias[l], 'm_norm': m_norm[l], 'w_up_a': w_up_a[l],
               'w_up_b': w_up_b[l], 'w_out': w_out[l]}
        past = {'cmp': cache_cmp_kv, 'slc': cache_slc_kv, 'win': cache_win_kv, 'layer': l,
                'C': state_mlstm_C[l], 'n': state_mlstm_n[l], 'm': state_mlstm_m[l],
                'conv': state_mlstm_conv[l], 'page_table': page_table}
        xp, st_p = mixer(xp, norm_mix[l], prm, None)
        xs, st_s = mixer(xs, norm_mix[l], prm, past)
        xp = _channel_mixer(xp, norm_ffn[l], l, ffn_w1, ffn_w3, ffn_w2, moe_router, moe_w1, moe_w3, moe_w2)
        xs = _channel_mixer(xs, norm_ffn[l], l, ffn_w1, ffn_w3, ffn_w2, moe_router, moe_w1, moe_w3, moe_w2)
        prompt_states.append(st_p)
        sample_states.append(st_s)
    y_prompt = _final_norm(xp, norm_final)
    y_sample = _final_norm(xs, norm_final)
    ps = [jnp.stack([s[i] for s in prompt_states]) for i in range(7)]
    ss = [jnp.stack([s[i] for s in sample_states]) for i in range(7)]
    return (y_prompt, y_sample, ps[0], ps[1], ps[2], ps[3], ps[4], ps[5], ps[6],
            ss[0], ss[1], ss[2], ss[3], ss[4], ss[5], ss[6])
```

```python
import functools
import math

import jax
import jax.numpy as jnp
import numpy as np
from jax import lax
from jax.experimental import pallas as pl
from jax.experimental.pallas import tpu as pltpu

D_MODEL = 1024
DEPTH = 2
NSA_HEADS = 8
NSA_KV_HEADS = 2
NSA_GROUP = NSA_HEADS // NSA_KV_HEADS
HEAD_DIM = 64
NSA_WIDTH = NSA_HEADS * HEAD_DIM
KV_WIDTH = NSA_KV_HEADS * HEAD_DIM
CMP_BLOCK = 32
CMP_STRIDE = 16
CMP_HIDDEN = 128
SEL_BLOCK = 64
SEL_TOP_N = 16
SEL_FORCE = 1e4
WINDOW = 512
WIN_Q_BLOCK = 128
REL_BUCKETS = 32
REL_MAX_DIST = 128
M_HEADS = 4
M_HEAD_DIM = 128
M_WIDTH = M_HEADS * M_HEAD_DIM
CONV_W = 4
M_CHUNK = 64
D_FF = 2816
N_EXPERTS = 8
TOP_K = 2
EPS = 1e-6
IN_WIDTHS = (NSA_WIDTH, KV_WIDTH, KV_WIDTH, KV_WIDTH, KV_WIDTH, KV_WIDTH, KV_WIDTH, 3 * NSA_HEADS, M_WIDTH, M_WIDTH,
             2 * M_HEADS, D_MODEL, D_MODEL)

VMEM_LIMIT_V7X = 52 * 1024 * 1024
LANE = 128

F32 = jnp.float32
BF16 = jnp.bfloat16


def _split3(x):
    hi = x.astype(BF16)
    r1 = x - hi.astype(F32)
    mid = r1.astype(BF16)
    lo = (r1 - mid.astype(F32)).astype(BF16)
    return hi, mid, lo


def _pick_tile(n, cands):
    for c in cands:
        if n % c == 0:
            return c
    return n


def _mm_kernel(*refs, norm, has_res):
    x_ref, g_ref, w_ref = refs[:3]
    res_ref = refs[3] if has_res else None
    o_ref, xs_ref = refs[-2], refs[-1]

    @pl.when(pl.program_id(1) == 0)
    def _():
        x = x_ref[...]
        if norm:
            x = x * lax.rsqrt(jnp.mean(x * x, axis=-1, keepdims=True) + EPS) * g_ref[...]
        xs_ref[...] = x.astype(BF16)

    acc = jnp.dot(xs_ref[...], w_ref[...].astype(BF16), preferred_element_type=F32)
    if has_res:
        acc = acc + res_ref[...]
    o_ref[...] = acc


def _mm(x, w, g=None, res=None):
    m, k = x.shape
    n = w.shape[1]
    tm = _pick_tile(m, (1024, 512, 256, 128))
    tn = _pick_tile(n, (512, 256, 128))
    norm = g is not None
    gg = (g if norm else jnp.ones((k,), F32)).reshape(1, k)
    in_specs = [pl.BlockSpec((tm, k), lambda i, j: (i, 0)),
                pl.BlockSpec((1, k), lambda i, j: (0, 0)),
                pl.BlockSpec((k, tn), lambda i, j: (0, j))]
    args = [x, gg, w]
    if res is not None:
        in_specs.append(pl.BlockSpec((tm, tn), lambda i, j: (i, j)))
        args.append(res)
    return pl.pallas_call(
        functools.partial(_mm_kernel, norm=norm, has_res=res is not None),
        grid=(m // tm, n // tn),
        in_specs=in_specs,
        out_specs=pl.BlockSpec((tm, tn), lambda i, j: (i, j)),
        out_shape=jax.ShapeDtypeStruct((m, n), F32),
        scratch_shapes=[pltpu.VMEM((tm, k), BF16)],
        compiler_params=pltpu.CompilerParams(dimension_semantics=("arbitrary", "arbitrary"),
                                             vmem_limit_bytes=VMEM_LIMIT_V7X),
        name="mm",
    )(*args)


def _proj_kernel(x_ref, g_ref, *refs):
    n_out = len(refs) // 2
    x = x_ref[...]
    xn = (x * lax.rsqrt(jnp.mean(x * x, axis=-1, keepdims=True) + EPS) * g_ref[...]).astype(BF16)
    for w_ref, o_ref in zip(refs[:n_out], refs[n_out:]):
        o_ref[...] = jnp.dot(xn, w_ref[...], preferred_element_type=F32)


def _proj(x, g, weights):
    m, k = x.shape
    tm = _pick_tile(m, (256, 128))
    row = lambda n: pl.BlockSpec((tm, n), lambda i: (i, 0))
    whole = lambda a: pl.BlockSpec(a.shape, lambda i: (0,) * a.ndim)
    return pl.pallas_call(
        _proj_kernel,
        grid=(m // tm,),
        in_specs=[row(k), whole(g.reshape(1, k))] + [whole(w) for w in weights],
        out_specs=[row(w.shape[1]) for w in weights],
        out_shape=[jax.ShapeDtypeStruct((m, w.shape[1]), F32) for w in weights],
        compiler_params=pltpu.CompilerParams(dimension_semantics=("arbitrary",), vmem_limit_bytes=VMEM_LIMIT_V7X),
        name="proj",
    )(x, g.reshape(1, k), *weights)


def _merge_out_kernel(x_ref, oa_ref, ob_ref, zg_ref, wa_ref, wb_ref, wo_ref, o_ref):
    d = x_ref.shape[-1]
    up_a = jnp.dot(oa_ref[...].astype(BF16), wa_ref[...], preferred_element_type=F32)
    up_b = jnp.dot(ob_ref[...].astype(BF16), wb_ref[...], preferred_element_type=F32)
    merged = jax.nn.sigmoid(zg_ref[:, :d]) * up_a + jax.nn.sigmoid(zg_ref[:, d:]) * up_b
    o_ref[...] = x_ref[...] + jnp.dot(merged.astype(BF16), wo_ref[...], preferred_element_type=F32)


def _merge_out(x, o_a, o_b, z_g, w_up_a, w_up_b, w_out):
    m, d = x.shape
    tm = _pick_tile(m, (512, 256, 128))
    row = lambda a: pl.BlockSpec((tm, a.shape[1]), lambda i: (i, 0))
    whole = lambda a: pl.BlockSpec(a.shape, lambda i: (0,) * a.ndim)
    return pl.pallas_call(
        _merge_out_kernel,
        grid=(m // tm,),
        in_specs=[row(x), row(o_a), row(o_b), row(z_g), whole(w_up_a), whole(w_up_b), whole(w_out)],
        out_specs=row(x),
        out_shape=jax.ShapeDtypeStruct((m, d), F32),
        compiler_params=pltpu.CompilerParams(dimension_semantics=("arbitrary",), vmem_limit_bytes=VMEM_LIMIT_V7X),
        name="merge_out",
    )(x, o_a, o_b, z_g, w_up_a, w_up_b, w_out)


def _ffn_body(x_ref, g_ref, w1_ref, w3_ref, w2_ref, o_ref, xs_ref, acc_ref, *, residual, grouped):
    j = pl.program_id(1)

    @pl.when(j == 0)
    def _():
        x = x_ref[...]
        xn = x * lax.rsqrt(jnp.mean(x * x, axis=-1, keepdims=True) + EPS) * g_ref[...]
        xs_ref[...] = xn.astype(BF16)
        acc_ref[...] = jnp.zeros_like(acc_ref)

    xs = xs_ref[...]
    w1 = w1_ref[0] if grouped else w1_ref[...]
    w3 = w3_ref[0] if grouped else w3_ref[...]
    w2 = w2_ref[0] if grouped else w2_ref[...]
    a = jnp.dot(xs, w1.astype(BF16), preferred_element_type=F32)
    b = jnp.dot(xs, w3.astype(BF16), preferred_element_type=F32)
    h = (a * jax.nn.sigmoid(a) * b).astype(BF16)
    acc_ref[...] += jnp.dot(h, w2.astype(BF16), preferred_element_type=F32)

    @pl.when(j == pl.num_programs(1) - 1)
    def _():
        if residual:
            o_ref[...] = x_ref[...] + acc_ref[...]
        else:
            o_ref[...] = acc_ref[...]


def _ffn_dense_kernel(x_ref, g_ref, w1_ref, w3_ref, w2_ref, o_ref, xs_ref, acc_ref):
    _ffn_body(x_ref, g_ref, w1_ref, w3_ref, w2_ref, o_ref, xs_ref, acc_ref, residual=True, grouped=False)


def _ffn_grouped_kernel(be_ref, x_ref, g_ref, w1_ref, w3_ref, w2_ref, o_ref, xs_ref, acc_ref):
    del be_ref
    _ffn_body(x_ref, g_ref, w1_ref, w3_ref, w2_ref, o_ref, xs_ref, acc_ref, residual=False, grouped=True)


def _ffn_dense(x, g, w1, w3, w2):
    m, d = x.shape
    f = w1.shape[1]
    tm = _pick_tile(m, (1024, 512, 256, 128))
    tf = _pick_tile(f, (256, 128))
    return pl.pallas_call(
        _ffn_dense_kernel,
        grid=(m // tm, f // tf),
        in_specs=[pl.BlockSpec((tm, d), lambda i, j: (i, 0)),
                  pl.BlockSpec((1, d), lambda i, j: (0, 0)),
                  pl.BlockSpec((d, tf), lambda i, j: (0, j)),
                  pl.BlockSpec((d, tf), lambda i, j: (0, j)),
                  pl.BlockSpec((tf, d), lambda i, j: (j, 0))],
        out_specs=pl.BlockSpec((tm, d), lambda i, j: (i, 0)),
        out_shape=jax.ShapeDtypeStruct((m, d), F32),
        scratch_shapes=[pltpu.VMEM((tm, d), BF16), pltpu.VMEM((tm, d), F32)],
        compiler_params=pltpu.CompilerParams(dimension_semantics=("arbitrary", "arbitrary"),
                                             vmem_limit_bytes=VMEM_LIMIT_V7X),
        name="ffn_dense",
    )(x, g.reshape(1, d), w1, w3, w2)


def _ffn_grouped(xd, blk_e, g, w1, w3, w2, tm):
    rows, d = xd.shape
    f = w1.shape[2]
    tf = _pick_tile(f, (256, 128))
    grid_spec = pltpu.PrefetchScalarGridSpec(
        num_scalar_prefetch=1,
        grid=(rows // tm, f // tf),
        in_specs=[pl.BlockSpec((tm, d), lambda i, j, be: (i, 0)),
                  pl.BlockSpec((1, d), lambda i, j, be: (0, 0)),
                  pl.BlockSpec((1, d, tf), lambda i, j, be: (be[i], 0, j)),
                  pl.BlockSpec((1, d, tf), lambda i, j, be: (be[i], 0, j)),
                  pl.BlockSpec((1, tf, d), lambda i, j, be: (be[i], j, 0))],
        out_specs=pl.BlockSpec((tm, d), lambda i, j, be: (i, 0)),
        scratch_shapes=[pltpu.VMEM((tm, d), BF16), pltpu.VMEM((tm, d), F32)],
    )
    return pl.pallas_call(
        _ffn_grouped_kernel,
        grid_spec=grid_spec,
        out_shape=jax.ShapeDtypeStruct((rows, d), F32),
        compiler_params=pltpu.CompilerParams(dimension_semantics=("arbitrary", "arbitrary"),
                                             vmem_limit_bytes=VMEM_LIMIT_V7X),
        name="ffn_grouped",
    )(blk_e, xd, g.reshape(1, d), w1, w3, w2)


def _moe(x, g, router, w1, w3, w2):
    n, d = x.shape
    tm = 1024 if n >= 8192 else 128
    router_p = jnp.pad(router, ((0, 0), (0, LANE - N_EXPERTS)))
    logits = _mm(x, router_p, g=g)[:, :N_EXPERTS]
    top_val, top_idx = lax.top_k(logits, TOP_K)
    gate = jax.nn.softmax(top_val, axis=-1).reshape(-1)
    e_flat = top_idx.reshape(-1)
    n_asg = n * TOP_K
    order = jnp.argsort(e_flat)
    onehot = (e_flat[:, None] == jnp.arange(N_EXPERTS, dtype=e_flat.dtype)[None, :]).astype(jnp.int32)
    csum = jnp.cumsum(onehot, axis=0)
    counts = csum[-1]
    rank = jnp.sum(onehot * csum, axis=1) - 1
    padded = (counts + tm - 1) // tm * tm
    pad_end = jnp.cumsum(padded)
    pad_start = pad_end - padded
    start = jnp.cumsum(counts) - counts
    pos = (pad_start[e_flat] + rank).astype(jnp.int32)
    n_blocks = n_asg // tm + N_EXPERTS
    blk_e = jnp.minimum(jnp.searchsorted(pad_end, jnp.arange(n_blocks) * tm, side='right'),
                        N_EXPERTS - 1).astype(jnp.int32)
    e_row = jnp.repeat(blk_e, tm)
    in_expert = jnp.arange(n_blocks * tm) - pad_start[e_row]
    src_asg = order[jnp.clip(start[e_row] + in_expert, 0, n_asg - 1)]
    src_tok = jnp.where(in_expert < counts[e_row], src_asg // TOP_K, 0).astype(jnp.int32)
    yd = _ffn_grouped(x[src_tok], blk_e, g, w1, w3, w2, tm)
    pos, gate = pos.reshape(n, TOP_K), gate.reshape(n, TOP_K)
    return x + sum(yd[pos[:, k]] * gate[:, k:k + 1] for k in range(TOP_K))


N_CHUNK = 128
CHUNK_W = CMP_STRIDE * HEAD_DIM


def _compress_kernel(x_ref, pe_ref, w1_ref, w2_ref, o_ref):
    c = x_ref[0]
    lo = jnp.dot((c + pe_ref[0:1]).astype(BF16), w1_ref[0].astype(BF16), preferred_element_type=F32)
    hi = jnp.dot((c + pe_ref[1:2]).astype(BF16), w1_ref[1].astype(BF16), preferred_element_type=F32)
    hid = jax.nn.gelu(lo + pltpu.roll(hi, N_CHUNK - 1, 0))
    out = jnp.dot(hid.astype(BF16), w2_ref[...].astype(BF16), preferred_element_type=F32)
    row = lax.broadcasted_iota(jnp.int32, out.shape, 0)
    o_ref[0] = jnp.where(row < N_CHUNK - 1, out, 0.0)


def _compress(xc, pe, w1, w2):
    nb = xc.shape[0]
    hidden = w1.shape[-1]
    return pl.pallas_call(
        _compress_kernel,
        grid=(nb,),
        in_specs=[pl.BlockSpec((1, N_CHUNK, CHUNK_W), lambda i: (i, 0, 0)),
                  pl.BlockSpec((2, CHUNK_W), lambda i: (0, 0)),
                  pl.BlockSpec((2, CHUNK_W, hidden), lambda i: (0, 0, 0)),
                  pl.BlockSpec((hidden, HEAD_DIM), lambda i: (0, 0))],
        out_specs=pl.BlockSpec((1, N_CHUNK, HEAD_DIM), lambda i: (i, 0, 0)),
        out_shape=jax.ShapeDtypeStruct((nb, N_CHUNK, HEAD_DIM), F32),
        compiler_params=pltpu.CompilerParams(dimension_semantics=("arbitrary",), vmem_limit_bytes=VMEM_LIMIT_V7X),
        name="compress",
    )(xc, pe.reshape(2, CHUNK_W), w1.reshape(2, CHUNK_W, hidden), w2)


def _compress_heads(kv, pe, w1, w2):
    b = kv.shape[0]
    xc = kv.reshape(b, N_CHUNK, CMP_STRIDE, NSA_KV_HEADS, HEAD_DIM).transpose(0, 3, 1, 2, 4)
    out = _compress(xc.reshape(b * NSA_KV_HEADS, N_CHUNK, CHUNK_W), pe, w1, w2)
    return out.reshape(b, NSA_KV_HEADS, N_CHUNK, HEAD_DIM)


TQ = 128
TK = 128
NEG = -1e30
N_BIAS_TILES = WINDOW // TK + 1
MASKED_TILE = N_BIAS_TILES
SEL_SUB = 4


def _nsa_prompt_kernel(q_ref, kct_ref, vcc_ref, kst_ref, vs_ref, kwt_ref, vw_ref, bcmp_ref, btile_ref, gate_ref,
                       ov_ref, exp_ref, o_ref, selneg_ref, m_ref, l_ref, acc_ref):
    i = pl.program_id(2)
    rows = NSA_GROUP * TQ
    qt = (q_ref[...] * (HEAD_DIM ** -0.5)).astype(BF16)
    q = jnp.concatenate([qt[:, a * HEAD_DIM:(a + 1) * HEAD_DIM] for a in range(NSA_GROUP)], axis=0)

    s = jnp.dot(q, kct_ref[0, 0], preferred_element_type=F32) + bcmp_ref[0].reshape(rows, N_CHUNK)
    m = jnp.max(s, axis=-1, keepdims=True)
    e = jnp.where(s > 0.1 * NEG, jnp.exp(s - m), 0.0)
    p = e / jnp.maximum(jnp.sum(e, axis=-1, keepdims=True), 1e-30)
    o_cmp = jnp.dot(p.astype(BF16), vcc_ref[0, 0], preferred_element_type=F32)

    p_sum = p[0:TQ] + p[TQ:2 * TQ] + p[2 * TQ:3 * TQ] + p[3 * TQ:4 * TQ]
    n_sel = T_PROMPT // SEL_BLOCK
    imp_t = sum(lax.dot_general(ov_ref[...], part, (((1,), (1,)), ((), ())), preferred_element_type=F32)
                for part in _split3(p_sum))[0:n_sel]
    blk = lax.broadcasted_iota(jnp.int32, (n_sel, TQ), 0)
    cur = (lax.broadcasted_iota(jnp.int32, (n_sel, TQ), 1) + i * TQ) // SEL_BLOCK
    valid = blk <= cur
    forced = (blk == 0) | (blk == cur) | (blk == cur - 1)
    score = jnp.where(valid, imp_t + jnp.where(forced, SEL_FORCE, 0.0), -1.0)
    rank = jnp.zeros((n_sel, TQ), F32)
    for c in range(n_sel):
        sc = score[c:c + 1, :]
        beats = (sc > score) | ((sc == score) & (blk > c))
        rank = rank + jnp.where(beats, 1.0, 0.0)
    sel_t = jnp.where(valid & (rank < SEL_TOP_N), 1.0, 0.0)
    sel = jnp.concatenate([sel_t, jnp.zeros((LANE - n_sel, TQ), F32)], axis=0).T.astype(BF16)
    sel_keys = jnp.dot(sel, exp_ref[...], preferred_element_type=F32)
    selneg_ref[...] = (sel_keys - 1.0) * (-NEG)

    halves = [slice(hf * NSA_GROUP // 2, (hf + 1) * NSA_GROUP // 2) for hf in range(2)]
    half_rows = [slice(gs.start * TQ, gs.stop * TQ) for gs in halves]

    def masked_scores(kt_ref, first_tile, n_sub, selected, gs, rs):
        k0 = pl.multiple_of(first_tile * TK, TK)
        s_all = jnp.dot(q[rs], kt_ref[0, 0, :, pl.ds(k0, n_sub * TK)], preferred_element_type=F32)
        pieces = []
        for u in range(n_sub):
            d0 = i - (first_tile + u)
            idx = jnp.where(d0 < 0, MASKED_TILE, jnp.minimum(d0, 2) if selected else d0)
            piece = s_all[:, u * TK:(u + 1) * TK].reshape(gs.stop - gs.start, TQ, TK) + btile_ref[0, idx, gs]
            if selected:
                piece = piece + selneg_ref[:, pl.ds(pl.multiple_of(k0 + u * TK, TK), TK)][None]
            pieces.append(piece.reshape(rs.stop - rs.start, TK))
        return pieces, k0

    def row_max(pieces):
        return jnp.max(functools.reduce(jnp.maximum, pieces), axis=-1, keepdims=True)

    def row_sum(pieces):
        return jnp.sum(functools.reduce(jnp.add, pieces), axis=-1, keepdims=True)

    m_ref[...] = jnp.full(m_ref.shape, -jnp.inf, F32)
    l_ref[...] = jnp.zeros(l_ref.shape, F32)
    acc_ref[...] = jnp.zeros(acc_ref.shape, F32)

    def sel_group(gi, carry):
        for gs, rs in zip(halves, half_rows):
            pieces, k0 = masked_scores(kst_ref, gi * SEL_SUB, SEL_SUB, True, gs, rs)
            m_old = m_ref[rs, :]
            m_new = jnp.maximum(m_old, row_max(pieces))
            pt = [jnp.exp(piece - m_new) for piece in pieces]
            alpha = jnp.exp(m_old - m_new)
            l_ref[rs, :] = alpha * l_ref[rs, :] + row_sum(pt)
            pv = jnp.dot(jnp.concatenate([x.astype(BF16) for x in pt], axis=1),
                         vs_ref[0, 0, pl.ds(k0, SEL_SUB * TK), :], preferred_element_type=F32)
            acc_ref[rs, :] = alpha * acc_ref[rs, :] + pv
            m_ref[rs, :] = m_new
        return carry

    lax.fori_loop(0, i // SEL_SUB + 1, sel_group, 0)
    o_sel = acc_ref[...] / l_ref[...]

    o_win = []
    for gs, rs in zip(halves, half_rows):
        pieces, k0 = masked_scores(kwt_ref, jnp.maximum(i - (N_BIAS_TILES - 1), 0), N_BIAS_TILES, False, gs, rs)
        m_win = row_max(pieces)
        pt = [jnp.exp(piece - m_win) for piece in pieces]
        o_win.append(jnp.dot(jnp.concatenate([x.astype(BF16) for x in pt], axis=1),
                             vw_ref[0, 0, pl.ds(k0, N_BIAS_TILES * TK), :], preferred_element_type=F32) / row_sum(pt))
    o_win = jnp.concatenate(o_win, axis=0)

    g = jax.nn.sigmoid(gate_ref[0, 0])
    pieces = []
    for a in range(NSA_GROUP):
        r = slice(a * TQ, (a + 1) * TQ)
        pieces.append(g[:, 3 * a:3 * a + 1] * o_cmp[r] + g[:, 3 * a + 1:3 * a + 2] * o_sel[r]
                      + g[:, 3 * a + 2:3 * a + 3] * o_win[r])
    o_ref[...] = jnp.concatenate(pieces, axis=1)


T_PROMPT = 2048


def _bias_lookup(table, dist):
    oh = jax.nn.one_hot(rel_bucket(dist), REL_BUCKETS, dtype=F32)
    return jnp.einsum('...r,rh->...h', oh, table, precision=lax.Precision.HIGHEST)


def _prompt_bias_tables(table):
    t = T_PROMPT
    q_pos = jnp.arange(t)
    block_end = jnp.arange(N_CHUNK) * CMP_STRIDE + (CMP_BLOCK - 1)
    dist = q_pos[:, None] - block_end[None, :]
    ok = (dist >= 0) & (jnp.arange(N_CHUNK)[None, :] < N_CHUNK - 1)
    bcmp = jnp.where(ok[..., None], _bias_lookup(table, dist), NEG)
    bcmp = bcmp.reshape(t, N_CHUNK, NSA_KV_HEADS, NSA_GROUP).transpose(2, 3, 0, 1)
    d0 = jnp.arange(N_BIAS_TILES)[:, None, None]
    dist = d0 * TK + jnp.arange(TQ)[None, :, None] - jnp.arange(TK)[None, None, :]
    ok = (dist >= 0) & (dist < WINDOW)
    bt = jnp.where(ok[..., None], _bias_lookup(table, dist), NEG)
    bt = bt.reshape(N_BIAS_TILES, TQ, TK, NSA_KV_HEADS, NSA_GROUP).transpose(3, 0, 4, 1, 2)
    bt = jnp.concatenate([bt, jnp.full((NSA_KV_HEADS, 1, NSA_GROUP, TQ, TK), NEG, F32)], axis=1)
    c0 = jnp.arange(N_CHUNK) * CMP_STRIDE
    s0 = jnp.arange(LANE) * SEL_BLOCK
    ov = jnp.clip(jnp.minimum(c0[:, None] + CMP_BLOCK, s0[None, :] + SEL_BLOCK)
                  - jnp.maximum(c0[:, None], s0[None, :]), 0, CMP_BLOCK).astype(F32) / CMP_BLOCK
    ov = jnp.where((jnp.arange(N_CHUNK)[:, None] < N_CHUNK - 1) & (jnp.arange(LANE)[None, :] < t // SEL_BLOCK), ov, 0.0)
    expand = (jnp.arange(LANE)[:, None] == (jnp.arange(t) // SEL_BLOCK)[None, :]).astype(BF16)
    return bcmp, bt, ov.T.astype(BF16), expand


def _nsa_prompt(z_a, kct, vcc, kst, vs, kwt, vw, gates, tables):
    bcmp, bt, ov, expand = tables
    b, _, _, t = kst.shape
    rows = NSA_GROUP * TQ
    n_q = t // TQ
    q_cols = NSA_GROUP * HEAD_DIM
    per_bh = lambda *blk: pl.BlockSpec((1, 1) + blk, lambda bi, h, i: (bi, h) + (0,) * len(blk))
    return pl.pallas_call(
        _nsa_prompt_kernel,
        grid=(b, NSA_KV_HEADS, n_q),
        in_specs=[pl.BlockSpec((TQ, q_cols), lambda bi, h, i: (bi * n_q + i, h)),
                  per_bh(HEAD_DIM, N_CHUNK), per_bh(N_CHUNK, HEAD_DIM),
                  per_bh(HEAD_DIM, t), per_bh(t, HEAD_DIM), per_bh(HEAD_DIM, t), per_bh(t, HEAD_DIM),
                  pl.BlockSpec((1, NSA_GROUP, TQ, N_CHUNK), lambda bi, h, i: (h, 0, i, 0)),
                  pl.BlockSpec((1, N_BIAS_TILES + 1, NSA_GROUP, TQ, TK), lambda bi, h, i: (h, 0, 0, 0, 0)),
                  pl.BlockSpec((1, 1, TQ, 3 * NSA_GROUP), lambda bi, h, i: (bi, h, i, 0)),
                  pl.BlockSpec((N_CHUNK, LANE), lambda bi, h, i: (0, 0)),
                  pl.BlockSpec((LANE, t), lambda bi, h, i: (0, 0))],
        out_specs=pl.BlockSpec((TQ, q_cols), lambda bi, h, i: (bi * n_q + i, h)),
        out_shape=jax.ShapeDtypeStruct((b * t, NSA_WIDTH), F32),
        scratch_shapes=[pltpu.VMEM((TQ, t), F32), pltpu.VMEM((rows, 1), F32), pltpu.VMEM((rows, 1), F32),
                        pltpu.VMEM((rows, HEAD_DIM), F32)],
        compiler_params=pltpu.CompilerParams(dimension_semantics=("arbitrary", "arbitrary", "arbitrary"),
                                             vmem_limit_bytes=VMEM_LIMIT_V7X),
        name="nsa_prompt",
    )(z_a, kct, vcc, kst, vs, kwt, vw, bcmp, bt, gates, ov, expand)


def _heads_t(x, b, t):
    return x.astype(BF16).reshape(b, t, NSA_KV_HEADS, HEAD_DIM).transpose(0, 2, 3, 1)


def _heads(x, b, t):
    return x.astype(BF16).reshape(b, t, NSA_KV_HEADS, HEAD_DIM).transpose(0, 2, 1, 3)


def _nsa_prompt_from_proj(z_a, z_s, b, t, tables, pe, w1, w2):
    col = lambda i: z_a[:, NSA_WIDTH + KV_WIDTH * i:NSA_WIDTH + KV_WIDTH * (i + 1)]
    kcc = _compress_heads(col(0).reshape(b, t, NSA_KV_HEADS, HEAD_DIM), pe[0], w1[0], w2[0])
    vcc = _compress_heads(col(1).reshape(b, t, NSA_KV_HEADS, HEAD_DIM), pe[1], w1[1], w2[1])
    gates = z_s[:, :3 * NSA_HEADS].reshape(b, t, NSA_KV_HEADS, 3 * NSA_GROUP).transpose(0, 2, 1, 3)
    return _nsa_prompt(z_a, kcc.astype(BF16).transpose(0, 1, 3, 2), vcc.astype(BF16),
                       _heads_t(col(2), b, t), _heads(col(3), b, t), _heads_t(col(4), b, t), _heads(col(5), b, t),
                       gates, tables)


T_SAMPLE = 4
T_PAD = 8
PAGE = 128
N_PAGES = 16
PAST = N_PAGES * PAGE
N_SEL_SAMPLE = -(-(PAST + T_SAMPLE) // SEL_BLOCK)


def _nsa_sample_kernel(pt_ref, *refs):
    del pt_ref
    cp, sp = refs[:N_PAGES], refs[N_PAGES:2 * N_PAGES]
    (wb_ref, za_ref, zs_ref, pe2_ref, w1_ref, w2_ref, bcmp_ref, bpast_ref, bwin_ref, bnew_ref, ov_ref, exp_ref,
     o_ref, new_ref, q_ref, g_ref, xcat_ref, kcat_ref, vcat_ref, s_ref, e_ref, oacc_ref,
     rowpg_ref) = refs[2 * N_PAGES:]
    rows = NSA_GROUP * T_PAD

    @pl.when(pl.program_id(0) == 0)
    def _():
        new_ref[...] = jnp.zeros(new_ref.shape, F32)
        q_ref[...] = jnp.zeros(q_ref.shape, F32)
        g_ref[...] = jnp.zeros(g_ref.shape, F32)

    new_ref[0:T_SAMPLE, :] = za_ref[0, :, NSA_WIDTH + 2 * KV_WIDTH:]
    q_ref[0:T_SAMPLE, :] = za_ref[0, :, :NSA_WIDTH] * (HEAD_DIM ** -0.5)
    g_ref[0:T_SAMPLE, :] = zs_ref[0]

    for p in range(N_PAGES):
        for kv in range(2):
            rowpg_ref[kv, p * PAGE:(p + 1) * PAGE, :] = cp[p][0, kv * KV_WIDTH:(kv + 1) * KV_WIDTH, :].T
    comp = []
    for kv in range(2):
        def place_row_offset(l, carry, kv=kv):
            xl = rowpg_ref[kv, pl.ds(l, N_CHUNK, stride=CMP_STRIDE), :]
            lanes = pl.ds(pl.multiple_of(l * KV_WIDTH, KV_WIDTH), KV_WIDTH)
            for half in range(2):
                xcat_ref[half, :, lanes] = (xl + pe2_ref[kv, pl.ds(l + CMP_STRIDE * half, 1), :]).astype(BF16)
            return carry

        lax.fori_loop(0, CMP_STRIDE, place_row_offset, 0)
        lo = jnp.dot(xcat_ref[0], w1_ref[kv, 0], preferred_element_type=F32)
        hi = jnp.dot(xcat_ref[1], w1_ref[kv, 1], preferred_element_type=F32)
        hid = jax.nn.gelu(lo + pltpu.roll(hi, N_CHUNK - 1, 0))
        out = jnp.dot(hid.astype(BF16), w2_ref[kv], preferred_element_type=F32)
        row = lax.broadcasted_iota(jnp.int32, out.shape, 0)
        comp.append(jnp.where(row < N_CHUNK - 1, out, 0.0).astype(BF16))

    for p in range(N_PAGES):
        kcat_ref[:, p * PAGE:(p + 1) * PAGE] = sp[p][0, 0:KV_WIDTH, :].astype(BF16)
        vcat_ref[:, p * PAGE:(p + 1) * PAGE] = sp[p][0, KV_WIDTH:, :].astype(BF16)
    kw_buf = wb_ref[0, 0:KV_WIDTH, :].astype(BF16)
    vw_buf = wb_ref[0, KV_WIDTH:, :].astype(BF16)
    ks_new, vs_new, kw_new, vw_new = (new_ref[:, i * KV_WIDTH:(i + 1) * KV_WIDTH].astype(BF16) for i in range(4))
    gates = jax.nn.sigmoid(g_ref[...])
    nt = (((1,), (1,)), ((), ()))
    zeros64 = jnp.zeros((rows, HEAD_DIM), F32)
    pieces = []
    key_chunk = 512

    def attend_window(qp, h):
        s_parts = [jnp.dot(qp, kw_buf, preferred_element_type=F32) + bwin_ref[h],
                   lax.dot_general(qp, kw_new, nt, preferred_element_type=F32) + bnew_ref[h]]
        m = jnp.maximum(*[jnp.max(s, axis=-1, keepdims=True) for s in s_parts])
        e_parts = [jnp.exp(s - m) for s in s_parts]
        den = jnp.add(*[jnp.sum(e, axis=-1, keepdims=True) for e in e_parts])
        num = (lax.dot_general(e_parts[0].astype(BF16), vw_buf, nt, preferred_element_type=F32)
               + jnp.dot(e_parts[1].astype(BF16), vw_new, preferred_element_type=F32))
        return (num / den)[:, h * HEAD_DIM:(h + 1) * HEAD_DIM]

    def attend_selected(qp, h):
        def scores(c, carry):
            ds = pl.ds(pl.multiple_of(c * key_chunk, key_chunk), key_chunk)
            s_ref[:, ds] += jnp.dot(qp, kcat_ref[:, ds], preferred_element_type=F32)
            return carry

        lax.fori_loop(0, PAST // key_chunk, scores, 0)
        s_ref[:, PAST:] += lax.dot_general(qp, ks_new, nt, preferred_element_type=F32)
        s = s_ref[...]
        e = jnp.exp(s - jnp.max(s, axis=-1, keepdims=True))
        den = jnp.sum(e, axis=-1, keepdims=True)
        e_ref[...] = e.astype(BF16)
        oacc_ref[...] = jnp.dot(e_ref[:, PAST:], vs_new, preferred_element_type=F32)

        def weighted(c, carry):
            ds = pl.ds(pl.multiple_of(c * key_chunk, key_chunk), key_chunk)
            oacc_ref[...] += lax.dot_general(e_ref[:, ds], vcat_ref[:, ds], nt, preferred_element_type=F32)
            return carry

        lax.fori_loop(0, PAST // key_chunk, weighted, 0)
        return (oacc_ref[...] / den)[:, h * HEAD_DIM:(h + 1) * HEAD_DIM]

    for h in range(NSA_KV_HEADS):
        q64 = jnp.concatenate([q_ref[:, (h * NSA_GROUP + a) * HEAD_DIM:(h * NSA_GROUP + a + 1) * HEAD_DIM]
                               for a in range(NSA_GROUP)], axis=0)
        qp = jnp.concatenate([q64, zeros64] if h == 0 else [zeros64, q64], axis=1).astype(BF16)

        s = lax.dot_general(qp, comp[0], nt, preferred_element_type=F32) + bcmp_ref[h]
        m = jnp.max(s, axis=-1, keepdims=True)
        e = jnp.where(s > 0.1 * NEG, jnp.exp(s - m), 0.0)
        p = e / jnp.maximum(jnp.sum(e, axis=-1, keepdims=True), 1e-30)
        o_cmp = jnp.dot(p.astype(BF16), comp[1], preferred_element_type=F32)[:, h * HEAD_DIM:(h + 1) * HEAD_DIM]

        p_sum = p[0:T_PAD] + p[T_PAD:2 * T_PAD] + p[2 * T_PAD:3 * T_PAD] + p[3 * T_PAD:4 * T_PAD]
        imp = sum(jnp.dot(part, ov_ref[...], preferred_element_type=F32) for part in _split3(p_sum))
        lane = lax.broadcasted_iota(jnp.int32, (T_PAD, LANE), 1)
        cur = (PAST + jnp.minimum(lax.broadcasted_iota(jnp.int32, (T_PAD, LANE), 0), T_SAMPLE - 1)) // SEL_BLOCK
        valid = lane <= cur
        forced = (lane == 0) | (lane == cur) | (lane == cur - 1)
        score = jnp.where(valid, imp + jnp.where(forced, SEL_FORCE, 0.0), -1.0)
        rank = jnp.zeros((T_PAD, LANE), F32)
        for c in range(N_SEL_SAMPLE):
            sc = score[:, c:c + 1]
            beats = (sc > score) | ((sc == score) & (lane > c))
            rank = rank + jnp.where(beats, 1.0, 0.0)
        sel = jnp.where(valid & (rank < SEL_TOP_N), 1.0, 0.0)
        selneg = (jnp.dot(sel.astype(BF16), exp_ref[...], preferred_element_type=F32) - 1.0) * (-NEG)
        selneg_new = (sel[:, PAST // SEL_BLOCK:PAST // SEL_BLOCK + 1] - 1.0) * (-NEG)

        def per_token(bias, tok):
            n = bias.shape[-1]
            return (bias.reshape(NSA_GROUP, T_PAD, n) + tok[None]).reshape(rows, n)

        s_ref[:, :PAST] = per_token(bpast_ref[h], selneg)
        s_ref[:, PAST:] = per_token(bnew_ref[h], jnp.broadcast_to(selneg_new, (T_PAD, LANE)))
        o_sel = attend_selected(qp, h)
        o_win = attend_window(qp, h)

        for a in range(NSA_GROUP):
            r = slice(a * T_PAD, (a + 1) * T_PAD)
            c0 = (h * NSA_GROUP + a) * 3
            pieces.append(gates[:, c0:c0 + 1] * o_cmp[r] + gates[:, c0 + 1:c0 + 2] * o_sel[r]
                          + gates[:, c0 + 2:c0 + 3] * o_win[r])
    o_ref[0] = jnp.concatenate(pieces, axis=1)[0:T_SAMPLE]


def _sample_bias_tables(table):
    tq = jnp.minimum(jnp.arange(T_PAD), T_SAMPLE - 1)
    q_pos = PAST + tq

    def lay(x):
        n = x.shape[1]
        return x.reshape(T_PAD, n, NSA_KV_HEADS, NSA_GROUP).transpose(2, 3, 0, 1).reshape(NSA_KV_HEADS, -1, n)

    block_end = jnp.arange(N_CHUNK) * CMP_STRIDE + (CMP_BLOCK - 1)
    dist = q_pos[:, None] - block_end[None, :]
    ok = (dist >= 0) & (jnp.arange(N_CHUNK)[None, :] < N_CHUNK - 1)
    bcmp = lay(jnp.where(ok[..., None], _bias_lookup(table, dist), NEG))
    dist = q_pos[:, None] - jnp.arange(PAST)[None, :]
    past = _bias_lookup(table, dist)
    bpast = lay(past)
    w_buf = min(WINDOW, PAST)
    bwin = lay(jnp.where((dist < WINDOW)[:, PAST - w_buf:, None], past[:, PAST - w_buf:], NEG))
    j = jnp.arange(LANE)
    dist = tq[:, None] - j[None, :]
    ok = (dist >= 0) & (j[None, :] < T_SAMPLE)
    bnew = lay(jnp.where(ok[..., None], _bias_lookup(table, dist), NEG))
    c0 = jnp.arange(N_CHUNK) * CMP_STRIDE
    s0 = jnp.arange(LANE) * SEL_BLOCK
    ov = jnp.clip(jnp.minimum(c0[:, None] + CMP_BLOCK, s0[None, :] + SEL_BLOCK)
                  - jnp.maximum(c0[:, None], s0[None, :]), 0, CMP_BLOCK).astype(F32) / CMP_BLOCK
    ov = jnp.where((jnp.arange(N_CHUNK)[:, None] < N_CHUNK - 1) & (j[None, :] < N_SEL_SAMPLE), ov, 0.0)
    expand = (j[:, None] == (jnp.arange(PAST) // SEL_BLOCK)[None, :]).astype(BF16)
    return bcmp, bpast, bwin, bnew, ov.astype(BF16), expand


def _nsa_sample(za, zs, cache_cmp, cache_slc, cache_win, page_table, layer, tables, pe, w1, w2):
    b = za.shape[0]
    n_phys = cache_cmp.shape[1]
    cmp_pages = jnp.transpose(cache_cmp, (0, 1, 3, 4, 5, 2)).reshape(DEPTH * n_phys, 2 * KV_WIDTH, PAGE)
    slc_pages = jnp.transpose(cache_slc, (0, 1, 3, 4, 5, 2)).reshape(DEPTH * n_phys, 2 * KV_WIDTH, PAGE)
    pt = page_table.reshape(-1).astype(jnp.int32) + layer * n_phys
    w_buf = cache_win.shape[2]
    win = jnp.transpose(cache_win, (0, 1, 3, 4, 5, 2)).reshape(DEPTH * b, 2 * KV_WIDTH, w_buf)
    bcmp, bpast, bwin, bnew, ov, expand = tables
    pe2 = jnp.concatenate([pe, pe], axis=-1)
    page_spec = lambda p: pl.BlockSpec((1, 2 * KV_WIDTH, PAGE), lambda bi, pt_: (pt_[bi * N_PAGES + p], 0, 0))
    whole = lambda x: pl.BlockSpec(x.shape, lambda bi, pt_: (0,) * x.ndim)
    eye = jnp.eye(NSA_KV_HEADS, dtype=F32)
    w1h = w1.reshape(2, 2, CMP_STRIDE, HEAD_DIM, CMP_HIDDEN)
    w1_bd = jnp.einsum('khlde,ab->khladbe', w1h, eye).reshape(2, 2, CMP_STRIDE * KV_WIDTH, NSA_KV_HEADS * CMP_HIDDEN)
    w2_bd = jnp.einsum('ked,ab->kaebd', w2, eye).reshape(2, NSA_KV_HEADS * CMP_HIDDEN, KV_WIDTH)
    consts = [pe2, w1_bd.astype(BF16), w2_bd.astype(BF16), bcmp, bpast, bwin, bnew, ov, expand]
    grid_spec = pltpu.PrefetchScalarGridSpec(
        num_scalar_prefetch=1,
        grid=(b,),
        in_specs=[page_spec(p) for p in range(N_PAGES)] * 2
        + [pl.BlockSpec((1, 2 * KV_WIDTH, w_buf), lambda bi, pt_: (layer * b + bi, 0, 0)),
           pl.BlockSpec((1, T_SAMPLE, za.shape[-1]), lambda bi, pt_: (bi, 0, 0)),
           pl.BlockSpec((1, T_SAMPLE, LANE), lambda bi, pt_: (bi, 0, 0))]
        + [whole(x) for x in consts],
        out_specs=pl.BlockSpec((1, T_SAMPLE, NSA_WIDTH), lambda bi, pt_: (bi, 0, 0)),
        scratch_shapes=[pltpu.VMEM((PAGE, 4 * KV_WIDTH), F32), pltpu.VMEM((T_PAD, NSA_WIDTH), F32),
                        pltpu.VMEM((T_PAD, LANE), F32), pltpu.VMEM((2, N_CHUNK, CMP_STRIDE * KV_WIDTH), BF16),
                        pltpu.VMEM((KV_WIDTH, PAST), BF16), pltpu.VMEM((KV_WIDTH, PAST), BF16),
                        pltpu.VMEM((NSA_GROUP * T_PAD, PAST + PAGE), F32),
                        pltpu.VMEM((NSA_GROUP * T_PAD, PAST + PAGE), BF16),
                        pltpu.VMEM((NSA_GROUP * T_PAD, KV_WIDTH), F32), pltpu.VMEM((2, PAST, KV_WIDTH), F32)],
    )
    return pl.pallas_call(
        _nsa_sample_kernel,
        grid_spec=grid_spec,
        out_shape=jax.ShapeDtypeStruct((b, T_SAMPLE, NSA_WIDTH), F32),
        compiler_params=pltpu.CompilerParams(dimension_semantics=("arbitrary",), vmem_limit_bytes=VMEM_LIMIT_V7X),
        name="nsa_sample",
    )(pt, *([cmp_pages] * N_PAGES), *([slc_pages] * N_PAGES), win, za, zs, *consts)


I_LANE = 3 * NSA_HEADS
F_LANE = I_LANE + M_HEADS
EXT_PAD = 8


def _mlstm_kernel(zu_ref, zs_ref, c0_ref, n0_ref, m0_ref, cprev_ref, convw_ref, convb_ref, wq_ref, wk_ref, wkt_ref,
                  wv_ref, gb_ref, normg_ref, tril_ref, selrow_ref,
                  o_ref, c_out, n_out, m_out, conv_out,
                  ext_ref, zsp_ref, c_sc, n_sc, m_sc, *, rows, rows_pad, n_valid):
    i = pl.program_id(1)
    L = M_CHUNK

    @pl.when(i == 0)
    def _():
        c_sc[...] = c0_ref[0]
        n_sc[...] = n0_ref[0]
        m_sc[...] = m0_ref[0]
        ext_ref[EXT_PAD - (CONV_W - 1):EXT_PAD, :] = cprev_ref[0]

    if rows < rows_pad:
        ext_ref[EXT_PAD:, :] = jnp.zeros((rows_pad, M_WIDTH), F32)
        zsp_ref[...] = jnp.zeros(zsp_ref.shape, F32)
    ext_ref[EXT_PAD:EXT_PAD + rows, :] = zu_ref[0, :, :M_WIDTH]
    zsp_ref[0:rows, :] = zs_ref[0]

    u = ext_ref[EXT_PAD:EXT_PAD + rows_pad, :]
    conv = convb_ref[...] + convw_ref[CONV_W - 1:CONV_W, :] * u
    for j in range(CONV_W - 1):
        lo = EXT_PAD - (CONV_W - 1) + j
        conv = conv + convw_ref[j:j + 1, :] * ext_ref[lo:lo + rows_pad, :]
    uc = (conv * jax.nn.sigmoid(conv)).astype(BF16)
    ub = u.astype(BF16)

    zb = zsp_ref[...] + gb_ref[...]
    bcum = sum(jnp.dot(tril_ref[...], part, preferred_element_type=F32) for part in _split3(jax.nn.log_sigmoid(zb)))
    bcum = pltpu.roll(bcum, LANE - M_HEADS, 1)
    a_col = zb - bcum
    a_row = sum(lax.dot_general(selrow_ref[...], part, (((1,), (1,)), ((), ())), preferred_element_type=F32)
                for part in _split3(a_col))

    tt = lax.broadcasted_iota(jnp.int32, (L, L), 0)
    ss = lax.broadcasted_iota(jnp.int32, (L, L), 1)
    causal = ss <= tt
    tok_col = lax.broadcasted_iota(jnp.int32, (L, 1), 0)
    tok_row = lax.broadcasted_iota(jnp.int32, (1, L), 1)

    for h in range(M_HEADS):
        hs = slice(h * M_HEAD_DIM, (h + 1) * M_HEAD_DIM)
        q_all = jnp.dot(uc[:, hs], wq_ref[h].astype(BF16), preferred_element_type=F32) * (M_HEAD_DIM ** -0.5)
        k_all = jnp.dot(uc[:, hs], wk_ref[h].astype(BF16), preferred_element_type=F32)
        kt_all = lax.dot_general(wkt_ref[h].astype(BF16), uc[:, hs], (((1,), (1,)), ((), ())),
                                 preferred_element_type=F32)
        v_all = jnp.dot(ub[:, hs], wv_ref[h].astype(BF16), preferred_element_type=F32)
        c_st = c_sc[h]
        n_st = n_sc[h:h + 1, :]
        m_st = m_sc[h:h + 1, 0:1]
        for c in range(rows_pad // L):
            r = slice(c * L, (c + 1) * L)
            q, k, kt, v = q_all[r].astype(BF16), k_all[r], kt_all[:, r], v_all[r].astype(BF16)
            b_col = bcum[r, I_LANE + h:I_LANE + h + 1]
            a_c = a_col[r, I_LANE + h:I_LANE + h + 1]
            a_r = a_row[h:h + 1, r]
            log_d = jnp.where(causal, b_col + a_r, NEG)
            m_col = jnp.maximum(b_col + m_st, jnp.max(log_d, axis=-1, keepdims=True))
            dw = jnp.exp(log_d - m_col)
            w_inter = jnp.exp(b_col + m_st - m_col)
            s = lax.dot_general(q, k.astype(BF16), (((1,), (1,)), ((), ())), preferred_element_type=F32) * dw
            num = (jnp.dot(s.astype(BF16), v, preferred_element_type=F32)
                   + w_inter * jnp.dot(q, c_st.astype(BF16), preferred_element_type=F32))
            den = (jnp.sum(s, axis=-1, keepdims=True)
                   + w_inter * jnp.sum(q_all[r] * n_st, axis=-1, keepdims=True))
            hh = num / jnp.maximum(jnp.abs(den), jnp.exp(-m_col))
            hn = hh * lax.rsqrt(jnp.mean(hh * hh, axis=-1, keepdims=True) + EPS)
            o_pre = zu_ref[0, :, M_WIDTH + h * M_HEAD_DIM:M_WIDTH + (h + 1) * M_HEAD_DIM]
            if rows < rows_pad:
                o_ref[0, :, hs] = jax.nn.sigmoid(o_pre) * hn[0:rows] * normg_ref[:, hs]
            else:
                o_ref[0, r, hs] = jax.nn.sigmoid(o_pre[r]) * hn * normg_ref[:, hs]
            b_last = b_col[n_valid - 1:n_valid, :]
            log_s = jnp.where(tok_col < n_valid, b_last + a_c, NEG)
            m_new = jnp.maximum(b_last + m_st, jnp.max(log_s, axis=0, keepdims=True))
            ws_col = jnp.exp(log_s - m_new)
            ws_row = jnp.where(tok_row < n_valid, jnp.exp(b_last + a_r - m_new), 0.0)
            wc = jnp.exp(b_last + m_st - m_new)
            c_st = wc * c_st + jnp.dot((kt * ws_row).astype(BF16), v, preferred_element_type=F32)
            n_st = wc * n_st + jnp.sum(k * ws_col, axis=0, keepdims=True)
            m_st = m_new
        c_sc[h] = c_st
        n_sc[h:h + 1, :] = n_st
        m_sc[h:h + 1, :] = jnp.broadcast_to(m_st, (1, LANE))

    tail = ext_ref[EXT_PAD + rows - (CONV_W - 1):EXT_PAD + rows, :]
    ext_ref[EXT_PAD - (CONV_W - 1):EXT_PAD, :] = tail

    @pl.when(i == pl.num_programs(1) - 1)
    def _():
        c_out[0] = c_sc[...]
        n_out[0] = n_sc[...]
        m_out[0] = m_sc[...]
        conv_out[0] = jnp.zeros(conv_out.shape[1:], F32)
        conv_out[0, 0:CONV_W - 1, :] = tail


def _mlstm(zu, zs, state, conv_prev, conv_w, conv_b, m_qkv, gate_b, norm_g):
    b, t, _ = zu.shape
    rows = min(t, 4 * M_CHUNK)
    rows_pad = -(-rows // M_CHUNK) * M_CHUNK
    n_valid = M_CHUNK if rows == rows_pad else rows
    assert t % rows == 0 and (rows == rows_pad or t == rows)
    c0, n0, m0 = state
    n0p = jnp.pad(n0, ((0, 0), (0, 8 - M_HEADS), (0, 0)))
    m0p = jnp.pad(jnp.broadcast_to(m0[:, :, None], (b, M_HEADS, LANE)), ((0, 0), (0, 8 - M_HEADS), (0, 0)))
    gb = jnp.zeros((1, LANE), F32).at[0, I_LANE:I_LANE + 2 * M_HEADS].set(gate_b.reshape(-1))
    idx = jnp.arange(rows_pad)
    tril = ((idx[:, None] >= idx[None, :]) & (idx[:, None] // M_CHUNK == idx[None, :] // M_CHUNK)).astype(BF16)
    selrow = ((jnp.arange(16)[:, None] + I_LANE == jnp.arange(LANE)[None, :])
              & (jnp.arange(16)[:, None] < M_HEADS)).astype(BF16)
    whole = lambda *shape: pl.BlockSpec(shape, lambda bi, i: (0,) * len(shape))
    per_b = lambda *shape: pl.BlockSpec((1,) + shape, lambda bi, i: (bi,) + (0,) * len(shape))
    out, c_new, n_new, m_new, conv_new = pl.pallas_call(
        functools.partial(_mlstm_kernel, rows=rows, rows_pad=rows_pad, n_valid=n_valid),
        grid=(b, t // rows),
        in_specs=[pl.BlockSpec((1, rows, 2 * M_WIDTH), lambda bi, i: (bi, i, 0)),
                  pl.BlockSpec((1, rows, LANE), lambda bi, i: (bi, i, 0)),
                  per_b(M_HEADS, M_HEAD_DIM, M_HEAD_DIM), per_b(8, M_HEAD_DIM), per_b(8, LANE),
                  per_b(CONV_W - 1, M_WIDTH),
                  whole(CONV_W, M_WIDTH), whole(1, M_WIDTH),
                  whole(M_HEADS, M_HEAD_DIM, M_HEAD_DIM), whole(M_HEADS, M_HEAD_DIM, M_HEAD_DIM),
                  whole(M_HEADS, M_HEAD_DIM, M_HEAD_DIM), whole(M_HEADS, M_HEAD_DIM, M_HEAD_DIM),
                  whole(1, LANE), whole(1, M_WIDTH), whole(rows_pad, rows_pad), whole(16, LANE)],
        out_specs=[pl.BlockSpec((1, rows, M_WIDTH), lambda bi, i: (bi, i, 0)),
                   per_b(M_HEADS, M_HEAD_DIM, M_HEAD_DIM), per_b(8, M_HEAD_DIM), per_b(8, LANE),
                   per_b(8, M_WIDTH)],
        out_shape=[jax.ShapeDtypeStruct((b, t, M_WIDTH), F32),
                   jax.ShapeDtypeStruct((b, M_HEADS, M_HEAD_DIM, M_HEAD_DIM), F32),
                   jax.ShapeDtypeStruct((b, 8, M_HEAD_DIM), F32),
                   jax.ShapeDtypeStruct((b, 8, LANE), F32),
                   jax.ShapeDtypeStruct((b, 8, M_WIDTH), F32)],
        scratch_shapes=[pltpu.VMEM((EXT_PAD + rows_pad, M_WIDTH), F32), pltpu.VMEM((rows_pad, LANE), F32),
                        pltpu.VMEM((M_HEADS, M_HEAD_DIM, M_HEAD_DIM), F32), pltpu.VMEM((8, M_HEAD_DIM), F32),
                        pltpu.VMEM((8, LANE), F32)],
        compiler_params=pltpu.CompilerParams(dimension_semantics=("arbitrary", "arbitrary"),
                                             vmem_limit_bytes=VMEM_LIMIT_V7X),
        name="mlstm",
    )(zu, zs, c0, n0p, m0p, conv_prev, conv_w, conv_b.reshape(1, M_WIDTH), m_qkv[0], m_qkv[1],
      m_qkv[1].transpose(0, 2, 1),
      m_qkv[2], gb, norm_g.reshape(1, M_WIDTH), tril, selrow)
    return out, (c_new, n_new[:, :M_HEADS], m_new[:, :M_HEADS, 0], conv_new[:, :CONV_W - 1])


def rel_bucket(dist):
    d = jnp.maximum(dist, 0)
    exact = REL_BUCKETS // 2
    log_part = exact + (jnp.log(jnp.maximum(d, 1).astype(F32) / exact)
                        / math.log(REL_MAX_DIST / exact) * (REL_BUCKETS - exact)).astype(jnp.int32)
    return jnp.where(d < exact, d, jnp.minimum(log_part, REL_BUCKETS - 1))


def _split_w_in(w_in):
    offs = np.cumsum((0,) + IN_WIDTHS)
    w_a = w_in[:, :offs[7]]
    w_small = jnp.concatenate([w_in[:, offs[7]:offs[8]], w_in[:, offs[10]:offs[11]]], axis=1)
    w_small = jnp.pad(w_small, ((0, 0), (0, LANE - w_small.shape[1])))
    w_u = w_in[:, offs[8]:offs[10]]
    w_g = w_in[:, offs[11]:]
    return w_a, w_small, w_u, w_g


def mixer(x, norm_g, prm, past):
    b, t, _ = x.shape
    x2 = x.reshape(b * t, D_MODEL)
    z_a, z_s, z_u, z_g = _proj(x2, norm_g, [w.astype(BF16) for w in _split_w_in(prm['w_in'])])
    z_a, z_s, z_u = (z.reshape(b, t, -1) for z in (z_a, z_s, z_u))
    new_cmp, new_slc, win_rows = (z_a[..., NSA_WIDTH + 2 * KV_WIDTH * i:NSA_WIDTH + 2 * KV_WIDTH * (i + 1)]
                                  .reshape(b, t, 2, NSA_KV_HEADS, HEAD_DIM) for i in range(3))
    if past is None:
        o_nsa = _nsa_prompt_from_proj(z_a.reshape(b * t, -1), z_s.reshape(b * t, -1), b, t, prm['prompt_tables'],
                                      prm['cmp_pe'], prm['cmp_w1'], prm['cmp_w2'])
        new_win = win_rows[:, t - min(WINDOW, t):]
        conv_prev = jnp.zeros((b, CONV_W - 1, M_WIDTH), F32)
        m_state = (jnp.zeros((b, M_HEADS, M_HEAD_DIM, M_HEAD_DIM), F32),
                   jnp.zeros((b, M_HEADS, M_HEAD_DIM), F32),
                   jnp.zeros((b, M_HEADS), F32))
    else:
        layer = past['layer']
        assert t == T_SAMPLE and past['page_table'].shape[1] == N_PAGES and past['cmp'].shape[2] == PAGE
        o_nsa = _nsa_sample(z_a, z_s, past['cmp'], past['slc'], past['win'], past['page_table'], layer,
                            prm['sample_tables'], prm['cmp_pe'], prm['cmp_w1'], prm['cmp_w2'])
        o_nsa = o_nsa.reshape(b * t, NSA_WIDTH)
        new_win = win_rows
        conv_prev = past['conv']
        m_state = (past['C'].astype(F32), past['n'].astype(F32), past['m'].astype(F32))
    o_m, (c_new, n_new, m_new, conv_new) = _mlstm(z_u, z_s, m_state, conv_prev, prm['conv_w'], prm['conv_b'],
                                                  prm['m_qkv'], prm['gate_b'], prm['m_norm'])
    y = _merge_out(x2, o_nsa, o_m.reshape(b * t, M_WIDTH), z_g, prm['w_up_a'].astype(BF16),
                   prm['w_up_b'].astype(BF16), prm['w_out'].astype(BF16)).reshape(b, t, D_MODEL)
    return y, (new_cmp, new_slc, new_win, c_new, n_new, m_new, conv_new)


def _channel_mixer(x, g, l, ffn_w1, ffn_w3, ffn_w2, moe_router, moe_w1, moe_w3, moe_w2):
    b, t, d = x.shape
    x2 = x.reshape(b * t, d)
    i = l // 2
    if l % 2 == 0:
        y = _ffn_dense(x2, g, ffn_w1[i], ffn_w3[i], ffn_w2[i])
    else:
        y = _moe(x2, g, moe_router[i], moe_w1[i], moe_w3[i], moe_w2[i])
    return y.reshape(b, t, d)


def _final_norm(x, g):
    xf = x.astype(F32)
    return xf * lax.rsqrt(jnp.mean(xf * xf, axis=-1, keepdims=True) + EPS) * g


def kernel(x_prompt, x_sample, cache_cmp_kv, cache_slc_kv, cache_win_kv, state_mlstm_C, state_mlstm_n,
           state_mlstm_m, state_mlstm_conv, page_table, rel_bias_table, norm_mix, norm_ffn, norm_final,
           w_in, cmp_pe, cmp_w1, cmp_w2, m_conv_w, m_conv_b, m_qkv, m_gate_bias, m_norm, w_up_a, w_up_b,
           w_out, ffn_w1, ffn_w3, ffn_w2, moe_router, moe_w1, moe_w3, moe_w2):
    xp, xs = x_prompt, x_sample
    prompt_states, sample_states = [], []
    prompt_tables = _prompt_bias_tables(rel_bias_table)
    sample_tables = _sample_bias_tables(rel_bias_table)
    for l in range(DEPTH):
        prm = {'prompt_tables': prompt_tables, 'sample_tables': sample_tables, 'w_in': w_in[l], 'cmp_pe': cmp_pe[l],
               'cmp_w1': cmp_w1[l], 'cmp_w2': cmp_w2[l], 'conv_w': m_conv_w[l], 'conv_b': m_conv_b[l],
               'm_qkv': m_qkv[l], 'gate_b': m_gate_bias[l], 'm_norm': m_norm[l], 'w_up_a': w_up_a[l],
               'w_up_b': w_up_b[l], 'w_out': w_out[l]}
        past = {'cmp': cache_cmp_kv, 'slc': cache_slc_kv, 'win': cache_win_kv, 'layer': l,
                'C': state_mlstm_C[l], 'n': state_mlstm_n[l], 'm': state_mlstm_m[l],
                'conv': state_mlstm_conv[l], 'page_table': page_table}
        xp, st_p = mixer(xp, norm_mix[l], prm, None)
        xs, st_s = mixer(xs, norm_mix[l], prm, past)
        xp = _channel_mixer(xp, norm_ffn[l], l, ffn_w1, ffn_w3, ffn_w2, moe_router, moe_w1, moe_w3, moe_w2)
        xs = _channel_mixer(xs, norm_ffn[l], l, ffn_w1, ffn_w3, ffn_w2, moe_router, moe_w1, moe_w3, moe_w2)
        prompt_states.append(st_p)
        sample_states.append(st_s)
    y_prompt = _final_norm(xp, norm_final)
    y_sample = _final_norm(xs, norm_final)
    ps = [jnp.stack([s[i] for s in prompt_states]) for i in range(7)]
    ss = [jnp.stack([s[i] for s in sample_states]) for i in range(7)]
    ss[2] = jnp.concatenate([cache_win_kv[:, :, x_sample.shape[1]:], ss[2]], axis=2)
    return (y_prompt, y_sample, ps[0], ps[1], ps[2], ps[3], ps[4], ps[5], ps[6],
            ss[0], ss[1], ss[2], ss[3], ss[4], ss[5], ss[6])
```

```python
import functools
import math

import jax
import jax.numpy as jnp
import numpy as np
from jax import lax
from jax.experimental import pallas as pl
from jax.experimental.pallas import tpu as pltpu

D_MODEL = 1024
DEPTH = 2
NSA_HEADS = 8
NSA_KV_HEADS = 2
NSA_GROUP = NSA_HEADS // NSA_KV_HEADS
HEAD_DIM = 64
NSA_WIDTH = NSA_HEADS * HEAD_DIM
KV_WIDTH = NSA_KV_HEADS * HEAD_DIM
CMP_BLOCK = 32
CMP_STRIDE = 16
CMP_HIDDEN = 128
SEL_BLOCK = 64
SEL_TOP_N = 16
SEL_FORCE = 1e4
WINDOW = 512
WIN_Q_BLOCK = 128
REL_BUCKETS = 32
REL_MAX_DIST = 128
M_HEADS = 4
M_HEAD_DIM = 128
M_WIDTH = M_HEADS * M_HEAD_DIM
CONV_W = 4
M_CHUNK = 64
D_FF = 2816
N_EXPERTS = 8
TOP_K = 2
EPS = 1e-6
IN_WIDTHS = (NSA_WIDTH, KV_WIDTH, KV_WIDTH, KV_WIDTH, KV_WIDTH, KV_WIDTH, KV_WIDTH, 3 * NSA_HEADS, M_WIDTH, M_WIDTH,
             2 * M_HEADS, D_MODEL, D_MODEL)

VMEM_LIMIT_V7X = 52 * 1024 * 1024
LANE = 128

F32 = jnp.float32
BF16 = jnp.bfloat16


def _split3(x):
    hi = x.astype(BF16)
    r1 = x - hi.astype(F32)
    mid = r1.astype(BF16)
    lo = (r1 - mid.astype(F32)).astype(BF16)
    return hi, mid, lo


def _pick_tile(n, cands):
    for c in cands:
        if n % c == 0:
            return c
    return n


def _mm_kernel(*refs, norm, has_res):
    x_ref, g_ref, w_ref = refs[:3]
    res_ref = refs[3] if has_res else None
    o_ref, xs_ref = refs[-2], refs[-1]

    @pl.when(pl.program_id(1) == 0)
    def _():
        x = x_ref[...]
        if norm:
            x = x * lax.rsqrt(jnp.mean(x * x, axis=-1, keepdims=True) + EPS) * g_ref[...]
        xs_ref[...] = x.astype(BF16)

    acc = jnp.dot(xs_ref[...], w_ref[...].astype(BF16), preferred_element_type=F32)
    if has_res:
        acc = acc + res_ref[...]
    o_ref[...] = acc


def _mm(x, w, g=None, res=None):
    m, k = x.shape
    n = w.shape[1]
    tm = _pick_tile(m, (1024, 512, 256, 128))
    tn = _pick_tile(n, (512, 256, 128))
    norm = g is not None
    gg = (g if norm else jnp.ones((k,), F32)).reshape(1, k)
    in_specs = [pl.BlockSpec((tm, k), lambda i, j: (i, 0)),
                pl.BlockSpec((1, k), lambda i, j: (0, 0)),
                pl.BlockSpec((k, tn), lambda i, j: (0, j))]
    args = [x, gg, w]
    if res is not None:
        in_specs.append(pl.BlockSpec((tm, tn), lambda i, j: (i, j)))
        args.append(res)
    return pl.pallas_call(
        functools.partial(_mm_kernel, norm=norm, has_res=res is not None),
        grid=(m // tm, n // tn),
        in_specs=in_specs,
        out_specs=pl.BlockSpec((tm, tn), lambda i, j: (i, j)),
        out_shape=jax.ShapeDtypeStruct((m, n), F32),
        scratch_shapes=[pltpu.VMEM((tm, k), BF16)],
        compiler_params=pltpu.CompilerParams(dimension_semantics=("arbitrary", "arbitrary"),
                                             vmem_limit_bytes=VMEM_LIMIT_V7X),
        name="mm",
    )(*args)


def _proj_kernel(x_ref, g_ref, *refs):
    n_out = len(refs) // 2
    x = x_ref[...]
    xn = (x * lax.rsqrt(jnp.mean(x * x, axis=-1, keepdims=True) + EPS) * g_ref[...]).astype(BF16)
    for w_ref, o_ref in zip(refs[:n_out], refs[n_out:]):
        o_ref[...] = jnp.dot(xn, w_ref[...], preferred_element_type=F32)


def _proj(x, g, weights):
    m, k = x.shape
    tm = _pick_tile(m, (256, 128))
    row = lambda n: pl.BlockSpec((tm, n), lambda i: (i, 0))
    whole = lambda a: pl.BlockSpec(a.shape, lambda i: (0,) * a.ndim)
    return pl.pallas_call(
        _proj_kernel,
        grid=(m // tm,),
        in_specs=[row(k), whole(g.reshape(1, k))] + [whole(w) for w in weights],
        out_specs=[row(w.shape[1]) for w in weights],
        out_shape=[jax.ShapeDtypeStruct((m, w.shape[1]), F32) for w in weights],
        compiler_params=pltpu.CompilerParams(dimension_semantics=("arbitrary",), vmem_limit_bytes=VMEM_LIMIT_V7X),
        name="proj",
    )(x, g.reshape(1, k), *weights)


def _merge_out_kernel(x_ref, oa_ref, ob_ref, zg_ref, wa_ref, wb_ref, wo_ref, o_ref):
    d = x_ref.shape[-1]
    up_a = jnp.dot(oa_ref[...].astype(BF16), wa_ref[...], preferred_element_type=F32)
    up_b = jnp.dot(ob_ref[...].astype(BF16), wb_ref[...], preferred_element_type=F32)
    merged = jax.nn.sigmoid(zg_ref[:, :d]) * up_a + jax.nn.sigmoid(zg_ref[:, d:]) * up_b
    o_ref[...] = x_ref[...] + jnp.dot(merged.astype(BF16), wo_ref[...], preferred_element_type=F32)


def _merge_out(x, o_a, o_b, z_g, w_up_a, w_up_b, w_out):
    m, d = x.shape
    tm = _pick_tile(m, (512, 256, 128))
    row = lambda a: pl.BlockSpec((tm, a.shape[1]), lambda i: (i, 0))
    whole = lambda a: pl.BlockSpec(a.shape, lambda i: (0,) * a.ndim)
    return pl.pallas_call(
        _merge_out_kernel,
        grid=(m // tm,),
        in_specs=[row(x), row(o_a), row(o_b), row(z_g), whole(w_up_a), whole(w_up_b), whole(w_out)],
        out_specs=row(x),
        out_shape=jax.ShapeDtypeStruct((m, d), F32),
        compiler_params=pltpu.CompilerParams(dimension_semantics=("arbitrary",), vmem_limit_bytes=VMEM_LIMIT_V7X),
        name="merge_out",
    )(x, o_a, o_b, z_g, w_up_a, w_up_b, w_out)


def _ffn_body(x_ref, g_ref, w1_ref, w3_ref, w2_ref, o_ref, xs_ref, acc_ref, *, residual, grouped, active=None):
    j = pl.program_id(1)

    @pl.when(j == 0)
    def _():
        x = x_ref[...]
        xn = x * lax.rsqrt(jnp.mean(x * x, axis=-1, keepdims=True) + EPS) * g_ref[...]
        xs_ref[...] = xn.astype(BF16)
        acc_ref[...] = jnp.zeros_like(acc_ref)

    def accumulate():
        xs = xs_ref[...]
        w1 = w1_ref[0] if grouped else w1_ref[...]
        w3 = w3_ref[0] if grouped else w3_ref[...]
        w2 = w2_ref[0] if grouped else w2_ref[...]
        a = jnp.dot(xs, w1.astype(BF16), preferred_element_type=F32)
        b = jnp.dot(xs, w3.astype(BF16), preferred_element_type=F32)
        h = (a * jax.nn.sigmoid(a) * b).astype(BF16)
        acc_ref[...] += jnp.dot(h, w2.astype(BF16), preferred_element_type=F32)

    if active is None:
        accumulate()
    else:
        pl.when(active)(accumulate)

    @pl.when(j == pl.num_programs(1) - 1)
    def _():
        if residual:
            o_ref[...] = x_ref[...] + acc_ref[...]
        else:
            o_ref[...] = acc_ref[...]


def _ffn_dense_kernel(x_ref, g_ref, w1_ref, w3_ref, w2_ref, o_ref, xs_ref, acc_ref):
    _ffn_body(x_ref, g_ref, w1_ref, w3_ref, w2_ref, o_ref, xs_ref, acc_ref, residual=True, grouped=False)


def _ffn_grouped_kernel(be_ref, nu_ref, x_ref, g_ref, w1_ref, w3_ref, w2_ref, o_ref, xs_ref, acc_ref):
    del be_ref
    _ffn_body(x_ref, g_ref, w1_ref, w3_ref, w2_ref, o_ref, xs_ref, acc_ref, residual=False, grouped=True,
              active=pl.program_id(0) < nu_ref[0])


def _ffn_dense(x, g, w1, w3, w2):
    m, d = x.shape
    f = w1.shape[1]
    tm = _pick_tile(m, (1024, 512, 256, 128))
    tf = _pick_tile(f, (256, 128))
    return pl.pallas_call(
        _ffn_dense_kernel,
        grid=(m // tm, f // tf),
        in_specs=[pl.BlockSpec((tm, d), lambda i, j: (i, 0)),
                  pl.BlockSpec((1, d), lambda i, j: (0, 0)),
                  pl.BlockSpec((d, tf), lambda i, j: (0, j)),
                  pl.BlockSpec((d, tf), lambda i, j: (0, j)),
                  pl.BlockSpec((tf, d), lambda i, j: (j, 0))],
        out_specs=pl.BlockSpec((tm, d), lambda i, j: (i, 0)),
        out_shape=jax.ShapeDtypeStruct((m, d), F32),
        scratch_shapes=[pltpu.VMEM((tm, d), BF16), pltpu.VMEM((tm, d), F32)],
        compiler_params=pltpu.CompilerParams(dimension_semantics=("arbitrary", "arbitrary"),
                                             vmem_limit_bytes=VMEM_LIMIT_V7X),
        name="ffn_dense",
    )(x, g.reshape(1, d), w1, w3, w2)


def _ffn_grouped(xd, blk_e, n_used, g, w1, w3, w2, tm):
    rows, d = xd.shape
    f = w1.shape[2]
    tf = _pick_tile(f, (256, 128))
    n_f = f // tf
    col = lambda i, j, nu: jnp.where(i < nu[0], j, n_f - 1)
    grid_spec = pltpu.PrefetchScalarGridSpec(
        num_scalar_prefetch=2,
        grid=(rows // tm, n_f),
        in_specs=[pl.BlockSpec((tm, d), lambda i, j, be, nu: (i, 0)),
                  pl.BlockSpec((1, d), lambda i, j, be, nu: (0, 0)),
                  pl.BlockSpec((1, d, tf), lambda i, j, be, nu: (be[i], 0, col(i, j, nu))),
                  pl.BlockSpec((1, d, tf), lambda i, j, be, nu: (be[i], 0, col(i, j, nu))),
                  pl.BlockSpec((1, tf, d), lambda i, j, be, nu: (be[i], col(i, j, nu), 0))],
        out_specs=pl.BlockSpec((tm, d), lambda i, j, be, nu: (i, 0)),
        scratch_shapes=[pltpu.VMEM((tm, d), BF16), pltpu.VMEM((tm, d), F32)],
    )
    return pl.pallas_call(
        _ffn_grouped_kernel,
        grid_spec=grid_spec,
        out_shape=jax.ShapeDtypeStruct((rows, d), F32),
        compiler_params=pltpu.CompilerParams(dimension_semantics=("arbitrary", "arbitrary"),
                                             vmem_limit_bytes=VMEM_LIMIT_V7X),
        name="ffn_grouped",
    )(blk_e, n_used, xd, g.reshape(1, d), w1, w3, w2)


def _moe(x, g, router, w1, w3, w2):
    n, d = x.shape
    tm = 1024 if n >= 8192 else 128
    router_p = jnp.pad(router, ((0, 0), (0, LANE - N_EXPERTS)))
    logits = _mm(x, router_p, g=g)[:, :N_EXPERTS]
    top_val, top_idx = lax.top_k(logits, TOP_K)
    gate = jax.nn.softmax(top_val, axis=-1).reshape(-1)
    e_flat = top_idx.reshape(-1)
    n_asg = n * TOP_K
    order = jnp.argsort(e_flat)
    onehot = (e_flat[:, None] == jnp.arange(N_EXPERTS, dtype=e_flat.dtype)[None, :]).astype(jnp.int32)
    csum = jnp.cumsum(onehot, axis=0)
    counts = csum[-1]
    rank = jnp.sum(onehot * csum, axis=1) - 1
    padded = (counts + tm - 1) // tm * tm
    pad_end = jnp.cumsum(padded)
    pad_start = pad_end - padded
    start = jnp.cumsum(counts) - counts
    pos = (pad_start[e_flat] + rank).astype(jnp.int32)
    n_blocks = n_asg // tm + N_EXPERTS
    blk_e = jnp.minimum(jnp.searchsorted(pad_end, jnp.arange(n_blocks) * tm, side='right'),
                        N_EXPERTS - 1).astype(jnp.int32)
    n_used = (pad_end[-1] // tm).astype(jnp.int32)
    blk_e = jnp.where(jnp.arange(n_blocks) < n_used, blk_e, blk_e[n_used - 1])
    e_row = jnp.repeat(blk_e, tm)
    in_expert = jnp.arange(n_blocks * tm) - pad_start[e_row]
    src_asg = order[jnp.clip(start[e_row] + in_expert, 0, n_asg - 1)]
    src_tok = jnp.where(in_expert < counts[e_row], src_asg // TOP_K, 0).astype(jnp.int32)
    yd = _ffn_grouped(x[src_tok], blk_e, n_used.reshape(1), g, w1, w3, w2, tm)
    pos, gate = pos.reshape(n, TOP_K), gate.reshape(n, TOP_K)
    return x + sum(yd[pos[:, k]] * gate[:, k:k + 1] for k in range(TOP_K))


N_CHUNK = 128
CHUNK_W = CMP_STRIDE * HEAD_DIM


def _compress_kernel(x_ref, pe_ref, w1_ref, w2_ref, o_ref):
    c = x_ref[0]
    lo = jnp.dot((c + pe_ref[0:1]).astype(BF16), w1_ref[0].astype(BF16), preferred_element_type=F32)
    hi = jnp.dot((c + pe_ref[1:2]).astype(BF16), w1_ref[1].astype(BF16), preferred_element_type=F32)
    hid = jax.nn.gelu(lo + pltpu.roll(hi, N_CHUNK - 1, 0))
    out = jnp.dot(hid.astype(BF16), w2_ref[...].astype(BF16), preferred_element_type=F32)
    row = lax.broadcasted_iota(jnp.int32, out.shape, 0)
    o_ref[0] = jnp.where(row < N_CHUNK - 1, out, 0.0)


def _compress(xc, pe, w1, w2):
    nb = xc.shape[0]
    hidden = w1.shape[-1]
    return pl.pallas_call(
        _compress_kernel,
        grid=(nb,),
        in_specs=[pl.BlockSpec((1, N_CHUNK, CHUNK_W), lambda i: (i, 0, 0)),
                  pl.BlockSpec((2, CHUNK_W), lambda i: (0, 0)),
                  pl.BlockSpec((2, CHUNK_W, hidden), lambda i: (0, 0, 0)),
                  pl.BlockSpec((hidden, HEAD_DIM), lambda i: (0, 0))],
        out_specs=pl.BlockSpec((1, N_CHUNK, HEAD_DIM), lambda i: (i, 0, 0)),
        out_shape=jax.ShapeDtypeStruct((nb, N_CHUNK, HEAD_DIM), F32),
        compiler_params=pltpu.CompilerParams(dimension_semantics=("arbitrary",), vmem_limit_bytes=VMEM_LIMIT_V7X),
        name="compress",
    )(xc, pe.reshape(2, CHUNK_W), w1.reshape(2, CHUNK_W, hidden), w2)


def _compress_heads(kv, pe, w1, w2):
    b = kv.shape[0]
    xc = kv.reshape(b, N_CHUNK, CMP_STRIDE, NSA_KV_HEADS, HEAD_DIM).transpose(0, 3, 1, 2, 4)
    out = _compress(xc.reshape(b * NSA_KV_HEADS, N_CHUNK, CHUNK_W), pe, w1, w2)
    return out.reshape(b, NSA_KV_HEADS, N_CHUNK, HEAD_DIM)


TQ = 128
TK = 128
NEG = -1e30
N_BIAS_TILES = WINDOW // TK + 1
MASKED_TILE = N_BIAS_TILES
SEL_SUB = 4


def _nsa_prompt_kernel(q_ref, kct_ref, vcc_ref, kst_ref, vs_ref, kwt_ref, vw_ref, bcmp_ref, btile_ref, gate_ref,
                       ov_ref, exp_ref, o_ref, selneg_ref, m_ref, l_ref, acc_ref):
    i = pl.program_id(2)
    rows = NSA_GROUP * TQ
    qt = (q_ref[...] * (HEAD_DIM ** -0.5)).astype(BF16)
    q = jnp.concatenate([qt[:, a * HEAD_DIM:(a + 1) * HEAD_DIM] for a in range(NSA_GROUP)], axis=0)

    s = jnp.dot(q, kct_ref[0, 0], preferred_element_type=F32) + bcmp_ref[0].reshape(rows, N_CHUNK)
    m = jnp.max(s, axis=-1, keepdims=True)
    e = jnp.where(s > 0.1 * NEG, jnp.exp(s - m), 0.0)
    p = e / jnp.maximum(jnp.sum(e, axis=-1, keepdims=True), 1e-30)
    o_cmp = jnp.dot(p.astype(BF16), vcc_ref[0, 0], preferred_element_type=F32)

    p_sum = p[0:TQ] + p[TQ:2 * TQ] + p[2 * TQ:3 * TQ] + p[3 * TQ:4 * TQ]
    n_sel = T_PROMPT // SEL_BLOCK
    imp_t = sum(lax.dot_general(ov_ref[...], part, (((1,), (1,)), ((), ())), preferred_element_type=F32)
                for part in _split3(p_sum))[0:n_sel]
    blk = lax.broadcasted_iota(jnp.int32, (n_sel, TQ), 0)
    cur = (lax.broadcasted_iota(jnp.int32, (n_sel, TQ), 1) + i * TQ) // SEL_BLOCK
    valid = blk <= cur
    forced = (blk == 0) | (blk == cur) | (blk == cur - 1)
    score = jnp.where(valid, imp_t + jnp.where(forced, SEL_FORCE, 0.0), -1.0)
    rank = jnp.zeros((n_sel, TQ), F32)
    for c in range(n_sel):
        sc = score[c:c + 1, :]
        beats = (sc > score) | ((sc == score) & (blk > c))
        rank = rank + jnp.where(beats, 1.0, 0.0)
    sel_t = jnp.where(valid & (rank < SEL_TOP_N), 1.0, 0.0)
    sel = jnp.concatenate([sel_t, jnp.zeros((LANE - n_sel, TQ), F32)], axis=0).T.astype(BF16)
    sel_keys = jnp.dot(sel, exp_ref[...], preferred_element_type=F32)
    selneg_ref[...] = (sel_keys - 1.0) * (-NEG)

    halves = [slice(hf * NSA_GROUP // 2, (hf + 1) * NSA_GROUP // 2) for hf in range(2)]
    half_rows = [slice(gs.start * TQ, gs.stop * TQ) for gs in halves]

    def masked_scores(kt_ref, first_tile, n_sub, selected, gs, rs):
        k0 = pl.multiple_of(first_tile * TK, TK)
        s_all = jnp.dot(q[rs], kt_ref[0, 0, :, pl.ds(k0, n_sub * TK)], preferred_element_type=F32)
        pieces = []
        for u in range(n_sub):
            d0 = i - (first_tile + u)
            idx = jnp.where(d0 < 0, MASKED_TILE, jnp.minimum(d0, 2) if selected else d0)
            piece = s_all[:, u * TK:(u + 1) * TK].reshape(gs.stop - gs.start, TQ, TK) + btile_ref[0, idx, gs]
            if selected:
                piece = piece + selneg_ref[:, pl.ds(pl.multiple_of(k0 + u * TK, TK), TK)][None]
            pieces.append(piece.reshape(rs.stop - rs.start, TK))
        return pieces, k0

    def row_max(pieces):
        return jnp.max(functools.reduce(jnp.maximum, pieces), axis=-1, keepdims=True)

    def row_sum(pieces):
        return jnp.sum(functools.reduce(jnp.add, pieces), axis=-1, keepdims=True)

    m_ref[...] = jnp.full(m_ref.shape, -jnp.inf, F32)
    l_ref[...] = jnp.zeros(l_ref.shape, F32)
    acc_ref[...] = jnp.zeros(acc_ref.shape, F32)

    def sel_group(gi, carry):
        for gs, rs in zip(halves, half_rows):
            pieces, k0 = masked_scores(kst_ref, gi * SEL_SUB, SEL_SUB, True, gs, rs)
            m_old = m_ref[rs, :]
            m_new = jnp.maximum(m_old, row_max(pieces))
            pt = [jnp.exp(piece - m_new) for piece in pieces]
            alpha = jnp.exp(m_old - m_new)
            l_ref[rs, :] = alpha * l_ref[rs, :] + row_sum(pt)
            pv = jnp.dot(jnp.concatenate([x.astype(BF16) for x in pt], axis=1),
                         vs_ref[0, 0, pl.ds(k0, SEL_SUB * TK), :], preferred_element_type=F32)
            acc_ref[rs, :] = alpha * acc_ref[rs, :] + pv
            m_ref[rs, :] = m_new
        return carry

    lax.fori_loop(0, i // SEL_SUB + 1, sel_group, 0)
    o_sel = acc_ref[...] / l_ref[...]

    o_win = []
    for gs, rs in zip(halves, half_rows):
        pieces, k0 = masked_scores(kwt_ref, jnp.maximum(i - (N_BIAS_TILES - 1), 0), N_BIAS_TILES, False, gs, rs)
        m_win = row_max(pieces)
        pt = [jnp.exp(piece - m_win) for piece in pieces]
        o_win.append(jnp.dot(jnp.concatenate([x.astype(BF16) for x in pt], axis=1),
                             vw_ref[0, 0, pl.ds(k0, N_BIAS_TILES * TK), :], preferred_element_type=F32) / row_sum(pt))
    o_win = jnp.concatenate(o_win, axis=0)

    g = jax.nn.sigmoid(gate_ref[0, 0])
    pieces = []
    for a in range(NSA_GROUP):
        r = slice(a * TQ, (a + 1) * TQ)
        pieces.append(g[:, 3 * a:3 * a + 1] * o_cmp[r] + g[:, 3 * a + 1:3 * a + 2] * o_sel[r]
                      + g[:, 3 * a + 2:3 * a + 3] * o_win[r])
    o_ref[...] = jnp.concatenate(pieces, axis=1)


T_PROMPT = 2048


def _bias_lookup(table, dist):
    oh = jax.nn.one_hot(rel_bucket(dist), REL_BUCKETS, dtype=F32)
    return jnp.einsum('...r,rh->...h', oh, table, precision=lax.Precision.HIGHEST)


def _prompt_bias_tables(table):
    t = T_PROMPT
    q_pos = jnp.arange(t)
    block_end = jnp.arange(N_CHUNK) * CMP_STRIDE + (CMP_BLOCK - 1)
    dist = q_pos[:, None] - block_end[None, :]
    ok = (dist >= 0) & (jnp.arange(N_CHUNK)[None, :] < N_CHUNK - 1)
    bcmp = jnp.where(ok[..., None], _bias_lookup(table, dist), NEG)
    bcmp = bcmp.reshape(t, N_CHUNK, NSA_KV_HEADS, NSA_GROUP).transpose(2, 3, 0, 1)
    d0 = jnp.arange(N_BIAS_TILES)[:, None, None]
    dist = d0 * TK + jnp.arange(TQ)[None, :, None] - jnp.arange(TK)[None, None, :]
    ok = (dist >= 0) & (dist < WINDOW)
    bt = jnp.where(ok[..., None], _bias_lookup(table, dist), NEG)
    bt = bt.reshape(N_BIAS_TILES, TQ, TK, NSA_KV_HEADS, NSA_GROUP).transpose(3, 0, 4, 1, 2)
    bt = jnp.concatenate([bt, jnp.full((NSA_KV_HEADS, 1, NSA_GROUP, TQ, TK), NEG, F32)], axis=1)
    c0 = jnp.arange(N_CHUNK) * CMP_STRIDE
    s0 = jnp.arange(LANE) * SEL_BLOCK
    ov = jnp.clip(jnp.minimum(c0[:, None] + CMP_BLOCK, s0[None, :] + SEL_BLOCK)
                  - jnp.maximum(c0[:, None], s0[None, :]), 0, CMP_BLOCK).astype(F32) / CMP_BLOCK
    ov = jnp.where((jnp.arange(N_CHUNK)[:, None] < N_CHUNK - 1) & (jnp.arange(LANE)[None, :] < t // SEL_BLOCK), ov, 0.0)
    expand = (jnp.arange(LANE)[:, None] == (jnp.arange(t) // SEL_BLOCK)[None, :]).astype(BF16)
    return bcmp, bt, ov.T.astype(BF16), expand


def _nsa_prompt(z_a, kct, vcc, kst, vs, kwt, vw, gates, tables):
    bcmp, bt, ov, expand = tables
    b, _, _, t = kst.shape
    rows = NSA_GROUP * TQ
    n_q = t // TQ
    q_cols = NSA_GROUP * HEAD_DIM
    per_bh = lambda *blk: pl.BlockSpec((1, 1) + blk, lambda bi, h, i: (bi, h) + (0,) * len(blk))
    return pl.pallas_call(
        _nsa_prompt_kernel,
        grid=(b, NSA_KV_HEADS, n_q),
        in_specs=[pl.BlockSpec((TQ, q_cols), lambda bi, h, i: (bi * n_q + i, h)),
                  per_bh(HEAD_DIM, N_CHUNK), per_bh(N_CHUNK, HEAD_DIM),
                  per_bh(HEAD_DIM, t), per_bh(t, HEAD_DIM), per_bh(HEAD_DIM, t), per_bh(t, HEAD_DIM),
                  pl.BlockSpec((1, NSA_GROUP, TQ, N_CHUNK), lambda bi, h, i: (h, 0, i, 0)),
                  pl.BlockSpec((1, N_BIAS_TILES + 1, NSA_GROUP, TQ, TK), lambda bi, h, i: (h, 0, 0, 0, 0)),
                  pl.BlockSpec((1, 1, TQ, 3 * NSA_GROUP), lambda bi, h, i: (bi, h, i, 0)),
                  pl.BlockSpec((N_CHUNK, LANE), lambda bi, h, i: (0, 0)),
                  pl.BlockSpec((LANE, t), lambda bi, h, i: (0, 0))],
        out_specs=pl.BlockSpec((TQ, q_cols), lambda bi, h, i: (bi * n_q + i, h)),
        out_shape=jax.ShapeDtypeStruct((b * t, NSA_WIDTH), F32),
        scratch_shapes=[pltpu.VMEM((TQ, t), F32), pltpu.VMEM((rows, 1), F32), pltpu.VMEM((rows, 1), F32),
                        pltpu.VMEM((rows, HEAD_DIM), F32)],
        compiler_params=pltpu.CompilerParams(dimension_semantics=("arbitrary", "arbitrary", "arbitrary"),
                                             vmem_limit_bytes=VMEM_LIMIT_V7X),
        name="nsa_prompt",
    )(z_a, kct, vcc, kst, vs, kwt, vw, bcmp, bt, gates, ov, expand)


def _heads_t(x, b, t):
    return x.astype(BF16).reshape(b, t, NSA_KV_HEADS, HEAD_DIM).transpose(0, 2, 3, 1)


def _heads(x, b, t):
    return x.astype(BF16).reshape(b, t, NSA_KV_HEADS, HEAD_DIM).transpose(0, 2, 1, 3)


def _nsa_prompt_from_proj(z_a, z_s, b, t, tables, pe, w1, w2):
    col = lambda i: z_a[:, NSA_WIDTH + KV_WIDTH * i:NSA_WIDTH + KV_WIDTH * (i + 1)]
    kcc = _compress_heads(col(0).reshape(b, t, NSA_KV_HEADS, HEAD_DIM), pe[0], w1[0], w2[0])
    vcc = _compress_heads(col(1).reshape(b, t, NSA_KV_HEADS, HEAD_DIM), pe[1], w1[1], w2[1])
    gates = z_s[:, :3 * NSA_HEADS].reshape(b, t, NSA_KV_HEADS, 3 * NSA_GROUP).transpose(0, 2, 1, 3)
    return _nsa_prompt(z_a, kcc.astype(BF16).transpose(0, 1, 3, 2), vcc.astype(BF16),
                       _heads_t(col(2), b, t), _heads(col(3), b, t), _heads_t(col(4), b, t), _heads(col(5), b, t),
                       gates, tables)


T_SAMPLE = 4
T_PAD = 8
PAGE = 128
N_PAGES = 16
PAST = N_PAGES * PAGE
N_SEL_SAMPLE = -(-(PAST + T_SAMPLE) // SEL_BLOCK)


def _nsa_sample_kernel(pt_ref, *refs):
    del pt_ref
    cp, sp = refs[:N_PAGES], refs[N_PAGES:2 * N_PAGES]
    (wb_ref, za_ref, zs_ref, pe2_ref, w1_ref, w2_ref, bcmp_ref, bpast_ref, bwin_ref, bnew_ref, ov_ref, exp_ref,
     o_ref, wout_ref, new_ref, q_ref, g_ref, xcat_ref, kcat_ref, vcat_ref, s_ref, e_ref, oacc_ref,
     rowpg_ref) = refs[2 * N_PAGES:]
    rows = NSA_GROUP * T_PAD

    @pl.when(pl.program_id(0) == 0)
    def _():
        new_ref[...] = jnp.zeros(new_ref.shape, F32)
        q_ref[...] = jnp.zeros(q_ref.shape, F32)
        g_ref[...] = jnp.zeros(g_ref.shape, F32)

    new_ref[0:T_SAMPLE, :] = za_ref[0, :, NSA_WIDTH + 2 * KV_WIDTH:]
    q_ref[0:T_SAMPLE, :] = za_ref[0, :, :NSA_WIDTH] * (HEAD_DIM ** -0.5)
    g_ref[0:T_SAMPLE, :] = zs_ref[0]

    for p in range(N_PAGES):
        for kv in range(2):
            rowpg_ref[kv, p * PAGE:(p + 1) * PAGE, :] = cp[p][0, kv * KV_WIDTH:(kv + 1) * KV_WIDTH, :].T
    comp = []
    for kv in range(2):
        def place_row_offset(l, carry, kv=kv):
            xl = rowpg_ref[kv, pl.ds(l, N_CHUNK, stride=CMP_STRIDE), :]
            lanes = pl.ds(pl.multiple_of(l * KV_WIDTH, KV_WIDTH), KV_WIDTH)
            for half in range(2):
                xcat_ref[half, :, lanes] = (xl + pe2_ref[kv, pl.ds(l + CMP_STRIDE * half, 1), :]).astype(BF16)
            return carry

        lax.fori_loop(0, CMP_STRIDE, place_row_offset, 0)
        lo = jnp.dot(xcat_ref[0], w1_ref[kv, 0], preferred_element_type=F32)
        hi = jnp.dot(xcat_ref[1], w1_ref[kv, 1], preferred_element_type=F32)
        hid = jax.nn.gelu(lo + pltpu.roll(hi, N_CHUNK - 1, 0))
        out = jnp.dot(hid.astype(BF16), w2_ref[kv], preferred_element_type=F32)
        row = lax.broadcasted_iota(jnp.int32, out.shape, 0)
        comp.append(jnp.where(row < N_CHUNK - 1, out, 0.0).astype(BF16))

    for p in range(N_PAGES):
        kcat_ref[:, p * PAGE:(p + 1) * PAGE] = sp[p][0, 0:KV_WIDTH, :].astype(BF16)
        vcat_ref[:, p * PAGE:(p + 1) * PAGE] = sp[p][0, KV_WIDTH:, :].astype(BF16)
    kw_buf = wb_ref[0, 0:KV_WIDTH, :].astype(BF16)
    vw_buf = wb_ref[0, KV_WIDTH:, :].astype(BF16)
    ks_new, vs_new, kw_new, vw_new = (new_ref[:, i * KV_WIDTH:(i + 1) * KV_WIDTH].astype(BF16) for i in range(4))
    gates = jax.nn.sigmoid(g_ref[...])
    nt = (((1,), (1,)), ((), ()))
    zeros64 = jnp.zeros((rows, HEAD_DIM), F32)
    pieces = []
    key_chunk = 512

    all_rows = NSA_KV_HEADS * rows

    def attend_window(qp):
        s_parts = [jnp.dot(qp, kw_buf, preferred_element_type=F32) + bwin_ref[...].reshape(all_rows, -1),
                   lax.dot_general(qp, kw_new, nt, preferred_element_type=F32) + bnew_ref[...].reshape(all_rows, LANE)]
        m = jnp.maximum(*[jnp.max(s, axis=-1, keepdims=True) for s in s_parts])
        e_parts = [jnp.exp(s - m) for s in s_parts]
        den = jnp.add(*[jnp.sum(e, axis=-1, keepdims=True) for e in e_parts])
        num = (lax.dot_general(e_parts[0].astype(BF16), vw_buf, nt, preferred_element_type=F32)
               + jnp.dot(e_parts[1].astype(BF16), vw_new, preferred_element_type=F32))
        return num / den

    def attend_selected(qp):
        def scores(c, carry):
            ds = pl.ds(pl.multiple_of(c * key_chunk, key_chunk), key_chunk)
            s_ref[:, ds] += jnp.dot(qp, kcat_ref[:, ds], preferred_element_type=F32)
            return carry

        lax.fori_loop(0, PAST // key_chunk, scores, 0)
        s_ref[:, PAST:] += lax.dot_general(qp, ks_new, nt, preferred_element_type=F32)
        s = s_ref[...]
        e = jnp.exp(s - jnp.max(s, axis=-1, keepdims=True))
        den = jnp.sum(e, axis=-1, keepdims=True)
        e_ref[...] = e.astype(BF16)
        oacc_ref[...] = jnp.dot(e_ref[:, PAST:], vs_new, preferred_element_type=F32)

        def weighted(c, carry):
            ds = pl.ds(pl.multiple_of(c * key_chunk, key_chunk), key_chunk)
            oacc_ref[...] += lax.dot_general(e_ref[:, ds], vcat_ref[:, ds], nt, preferred_element_type=F32)
            return carry

        lax.fori_loop(0, PAST // key_chunk, weighted, 0)
        return oacc_ref[...] / den

    qp = []
    for h in range(NSA_KV_HEADS):
        q64 = jnp.concatenate([q_ref[:, (h * NSA_GROUP + a) * HEAD_DIM:(h * NSA_GROUP + a + 1) * HEAD_DIM]
                               for a in range(NSA_GROUP)], axis=0)
        qp.append(jnp.concatenate([q64, zeros64] if h == 0 else [zeros64, q64], axis=1))
    qp = jnp.concatenate(qp, axis=0).astype(BF16)

    s = lax.dot_general(qp, comp[0], nt, preferred_element_type=F32) + bcmp_ref[...].reshape(all_rows, N_CHUNK)
    m = jnp.max(s, axis=-1, keepdims=True)
    e = jnp.where(s > 0.1 * NEG, jnp.exp(s - m), 0.0)
    p = e / jnp.maximum(jnp.sum(e, axis=-1, keepdims=True), 1e-30)
    o_cmp = jnp.dot(p.astype(BF16), comp[1], preferred_element_type=F32)

    sel_rows = NSA_KV_HEADS * T_PAD
    p_sum = jnp.concatenate([sum(p[h * rows + a * T_PAD:h * rows + (a + 1) * T_PAD] for a in range(NSA_GROUP))
                             for h in range(NSA_KV_HEADS)], axis=0)
    imp = sum(jnp.dot(part, ov_ref[...], preferred_element_type=F32) for part in _split3(p_sum))
    lane = lax.broadcasted_iota(jnp.int32, (sel_rows, LANE), 1)
    tok = lax.broadcasted_iota(jnp.int32, (sel_rows, LANE), 0) % T_PAD
    cur = (PAST + jnp.minimum(tok, T_SAMPLE - 1)) // SEL_BLOCK
    valid = lane <= cur
    forced = (lane == 0) | (lane == cur) | (lane == cur - 1)
    score = jnp.where(valid, imp + jnp.where(forced, SEL_FORCE, 0.0), -1.0)
    rank = jnp.zeros((sel_rows, LANE), F32)
    for c in range(N_SEL_SAMPLE):
        sc = score[:, c:c + 1]
        beats = (sc > score) | ((sc == score) & (lane > c))
        rank = rank + jnp.where(beats, 1.0, 0.0)
    sel = jnp.where(valid & (rank < SEL_TOP_N), 1.0, 0.0)
    selneg = (jnp.dot(sel.astype(BF16), exp_ref[...], preferred_element_type=F32) - 1.0) * (-NEG)
    selneg_new = (sel[:, PAST // SEL_BLOCK:PAST // SEL_BLOCK + 1] - 1.0) * (-NEG)

    def per_token(bias, tok_term):
        n = bias.shape[-1]
        return (bias.reshape(NSA_KV_HEADS, NSA_GROUP, T_PAD, n)
                + tok_term.reshape(NSA_KV_HEADS, 1, T_PAD, n)).reshape(all_rows, n)

    s_ref[:, :PAST] = per_token(bpast_ref[...], selneg)
    s_ref[:, PAST:] = per_token(bnew_ref[...], jnp.broadcast_to(selneg_new, (sel_rows, LANE)))
    o_sel = attend_selected(qp)
    o_win = attend_window(qp)

    for h in range(NSA_KV_HEADS):
        lanes = slice(h * HEAD_DIM, (h + 1) * HEAD_DIM)
        for a in range(NSA_GROUP):
            r = slice(h * rows + a * T_PAD, h * rows + (a + 1) * T_PAD)
            c0 = (h * NSA_GROUP + a) * 3
            pieces.append(gates[:, c0:c0 + 1] * o_cmp[r, lanes] + gates[:, c0 + 1:c0 + 2] * o_sel[r, lanes]
                          + gates[:, c0 + 2:c0 + 3] * o_win[r, lanes])
    o_ref[0] = jnp.concatenate(pieces, axis=1)[0:T_SAMPLE]

    w_buf = wb_ref.shape[-1]
    shifted = pltpu.roll(wb_ref[0], w_buf - T_SAMPLE, 1)
    new_t = jnp.concatenate([new_ref[:, 2 * KV_WIDTH:3 * KV_WIDTH].T, new_ref[:, 3 * KV_WIDTH:].T], axis=0)
    new_t = pltpu.roll(new_t, LANE - T_SAMPLE, 1)
    lane = lax.broadcasted_iota(jnp.int32, new_t.shape, 1)
    tail = jnp.where(lane >= LANE - T_SAMPLE, new_t, shifted[:, w_buf - LANE:])
    wout_ref[0] = jnp.concatenate([shifted[:, :w_buf - LANE], tail], axis=1)


def _sample_bias_tables(table):
    tq = jnp.minimum(jnp.arange(T_PAD), T_SAMPLE - 1)
    q_pos = PAST + tq

    def lay(x):
        n = x.shape[1]
        return x.reshape(T_PAD, n, NSA_KV_HEADS, NSA_GROUP).transpose(2, 3, 0, 1).reshape(NSA_KV_HEADS, -1, n)

    block_end = jnp.arange(N_CHUNK) * CMP_STRIDE + (CMP_BLOCK - 1)
    dist = q_pos[:, None] - block_end[None, :]
    ok = (dist >= 0) & (jnp.arange(N_CHUNK)[None, :] < N_CHUNK - 1)
    bcmp = lay(jnp.where(ok[..., None], _bias_lookup(table, dist), NEG))
    dist = q_pos[:, None] - jnp.arange(PAST)[None, :]
    past = _bias_lookup(table, dist)
    bpast = lay(past)
    w_buf = min(WINDOW, PAST)
    bwin = lay(jnp.where((dist < WINDOW)[:, PAST - w_buf:, None], past[:, PAST - w_buf:], NEG))
    j = jnp.arange(LANE)
    dist = tq[:, None] - j[None, :]
    ok = (dist >= 0) & (j[None, :] < T_SAMPLE)
    bnew = lay(jnp.where(ok[..., None], _bias_lookup(table, dist), NEG))
    c0 = jnp.arange(N_CHUNK) * CMP_STRIDE
    s0 = jnp.arange(LANE) * SEL_BLOCK
    ov = jnp.clip(jnp.minimum(c0[:, None] + CMP_BLOCK, s0[None, :] + SEL_BLOCK)
                  - jnp.maximum(c0[:, None], s0[None, :]), 0, CMP_BLOCK).astype(F32) / CMP_BLOCK
    ov = jnp.where((jnp.arange(N_CHUNK)[:, None] < N_CHUNK - 1) & (j[None, :] < N_SEL_SAMPLE), ov, 0.0)
    expand = (j[:, None] == (jnp.arange(PAST) // SEL_BLOCK)[None, :]).astype(BF16)
    return bcmp, bpast, bwin, bnew, ov.astype(BF16), expand


def _nsa_sample(za, zs, cache_cmp, cache_slc, cache_win, page_table, layer, tables, pe, w1, w2):
    b = za.shape[0]
    n_phys = cache_cmp.shape[1]
    cmp_pages = jnp.transpose(cache_cmp, (0, 1, 3, 4, 5, 2)).reshape(DEPTH * n_phys, 2 * KV_WIDTH, PAGE)
    slc_pages = jnp.transpose(cache_slc, (0, 1, 3, 4, 5, 2)).reshape(DEPTH * n_phys, 2 * KV_WIDTH, PAGE)
    pt = page_table.reshape(-1).astype(jnp.int32) + layer * n_phys
    w_buf = cache_win.shape[2]
    win = jnp.transpose(cache_win, (0, 1, 3, 4, 5, 2)).reshape(DEPTH * b, 2 * KV_WIDTH, w_buf)
    bcmp, bpast, bwin, bnew, ov, expand = tables
    pe2 = jnp.concatenate([pe, pe], axis=-1)
    page_spec = lambda p: pl.BlockSpec((1, 2 * KV_WIDTH, PAGE), lambda bi, pt_: (pt_[bi * N_PAGES + p], 0, 0))
    whole = lambda x: pl.BlockSpec(x.shape, lambda bi, pt_: (0,) * x.ndim)
    eye = jnp.eye(NSA_KV_HEADS, dtype=F32)
    w1h = w1.reshape(2, 2, CMP_STRIDE, HEAD_DIM, CMP_HIDDEN)
    w1_bd = jnp.einsum('khlde,ab->khladbe', w1h, eye).reshape(2, 2, CMP_STRIDE * KV_WIDTH, NSA_KV_HEADS * CMP_HIDDEN)
    w2_bd = jnp.einsum('ked,ab->kaebd', w2, eye).reshape(2, NSA_KV_HEADS * CMP_HIDDEN, KV_WIDTH)
    consts = [pe2, w1_bd.astype(BF16), w2_bd.astype(BF16), bcmp, bpast, bwin, bnew, ov, expand]
    grid_spec = pltpu.PrefetchScalarGridSpec(
        num_scalar_prefetch=1,
        grid=(b,),
        in_specs=[page_spec(p) for p in range(N_PAGES)] * 2
        + [pl.BlockSpec((1, 2 * KV_WIDTH, w_buf), lambda bi, pt_: (layer * b + bi, 0, 0)),
           pl.BlockSpec((1, T_SAMPLE, za.shape[-1]), lambda bi, pt_: (bi, 0, 0)),
           pl.BlockSpec((1, T_SAMPLE, LANE), lambda bi, pt_: (bi, 0, 0))]
        + [whole(x) for x in consts],
        out_specs=[pl.BlockSpec((1, T_SAMPLE, NSA_WIDTH), lambda bi, pt_: (bi, 0, 0)),
                   pl.BlockSpec((1, 2 * KV_WIDTH, w_buf), lambda bi, pt_: (bi, 0, 0))],
        scratch_shapes=[pltpu.VMEM((PAGE, 4 * KV_WIDTH), F32), pltpu.VMEM((T_PAD, NSA_WIDTH), F32),
                        pltpu.VMEM((T_PAD, LANE), F32), pltpu.VMEM((2, N_CHUNK, CMP_STRIDE * KV_WIDTH), BF16),
                        pltpu.VMEM((KV_WIDTH, PAST), BF16), pltpu.VMEM((KV_WIDTH, PAST), BF16),
                        pltpu.VMEM((NSA_HEADS * T_PAD, PAST + PAGE), F32),
                        pltpu.VMEM((NSA_HEADS * T_PAD, PAST + PAGE), BF16),
                        pltpu.VMEM((NSA_HEADS * T_PAD, KV_WIDTH), F32), pltpu.VMEM((2, PAST, KV_WIDTH), F32)],
    )
    return pl.pallas_call(
        _nsa_sample_kernel,
        grid_spec=grid_spec,
        out_shape=[jax.ShapeDtypeStruct((b, T_SAMPLE, NSA_WIDTH), F32),
                   jax.ShapeDtypeStruct((b, 2 * KV_WIDTH, w_buf), F32)],
        compiler_params=pltpu.CompilerParams(dimension_semantics=("arbitrary",), vmem_limit_bytes=VMEM_LIMIT_V7X),
        name="nsa_sample",
    )(pt, *([cmp_pages] * N_PAGES), *([slc_pages] * N_PAGES), win, za, zs, *consts)


I_LANE = 3 * NSA_HEADS
F_LANE = I_LANE + M_HEADS
EXT_PAD = 8


def _mlstm_kernel(zu_ref, zs_ref, c0_ref, n0_ref, m0_ref, cprev_ref, convw_ref, convb_ref, wq_ref, wk_ref, wkt_ref,
                  wv_ref, gb_ref, normg_ref, tril_ref, selrow_ref,
                  o_ref, c_out, n_out, m_out, conv_out,
                  ext_ref, zsp_ref, c_sc, n_sc, m_sc, *, rows, rows_pad, n_valid):
    i = pl.program_id(1)
    L = M_CHUNK

    @pl.when(i == 0)
    def _():
        c_sc[...] = c0_ref[0]
        n_sc[...] = n0_ref[0]
        m_sc[...] = m0_ref[0]
        ext_ref[EXT_PAD - (CONV_W - 1):EXT_PAD, :] = cprev_ref[0]

    if rows < rows_pad:
        ext_ref[EXT_PAD:, :] = jnp.zeros((rows_pad, M_WIDTH), F32)
        zsp_ref[...] = jnp.zeros(zsp_ref.shape, F32)
    ext_ref[EXT_PAD:EXT_PAD + rows, :] = zu_ref[0, :, :M_WIDTH]
    zsp_ref[0:rows, :] = zs_ref[0]

    u = ext_ref[EXT_PAD:EXT_PAD + rows_pad, :]
    conv = convb_ref[...] + convw_ref[CONV_W - 1:CONV_W, :] * u
    for j in range(CONV_W - 1):
        lo = EXT_PAD - (CONV_W - 1) + j
        conv = conv + convw_ref[j:j + 1, :] * ext_ref[lo:lo + rows_pad, :]
    uc = (conv * jax.nn.sigmoid(conv)).astype(BF16)
    ub = u.astype(BF16)

    zb = zsp_ref[...] + gb_ref[...]
    bcum = sum(jnp.dot(tril_ref[...], part, preferred_element_type=F32) for part in _split3(jax.nn.log_sigmoid(zb)))
    bcum = pltpu.roll(bcum, LANE - M_HEADS, 1)
    a_col = zb - bcum
    a_row = sum(lax.dot_general(selrow_ref[...], part, (((1,), (1,)), ((), ())), preferred_element_type=F32)
                for part in _split3(a_col))

    tt = lax.broadcasted_iota(jnp.int32, (L, L), 0)
    ss = lax.broadcasted_iota(jnp.int32, (L, L), 1)
    causal = ss <= tt
    tok_col = lax.broadcasted_iota(jnp.int32, (L, 1), 0)
    tok_row = lax.broadcasted_iota(jnp.int32, (1, L), 1)

    for h in range(M_HEADS):
        hs = slice(h * M_HEAD_DIM, (h + 1) * M_HEAD_DIM)
        q_all = jnp.dot(uc[:, hs], wq_ref[h].astype(BF16), preferred_element_type=F32) * (M_HEAD_DIM ** -0.5)
        k_all = jnp.dot(uc[:, hs], wk_ref[h].astype(BF16), preferred_element_type=F32)
        kt_all = lax.dot_general(wkt_ref[h].astype(BF16), uc[:, hs], (((1,), (1,)), ((), ())),
                                 preferred_element_type=F32)
        v_all = jnp.dot(ub[:, hs], wv_ref[h].astype(BF16), preferred_element_type=F32)
        c_st = c_sc[h]
        n_st = n_sc[h:h + 1, :]
        m_st = m_sc[h:h + 1, 0:1]
        for c in range(rows_pad // L):
            r = slice(c * L, (c + 1) * L)
            q, k, kt, v = q_all[r].astype(BF16), k_all[r], kt_all[:, r], v_all[r].astype(BF16)
            b_col = bcum[r, I_LANE + h:I_LANE + h + 1]
            a_c = a_col[r, I_LANE + h:I_LANE + h + 1]
            a_r = a_row[h:h + 1, r]
            log_d = jnp.where(causal, b_col + a_r, NEG)
            m_col = jnp.maximum(b_col + m_st, jnp.max(log_d, axis=-1, keepdims=True))
            dw = jnp.exp(log_d - m_col)
            w_inter = jnp.exp(b_col + m_st - m_col)
            s = lax.dot_general(q, k.astype(BF16), (((1,), (1,)), ((), ())), preferred_element_type=F32) * dw
            num = (jnp.dot(s.astype(BF16), v, preferred_element_type=F32)
                   + w_inter * jnp.dot(q, c_st.astype(BF16), preferred_element_type=F32))
            den = (jnp.sum(s, axis=-1, keepdims=True)
                   + w_inter * jnp.sum(q_all[r] * n_st, axis=-1, keepdims=True))
            hh = num / jnp.maximum(jnp.abs(den), jnp.exp(-m_col))
            hn = hh * lax.rsqrt(jnp.mean(hh * hh, axis=-1, keepdims=True) + EPS)
            o_pre = zu_ref[0, :, M_WIDTH + h * M_HEAD_DIM:M_WIDTH + (h + 1) * M_HEAD_DIM]
            if rows < rows_pad:
                o_ref[0, :, hs] = jax.nn.sigmoid(o_pre) * hn[0:rows] * normg_ref[:, hs]
            else:
                o_ref[0, r, hs] = jax.nn.sigmoid(o_pre[r]) * hn * normg_ref[:, hs]
            b_last = b_col[n_valid - 1:n_valid, :]
            log_s = jnp.where(tok_col < n_valid, b_last + a_c, NEG)
            m_new = jnp.maximum(b_last + m_st, jnp.max(log_s, axis=0, keepdims=True))
            ws_col = jnp.exp(log_s - m_new)
            ws_row = jnp.where(tok_row < n_valid, jnp.exp(b_last + a_r - m_new), 0.0)
            wc = jnp.exp(b_last + m_st - m_new)
            c_st = wc * c_st + jnp.dot((kt * ws_row).astype(BF16), v, preferred_element_type=F32)
            n_st = wc * n_st + jnp.sum(k * ws_col, axis=0, keepdims=True)
            m_st = m_new
        c_sc[h] = c_st
        n_sc[h:h + 1, :] = n_st
        m_sc[h:h + 1, :] = jnp.broadcast_to(m_st, (1, LANE))

    tail = ext_ref[EXT_PAD + rows - (CONV_W - 1):EXT_PAD + rows, :]
    ext_ref[EXT_PAD - (CONV_W - 1):EXT_PAD, :] = tail

    @pl.when(i == pl.num_programs(1) - 1)
    def _():
        c_out[0] = c_sc[...]
        n_out[0] = n_sc[...]
        m_out[0] = m_sc[...]
        conv_out[0] = jnp.zeros(conv_out.shape[1:], F32)
        conv_out[0, 0:CONV_W - 1, :] = tail


def _mlstm(zu, zs, state, conv_prev, conv_w, conv_b, m_qkv, gate_b, norm_g):
    b, t, _ = zu.shape
    rows = min(t, 4 * M_CHUNK)
    rows_pad = -(-rows // M_CHUNK) * M_CHUNK
    n_valid = M_CHUNK if rows == rows_pad else rows
    assert t % rows == 0 and (rows == rows_pad or t == rows)
    c0, n0, m0 = state
    n0p = jnp.pad(n0, ((0, 0), (0, 8 - M_HEADS), (0, 0)))
    m0p = jnp.pad(jnp.broadcast_to(m0[:, :, None], (b, M_HEADS, LANE)), ((0, 0), (0, 8 - M_HEADS), (0, 0)))
    gb = jnp.zeros((1, LANE), F32).at[0, I_LANE:I_LANE + 2 * M_HEADS].set(gate_b.reshape(-1))
    idx = jnp.arange(rows_pad)
    tril = ((idx[:, None] >= idx[None, :]) & (idx[:, None] // M_CHUNK == idx[None, :] // M_CHUNK)).astype(BF16)
    selrow = ((jnp.arange(16)[:, None] + I_LANE == jnp.arange(LANE)[None, :])
              & (jnp.arange(16)[:, None] < M_HEADS)).astype(BF16)
    whole = lambda *shape: pl.BlockSpec(shape, lambda bi, i: (0,) * len(shape))
    per_b = lambda *shape: pl.BlockSpec((1,) + shape, lambda bi, i: (bi,) + (0,) * len(shape))
    out, c_new, n_new, m_new, conv_new = pl.pallas_call(
        functools.partial(_mlstm_kernel, rows=rows, rows_pad=rows_pad, n_valid=n_valid),
        grid=(b, t // rows),
        in_specs=[pl.BlockSpec((1, rows, 2 * M_WIDTH), lambda bi, i: (bi, i, 0)),
                  pl.BlockSpec((1, rows, LANE), lambda bi, i: (bi, i, 0)),
                  per_b(M_HEADS, M_HEAD_DIM, M_HEAD_DIM), per_b(8, M_HEAD_DIM), per_b(8, LANE),
                  per_b(CONV_W - 1, M_WIDTH),
                  whole(CONV_W, M_WIDTH), whole(1, M_WIDTH),
                  whole(M_HEADS, M_HEAD_DIM, M_HEAD_DIM), whole(M_HEADS, M_HEAD_DIM, M_HEAD_DIM),
                  whole(M_HEADS, M_HEAD_DIM, M_HEAD_DIM), whole(M_HEADS, M_HEAD_DIM, M_HEAD_DIM),
                  whole(1, LANE), whole(1, M_WIDTH), whole(rows_pad, rows_pad), whole(16, LANE)],
        out_specs=[pl.BlockSpec((1, rows, M_WIDTH), lambda bi, i: (bi, i, 0)),
                   per_b(M_HEADS, M_HEAD_DIM, M_HEAD_DIM), per_b(8, M_HEAD_DIM), per_b(8, LANE),
                   per_b(8, M_WIDTH)],
        out_shape=[jax.ShapeDtypeStruct((b, t, M_WIDTH), F32),
                   jax.ShapeDtypeStruct((b, M_HEADS, M_HEAD_DIM, M_HEAD_DIM), F32),
                   jax.ShapeDtypeStruct((b, 8, M_HEAD_DIM), F32),
                   jax.ShapeDtypeStruct((b, 8, LANE), F32),
                   jax.ShapeDtypeStruct((b, 8, M_WIDTH), F32)],
        scratch_shapes=[pltpu.VMEM((EXT_PAD + rows_pad, M_WIDTH), F32), pltpu.VMEM((rows_pad, LANE), F32),
                        pltpu.VMEM((M_HEADS, M_HEAD_DIM, M_HEAD_DIM), F32), pltpu.VMEM((8, M_HEAD_DIM), F32),
                        pltpu.VMEM((8, LANE), F32)],
        compiler_params=pltpu.CompilerParams(dimension_semantics=("arbitrary", "arbitrary"),
                                             vmem_limit_bytes=VMEM_LIMIT_V7X),
        name="mlstm",
    )(zu, zs, c0, n0p, m0p, conv_prev, conv_w, conv_b.reshape(1, M_WIDTH), m_qkv[0], m_qkv[1],
      m_qkv[1].transpose(0, 2, 1),
      m_qkv[2], gb, norm_g.reshape(1, M_WIDTH), tril, selrow)
    return out, (c_new, n_new[:, :M_HEADS], m_new[:, :M_HEADS, 0], conv_new[:, :CONV_W - 1])


def rel_bucket(dist):
    d = jnp.maximum(dist, 0)
    exact = REL_BUCKETS // 2
    log_part = exact + (jnp.log(jnp.maximum(d, 1).astype(F32) / exact)
                        / math.log(REL_MAX_DIST / exact) * (REL_BUCKETS - exact)).astype(jnp.int32)
    return jnp.where(d < exact, d, jnp.minimum(log_part, REL_BUCKETS - 1))


def _split_w_in(w_in):
    offs = np.cumsum((0,) + IN_WIDTHS)
    w_a = w_in[:, :offs[7]]
    w_small = jnp.concatenate([w_in[:, offs[7]:offs[8]], w_in[:, offs[10]:offs[11]]], axis=1)
    w_small = jnp.pad(w_small, ((0, 0), (0, LANE - w_small.shape[1])))
    w_u = w_in[:, offs[8]:offs[10]]
    w_g = w_in[:, offs[11]:]
    return w_a, w_small, w_u, w_g


def mixer(x, norm_g, prm, past):
    b, t, _ = x.shape
    x2 = x.reshape(b * t, D_MODEL)
    z_a, z_s, z_u, z_g = _proj(x2, norm_g, [w.astype(BF16) for w in _split_w_in(prm['w_in'])])
    z_a, z_s, z_u = (z.reshape(b, t, -1) for z in (z_a, z_s, z_u))
    new_cmp, new_slc, win_rows = (z_a[..., NSA_WIDTH + 2 * KV_WIDTH * i:NSA_WIDTH + 2 * KV_WIDTH * (i + 1)]
                                  .reshape(b, t, 2, NSA_KV_HEADS, HEAD_DIM) for i in range(3))
    if past is None:
        o_nsa = _nsa_prompt_from_proj(z_a.reshape(b * t, -1), z_s.reshape(b * t, -1), b, t, prm['prompt_tables'],
                                      prm['cmp_pe'], prm['cmp_w1'], prm['cmp_w2'])
        new_win = win_rows[:, t - min(WINDOW, t):]
        conv_prev = jnp.zeros((b, CONV_W - 1, M_WIDTH), F32)
        m_state = (jnp.zeros((b, M_HEADS, M_HEAD_DIM, M_HEAD_DIM), F32),
                   jnp.zeros((b, M_HEADS, M_HEAD_DIM), F32),
                   jnp.zeros((b, M_HEADS), F32))
    else:
        layer = past['layer']
        assert t == T_SAMPLE and past['page_table'].shape[1] == N_PAGES and past['cmp'].shape[2] == PAGE
        o_nsa, win_t = _nsa_sample(z_a, z_s, past['cmp'], past['slc'], past['win'], past['page_table'], layer,
                                   prm['sample_tables'], prm['cmp_pe'], prm['cmp_w1'], prm['cmp_w2'])
        o_nsa = o_nsa.reshape(b * t, NSA_WIDTH)
        new_win = win_t.reshape(b, 2, NSA_KV_HEADS, HEAD_DIM, -1).transpose(0, 4, 1, 2, 3)
        conv_prev = past['conv']
        m_state = (past['C'].astype(F32), past['n'].astype(F32), past['m'].astype(F32))
    o_m, (c_new, n_new, m_new, conv_new) = _mlstm(z_u, z_s, m_state, conv_prev, prm['conv_w'], prm['conv_b'],
                                                  prm['m_qkv'], prm['gate_b'], prm['m_norm'])
    y = _merge_out(x2, o_nsa, o_m.reshape(b * t, M_WIDTH), z_g, prm['w_up_a'].astype(BF16),
                   prm['w_up_b'].astype(BF16), prm['w_out'].astype(BF16)).reshape(b, t, D_MODEL)
    return y, (new_cmp, new_slc, new_win, c_new, n_new, m_new, conv_new)


def _channel_mixer(x, g, l, ffn_w1, ffn_w3, ffn_w2, moe_router, moe_w1, moe_w3, moe_w2):
    b, t, d = x.shape
    x2 = x.reshape(b * t, d)
    i = l // 2
    if l % 2 == 0:
        y = _ffn_dense(x2, g, ffn_w1[i], ffn_w3[i], ffn_w2[i])
    else:
        y = _moe(x2, g, moe_router[i], moe_w1[i], moe_w3[i], moe_w2[i])
    return y.reshape(b, t, d)


def _final_norm(x, g):
    xf = x.astype(F32)
    return xf * lax.rsqrt(jnp.mean(xf * xf, axis=-1, keepdims=True) + EPS) * g


def kernel(x_prompt, x_sample, cache_cmp_kv, cache_slc_kv, cache_win_kv, state_mlstm_C, state_mlstm_n,
           state_mlstm_m, state_mlstm_conv, page_table, rel_bias_table, norm_mix, norm_ffn, norm_final,
           w_in, cmp_pe, cmp_w1, cmp_w2, m_conv_w, m_conv_b, m_qkv, m_gate_bias, m_norm, w_up_a, w_up_b,
           w_out, ffn_w1, ffn_w3, ffn_w2, moe_router, moe_w1, moe_w3, moe_w2):
    xp, xs = x_prompt, x_sample
    prompt_states, sample_states = [], []
    prompt_tables = _prompt_bias_tables(rel_bias_table)
    sample_tables = _sample_bias_tables(rel_bias_table)
    for l in range(DEPTH):
        prm = {'prompt_tables': prompt_tables, 'sample_tables': sample_tables, 'w_in': w_in[l], 'cmp_pe': cmp_pe[l],
               'cmp_w1': cmp_w1[l], 'cmp_w2': cmp_w2[l], 'conv_w': m_conv_w[l], 'conv_b': m_conv_b[l],
               'm_qkv': m_qkv[l], 'gate_b': m_gate_bias[l], 'm_norm': m_norm[l], 'w_up_a': w_up_a[l],
               'w_up_b': w_up_b[l], 'w_out': w_out[l]}
        past = {'cmp': cache_cmp_kv, 'slc': cache_slc_kv, 'win': cache_win_kv, 'layer': l,
                'C': state_mlstm_C[l], 'n': state_mlstm_n[l], 'm': state_mlstm_m[l],
                'conv': state_mlstm_conv[l], 'page_table': page_table}
        xp, st_p = mixer(xp, norm_mix[l], prm, None)
        xs, st_s = mixer(xs, norm_mix[l], prm, past)
        xp = _channel_mixer(xp, norm_ffn[l], l, ffn_w1, ffn_w3, ffn_w2, moe_router, moe_w1, moe_w3, moe_w2)
        xs = _channel_mixer(xs, norm_ffn[l], l, ffn_w1, ffn_w3, ffn_w2, moe_router, moe_w1, moe_w3, moe_w2)
        prompt_states.append(st_p)
        sample_states.append(st_s)
    y_prompt = _final_norm(xp, norm_final)
    y_sample = _final_norm(xs, norm_final)
    ps = [jnp.stack([s[i] for s in prompt_states]) for i in range(7)]
    ss = [jnp.stack([s[i] for s in sample_states]) for i in range(7)]
    return (y_prompt, y_sample, ps[0], ps[1], ps[2], ps[3], ps[4], ps[5], ps[6],
            ss[0], ss[1], ss[2], ss[3], ss[4], ss[5], ss[6])
```

```python
import functools
import math

import jax
import jax.numpy as jnp
import numpy as np
from jax import lax
from jax.experimental import pallas as pl
from jax.experimental.pallas import tpu as pltpu

D_MODEL = 1024
DEPTH = 2
NSA_HEADS = 8
NSA_KV_HEADS = 2
NSA_GROUP = NSA_HEADS // NSA_KV_HEADS
HEAD_DIM = 64
NSA_WIDTH = NSA_HEADS * HEAD_DIM
KV_WIDTH = NSA_KV_HEADS * HEAD_DIM
CMP_BLOCK = 32
CMP_STRIDE = 16
CMP_HIDDEN = 128
SEL_BLOCK = 64
SEL_TOP_N = 16
SEL_FORCE = 1e4
WINDOW = 512
WIN_Q_BLOCK = 128
REL_BUCKETS = 32
REL_MAX_DIST = 128
M_HEADS = 4
M_HEAD_DIM = 128
M_WIDTH = M_HEADS * M_HEAD_DIM
CONV_W = 4
M_CHUNK = 64
D_FF = 2816
N_EXPERTS = 8
TOP_K = 2
EPS = 1e-6
IN_WIDTHS = (NSA_WIDTH, KV_WIDTH, KV_WIDTH, KV_WIDTH, KV_WIDTH, KV_WIDTH, KV_WIDTH, 3 * NSA_HEADS, M_WIDTH, M_WIDTH,
             2 * M_HEADS, D_MODEL, D_MODEL)

VMEM_LIMIT_V7X = 52 * 1024 * 1024
LANE = 128

F32 = jnp.float32
BF16 = jnp.bfloat16


def _split3(x):
    hi = x.astype(BF16)
    r1 = x - hi.astype(F32)
    mid = r1.astype(BF16)
    lo = (r1 - mid.astype(F32)).astype(BF16)
    return hi, mid, lo


def _pick_tile(n, cands):
    for c in cands:
        if n % c == 0:
            return c
    return n


def _mm_kernel(*refs, norm, has_res):
    x_ref, g_ref, w_ref = refs[:3]
    res_ref = refs[3] if has_res else None
    o_ref, xs_ref = refs[-2], refs[-1]

    @pl.when(pl.program_id(1) == 0)
    def _():
        x = x_ref[...]
        if norm:
            x = x * lax.rsqrt(jnp.mean(x * x, axis=-1, keepdims=True) + EPS) * g_ref[...]
        xs_ref[...] = x.astype(BF16)

    acc = jnp.dot(xs_ref[...], w_ref[...].astype(BF16), preferred_element_type=F32)
    if has_res:
        acc = acc + res_ref[...]
    o_ref[...] = acc


def _mm(x, w, g=None, res=None):
    m, k = x.shape
    n = w.shape[1]
    tm = _pick_tile(m, (1024, 512, 256, 128))
    tn = _pick_tile(n, (512, 256, 128))
    norm = g is not None
    gg = (g if norm else jnp.ones((k,), F32)).reshape(1, k)
    in_specs = [pl.BlockSpec((tm, k), lambda i, j: (i, 0)),
                pl.BlockSpec((1, k), lambda i, j: (0, 0)),
                pl.BlockSpec((k, tn), lambda i, j: (0, j))]
    args = [x, gg, w]
    if res is not None:
        in_specs.append(pl.BlockSpec((tm, tn), lambda i, j: (i, j)))
        args.append(res)
    return pl.pallas_call(
        functools.partial(_mm_kernel, norm=norm, has_res=res is not None),
        grid=(m // tm, n // tn),
        in_specs=in_specs,
        out_specs=pl.BlockSpec((tm, tn), lambda i, j: (i, j)),
        out_shape=jax.ShapeDtypeStruct((m, n), F32),
        scratch_shapes=[pltpu.VMEM((tm, k), BF16)],
        compiler_params=pltpu.CompilerParams(dimension_semantics=("arbitrary", "arbitrary"),
                                             vmem_limit_bytes=VMEM_LIMIT_V7X),
        name="mm",
    )(*args)


def _proj_kernel(x_ref, g_ref, *refs):
    n_out = len(refs) // 2
    x = x_ref[...]
    xn = (x * lax.rsqrt(jnp.mean(x * x, axis=-1, keepdims=True) + EPS) * g_ref[...]).astype(BF16)
    for w_ref, o_ref in zip(refs[:n_out], refs[n_out:]):
        o_ref[...] = jnp.dot(xn, w_ref[...], preferred_element_type=F32)


def _proj(x, g, weights):
    m, k = x.shape
    tm = _pick_tile(m, (256, 128))
    row = lambda n: pl.BlockSpec((tm, n), lambda i: (i, 0))
    whole = lambda a: pl.BlockSpec(a.shape, lambda i: (0,) * a.ndim)
    return pl.pallas_call(
        _proj_kernel,
        grid=(m // tm,),
        in_specs=[row(k), whole(g.reshape(1, k))] + [whole(w) for w in weights],
        out_specs=[row(w.shape[1]) for w in weights],
        out_shape=[jax.ShapeDtypeStruct((m, w.shape[1]), F32) for w in weights],
        compiler_params=pltpu.CompilerParams(dimension_semantics=("arbitrary",), vmem_limit_bytes=VMEM_LIMIT_V7X),
        name="proj",
    )(x, g.reshape(1, k), *weights)


def _merge_out_kernel(x_ref, oa_ref, ob_ref, zg_ref, wa_ref, wb_ref, wo_ref, o_ref):
    d = x_ref.shape[-1]
    up_a = jnp.dot(oa_ref[...].astype(BF16), wa_ref[...], preferred_element_type=F32)
    up_b = jnp.dot(ob_ref[...].astype(BF16), wb_ref[...], preferred_element_type=F32)
    merged = jax.nn.sigmoid(zg_ref[:, :d]) * up_a + jax.nn.sigmoid(zg_ref[:, d:]) * up_b
    o_ref[...] = x_ref[...] + jnp.dot(merged.astype(BF16), wo_ref[...], preferred_element_type=F32)


def _merge_out(x, o_a, o_b, z_g, w_up_a, w_up_b, w_out):
    m, d = x.shape
    tm = _pick_tile(m, (512, 256, 128))
    row = lambda a: pl.BlockSpec((tm, a.shape[1]), lambda i: (i, 0))
    whole = lambda a: pl.BlockSpec(a.shape, lambda i: (0,) * a.ndim)
    return pl.pallas_call(
        _merge_out_kernel,
        grid=(m // tm,),
        in_specs=[row(x), row(o_a), row(o_b), row(z_g), whole(w_up_a), whole(w_up_b), whole(w_out)],
        out_specs=row(x),
        out_shape=jax.ShapeDtypeStruct((m, d), F32),
        compiler_params=pltpu.CompilerParams(dimension_semantics=("arbitrary",), vmem_limit_bytes=VMEM_LIMIT_V7X),
        name="merge_out",
    )(x, o_a, o_b, z_g, w_up_a, w_up_b, w_out)


def _ffn_body(x_ref, g_ref, w1_ref, w3_ref, w2_ref, o_ref, xs_ref, acc_ref, *, residual, grouped, active=None):
    j = pl.program_id(1)

    @pl.when(j == 0)
    def _():
        x = x_ref[...]
        xn = x * lax.rsqrt(jnp.mean(x * x, axis=-1, keepdims=True) + EPS) * g_ref[...]
        xs_ref[...] = xn.astype(BF16)
        acc_ref[...] = jnp.zeros_like(acc_ref)

    def accumulate():
        xs = xs_ref[...]
        w1 = w1_ref[0] if grouped else w1_ref[...]
        w3 = w3_ref[0] if grouped else w3_ref[...]
        w2 = w2_ref[0] if grouped else w2_ref[...]
        a = jnp.dot(xs, w1.astype(BF16), preferred_element_type=F32)
        b = jnp.dot(xs, w3.astype(BF16), preferred_element_type=F32)
        h = (a * jax.nn.sigmoid(a) * b).astype(BF16)
        acc_ref[...] += jnp.dot(h, w2.astype(BF16), preferred_element_type=F32)

    if active is None:
        accumulate()
    else:
        pl.when(active)(accumulate)

    @pl.when(j == pl.num_programs(1) - 1)
    def _():
        if residual:
            o_ref[...] = x_ref[...] + acc_ref[...]
        else:
            o_ref[...] = acc_ref[...]


def _ffn_dense_kernel(x_ref, g_ref, w1_ref, w3_ref, w2_ref, o_ref, xs_ref, acc_ref):
    _ffn_body(x_ref, g_ref, w1_ref, w3_ref, w2_ref, o_ref, xs_ref, acc_ref, residual=True, grouped=False)


def _ffn_grouped_kernel(be_ref, nu_ref, x_ref, g_ref, w1_ref, w3_ref, w2_ref, o_ref, xs_ref, acc_ref):
    del be_ref
    _ffn_body(x_ref, g_ref, w1_ref, w3_ref, w2_ref, o_ref, xs_ref, acc_ref, residual=False, grouped=True,
              active=pl.program_id(0) < nu_ref[0])


def _ffn_dense(x, g, w1, w3, w2):
    m, d = x.shape
    f = w1.shape[1]
    tm = _pick_tile(m, (1024, 512, 256, 128))
    tf = _pick_tile(f, (256, 128))
    return pl.pallas_call(
        _ffn_dense_kernel,
        grid=(m // tm, f // tf),
        in_specs=[pl.BlockSpec((tm, d), lambda i, j: (i, 0)),
                  pl.BlockSpec((1, d), lambda i, j: (0, 0)),
                  pl.BlockSpec((d, tf), lambda i, j: (0, j)),
                  pl.BlockSpec((d, tf), lambda i, j: (0, j)),
                  pl.BlockSpec((tf, d), lambda i, j: (j, 0))],
        out_specs=pl.BlockSpec((tm, d), lambda i, j: (i, 0)),
        out_shape=jax.ShapeDtypeStruct((m, d), F32),
        scratch_shapes=[pltpu.VMEM((tm, d), BF16), pltpu.VMEM((tm, d), F32)],
        compiler_params=pltpu.CompilerParams(dimension_semantics=("arbitrary", "arbitrary"),
                                             vmem_limit_bytes=VMEM_LIMIT_V7X),
        name="ffn_dense",
    )(x, g.reshape(1, d), w1, w3, w2)


def _ffn_grouped(xd, blk_e, n_used, g, w1, w3, w2, tm):
    rows, d = xd.shape
    f = w1.shape[2]
    tf = _pick_tile(f, (256, 128))
    n_f = f // tf
    col = lambda i, j, nu: jnp.where(i < nu[0], j, n_f - 1)
    grid_spec = pltpu.PrefetchScalarGridSpec(
        num_scalar_prefetch=2,
        grid=(rows // tm, n_f),
        in_specs=[pl.BlockSpec((tm, d), lambda i, j, be, nu: (i, 0)),
                  pl.BlockSpec((1, d), lambda i, j, be, nu: (0, 0)),
                  pl.BlockSpec((1, d, tf), lambda i, j, be, nu: (be[i], 0, col(i, j, nu))),
                  pl.BlockSpec((1, d, tf), lambda i, j, be, nu: (be[i], 0, col(i, j, nu))),
                  pl.BlockSpec((1, tf, d), lambda i, j, be, nu: (be[i], col(i, j, nu), 0))],
        out_specs=pl.BlockSpec((tm, d), lambda i, j, be, nu: (i, 0)),
        scratch_shapes=[pltpu.VMEM((tm, d), BF16), pltpu.VMEM((tm, d), F32)],
    )
    return pl.pallas_call(
        _ffn_grouped_kernel,
        grid_spec=grid_spec,
        out_shape=jax.ShapeDtypeStruct((rows, d), F32),
        compiler_params=pltpu.CompilerParams(dimension_semantics=("arbitrary", "arbitrary"),
                                             vmem_limit_bytes=VMEM_LIMIT_V7X),
        name="ffn_grouped",
    )(blk_e, n_used, xd, g.reshape(1, d), w1, w3, w2)


def _moe(x, g, router, w1, w3, w2):
    n, d = x.shape
    tm = 1024 if n >= 8192 else 128
    router_p = jnp.pad(router, ((0, 0), (0, LANE - N_EXPERTS)))
    logits = _mm(x, router_p, g=g)[:, :N_EXPERTS]
    top_val, top_idx = lax.top_k(logits, TOP_K)
    gate = jax.nn.softmax(top_val, axis=-1).reshape(-1)
    e_flat = top_idx.reshape(-1)
    n_asg = n * TOP_K
    order = jnp.argsort(e_flat)
    onehot = (e_flat[:, None] == jnp.arange(N_EXPERTS, dtype=e_flat.dtype)[None, :]).astype(jnp.int32)
    csum = jnp.cumsum(onehot, axis=0)
    counts = csum[-1]
    rank = jnp.sum(onehot * csum, axis=1) - 1
    padded = (counts + tm - 1) // tm * tm
    pad_end = jnp.cumsum(padded)
    pad_start = pad_end - padded
    start = jnp.cumsum(counts) - counts
    pos = (pad_start[e_flat] + rank).astype(jnp.int32)
    n_blocks = n_asg // tm + N_EXPERTS
    blk_e = jnp.minimum(jnp.searchsorted(pad_end, jnp.arange(n_blocks) * tm, side='right'),
                        N_EXPERTS - 1).astype(jnp.int32)
    n_used = (pad_end[-1] // tm).astype(jnp.int32)
    blk_e = jnp.where(jnp.arange(n_blocks) < n_used, blk_e, blk_e[n_used - 1])
    e_row = jnp.repeat(blk_e, tm)
    in_expert = jnp.arange(n_blocks * tm) - pad_start[e_row]
    src_asg = order[jnp.clip(start[e_row] + in_expert, 0, n_asg - 1)]
    src_tok = jnp.where(in_expert < counts[e_row], src_asg // TOP_K, 0).astype(jnp.int32)
    yd = _ffn_grouped(x[src_tok], blk_e, n_used.reshape(1), g, w1, w3, w2, tm)
    pos, gate = pos.reshape(n, TOP_K), gate.reshape(n, TOP_K)
    return x + sum(yd[pos[:, k]] * gate[:, k:k + 1] for k in range(TOP_K))


N_CHUNK = 128
CHUNK_W = CMP_STRIDE * HEAD_DIM


def _compress_kernel(x_ref, pe_ref, w1_ref, w2_ref, o_ref):
    c = x_ref[0]
    lo = jnp.dot((c + pe_ref[0:1]).astype(BF16), w1_ref[0].astype(BF16), preferred_element_type=F32)
    hi = jnp.dot((c + pe_ref[1:2]).astype(BF16), w1_ref[1].astype(BF16), preferred_element_type=F32)
    hid = jax.nn.gelu(lo + pltpu.roll(hi, N_CHUNK - 1, 0))
    out = jnp.dot(hid.astype(BF16), w2_ref[...].astype(BF16), preferred_element_type=F32)
    row = lax.broadcasted_iota(jnp.int32, out.shape, 0)
    o_ref[0] = jnp.where(row < N_CHUNK - 1, out, 0.0)


def _compress(xc, pe, w1, w2):
    nb = xc.shape[0]
    hidden = w1.shape[-1]
    return pl.pallas_call(
        _compress_kernel,
        grid=(nb,),
        in_specs=[pl.BlockSpec((1, N_CHUNK, CHUNK_W), lambda i: (i, 0, 0)),
                  pl.BlockSpec((2, CHUNK_W), lambda i: (0, 0)),
                  pl.BlockSpec((2, CHUNK_W, hidden), lambda i: (0, 0, 0)),
                  pl.BlockSpec((hidden, HEAD_DIM), lambda i: (0, 0))],
        out_specs=pl.BlockSpec((1, N_CHUNK, HEAD_DIM), lambda i: (i, 0, 0)),
        out_shape=jax.ShapeDtypeStruct((nb, N_CHUNK, HEAD_DIM), F32),
        compiler_params=pltpu.CompilerParams(dimension_semantics=("arbitrary",), vmem_limit_bytes=VMEM_LIMIT_V7X),
        name="compress",
    )(xc, pe.reshape(2, CHUNK_W), w1.reshape(2, CHUNK_W, hidden), w2)


def _compress_heads(kv, pe, w1, w2):
    b = kv.shape[0]
    xc = kv.reshape(b, N_CHUNK, CMP_STRIDE, NSA_KV_HEADS, HEAD_DIM).transpose(0, 3, 1, 2, 4)
    out = _compress(xc.reshape(b * NSA_KV_HEADS, N_CHUNK, CHUNK_W), pe, w1, w2)
    return out.reshape(b, NSA_KV_HEADS, N_CHUNK, HEAD_DIM)


TQ = 128
TK = 128
NEG = -1e30
N_BIAS_TILES = WINDOW // TK + 1
MASKED_TILE = N_BIAS_TILES
SEL_SUB = 4


def _nsa_prompt_kernel(q_ref, kct_ref, vcc_ref, kst_ref, vs_ref, kwt_ref, vw_ref, bcmp_ref, btile_ref, gate_ref,
                       ov_ref, exp_ref, o_ref, selneg_ref, m_ref, l_ref, acc_ref):
    i = pl.program_id(2)
    rows = NSA_GROUP * TQ
    qt = (q_ref[...] * (HEAD_DIM ** -0.5)).astype(BF16)
    q = jnp.concatenate([qt[:, a * HEAD_DIM:(a + 1) * HEAD_DIM] for a in range(NSA_GROUP)], axis=0)

    s = jnp.dot(q, kct_ref[0, 0], preferred_element_type=F32) + bcmp_ref[0].reshape(rows, N_CHUNK)
    m = jnp.max(s, axis=-1, keepdims=True)
    e = jnp.where(s > 0.1 * NEG, jnp.exp(s - m), 0.0)
    p = e / jnp.maximum(jnp.sum(e, axis=-1, keepdims=True), 1e-30)
    o_cmp = jnp.dot(p.astype(BF16), vcc_ref[0, 0], preferred_element_type=F32)

    p_sum = p[0:TQ] + p[TQ:2 * TQ] + p[2 * TQ:3 * TQ] + p[3 * TQ:4 * TQ]
    n_sel = T_PROMPT // SEL_BLOCK
    imp_t = sum(lax.dot_general(ov_ref[...], part, (((1,), (1,)), ((), ())), preferred_element_type=F32)
                for part in _split3(p_sum))[0:n_sel]
    blk = lax.broadcasted_iota(jnp.int32, (n_sel, TQ), 0)
    cur = (lax.broadcasted_iota(jnp.int32, (n_sel, TQ), 1) + i * TQ) // SEL_BLOCK
    valid = blk <= cur
    forced = (blk == 0) | (blk == cur) | (blk == cur - 1)
    score = jnp.where(valid, imp_t + jnp.where(forced, SEL_FORCE, 0.0), -1.0)
    rank = jnp.zeros((n_sel, TQ), F32)
    for c in range(n_sel):
        sc = score[c:c + 1, :]
        beats = (sc > score) | ((sc == score) & (blk > c))
        rank = rank + jnp.where(beats, 1.0, 0.0)
    sel_t = jnp.where(valid & (rank < SEL_TOP_N), 1.0, 0.0)
    sel = jnp.concatenate([sel_t, jnp.zeros((LANE - n_sel, TQ), F32)], axis=0).T.astype(BF16)
    sel_keys = jnp.dot(sel, exp_ref[...], preferred_element_type=F32)
    selneg_ref[...] = (sel_keys - 1.0) * (-NEG)

    halves = [slice(hf * NSA_GROUP // 2, (hf + 1) * NSA_GROUP // 2) for hf in range(2)]
    half_rows = [slice(gs.start * TQ, gs.stop * TQ) for gs in halves]

    def masked_scores(kt_ref, first_tile, n_sub, selected, gs, rs):
        k0 = pl.multiple_of(first_tile * TK, TK)
        s_all = jnp.dot(q[rs], kt_ref[0, 0, :, pl.ds(k0, n_sub * TK)], preferred_element_type=F32)
        pieces = []
        for u in range(n_sub):
            d0 = i - (first_tile + u)
            idx = jnp.where(d0 < 0, MASKED_TILE, jnp.minimum(d0, 2) if selected else d0)
            piece = s_all[:, u * TK:(u + 1) * TK].reshape(gs.stop - gs.start, TQ, TK) + btile_ref[0, idx, gs]
            if selected:
                piece = piece + selneg_ref[:, pl.ds(pl.multiple_of(k0 + u * TK, TK), TK)][None]
            pieces.append(piece.reshape(rs.stop - rs.start, TK))
        return pieces, k0

    def row_max(pieces):
        return jnp.max(functools.reduce(jnp.maximum, pieces), axis=-1, keepdims=True)

    def row_sum(pieces):
        return jnp.sum(functools.reduce(jnp.add, pieces), axis=-1, keepdims=True)

    m_ref[...] = jnp.full(m_ref.shape, -jnp.inf, F32)
    l_ref[...] = jnp.zeros(l_ref.shape, F32)
    acc_ref[...] = jnp.zeros(acc_ref.shape, F32)

    def sel_group(gi, carry):
        for gs, rs in zip(halves, half_rows):
            pieces, k0 = masked_scores(kst_ref, gi * SEL_SUB, SEL_SUB, True, gs, rs)
            m_old = m_ref[rs, :]
            m_new = jnp.maximum(m_old, row_max(pieces))
            pt = [jnp.exp(piece - m_new) for piece in pieces]
            alpha = jnp.exp(m_old - m_new)
            l_ref[rs, :] = alpha * l_ref[rs, :] + row_sum(pt)
            pv = jnp.dot(jnp.concatenate([x.astype(BF16) for x in pt], axis=1),
                         vs_ref[0, 0, pl.ds(k0, SEL_SUB * TK), :], preferred_element_type=F32)
            acc_ref[rs, :] = alpha * acc_ref[rs, :] + pv
            m_ref[rs, :] = m_new
        return carry

    lax.fori_loop(0, i // SEL_SUB + 1, sel_group, 0)
    o_sel = acc_ref[...] / l_ref[...]

    o_win = []
    for gs, rs in zip(halves, half_rows):
        pieces, k0 = masked_scores(kwt_ref, jnp.maximum(i - (N_BIAS_TILES - 1), 0), N_BIAS_TILES, False, gs, rs)
        m_win = row_max(pieces)
        pt = [jnp.exp(piece - m_win) for piece in pieces]
        o_win.append(jnp.dot(jnp.concatenate([x.astype(BF16) for x in pt], axis=1),
                             vw_ref[0, 0, pl.ds(k0, N_BIAS_TILES * TK), :], preferred_element_type=F32) / row_sum(pt))
    o_win = jnp.concatenate(o_win, axis=0)

    g = jax.nn.sigmoid(gate_ref[0, 0])
    pieces = []
    for a in range(NSA_GROUP):
        r = slice(a * TQ, (a + 1) * TQ)
        pieces.append(g[:, 3 * a:3 * a + 1] * o_cmp[r] + g[:, 3 * a + 1:3 * a + 2] * o_sel[r]
                      + g[:, 3 * a + 2:3 * a + 3] * o_win[r])
    o_ref[...] = jnp.concatenate(pieces, axis=1)


T_PROMPT = 2048


def _bias_lookup(table, dist):
    oh = jax.nn.one_hot(rel_bucket(dist), REL_BUCKETS, dtype=F32)
    return jnp.einsum('...r,rh->...h', oh, table, precision=lax.Precision.HIGHEST)


def _prompt_bias_tables(table):
    t = T_PROMPT
    q_pos = jnp.arange(t)
    block_end = jnp.arange(N_CHUNK) * CMP_STRIDE + (CMP_BLOCK - 1)
    dist = q_pos[:, None] - block_end[None, :]
    ok = (dist >= 0) & (jnp.arange(N_CHUNK)[None, :] < N_CHUNK - 1)
    bcmp = jnp.where(ok[..., None], _bias_lookup(table, dist), NEG)
    bcmp = bcmp.reshape(t, N_CHUNK, NSA_KV_HEADS, NSA_GROUP).transpose(2, 3, 0, 1)
    d0 = jnp.arange(N_BIAS_TILES)[:, None, None]
    dist = d0 * TK + jnp.arange(TQ)[None, :, None] - jnp.arange(TK)[None, None, :]
    ok = (dist >= 0) & (dist < WINDOW)
    bt = jnp.where(ok[..., None], _bias_lookup(table, dist), NEG)
    bt = bt.reshape(N_BIAS_TILES, TQ, TK, NSA_KV_HEADS, NSA_GROUP).transpose(3, 0, 4, 1, 2)
    bt = jnp.concatenate([bt, jnp.full((NSA_KV_HEADS, 1, NSA_GROUP, TQ, TK), NEG, F32)], axis=1)
    c0 = jnp.arange(N_CHUNK) * CMP_STRIDE
    s0 = jnp.arange(LANE) * SEL_BLOCK
    ov = jnp.clip(jnp.minimum(c0[:, None] + CMP_BLOCK, s0[None, :] + SEL_BLOCK)
                  - jnp.maximum(c0[:, None], s0[None, :]), 0, CMP_BLOCK).astype(F32) / CMP_BLOCK
    ov = jnp.where((jnp.arange(N_CHUNK)[:, None] < N_CHUNK - 1) & (jnp.arange(LANE)[None, :] < t // SEL_BLOCK), ov, 0.0)
    expand = (jnp.arange(LANE)[:, None] == (jnp.arange(t) // SEL_BLOCK)[None, :]).astype(BF16)
    return bcmp, bt, ov.T.astype(BF16), expand


def _nsa_prompt(z_a, kct, vcc, kst, vs, kwt, vw, gates, tables):
    bcmp, bt, ov, expand = tables
    b, _, _, t = kst.shape
    rows = NSA_GROUP * TQ
    n_q = t // TQ
    q_cols = NSA_GROUP * HEAD_DIM
    per_bh = lambda *blk: pl.BlockSpec((1, 1) + blk, lambda bi, h, i: (bi, h) + (0,) * len(blk))
    return pl.pallas_call(
        _nsa_prompt_kernel,
        grid=(b, NSA_KV_HEADS, n_q),
        in_specs=[pl.BlockSpec((TQ, q_cols), lambda bi, h, i: (bi * n_q + i, h)),
                  per_bh(HEAD_DIM, N_CHUNK), per_bh(N_CHUNK, HEAD_DIM),
                  per_bh(HEAD_DIM, t), per_bh(t, HEAD_DIM), per_bh(HEAD_DIM, t), per_bh(t, HEAD_DIM),
                  pl.BlockSpec((1, NSA_GROUP, TQ, N_CHUNK), lambda bi, h, i: (h, 0, i, 0)),
                  pl.BlockSpec((1, N_BIAS_TILES + 1, NSA_GROUP, TQ, TK), lambda bi, h, i: (h, 0, 0, 0, 0)),
                  pl.BlockSpec((1, 1, TQ, 3 * NSA_GROUP), lambda bi, h, i: (bi, h, i, 0)),
                  pl.BlockSpec((N_CHUNK, LANE), lambda bi, h, i: (0, 0)),
                  pl.BlockSpec((LANE, t), lambda bi, h, i: (0, 0))],
        out_specs=pl.BlockSpec((TQ, q_cols), lambda bi, h, i: (bi * n_q + i, h)),
        out_shape=jax.ShapeDtypeStruct((b * t, NSA_WIDTH), F32),
        scratch_shapes=[pltpu.VMEM((TQ, t), F32), pltpu.VMEM((rows, 1), F32), pltpu.VMEM((rows, 1), F32),
                        pltpu.VMEM((rows, HEAD_DIM), F32)],
        compiler_params=pltpu.CompilerParams(dimension_semantics=("arbitrary", "arbitrary", "arbitrary"),
                                             vmem_limit_bytes=VMEM_LIMIT_V7X),
        name="nsa_prompt",
    )(z_a, kct, vcc, kst, vs, kwt, vw, bcmp, bt, gates, ov, expand)


def _heads_t(x, b, t):
    return x.astype(BF16).reshape(b, t, NSA_KV_HEADS, HEAD_DIM).transpose(0, 2, 3, 1)


def _heads(x, b, t):
    return x.astype(BF16).reshape(b, t, NSA_KV_HEADS, HEAD_DIM).transpose(0, 2, 1, 3)


def _nsa_prompt_from_proj(z_a, z_s, b, t, tables, pe, w1, w2):
    col = lambda i: z_a[:, NSA_WIDTH + KV_WIDTH * i:NSA_WIDTH + KV_WIDTH * (i + 1)]
    kcc = _compress_heads(col(0).reshape(b, t, NSA_KV_HEADS, HEAD_DIM), pe[0], w1[0], w2[0])
    vcc = _compress_heads(col(1).reshape(b, t, NSA_KV_HEADS, HEAD_DIM), pe[1], w1[1], w2[1])
    gates = z_s[:, :3 * NSA_HEADS].reshape(b, t, NSA_KV_HEADS, 3 * NSA_GROUP).transpose(0, 2, 1, 3)
    return _nsa_prompt(z_a, kcc.astype(BF16).transpose(0, 1, 3, 2), vcc.astype(BF16),
                       _heads_t(col(2), b, t), _heads(col(3), b, t), _heads_t(col(4), b, t), _heads(col(5), b, t),
                       gates, tables)


T_SAMPLE = 4
T_PAD = 8
PAGE = 128
N_PAGES = 16
PAST = N_PAGES * PAGE
N_SEL_SAMPLE = -(-(PAST + T_SAMPLE) // SEL_BLOCK)


def _nsa_sample_kernel(pt_ref, *refs):
    del pt_ref
    cp, sp = refs[:N_PAGES], refs[N_PAGES:2 * N_PAGES]
    (wb_ref, za_ref, zs_ref, pe2_ref, w1_ref, w2_ref, bcmp_ref, bpast_ref, bwin_ref, bnew_ref, ov_ref, exp_ref,
     o_ref, wout_ref, new_ref, q_ref, g_ref, xcat_ref, kcat_ref, vcat_ref, s_ref, e_ref, oacc_ref,
     rowpg_ref) = refs[2 * N_PAGES:]
    rows = NSA_GROUP * T_PAD

    @pl.when(pl.program_id(0) == 0)
    def _():
        new_ref[...] = jnp.zeros(new_ref.shape, F32)
        q_ref[...] = jnp.zeros(q_ref.shape, F32)
        g_ref[...] = jnp.zeros(g_ref.shape, F32)

    new_ref[0:T_SAMPLE, :] = za_ref[0, :, NSA_WIDTH + 2 * KV_WIDTH:]
    q_ref[0:T_SAMPLE, :] = za_ref[0, :, :NSA_WIDTH] * (HEAD_DIM ** -0.5)
    g_ref[0:T_SAMPLE, :] = zs_ref[0]

    for p in range(N_PAGES):
        for kv in range(2):
            rowpg_ref[kv, p * PAGE:(p + 1) * PAGE, :] = cp[p][0, kv * KV_WIDTH:(kv + 1) * KV_WIDTH, :].T
    comp = []
    for kv in range(2):
        def place_row_offset(l, carry, kv=kv):
            xl = rowpg_ref[kv, pl.ds(l, N_CHUNK, stride=CMP_STRIDE), :]
            lanes = pl.ds(pl.multiple_of(l * KV_WIDTH, KV_WIDTH), KV_WIDTH)
            for half in range(2):
                xcat_ref[half, :, lanes] = (xl + pe2_ref[kv, pl.ds(l + CMP_STRIDE * half, 1), :]).astype(BF16)
            return carry

        lax.fori_loop(0, CMP_STRIDE, place_row_offset, 0)
        lo = jnp.dot(xcat_ref[0], w1_ref[kv, 0], preferred_element_type=F32)
        hi = jnp.dot(xcat_ref[1], w1_ref[kv, 1], preferred_element_type=F32)
        hid = jax.nn.gelu(lo + pltpu.roll(hi, N_CHUNK - 1, 0))
        out = jnp.dot(hid.astype(BF16), w2_ref[kv], preferred_element_type=F32)
        row = lax.broadcasted_iota(jnp.int32, out.shape, 0)
        comp.append(jnp.where(row < N_CHUNK - 1, out, 0.0).astype(BF16))

    for p in range(N_PAGES):
        kcat_ref[:, p * PAGE:(p + 1) * PAGE] = sp[p][0, 0:KV_WIDTH, :].astype(BF16)
        vcat_ref[:, p * PAGE:(p + 1) * PAGE] = sp[p][0, KV_WIDTH:, :].astype(BF16)
    kw_buf = wb_ref[0, 0:KV_WIDTH, :].astype(BF16)
    vw_buf = wb_ref[0, KV_WIDTH:, :].astype(BF16)
    ks_new, vs_new, kw_new, vw_new = (new_ref[:, i * KV_WIDTH:(i + 1) * KV_WIDTH].astype(BF16) for i in range(4))
    gates = jax.nn.sigmoid(g_ref[...])
    nt = (((1,), (1,)), ((), ()))
    zeros64 = jnp.zeros((rows, HEAD_DIM), F32)
    pieces = []
    key_chunk = 512

    all_rows = NSA_KV_HEADS * rows

    def attend_window(qp):
        s_parts = [jnp.dot(qp, kw_buf, preferred_element_type=F32) + bwin_ref[...].reshape(all_rows, -1),
                   lax.dot_general(qp, kw_new, nt, preferred_element_type=F32) + bnew_ref[...].reshape(all_rows, LANE)]
        m = jnp.maximum(*[jnp.max(s, axis=-1, keepdims=True) for s in s_parts])
        e_parts = [jnp.exp(s - m) for s in s_parts]
        den = jnp.add(*[jnp.sum(e, axis=-1, keepdims=True) for e in e_parts])
        num = (lax.dot_general(e_parts[0].astype(BF16), vw_buf, nt, preferred_element_type=F32)
               + jnp.dot(e_parts[1].astype(BF16), vw_new, preferred_element_type=F32))
        return num / den

    def attend_selected(qp):
        def scores(c, carry):
            ds = pl.ds(pl.multiple_of(c * key_chunk, key_chunk), key_chunk)
            s_ref[:, ds] += jnp.dot(qp, kcat_ref[:, ds], preferred_element_type=F32)
            return carry

        lax.fori_loop(0, PAST // key_chunk, scores, 0)
        s_ref[:, PAST:] += lax.dot_general(qp, ks_new, nt, preferred_element_type=F32)
        s = s_ref[...]
        e = jnp.exp(s - jnp.max(s, axis=-1, keepdims=True))
        den = jnp.sum(e, axis=-1, keepdims=True)
        e_ref[...] = e.astype(BF16)
        oacc_ref[...] = jnp.dot(e_ref[:, PAST:], vs_new, preferred_element_type=F32)

        def weighted(c, carry):
            ds = pl.ds(pl.multiple_of(c * key_chunk, key_chunk), key_chunk)
            oacc_ref[...] += lax.dot_general(e_ref[:, ds], vcat_ref[:, ds], nt, preferred_element_type=F32)
            return carry

        lax.fori_loop(0, PAST // key_chunk, weighted, 0)
        return oacc_ref[...] / den

    qp = []
    for h in range(NSA_KV_HEADS):
        q64 = jnp.concatenate([q_ref[:, (h * NSA_GROUP + a) * HEAD_DIM:(h * NSA_GROUP + a + 1) * HEAD_DIM]
                               for a in range(NSA_GROUP)], axis=0)
        qp.append(jnp.concatenate([q64, zeros64] if h == 0 else [zeros64, q64], axis=1))
    qp = jnp.concatenate(qp, axis=0).astype(BF16)

    s = lax.dot_general(qp, comp[0], nt, preferred_element_type=F32) + bcmp_ref[...].reshape(all_rows, N_CHUNK)
    m = jnp.max(s, axis=-1, keepdims=True)
    e = jnp.where(s > 0.1 * NEG, jnp.exp(s - m), 0.0)
    p = e / jnp.maximum(jnp.sum(e, axis=-1, keepdims=True), 1e-30)
    o_cmp = jnp.dot(p.astype(BF16), comp[1], preferred_element_type=F32)

    sel_rows = NSA_KV_HEADS * T_PAD
    p_sum = jnp.concatenate([sum(p[h * rows + a * T_PAD:h * rows + (a + 1) * T_PAD] for a in range(NSA_GROUP))
                             for h in range(NSA_KV_HEADS)], axis=0)
    imp = sum(jnp.dot(part, ov_ref[...], preferred_element_type=F32) for part in _split3(p_sum))
    lane = lax.broadcasted_iota(jnp.int32, (sel_rows, LANE), 1)
    tok = lax.broadcasted_iota(jnp.int32, (sel_rows, LANE), 0) % T_PAD
    cur = (PAST + jnp.minimum(tok, T_SAMPLE - 1)) // SEL_BLOCK
    valid = lane <= cur
    forced = (lane == 0) | (lane == cur) | (lane == cur - 1)
    score = jnp.where(valid, imp + jnp.where(forced, SEL_FORCE, 0.0), -1.0)
    rank = jnp.zeros((sel_rows, LANE), F32)
    for c in range(N_SEL_SAMPLE):
        sc = score[:, c:c + 1]
        beats = (sc > score) | ((sc == score) & (lane > c))
        rank = rank + jnp.where(beats, 1.0, 0.0)
    sel = jnp.where(valid & (rank < SEL_TOP_N), 1.0, 0.0)
    selneg = (jnp.dot(sel.astype(BF16), exp_ref[...], preferred_element_type=F32) - 1.0) * (-NEG)
    selneg_new = (sel[:, PAST // SEL_BLOCK:PAST // SEL_BLOCK + 1] - 1.0) * (-NEG)

    def per_token(bias, tok_term):
        n = bias.shape[-1]
        return (bias.reshape(NSA_KV_HEADS, NSA_GROUP, T_PAD, n)
                + tok_term.reshape(NSA_KV_HEADS, 1, T_PAD, n)).reshape(all_rows, n)

    s_ref[:, :PAST] = per_token(bpast_ref[...], selneg)
    s_ref[:, PAST:] = per_token(bnew_ref[...], jnp.broadcast_to(selneg_new, (sel_rows, LANE)))
    o_sel = attend_selected(qp)
    o_win = attend_window(qp)

    for h in range(NSA_KV_HEADS):
        lanes = slice(h * HEAD_DIM, (h + 1) * HEAD_DIM)
        for a in range(NSA_GROUP):
            r = slice(h * rows + a * T_PAD, h * rows + (a + 1) * T_PAD)
            c0 = (h * NSA_GROUP + a) * 3
            pieces.append(gates[:, c0:c0 + 1] * o_cmp[r, lanes] + gates[:, c0 + 1:c0 + 2] * o_sel[r, lanes]
                          + gates[:, c0 + 2:c0 + 3] * o_win[r, lanes])
    o_ref[0] = jnp.concatenate(pieces, axis=1)[0:T_SAMPLE]

    w_buf = wb_ref.shape[-1]
    shifted = pltpu.roll(wb_ref[0], w_buf - T_SAMPLE, 1)
    new_t = jnp.concatenate([new_ref[:, 2 * KV_WIDTH:3 * KV_WIDTH].T, new_ref[:, 3 * KV_WIDTH:].T], axis=0)
    new_t = pltpu.roll(new_t, LANE - T_SAMPLE, 1)
    lane = lax.broadcasted_iota(jnp.int32, new_t.shape, 1)
    tail = jnp.where(lane >= LANE - T_SAMPLE, new_t, shifted[:, w_buf - LANE:])
    wout_ref[0] = jnp.concatenate([shifted[:, :w_buf - LANE], tail], axis=1)


def _sample_bias_tables(table):
    tq = jnp.minimum(jnp.arange(T_PAD), T_SAMPLE - 1)
    q_pos = PAST + tq

    def lay(x):
        n = x.shape[1]
        return x.reshape(T_PAD, n, NSA_KV_HEADS, NSA_GROUP).transpose(2, 3, 0, 1).reshape(NSA_KV_HEADS, -1, n)

    block_end = jnp.arange(N_CHUNK) * CMP_STRIDE + (CMP_BLOCK - 1)
    dist = q_pos[:, None] - block_end[None, :]
    ok = (dist >= 0) & (jnp.arange(N_CHUNK)[None, :] < N_CHUNK - 1)
    bcmp = lay(jnp.where(ok[..., None], _bias_lookup(table, dist), NEG))
    dist = q_pos[:, None] - jnp.arange(PAST)[None, :]
    past = _bias_lookup(table, dist)
    bpast = lay(past)
    w_buf = min(WINDOW, PAST)
    bwin = lay(jnp.where((dist < WINDOW)[:, PAST - w_buf:, None], past[:, PAST - w_buf:], NEG))
    j = jnp.arange(LANE)
    dist = tq[:, None] - j[None, :]
    ok = (dist >= 0) & (j[None, :] < T_SAMPLE)
    bnew = lay(jnp.where(ok[..., None], _bias_lookup(table, dist), NEG))
    c0 = jnp.arange(N_CHUNK) * CMP_STRIDE
    s0 = jnp.arange(LANE) * SEL_BLOCK
    ov = jnp.clip(jnp.minimum(c0[:, None] + CMP_BLOCK, s0[None, :] + SEL_BLOCK)
                  - jnp.maximum(c0[:, None], s0[None, :]), 0, CMP_BLOCK).astype(F32) / CMP_BLOCK
    ov = jnp.where((jnp.arange(N_CHUNK)[:, None] < N_CHUNK - 1) & (j[None, :] < N_SEL_SAMPLE), ov, 0.0)
    expand = (j[:, None] == (jnp.arange(PAST) // SEL_BLOCK)[None, :]).astype(BF16)
    return bcmp, bpast, bwin, bnew, ov.astype(BF16), expand


def _nsa_sample(za, zs, cache_cmp, cache_slc, cache_win, page_table, layer, tables, pe, w1, w2):
    b = za.shape[0]
    n_phys = cache_cmp.shape[1]
    cmp_pages = jnp.transpose(cache_cmp, (0, 1, 3, 4, 5, 2)).reshape(DEPTH * n_phys, 2 * KV_WIDTH, PAGE)
    slc_pages = jnp.transpose(cache_slc, (0, 1, 3, 4, 5, 2)).reshape(DEPTH * n_phys, 2 * KV_WIDTH, PAGE)
    pt = page_table.reshape(-1).astype(jnp.int32) + layer * n_phys
    w_buf = cache_win.shape[2]
    win = jnp.transpose(cache_win, (0, 1, 3, 4, 5, 2)).reshape(DEPTH * b, 2 * KV_WIDTH, w_buf)
    bcmp, bpast, bwin, bnew, ov, expand = tables
    pe2 = jnp.concatenate([pe, pe], axis=-1)
    page_spec = lambda p: pl.BlockSpec((1, 2 * KV_WIDTH, PAGE), lambda bi, pt_: (pt_[bi * N_PAGES + p], 0, 0))
    whole = lambda x: pl.BlockSpec(x.shape, lambda bi, pt_: (0,) * x.ndim)
    eye = jnp.eye(NSA_KV_HEADS, dtype=F32)
    w1h = w1.reshape(2, 2, CMP_STRIDE, HEAD_DIM, CMP_HIDDEN)
    w1_bd = jnp.einsum('khlde,ab->khladbe', w1h, eye).reshape(2, 2, CMP_STRIDE * KV_WIDTH, NSA_KV_HEADS * CMP_HIDDEN)
    w2_bd = jnp.einsum('ked,ab->kaebd', w2, eye).reshape(2, NSA_KV_HEADS * CMP_HIDDEN, KV_WIDTH)
    consts = [pe2, w1_bd.astype(BF16), w2_bd.astype(BF16), bcmp, bpast, bwin, bnew, ov, expand]
    grid_spec = pltpu.PrefetchScalarGridSpec(
        num_scalar_prefetch=1,
        grid=(b,),
        in_specs=[page_spec(p) for p in range(N_PAGES)] * 2
        + [pl.BlockSpec((1, 2 * KV_WIDTH, w_buf), lambda bi, pt_: (layer * b + bi, 0, 0)),
           pl.BlockSpec((1, T_SAMPLE, za.shape[-1]), lambda bi, pt_: (bi, 0, 0)),
           pl.BlockSpec((1, T_SAMPLE, LANE), lambda bi, pt_: (bi, 0, 0))]
        + [whole(x) for x in consts],
        out_specs=[pl.BlockSpec((1, T_SAMPLE, NSA_WIDTH), lambda bi, pt_: (bi, 0, 0)),
                   pl.BlockSpec((1, 2 * KV_WIDTH, w_buf), lambda bi, pt_: (bi, 0, 0))],
        scratch_shapes=[pltpu.VMEM((PAGE, 4 * KV_WIDTH), F32), pltpu.VMEM((T_PAD, NSA_WIDTH), F32),
                        pltpu.VMEM((T_PAD, LANE), F32), pltpu.VMEM((2, N_CHUNK, CMP_STRIDE * KV_WIDTH), BF16),
                        pltpu.VMEM((KV_WIDTH, PAST), BF16), pltpu.VMEM((KV_WIDTH, PAST), BF16),
                        pltpu.VMEM((NSA_HEADS * T_PAD, PAST + PAGE), F32),
                        pltpu.VMEM((NSA_HEADS * T_PAD, PAST + PAGE), BF16),
                        pltpu.VMEM((NSA_HEADS * T_PAD, KV_WIDTH), F32), pltpu.VMEM((2, PAST, KV_WIDTH), F32)],
    )
    return pl.pallas_call(
        _nsa_sample_kernel,
        grid_spec=grid_spec,
        out_shape=[jax.ShapeDtypeStruct((b, T_SAMPLE, NSA_WIDTH), F32),
                   jax.ShapeDtypeStruct((b, 2 * KV_WIDTH, w_buf), F32)],
        compiler_params=pltpu.CompilerParams(dimension_semantics=("arbitrary",), vmem_limit_bytes=VMEM_LIMIT_V7X),
        name="nsa_sample",
    )(pt, *([cmp_pages] * N_PAGES), *([slc_pages] * N_PAGES), win, za, zs, *consts)


I_LANE = 3 * NSA_HEADS
F_LANE = I_LANE + M_HEADS
EXT_PAD = 8


def _mlstm_kernel(zu_ref, zs_ref, c0_ref, n0_ref, m0_ref, cprev_ref, convw_ref, convb_ref, wq_ref, wk_ref, wkt_ref,
                  wv_ref, gb_ref, normg_ref, tril_ref, selrow_ref,
                  o_ref, c_out, n_out, m_out, conv_out,
                  ext_ref, zsp_ref, c_sc, n_sc, m_sc, *, rows, rows_pad, n_valid):
    i = pl.program_id(1)
    L = M_CHUNK

    @pl.when(i == 0)
    def _():
        c_sc[...] = c0_ref[0]
        n_sc[...] = n0_ref[0]
        m_sc[...] = m0_ref[0]
        ext_ref[EXT_PAD - (CONV_W - 1):EXT_PAD, :] = cprev_ref[0]

    if rows < rows_pad:
        ext_ref[EXT_PAD:, :] = jnp.zeros((rows_pad, M_WIDTH), F32)
        zsp_ref[...] = jnp.zeros(zsp_ref.shape, F32)
    ext_ref[EXT_PAD:EXT_PAD + rows, :] = zu_ref[0, :, :M_WIDTH]
    zsp_ref[0:rows, :] = zs_ref[0]

    u = ext_ref[EXT_PAD:EXT_PAD + rows_pad, :]
    conv = convb_ref[...] + convw_ref[CONV_W - 1:CONV_W, :] * u
    for j in range(CONV_W - 1):
        lo = EXT_PAD - (CONV_W - 1) + j
        conv = conv + convw_ref[j:j + 1, :] * ext_ref[lo:lo + rows_pad, :]
    uc = (conv * jax.nn.sigmoid(conv)).astype(BF16)
    ub = u.astype(BF16)

    zb = zsp_ref[...] + gb_ref[...]
    bcum = sum(jnp.dot(tril_ref[...], part, preferred_element_type=F32) for part in _split3(jax.nn.log_sigmoid(zb)))
    bcum = pltpu.roll(bcum, LANE - M_HEADS, 1)
    a_col = zb - bcum
    a_row = sum(lax.dot_general(selrow_ref[...], part, (((1,), (1,)), ((), ())), preferred_element_type=F32)
                for part in _split3(a_col))

    tt = lax.broadcasted_iota(jnp.int32, (L, L), 0)
    ss = lax.broadcasted_iota(jnp.int32, (L, L), 1)
    causal = ss <= tt
    tok_col = lax.broadcasted_iota(jnp.int32, (L, 1), 0)
    tok_row = lax.broadcasted_iota(jnp.int32, (1, L), 1)

    heads = []
    for h in range(M_HEADS):
        hs = slice(h * M_HEAD_DIM, (h + 1) * M_HEAD_DIM)
        heads.append(dict(
            hs=hs,
            q=jnp.dot(uc[:, hs], wq_ref[h].astype(BF16), preferred_element_type=F32) * (M_HEAD_DIM ** -0.5),
            k=jnp.dot(uc[:, hs], wk_ref[h].astype(BF16), preferred_element_type=F32),
            kt=lax.dot_general(wkt_ref[h].astype(BF16), uc[:, hs], (((1,), (1,)), ((), ())),
                               preferred_element_type=F32),
            v=jnp.dot(ub[:, hs], wv_ref[h].astype(BF16), preferred_element_type=F32),
            c=c_sc[h], n=n_sc[h:h + 1, :], m=m_sc[h:h + 1, 0:1]))

    for c in range(rows_pad // L):
        r = slice(c * L, (c + 1) * L)
        for h, hd in enumerate(heads):
            hs, q_all, c_st, n_st, m_st = hd['hs'], hd['q'], hd['c'], hd['n'], hd['m']
            q, k, kt, v = q_all[r].astype(BF16), hd['k'][r], hd['kt'][:, r], hd['v'][r].astype(BF16)
            b_col = bcum[r, I_LANE + h:I_LANE + h + 1]
            a_c = a_col[r, I_LANE + h:I_LANE + h + 1]
            a_r = a_row[h:h + 1, r]
            log_d = jnp.where(causal, b_col + a_r, NEG)
            m_col = jnp.maximum(b_col + m_st, jnp.max(log_d, axis=-1, keepdims=True))
            dw = jnp.exp(log_d - m_col)
            w_inter = jnp.exp(b_col + m_st - m_col)
            s = lax.dot_general(q, k.astype(BF16), (((1,), (1,)), ((), ())), preferred_element_type=F32) * dw
            num = (jnp.dot(s.astype(BF16), v, preferred_element_type=F32)
                   + w_inter * jnp.dot(q, c_st.astype(BF16), preferred_element_type=F32))
            den = (jnp.sum(s, axis=-1, keepdims=True)
                   + w_inter * jnp.sum(q_all[r] * n_st, axis=-1, keepdims=True))
            hh = num / jnp.maximum(jnp.abs(den), jnp.exp(-m_col))
            hn = hh * lax.rsqrt(jnp.mean(hh * hh, axis=-1, keepdims=True) + EPS)
            o_pre = zu_ref[0, :, M_WIDTH + h * M_HEAD_DIM:M_WIDTH + (h + 1) * M_HEAD_DIM]
            if rows < rows_pad:
                o_ref[0, :, hs] = jax.nn.sigmoid(o_pre) * hn[0:rows] * normg_ref[:, hs]
            else:
                o_ref[0, r, hs] = jax.nn.sigmoid(o_pre[r]) * hn * normg_ref[:, hs]
            b_last = b_col[n_valid - 1:n_valid, :]
            log_s = jnp.where(tok_col < n_valid, b_last + a_c, NEG)
            m_new = jnp.maximum(b_last + m_st, jnp.max(log_s, axis=0, keepdims=True))
            ws_col = jnp.exp(log_s - m_new)
            ws_row = jnp.where(tok_row < n_valid, jnp.exp(b_last + a_r - m_new), 0.0)
            wc = jnp.exp(b_last + m_st - m_new)
            hd['c'] = wc * c_st + jnp.dot((kt * ws_row).astype(BF16), v, preferred_element_type=F32)
            hd['n'] = wc * n_st + jnp.sum(k * ws_col, axis=0, keepdims=True)
            hd['m'] = m_new
    for h, hd in enumerate(heads):
        c_sc[h] = hd['c']
        n_sc[h:h + 1, :] = hd['n']
        m_sc[h:h + 1, :] = jnp.broadcast_to(hd['m'], (1, LANE))

    tail = ext_ref[EXT_PAD + rows - (CONV_W - 1):EXT_PAD + rows, :]
    ext_ref[EXT_PAD - (CONV_W - 1):EXT_PAD, :] = tail

    @pl.when(i == pl.num_programs(1) - 1)
    def _():
        c_out[0] = c_sc[...]
        n_out[0] = n_sc[...]
        m_out[0] = m_sc[...]
        conv_out[0] = jnp.zeros(conv_out.shape[1:], F32)
        conv_out[0, 0:CONV_W - 1, :] = tail


def _mlstm(zu, zs, state, conv_prev, conv_w, conv_b, m_qkv, gate_b, norm_g):
    b, t, _ = zu.shape
    rows = min(t, 4 * M_CHUNK)
    rows_pad = -(-rows // M_CHUNK) * M_CHUNK
    n_valid = M_CHUNK if rows == rows_pad else rows
    assert t % rows == 0 and (rows == rows_pad or t == rows)
    c0, n0, m0 = state
    n0p = jnp.pad(n0, ((0, 0), (0, 8 - M_HEADS), (0, 0)))
    m0p = jnp.pad(jnp.broadcast_to(m0[:, :, None], (b, M_HEADS, LANE)), ((0, 0), (0, 8 - M_HEADS), (0, 0)))
    gb = jnp.zeros((1, LANE), F32).at[0, I_LANE:I_LANE + 2 * M_HEADS].set(gate_b.reshape(-1))
    idx = jnp.arange(rows_pad)
    tril = ((idx[:, None] >= idx[None, :]) & (idx[:, None] // M_CHUNK == idx[None, :] // M_CHUNK)).astype(BF16)
    selrow = ((jnp.arange(16)[:, None] + I_LANE == jnp.arange(LANE)[None, :])
              & (jnp.arange(16)[:, None] < M_HEADS)).astype(BF16)
    whole = lambda *shape: pl.BlockSpec(shape, lambda bi, i: (0,) * len(shape))
    per_b = lambda *shape: pl.BlockSpec((1,) + shape, lambda bi, i: (bi,) + (0,) * len(shape))
    out, c_new, n_new, m_new, conv_new = pl.pallas_call(
        functools.partial(_mlstm_kernel, rows=rows, rows_pad=rows_pad, n_valid=n_valid),
        grid=(b, t // rows),
        in_specs=[pl.BlockSpec((1, rows, 2 * M_WIDTH), lambda bi, i: (bi, i, 0)),
                  pl.BlockSpec((1, rows, LANE), lambda bi, i: (bi, i, 0)),
                  per_b(M_HEADS, M_HEAD_DIM, M_HEAD_DIM), per_b(8, M_HEAD_DIM), per_b(8, LANE),
                  per_b(CONV_W - 1, M_WIDTH),
                  whole(CONV_W, M_WIDTH), whole(1, M_WIDTH),
                  whole(M_HEADS, M_HEAD_DIM, M_HEAD_DIM), whole(M_HEADS, M_HEAD_DIM, M_HEAD_DIM),
                  whole(M_HEADS, M_HEAD_DIM, M_HEAD_DIM), whole(M_HEADS, M_HEAD_DIM, M_HEAD_DIM),
                  whole(1, LANE), whole(1, M_WIDTH), whole(rows_pad, rows_pad), whole(16, LANE)],
        out_specs=[pl.BlockSpec((1, rows, M_WIDTH), lambda bi, i: (bi, i, 0)),
                   per_b(M_HEADS, M_HEAD_DIM, M_HEAD_DIM), per_b(8, M_HEAD_DIM), per_b(8, LANE),
                   per_b(8, M_WIDTH)],
        out_shape=[jax.ShapeDtypeStruct((b, t, M_WIDTH), F32),
                   jax.ShapeDtypeStruct((b, M_HEADS, M_HEAD_DIM, M_HEAD_DIM), F32),
                   jax.ShapeDtypeStruct((b, 8, M_HEAD_DIM), F32),
                   jax.ShapeDtypeStruct((b, 8, LANE), F32),
                   jax.ShapeDtypeStruct((b, 8, M_WIDTH), F32)],
        scratch_shapes=[pltpu.VMEM((EXT_PAD + rows_pad, M_WIDTH), F32), pltpu.VMEM((rows_pad, LANE), F32),
                        pltpu.VMEM((M_HEADS, M_HEAD_DIM, M_HEAD_DIM), F32), pltpu.VMEM((8, M_HEAD_DIM), F32),
                        pltpu.VMEM((8, LANE), F32)],
        compiler_params=pltpu.CompilerParams(dimension_semantics=("arbitrary", "arbitrary"),
                                             vmem_limit_bytes=VMEM_LIMIT_V7X),
        name="mlstm",
    )(zu, zs, c0, n0p, m0p, conv_prev, conv_w, conv_b.reshape(1, M_WIDTH), m_qkv[0], m_qkv[1],
      m_qkv[1].transpose(0, 2, 1),
      m_qkv[2], gb, norm_g.reshape(1, M_WIDTH), tril, selrow)
    return out, (c_new, n_new[:, :M_HEADS], m_new[:, :M_HEADS, 0], conv_new[:, :CONV_W - 1])


def rel_bucket(dist):
    d = jnp.maximum(dist, 0)
    exact = REL_BUCKETS // 2
    log_part = exact + (jnp.log(jnp.maximum(d, 1).astype(F32) / exact)
                        / math.log(REL_MAX_DIST / exact) * (REL_BUCKETS - exact)).astype(jnp.int32)
    return jnp.where(d < exact, d, jnp.minimum(log_part, REL_BUCKETS - 1))


def _split_w_in(w_in):
    offs = np.cumsum((0,) + IN_WIDTHS)
    w_a = w_in[:, :offs[7]]
    w_small = jnp.concatenate([w_in[:, offs[7]:offs[8]], w_in[:, offs[10]:offs[11]]], axis=1)
    w_small = jnp.pad(w_small, ((0, 0), (0, LANE - w_small.shape[1])))
    w_u = w_in[:, offs[8]:offs[10]]
    w_g = w_in[:, offs[11]:]
    return w_a, w_small, w_u, w_g


def mixer(x, norm_g, prm, past):
    b, t, _ = x.shape
    x2 = x.reshape(b * t, D_MODEL)
    z_a, z_s, z_u, z_g = _proj(x2, norm_g, [w.astype(BF16) for w in _split_w_in(prm['w_in'])])
    z_a, z_s, z_u = (z.reshape(b, t, -1) for z in (z_a, z_s, z_u))
    new_cmp, new_slc, win_rows = (z_a[..., NSA_WIDTH + 2 * KV_WIDTH * i:NSA_WIDTH + 2 * KV_WIDTH * (i + 1)]
                                  .reshape(b, t, 2, NSA_KV_HEADS, HEAD_DIM) for i in range(3))
    if past is None:
        o_nsa = _nsa_prompt_from_proj(z_a.reshape(b * t, -1), z_s.reshape(b * t, -1), b, t, prm['prompt_tables'],
                                      prm['cmp_pe'], prm['cmp_w1'], prm['cmp_w2'])
        new_win = win_rows[:, t - min(WINDOW, t):]
        conv_prev = jnp.zeros((b, CONV_W - 1, M_WIDTH), F32)
        m_state = (jnp.zeros((b, M_HEADS, M_HEAD_DIM, M_HEAD_DIM), F32),
                   jnp.zeros((b, M_HEADS, M_HEAD_DIM), F32),
                   jnp.zeros((b, M_HEADS), F32))
    else:
        layer = past['layer']
        assert t == T_SAMPLE and past['page_table'].shape[1] == N_PAGES and past['cmp'].shape[2] == PAGE
        o_nsa, win_t = _nsa_sample(z_a, z_s, past['cmp'], past['slc'], past['win'], past['page_table'], layer,
                                   prm['sample_tables'], prm['cmp_pe'], prm['cmp_w1'], prm['cmp_w2'])
        o_nsa = o_nsa.reshape(b * t, NSA_WIDTH)
        new_win = win_t.reshape(b, 2, NSA_KV_HEADS, HEAD_DIM, -1).transpose(0, 4, 1, 2, 3)
        conv_prev = past['conv']
        m_state = (past['C'].astype(F32), past['n'].astype(F32), past['m'].astype(F32))
    o_m, (c_new, n_new, m_new, conv_new) = _mlstm(z_u, z_s, m_state, conv_prev, prm['conv_w'], prm['conv_b'],
                                                  prm['m_qkv'], prm['gate_b'], prm['m_norm'])
    y = _merge_out(x2, o_nsa, o_m.reshape(b * t, M_WIDTH), z_g, prm['w_up_a'].astype(BF16),
                   prm['w_up_b'].astype(BF16), prm['w_out'].astype(BF16)).reshape(b, t, D_MODEL)
    return y, (new_cmp, new_slc, new_win, c_new, n_new, m_new, conv_new)


def _channel_mixer(xp, xs, g, l, ffn_w1, ffn_w3, ffn_w2, moe_router, moe_w1, moe_w3, moe_w2):
    d = xp.shape[-1]
    i = l // 2
    if l % 2 == 0:
        return tuple(_ffn_dense(x.reshape(-1, d), g, ffn_w1[i], ffn_w3[i], ffn_w2[i]).reshape(x.shape)
                     for x in (xp, xs))
    x2 = jnp.concatenate([xp.reshape(-1, d), xs.reshape(-1, d)], axis=0)
    y = _moe(x2, g, moe_router[i], moe_w1[i], moe_w3[i], moe_w2[i])
    n_p = xp.size // d
    return y[:n_p].reshape(xp.shape), y[n_p:].reshape(xs.shape)


def _final_norm(x, g):
    xf = x.astype(F32)
    return xf * lax.rsqrt(jnp.mean(xf * xf, axis=-1, keepdims=True) + EPS) * g


def kernel(x_prompt, x_sample, cache_cmp_kv, cache_slc_kv, cache_win_kv, state_mlstm_C, state_mlstm_n,
           state_mlstm_m, state_mlstm_conv, page_table, rel_bias_table, norm_mix, norm_ffn, norm_final,
           w_in, cmp_pe, cmp_w1, cmp_w2, m_conv_w, m_conv_b, m_qkv, m_gate_bias, m_norm, w_up_a, w_up_b,
           w_out, ffn_w1, ffn_w3, ffn_w2, moe_router, moe_w1, moe_w3, moe_w2):
    xp, xs = x_prompt, x_sample
    prompt_states, sample_states = [], []
    prompt_tables = _prompt_bias_tables(rel_bias_table)
    sample_tables = _sample_bias_tables(rel_bias_table)
    for l in range(DEPTH):
        prm = {'prompt_tables': prompt_tables, 'sample_tables': sample_tables, 'w_in': w_in[l], 'cmp_pe': cmp_pe[l],
               'cmp_w1': cmp_w1[l], 'cmp_w2': cmp_w2[l], 'conv_w': m_conv_w[l], 'conv_b': m_conv_b[l],
               'm_qkv': m_qkv[l], 'gate_b': m_gate_bias[l], 'm_norm': m_norm[l], 'w_up_a': w_up_a[l],
               'w_up_b': w_up_b[l], 'w_out': w_out[l]}
        past = {'cmp': cache_cmp_kv, 'slc': cache_slc_kv, 'win': cache_win_kv, 'layer': l,
                'C': state_mlstm_C[l], 'n': state_mlstm_n[l], 'm': state_mlstm_m[l],
                'conv': state_mlstm_conv[l], 'page_table': page_table}
        xp, st_p = mixer(xp, norm_mix[l], prm, None)
        xs, st_s = mixer(xs, norm_mix[l], prm, past)
        xp, xs = _channel_mixer(xp, xs, norm_ffn[l], l, ffn_w1, ffn_w3, ffn_w2, moe_router, moe_w1, moe_w3, moe_w2)
        prompt_states.append(st_p)
        sample_states.append(st_s)
    y_prompt = _final_norm(xp, norm_final)
    y_sample = _final_norm(xs, norm_final)
    ps = [jnp.stack([s[i] for s in prompt_states]) for i in range(7)]
    ss = [jnp.stack([s[i] for s in sample_states]) for i in range(7)]
    return (y_prompt, y_sample, ps[0], ps[1], ps[2], ps[3], ps[4], ps[5], ps[6],
            ss[0], ss[1], ss[2], ss[3], ss[4], ss[5], ss[6])
```

```python
import functools
import math

import jax
import jax.numpy as jnp
import numpy as np
from jax import lax
from jax.experimental import pallas as pl
from jax.experimental.pallas import tpu as pltpu

D_MODEL = 1024
DEPTH = 2
NSA_HEADS = 8
NSA_KV_HEADS = 2
NSA_GROUP = NSA_HEADS // NSA_KV_HEADS
HEAD_DIM = 64
NSA_WIDTH = NSA_HEADS * HEAD_DIM
KV_WIDTH = NSA_KV_HEADS * HEAD_DIM
CMP_BLOCK = 32
CMP_STRIDE = 16
CMP_HIDDEN = 128
SEL_BLOCK = 64
SEL_TOP_N = 16
SEL_FORCE = 1e4
WINDOW = 512
WIN_Q_BLOCK = 128
REL_BUCKETS = 32
REL_MAX_DIST = 128
M_HEADS = 4
M_HEAD_DIM = 128
M_WIDTH = M_HEADS * M_HEAD_DIM
CONV_W = 4
M_CHUNK = 64
D_FF = 2816
N_EXPERTS = 8
TOP_K = 2
EPS = 1e-6
IN_WIDTHS = (NSA_WIDTH, KV_WIDTH, KV_WIDTH, KV_WIDTH, KV_WIDTH, KV_WIDTH, KV_WIDTH, 3 * NSA_HEADS, M_WIDTH, M_WIDTH,
             2 * M_HEADS, D_MODEL, D_MODEL)

VMEM_LIMIT_V7X = 52 * 1024 * 1024
LANE = 128

F32 = jnp.float32
BF16 = jnp.bfloat16


def _split3(x):
    hi = x.astype(BF16)
    r1 = x - hi.astype(F32)
    mid = r1.astype(BF16)
    lo = (r1 - mid.astype(F32)).astype(BF16)
    return hi, mid, lo


def _pick_tile(n, cands):
    for c in cands:
        if n % c == 0:
            return c
    return n


def _mm_kernel(*refs, norm, has_res):
    x_ref, g_ref, w_ref = refs[:3]
    res_ref = refs[3] if has_res else None
    o_ref, xs_ref = refs[-2], refs[-1]

    @pl.when(pl.program_id(1) == 0)
    def _():
        x = x_ref[...]
        if norm:
            x = x * lax.rsqrt(jnp.mean(x * x, axis=-1, keepdims=True) + EPS) * g_ref[...]
        xs_ref[...] = x.astype(BF16)

    acc = jnp.dot(xs_ref[...], w_ref[...].astype(BF16), preferred_element_type=F32)
    if has_res:
        acc = acc + res_ref[...]
    o_ref[...] = acc


def _mm(x, w, g=None, res=None):
    m, k = x.shape
    n = w.shape[1]
    tm = _pick_tile(m, (1024, 512, 256, 128))
    tn = _pick_tile(n, (512, 256, 128))
    norm = g is not None
    gg = (g if norm else jnp.ones((k,), F32)).reshape(1, k)
    in_specs = [pl.BlockSpec((tm, k), lambda i, j: (i, 0)),
                pl.BlockSpec((1, k), lambda i, j: (0, 0)),
                pl.BlockSpec((k, tn), lambda i, j: (0, j))]
    args = [x, gg, w]
    if res is not None:
        in_specs.append(pl.BlockSpec((tm, tn), lambda i, j: (i, j)))
        args.append(res)
    return pl.pallas_call(
        functools.partial(_mm_kernel, norm=norm, has_res=res is not None),
        grid=(m // tm, n // tn),
        in_specs=in_specs,
        out_specs=pl.BlockSpec((tm, tn), lambda i, j: (i, j)),
        out_shape=jax.ShapeDtypeStruct((m, n), F32),
        scratch_shapes=[pltpu.VMEM((tm, k), BF16)],
        compiler_params=pltpu.CompilerParams(dimension_semantics=("arbitrary", "arbitrary"),
                                             vmem_limit_bytes=VMEM_LIMIT_V7X),
        name="mm",
    )(*args)


def _proj_kernel(x_ref, g_ref, *refs):
    n_out = len(refs) // 2
    x = x_ref[...]
    xn = (x * lax.rsqrt(jnp.mean(x * x, axis=-1, keepdims=True) + EPS) * g_ref[...]).astype(BF16)
    for w_ref, o_ref in zip(refs[:n_out], refs[n_out:]):
        o_ref[...] = jnp.dot(xn, w_ref[...], preferred_element_type=F32)


def _proj(x, g, weights):
    m, k = x.shape
    tm = _pick_tile(m, (256, 128))
    row = lambda n: pl.BlockSpec((tm, n), lambda i: (i, 0))
    whole = lambda a: pl.BlockSpec(a.shape, lambda i: (0,) * a.ndim)
    return pl.pallas_call(
        _proj_kernel,
        grid=(m // tm,),
        in_specs=[row(k), whole(g.reshape(1, k))] + [whole(w) for w in weights],
        out_specs=[row(w.shape[1]) for w in weights],
        out_shape=[jax.ShapeDtypeStruct((m, w.shape[1]), F32) for w in weights],
        compiler_params=pltpu.CompilerParams(dimension_semantics=("arbitrary",), vmem_limit_bytes=VMEM_LIMIT_V7X),
        name="proj",
    )(x, g.reshape(1, k), *weights)


def _merge_out_kernel(x_ref, oa_ref, ob_ref, zg_ref, wa_ref, wb_ref, wo_ref, o_ref):
    d = x_ref.shape[-1]
    up_a = jnp.dot(oa_ref[...].astype(BF16), wa_ref[...], preferred_element_type=F32)
    up_b = jnp.dot(ob_ref[...].astype(BF16), wb_ref[...], preferred_element_type=F32)
    merged = jax.nn.sigmoid(zg_ref[:, :d]) * up_a + jax.nn.sigmoid(zg_ref[:, d:]) * up_b
    o_ref[...] = x_ref[...] + jnp.dot(merged.astype(BF16), wo_ref[...], preferred_element_type=F32)


def _merge_out(x, o_a, o_b, z_g, w_up_a, w_up_b, w_out):
    m, d = x.shape
    tm = _pick_tile(m, (512, 256, 128))
    row = lambda a: pl.BlockSpec((tm, a.shape[1]), lambda i: (i, 0))
    whole = lambda a: pl.BlockSpec(a.shape, lambda i: (0,) * a.ndim)
    return pl.pallas_call(
        _merge_out_kernel,
        grid=(m // tm,),
        in_specs=[row(x), row(o_a), row(o_b), row(z_g), whole(w_up_a), whole(w_up_b), whole(w_out)],
        out_specs=row(x),
        out_shape=jax.ShapeDtypeStruct((m, d), F32),
        compiler_params=pltpu.CompilerParams(dimension_semantics=("arbitrary",), vmem_limit_bytes=VMEM_LIMIT_V7X),
        name="merge_out",
    )(x, o_a, o_b, z_g, w_up_a, w_up_b, w_out)


def _ffn_body(x_ref, g_ref, w1_ref, w3_ref, w2_ref, o_ref, xs_ref, acc_ref, *, residual, grouped, active=None):
    j = pl.program_id(1)

    @pl.when(j == 0)
    def _():
        x = x_ref[...]
        xn = x * lax.rsqrt(jnp.mean(x * x, axis=-1, keepdims=True) + EPS) * g_ref[...]
        xs_ref[...] = xn.astype(BF16)
        acc_ref[...] = jnp.zeros_like(acc_ref)

    def accumulate():
        xs = xs_ref[...]
        w1 = w1_ref[0] if grouped else w1_ref[...]
        w3 = w3_ref[0] if grouped else w3_ref[...]
        w2 = w2_ref[0] if grouped else w2_ref[...]
        a = jnp.dot(xs, w1.astype(BF16), preferred_element_type=F32)
        b = jnp.dot(xs, w3.astype(BF16), preferred_element_type=F32)
        h = (a * jax.nn.sigmoid(a) * b).astype(BF16)
        acc_ref[...] += jnp.dot(h, w2.astype(BF16), preferred_element_type=F32)

    if active is None:
        accumulate()
    else:
        pl.when(active)(accumulate)

    @pl.when(j == pl.num_programs(1) - 1)
    def _():
        if residual:
            o_ref[...] = x_ref[...] + acc_ref[...]
        else:
            o_ref[...] = acc_ref[...]


def _ffn_dense_kernel(x_ref, g_ref, w1_ref, w3_ref, w2_ref, o_ref, xs_ref, acc_ref):
    _ffn_body(x_ref, g_ref, w1_ref, w3_ref, w2_ref, o_ref, xs_ref, acc_ref, residual=True, grouped=False)


def _ffn_grouped_kernel(be_ref, nu_ref, x_ref, g_ref, w1_ref, w3_ref, w2_ref, o_ref, xs_ref, acc_ref):
    del be_ref
    _ffn_body(x_ref, g_ref, w1_ref, w3_ref, w2_ref, o_ref, xs_ref, acc_ref, residual=False, grouped=True,
              active=pl.program_id(0) < nu_ref[0])


def _ffn_dense(x, g, w1, w3, w2):
    m, d = x.shape
    f = w1.shape[1]
    tm = _pick_tile(m, (1024, 512, 256, 128))
    tf = _pick_tile(f, (256, 128))
    return pl.pallas_call(
        _ffn_dense_kernel,
        grid=(m // tm, f // tf),
        in_specs=[pl.BlockSpec((tm, d), lambda i, j: (i, 0)),
                  pl.BlockSpec((1, d), lambda i, j: (0, 0)),
                  pl.BlockSpec((d, tf), lambda i, j: (0, j)),
                  pl.BlockSpec((d, tf), lambda i, j: (0, j)),
                  pl.BlockSpec((tf, d), lambda i, j: (j, 0))],
        out_specs=pl.BlockSpec((tm, d), lambda i, j: (i, 0)),
        out_shape=jax.ShapeDtypeStruct((m, d), F32),
        scratch_shapes=[pltpu.VMEM((tm, d), BF16), pltpu.VMEM((tm, d), F32)],
        compiler_params=pltpu.CompilerParams(dimension_semantics=("arbitrary", "arbitrary"),
                                             vmem_limit_bytes=VMEM_LIMIT_V7X),
        name="ffn_dense",
    )(x, g.reshape(1, d), w1, w3, w2)


def _ffn_grouped(xd, blk_e, n_used, g, w1, w3, w2, tm):
    rows, d = xd.shape
    f = w1.shape[2]
    tf = _pick_tile(f, (256, 128))
    n_f = f // tf
    col = lambda i, j, nu: jnp.where(i < nu[0], j, n_f - 1)
    grid_spec = pltpu.PrefetchScalarGridSpec(
        num_scalar_prefetch=2,
        grid=(rows // tm, n_f),
        in_specs=[pl.BlockSpec((tm, d), lambda i, j, be, nu: (i, 0)),
                  pl.BlockSpec((1, d), lambda i, j, be, nu: (0, 0)),
                  pl.BlockSpec((1, d, tf), lambda i, j, be, nu: (be[i], 0, col(i, j, nu))),
                  pl.BlockSpec((1, d, tf), lambda i, j, be, nu: (be[i], 0, col(i, j, nu))),
                  pl.BlockSpec((1, tf, d), lambda i, j, be, nu: (be[i], col(i, j, nu), 0))],
        out_specs=pl.BlockSpec((tm, d), lambda i, j, be, nu: (i, 0)),
        scratch_shapes=[pltpu.VMEM((tm, d), BF16), pltpu.VMEM((tm, d), F32)],
    )
    return pl.pallas_call(
        _ffn_grouped_kernel,
        grid_spec=grid_spec,
        out_shape=jax.ShapeDtypeStruct((rows, d), F32),
        compiler_params=pltpu.CompilerParams(dimension_semantics=("arbitrary", "arbitrary"),
                                             vmem_limit_bytes=VMEM_LIMIT_V7X),
        name="ffn_grouped",
    )(blk_e, n_used, xd, g.reshape(1, d), w1, w3, w2)


def _moe(x, g, router, w1, w3, w2):
    n, d = x.shape
    tm = 1024 if n >= 8192 else 128
    router_p = jnp.pad(router, ((0, 0), (0, LANE - N_EXPERTS)))
    logits = _mm(x, router_p, g=g)[:, :N_EXPERTS]
    top_val, top_idx = lax.top_k(logits, TOP_K)
    gate = jax.nn.softmax(top_val, axis=-1).reshape(-1)
    e_flat = top_idx.reshape(-1)
    n_asg = n * TOP_K
    order = jnp.argsort(e_flat)
    onehot = (e_flat[:, None] == jnp.arange(N_EXPERTS, dtype=e_flat.dtype)[None, :]).astype(jnp.int32)
    csum = jnp.cumsum(onehot, axis=0)
    counts = csum[-1]
    rank = jnp.sum(onehot * csum, axis=1) - 1
    padded = (counts + tm - 1) // tm * tm
    pad_end = jnp.cumsum(padded)
    pad_start = pad_end - padded
    start = jnp.cumsum(counts) - counts
    pos = (pad_start[e_flat] + rank).astype(jnp.int32)
    n_blocks = n_asg // tm + N_EXPERTS
    blk_e = jnp.minimum(jnp.searchsorted(pad_end, jnp.arange(n_blocks) * tm, side='right'),
                        N_EXPERTS - 1).astype(jnp.int32)
    n_used = (pad_end[-1] // tm).astype(jnp.int32)
    blk_e = jnp.where(jnp.arange(n_blocks) < n_used, blk_e, blk_e[n_used - 1])
    e_row = jnp.repeat(blk_e, tm)
    in_expert = jnp.arange(n_blocks * tm) - pad_start[e_row]
    src_asg = order[jnp.clip(start[e_row] + in_expert, 0, n_asg - 1)]
    src_tok = jnp.where(in_expert < counts[e_row], src_asg // TOP_K, 0).astype(jnp.int32)
    yd = _ffn_grouped(x[src_tok], blk_e, n_used.reshape(1), g, w1, w3, w2, tm)
    pos, gate = pos.reshape(n, TOP_K), gate.reshape(n, TOP_K)
    return x + sum(yd[pos[:, k]] * gate[:, k:k + 1] for k in range(TOP_K))


N_CHUNK = 128
CHUNK_W = CMP_STRIDE * HEAD_DIM


def _compress_kernel(x_ref, pe_ref, w1_ref, w2_ref, o_ref):
    c = x_ref[0]
    lo = jnp.dot((c + pe_ref[0:1]).astype(BF16), w1_ref[0].astype(BF16), preferred_element_type=F32)
    hi = jnp.dot((c + pe_ref[1:2]).astype(BF16), w1_ref[1].astype(BF16), preferred_element_type=F32)
    hid = jax.nn.gelu(lo + pltpu.roll(hi, N_CHUNK - 1, 0))
    out = jnp.dot(hid.astype(BF16), w2_ref[...].astype(BF16), preferred_element_type=F32)
    row = lax.broadcasted_iota(jnp.int32, out.shape, 0)
    o_ref[0] = jnp.where(row < N_CHUNK - 1, out, 0.0)


def _compress(xc, pe, w1, w2):
    nb = xc.shape[0]
    hidden = w1.shape[-1]
    return pl.pallas_call(
        _compress_kernel,
        grid=(nb,),
        in_specs=[pl.BlockSpec((1, N_CHUNK, CHUNK_W), lambda i: (i, 0, 0)),
                  pl.BlockSpec((2, CHUNK_W), lambda i: (0, 0)),
                  pl.BlockSpec((2, CHUNK_W, hidden), lambda i: (0, 0, 0)),
                  pl.BlockSpec((hidden, HEAD_DIM), lambda i: (0, 0))],
        out_specs=pl.BlockSpec((1, N_CHUNK, HEAD_DIM), lambda i: (i, 0, 0)),
        out_shape=jax.ShapeDtypeStruct((nb, N_CHUNK, HEAD_DIM), F32),
        compiler_params=pltpu.CompilerParams(dimension_semantics=("arbitrary",), vmem_limit_bytes=VMEM_LIMIT_V7X),
        name="compress",
    )(xc, pe.reshape(2, CHUNK_W), w1.reshape(2, CHUNK_W, hidden), w2)


def _compress_heads(kv, pe, w1, w2):
    b = kv.shape[0]
    xc = kv.reshape(b, N_CHUNK, CMP_STRIDE, NSA_KV_HEADS, HEAD_DIM).transpose(0, 3, 1, 2, 4)
    out = _compress(xc.reshape(b * NSA_KV_HEADS, N_CHUNK, CHUNK_W), pe, w1, w2)
    return out.reshape(b, NSA_KV_HEADS, N_CHUNK, HEAD_DIM)


TQ = 128
TK = 128
NEG = -1e30
N_BIAS_TILES = WINDOW // TK + 1
MASKED_TILE = N_BIAS_TILES
SEL_SUB = 4


def _nsa_prompt_kernel(q_ref, kct_ref, vcc_ref, kst_ref, vs_ref, kwt_ref, vw_ref, bcmp_ref, btile_ref, gate_ref,
                       ov_ref, exp_ref, o_ref, selneg_ref, m_ref, l_ref, acc_ref):
    i = pl.program_id(2)
    rows = NSA_GROUP * TQ
    qt = (q_ref[...] * (HEAD_DIM ** -0.5)).astype(BF16)
    q = jnp.concatenate([qt[:, a * HEAD_DIM:(a + 1) * HEAD_DIM] for a in range(NSA_GROUP)], axis=0)

    s = jnp.dot(q, kct_ref[0, 0], preferred_element_type=F32) + bcmp_ref[0].reshape(rows, N_CHUNK)
    m = jnp.max(s, axis=-1, keepdims=True)
    e = jnp.where(s > 0.1 * NEG, jnp.exp(s - m), 0.0)
    p = e / jnp.maximum(jnp.sum(e, axis=-1, keepdims=True), 1e-30)
    o_cmp = jnp.dot(p.astype(BF16), vcc_ref[0, 0], preferred_element_type=F32)

    p_sum = p[0:TQ] + p[TQ:2 * TQ] + p[2 * TQ:3 * TQ] + p[3 * TQ:4 * TQ]
    n_sel = T_PROMPT // SEL_BLOCK
    imp_t = sum(lax.dot_general(ov_ref[...], part, (((1,), (1,)), ((), ())), preferred_element_type=F32)
                for part in _split3(p_sum))[0:n_sel]
    blk = lax.broadcasted_iota(jnp.int32, (n_sel, TQ), 0)
    cur = (lax.broadcasted_iota(jnp.int32, (n_sel, TQ), 1) + i * TQ) // SEL_BLOCK
    valid = blk <= cur
    forced = (blk == 0) | (blk == cur) | (blk == cur - 1)
    score = jnp.where(valid, imp_t + jnp.where(forced, SEL_FORCE, 0.0), -1.0)
    rank = jnp.zeros((n_sel, TQ), F32)
    for c in range(n_sel):
        sc = score[c:c + 1, :]
        beats = (sc > score) | ((sc == score) & (blk > c))
        rank = rank + jnp.where(beats, 1.0, 0.0)
    sel_t = jnp.where(valid & (rank < SEL_TOP_N), 1.0, 0.0)
    sel = jnp.concatenate([sel_t, jnp.zeros((LANE - n_sel, TQ), F32)], axis=0).T.astype(BF16)
    sel_keys = jnp.dot(sel, exp_ref[...], preferred_element_type=F32)
    selneg_ref[...] = (sel_keys - 1.0) * (-NEG)

    halves = [slice(hf * NSA_GROUP // 2, (hf + 1) * NSA_GROUP // 2) for hf in range(2)]
    half_rows = [slice(gs.start * TQ, gs.stop * TQ) for gs in halves]

    def masked_scores(kt_ref, first_tile, n_sub, selected, gs, rs):
        k0 = pl.multiple_of(first_tile * TK, TK)
        s_all = jnp.dot(q[rs], kt_ref[0, 0, :, pl.ds(k0, n_sub * TK)], preferred_element_type=F32)
        pieces = []
        for u in range(n_sub):
            d0 = i - (first_tile + u)
            idx = jnp.where(d0 < 0, MASKED_TILE, jnp.minimum(d0, 2) if selected else d0)
            piece = s_all[:, u * TK:(u + 1) * TK].reshape(gs.stop - gs.start, TQ, TK) + btile_ref[0, idx, gs]
            if selected:
                piece = piece + selneg_ref[:, pl.ds(pl.multiple_of(k0 + u * TK, TK), TK)][None]
            pieces.append(piece.reshape(rs.stop - rs.start, TK))
        return pieces, k0

    def row_max(pieces):
        return jnp.max(functools.reduce(jnp.maximum, pieces), axis=-1, keepdims=True)

    def row_sum(pieces):
        return jnp.sum(functools.reduce(jnp.add, pieces), axis=-1, keepdims=True)

    m_ref[...] = jnp.full(m_ref.shape, -jnp.inf, F32)
    l_ref[...] = jnp.zeros(l_ref.shape, F32)
    acc_ref[...] = jnp.zeros(acc_ref.shape, F32)

    def sel_group(gi, carry):
        for gs, rs in zip(halves, half_rows):
            pieces, k0 = masked_scores(kst_ref, gi * SEL_SUB, SEL_SUB, True, gs, rs)
            m_old = m_ref[rs, :]
            m_new = jnp.maximum(m_old, row_max(pieces))
            pt = [jnp.exp(piece - m_new) for piece in pieces]
            alpha = jnp.exp(m_old - m_new)
            l_ref[rs, :] = alpha * l_ref[rs, :] + row_sum(pt)
            pv = jnp.dot(jnp.concatenate([x.astype(BF16) for x in pt], axis=1),
                         vs_ref[0, 0, pl.ds(k0, SEL_SUB * TK), :], preferred_element_type=F32)
            acc_ref[rs, :] = alpha * acc_ref[rs, :] + pv
            m_ref[rs, :] = m_new
        return carry

    lax.fori_loop(0, i // SEL_SUB + 1, sel_group, 0)
    o_sel = acc_ref[...] / l_ref[...]

    o_win = []
    for gs, rs in zip(halves, half_rows):
        pieces, k0 = masked_scores(kwt_ref, jnp.maximum(i - (N_BIAS_TILES - 1), 0), N_BIAS_TILES, False, gs, rs)
        m_win = row_max(pieces)
        pt = [jnp.exp(piece - m_win) for piece in pieces]
        o_win.append(jnp.dot(jnp.concatenate([x.astype(BF16) for x in pt], axis=1),
                             vw_ref[0, 0, pl.ds(k0, N_BIAS_TILES * TK), :], preferred_element_type=F32) / row_sum(pt))
    o_win = jnp.concatenate(o_win, axis=0)

    g = jax.nn.sigmoid(gate_ref[0, 0])
    pieces = []
    for a in range(NSA_GROUP):
        r = slice(a * TQ, (a + 1) * TQ)
        pieces.append(g[:, 3 * a:3 * a + 1] * o_cmp[r] + g[:, 3 * a + 1:3 * a + 2] * o_sel[r]
                      + g[:, 3 * a + 2:3 * a + 3] * o_win[r])
    o_ref[...] = jnp.concatenate(pieces, axis=1)


T_PROMPT = 2048


def _bias_lookup(table, dist):
    oh = jax.nn.one_hot(rel_bucket(dist), REL_BUCKETS, dtype=F32)
    return jnp.einsum('...r,rh->...h', oh, table, precision=lax.Precision.HIGHEST)


def _prompt_bias_tables(table):
    t = T_PROMPT
    q_pos = jnp.arange(t)
    block_end = jnp.arange(N_CHUNK) * CMP_STRIDE + (CMP_BLOCK - 1)
    dist = q_pos[:, None] - block_end[None, :]
    ok = (dist >= 0) & (jnp.arange(N_CHUNK)[None, :] < N_CHUNK - 1)
    bcmp = jnp.where(ok[..., None], _bias_lookup(table, dist), NEG)
    bcmp = bcmp.reshape(t, N_CHUNK, NSA_KV_HEADS, NSA_GROUP).transpose(2, 3, 0, 1)
    d0 = jnp.arange(N_BIAS_TILES)[:, None, None]
    dist = d0 * TK + jnp.arange(TQ)[None, :, None] - jnp.arange(TK)[None, None, :]
    ok = (dist >= 0) & (dist < WINDOW)
    bt = jnp.where(ok[..., None], _bias_lookup(table, dist), NEG)
    bt = bt.reshape(N_BIAS_TILES, TQ, TK, NSA_KV_HEADS, NSA_GROUP).transpose(3, 0, 4, 1, 2)
    bt = jnp.concatenate([bt, jnp.full((NSA_KV_HEADS, 1, NSA_GROUP, TQ, TK), NEG, F32)], axis=1)
    c0 = jnp.arange(N_CHUNK) * CMP_STRIDE
    s0 = jnp.arange(LANE) * SEL_BLOCK
    ov = jnp.clip(jnp.minimum(c0[:, None] + CMP_BLOCK, s0[None, :] + SEL_BLOCK)
                  - jnp.maximum(c0[:, None], s0[None, :]), 0, CMP_BLOCK).astype(F32) / CMP_BLOCK
    ov = jnp.where((jnp.arange(N_CHUNK)[:, None] < N_CHUNK - 1) & (jnp.arange(LANE)[None, :] < t // SEL_BLOCK), ov, 0.0)
    expand = (jnp.arange(LANE)[:, None] == (jnp.arange(t) // SEL_BLOCK)[None, :]).astype(BF16)
    return bcmp, bt, ov.T.astype(BF16), expand


def _nsa_prompt(z_a, kct, vcc, kst, vs, kwt, vw, gates, tables):
    bcmp, bt, ov, expand = tables
    b, _, _, t = kst.shape
    rows = NSA_GROUP * TQ
    n_q = t // TQ
    q_cols = NSA_GROUP * HEAD_DIM
    per_bh = lambda *blk: pl.BlockSpec((1, 1) + blk, lambda bi, h, i: (bi, h) + (0,) * len(blk))
    return pl.pallas_call(
        _nsa_prompt_kernel,
        grid=(b, NSA_KV_HEADS, n_q),
        in_specs=[pl.BlockSpec((TQ, q_cols), lambda bi, h, i: (bi * n_q + i, h)),
                  per_bh(HEAD_DIM, N_CHUNK), per_bh(N_CHUNK, HEAD_DIM),
                  per_bh(HEAD_DIM, t), per_bh(t, HEAD_DIM), per_bh(HEAD_DIM, t), per_bh(t, HEAD_DIM),
                  pl.BlockSpec((1, NSA_GROUP, TQ, N_CHUNK), lambda bi, h, i: (h, 0, i, 0)),
                  pl.BlockSpec((1, N_BIAS_TILES + 1, NSA_GROUP, TQ, TK), lambda bi, h, i: (h, 0, 0, 0, 0)),
                  pl.BlockSpec((1, 1, TQ, 3 * NSA_GROUP), lambda bi, h, i: (bi, h, i, 0)),
                  pl.BlockSpec((N_CHUNK, LANE), lambda bi, h, i: (0, 0)),
                  pl.BlockSpec((LANE, t), lambda bi, h, i: (0, 0))],
        out_specs=pl.BlockSpec((TQ, q_cols), lambda bi, h, i: (bi * n_q + i, h)),
        out_shape=jax.ShapeDtypeStruct((b * t, NSA_WIDTH), F32),
        scratch_shapes=[pltpu.VMEM((TQ, t), F32), pltpu.VMEM((rows, 1), F32), pltpu.VMEM((rows, 1), F32),
                        pltpu.VMEM((rows, HEAD_DIM), F32)],
        compiler_params=pltpu.CompilerParams(dimension_semantics=("arbitrary", "arbitrary", "arbitrary"),
                                             vmem_limit_bytes=VMEM_LIMIT_V7X),
        name="nsa_prompt",
    )(z_a, kct, vcc, kst, vs, kwt, vw, bcmp, bt, gates, ov, expand)


def _heads_t(x, b, t):
    return x.astype(BF16).reshape(b, t, NSA_KV_HEADS, HEAD_DIM).transpose(0, 2, 3, 1)


def _heads(x, b, t):
    return x.astype(BF16).reshape(b, t, NSA_KV_HEADS, HEAD_DIM).transpose(0, 2, 1, 3)


def _nsa_prompt_from_proj(z_a, z_s, b, t, tables, pe, w1, w2):
    col = lambda i: z_a[:, NSA_WIDTH + KV_WIDTH * i:NSA_WIDTH + KV_WIDTH * (i + 1)]
    kcc = _compress_heads(col(0).reshape(b, t, NSA_KV_HEADS, HEAD_DIM), pe[0], w1[0], w2[0])
    vcc = _compress_heads(col(1).reshape(b, t, NSA_KV_HEADS, HEAD_DIM), pe[1], w1[1], w2[1])
    gates = z_s[:, :3 * NSA_HEADS].reshape(b, t, NSA_KV_HEADS, 3 * NSA_GROUP).transpose(0, 2, 1, 3)
    return _nsa_prompt(z_a, kcc.astype(BF16).transpose(0, 1, 3, 2), vcc.astype(BF16),
                       _heads_t(col(2), b, t), _heads(col(3), b, t), _heads_t(col(4), b, t), _heads(col(5), b, t),
                       gates, tables)


T_SAMPLE = 4
T_PAD = 8
PAGE = 128
N_PAGES = 16
PAST = N_PAGES * PAGE
N_SEL_SAMPLE = -(-(PAST + T_SAMPLE) // SEL_BLOCK)


def _nsa_sample_kernel(pt_ref, *refs):
    del pt_ref
    cp, sp = refs[:N_PAGES], refs[N_PAGES:2 * N_PAGES]
    (wb_ref, za_ref, zs_ref, pe2_ref, w1_ref, w2_ref, bcmp_ref, bpast_ref, bwin_ref, bnew_ref, ov_ref, exp_ref,
     o_ref, wout_ref, new_ref, q_ref, g_ref, xcat_ref, kcat_ref, vcat_ref, s_ref, e_ref, oacc_ref,
     rowpg_ref) = refs[2 * N_PAGES:]
    rows = NSA_GROUP * T_PAD

    @pl.when(pl.program_id(0) == 0)
    def _():
        new_ref[...] = jnp.zeros(new_ref.shape, F32)
        q_ref[...] = jnp.zeros(q_ref.shape, F32)
        g_ref[...] = jnp.zeros(g_ref.shape, F32)

    new_ref[0:T_SAMPLE, :] = za_ref[0, :, NSA_WIDTH + 2 * KV_WIDTH:]
    q_ref[0:T_SAMPLE, :] = za_ref[0, :, :NSA_WIDTH] * (HEAD_DIM ** -0.5)
    g_ref[0:T_SAMPLE, :] = zs_ref[0]

    for p in range(N_PAGES):
        for kv in range(2):
            rowpg_ref[kv, p * PAGE:(p + 1) * PAGE, :] = cp[p][0, kv * KV_WIDTH:(kv + 1) * KV_WIDTH, :].T
    comp = []
    for kv in range(2):
        def place_row_offset(l, carry, kv=kv):
            xl = rowpg_ref[kv, pl.ds(l, N_CHUNK, stride=CMP_STRIDE), :]
            lanes = pl.ds(pl.multiple_of(l * KV_WIDTH, KV_WIDTH), KV_WIDTH)
            for half in range(2):
                xcat_ref[half, :, lanes] = (xl + pe2_ref[kv, pl.ds(l + CMP_STRIDE * half, 1), :]).astype(BF16)
            return carry

        lax.fori_loop(0, CMP_STRIDE, place_row_offset, 0)
        lo = jnp.dot(xcat_ref[0], w1_ref[kv, 0], preferred_element_type=F32)
        hi = jnp.dot(xcat_ref[1], w1_ref[kv, 1], preferred_element_type=F32)
        hid = jax.nn.gelu(lo + pltpu.roll(hi, N_CHUNK - 1, 0))
        out = jnp.dot(hid.astype(BF16), w2_ref[kv], preferred_element_type=F32)
        row = lax.broadcasted_iota(jnp.int32, out.shape, 0)
        comp.append(jnp.where(row < N_CHUNK - 1, out, 0.0).astype(BF16))

    for p in range(N_PAGES):
        kcat_ref[:, p * PAGE:(p + 1) * PAGE] = sp[p][0, 0:KV_WIDTH, :].astype(BF16)
        vcat_ref[:, p * PAGE:(p + 1) * PAGE] = sp[p][0, KV_WIDTH:, :].astype(BF16)
    kw_buf = wb_ref[0, 0:KV_WIDTH, :].astype(BF16)
    vw_buf = wb_ref[0, KV_WIDTH:, :].astype(BF16)
    ks_new, vs_new, kw_new, vw_new = (new_ref[:, i * KV_WIDTH:(i + 1) * KV_WIDTH].astype(BF16) for i in range(4))
    gates = jax.nn.sigmoid(g_ref[...])
    nt = (((1,), (1,)), ((), ()))
    zeros64 = jnp.zeros((rows, HEAD_DIM), F32)
    pieces = []
    key_chunk = 512

    all_rows = NSA_KV_HEADS * rows

    def attend_window(qp):
        s_parts = [jnp.dot(qp, kw_buf, preferred_element_type=F32) + bwin_ref[...].reshape(all_rows, -1),
                   lax.dot_general(qp, kw_new, nt, preferred_element_type=F32) + bnew_ref[...].reshape(all_rows, LANE)]
        m = jnp.maximum(*[jnp.max(s, axis=-1, keepdims=True) for s in s_parts])
        e_parts = [jnp.exp(s - m) for s in s_parts]
        den = jnp.add(*[jnp.sum(e, axis=-1, keepdims=True) for e in e_parts])
        num = (lax.dot_general(e_parts[0].astype(BF16), vw_buf, nt, preferred_element_type=F32)
               + jnp.dot(e_parts[1].astype(BF16), vw_new, preferred_element_type=F32))
        return num / den

    def attend_selected(qp):
        def scores(c, carry):
            ds = pl.ds(pl.multiple_of(c * key_chunk, key_chunk), key_chunk)
            s_ref[:, ds] += jnp.dot(qp, kcat_ref[:, ds], preferred_element_type=F32)
            return carry

        lax.fori_loop(0, PAST // key_chunk, scores, 0)
        s_ref[:, PAST:] += lax.dot_general(qp, ks_new, nt, preferred_element_type=F32)
        s = s_ref[...]
        e = jnp.exp(s - jnp.max(s, axis=-1, keepdims=True))
        den = jnp.sum(e, axis=-1, keepdims=True)
        e_ref[...] = e.astype(BF16)
        oacc_ref[...] = jnp.dot(e_ref[:, PAST:], vs_new, preferred_element_type=F32)

        def weighted(c, carry):
            ds = pl.ds(pl.multiple_of(c * key_chunk, key_chunk), key_chunk)
            oacc_ref[...] += lax.dot_general(e_ref[:, ds], vcat_ref[:, ds], nt, preferred_element_type=F32)
            return carry

        lax.fori_loop(0, PAST // key_chunk, weighted, 0)
        return oacc_ref[...] / den

    qp = []
    for h in range(NSA_KV_HEADS):
        q64 = jnp.concatenate([q_ref[:, (h * NSA_GROUP + a) * HEAD_DIM:(h * NSA_GROUP + a + 1) * HEAD_DIM]
                               for a in range(NSA_GROUP)], axis=0)
        qp.append(jnp.concatenate([q64, zeros64] if h == 0 else [zeros64, q64], axis=1))
    qp = jnp.concatenate(qp, axis=0).astype(BF16)

    s = lax.dot_general(qp, comp[0], nt, preferred_element_type=F32) + bcmp_ref[...].reshape(all_rows, N_CHUNK)
    m = jnp.max(s, axis=-1, keepdims=True)
    e = jnp.where(s > 0.1 * NEG, jnp.exp(s - m), 0.0)
    p = e / jnp.maximum(jnp.sum(e, axis=-1, keepdims=True), 1e-30)
    o_cmp = jnp.dot(p.astype(BF16), comp[1], preferred_element_type=F32)

    sel_rows = NSA_KV_HEADS * T_PAD
    p_sum = jnp.concatenate([sum(p[h * rows + a * T_PAD:h * rows + (a + 1) * T_PAD] for a in range(NSA_GROUP))
                             for h in range(NSA_KV_HEADS)], axis=0)
    imp = sum(jnp.dot(part, ov_ref[...], preferred_element_type=F32) for part in _split3(p_sum))
    lane = lax.broadcasted_iota(jnp.int32, (sel_rows, LANE), 1)
    tok = lax.broadcasted_iota(jnp.int32, (sel_rows, LANE), 0) % T_PAD
    cur = (PAST + jnp.minimum(tok, T_SAMPLE - 1)) // SEL_BLOCK
    valid = lane <= cur
    forced = (lane == 0) | (lane == cur) | (lane == cur - 1)
    score = jnp.where(valid, imp + jnp.where(forced, SEL_FORCE, 0.0), -1.0)
    rank = jnp.zeros((sel_rows, LANE), F32)
    for c in range(N_SEL_SAMPLE):
        sc = score[:, c:c + 1]
        beats = (sc > score) | ((sc == score) & (lane > c))
        rank = rank + jnp.where(beats, 1.0, 0.0)
    sel = jnp.where(valid & (rank < SEL_TOP_N), 1.0, 0.0)
    selneg = (jnp.dot(sel.astype(BF16), exp_ref[...], preferred_element_type=F32) - 1.0) * (-NEG)
    selneg_new = (sel[:, PAST // SEL_BLOCK:PAST // SEL_BLOCK + 1] - 1.0) * (-NEG)

    def per_token(bias, tok_term):
        n = bias.shape[-1]
        return (bias.reshape(NSA_KV_HEADS, NSA_GROUP, T_PAD, n)
                + tok_term.reshape(NSA_KV_HEADS, 1, T_PAD, n)).reshape(all_rows, n)

    s_ref[:, :PAST] = per_token(bpast_ref[...], selneg)
    s_ref[:, PAST:] = per_token(bnew_ref[...], jnp.broadcast_to(selneg_new, (sel_rows, LANE)))
    o_sel = attend_selected(qp)
    o_win = attend_window(qp)

    for h in range(NSA_KV_HEADS):
        lanes = slice(h * HEAD_DIM, (h + 1) * HEAD_DIM)
        for a in range(NSA_GROUP):
            r = slice(h * rows + a * T_PAD, h * rows + (a + 1) * T_PAD)
            c0 = (h * NSA_GROUP + a) * 3
            pieces.append(gates[:, c0:c0 + 1] * o_cmp[r, lanes] + gates[:, c0 + 1:c0 + 2] * o_sel[r, lanes]
                          + gates[:, c0 + 2:c0 + 3] * o_win[r, lanes])
    o_ref[0] = jnp.concatenate(pieces, axis=1)[0:T_SAMPLE]

    w_buf = wb_ref.shape[-1]
    shifted = pltpu.roll(wb_ref[0], w_buf - T_SAMPLE, 1)
    new_t = jnp.concatenate([new_ref[:, 2 * KV_WIDTH:3 * KV_WIDTH].T, new_ref[:, 3 * KV_WIDTH:].T], axis=0)
    new_t = pltpu.roll(new_t, LANE - T_SAMPLE, 1)
    lane = lax.broadcasted_iota(jnp.int32, new_t.shape, 1)
    tail = jnp.where(lane >= LANE - T_SAMPLE, new_t, shifted[:, w_buf - LANE:])
    wout_ref[0] = jnp.concatenate([shifted[:, :w_buf - LANE], tail], axis=1)


def _sample_bias_tables(table):
    tq = jnp.minimum(jnp.arange(T_PAD), T_SAMPLE - 1)
    q_pos = PAST + tq

    def lay(x):
        n = x.shape[1]
        return x.reshape(T_PAD, n, NSA_KV_HEADS, NSA_GROUP).transpose(2, 3, 0, 1).reshape(NSA_KV_HEADS, -1, n)

    block_end = jnp.arange(N_CHUNK) * CMP_STRIDE + (CMP_BLOCK - 1)
    dist = q_pos[:, None] - block_end[None, :]
    ok = (dist >= 0) & (jnp.arange(N_CHUNK)[None, :] < N_CHUNK - 1)
    bcmp = lay(jnp.where(ok[..., None], _bias_lookup(table, dist), NEG))
    dist = q_pos[:, None] - jnp.arange(PAST)[None, :]
    past = _bias_lookup(table, dist)
    bpast = lay(past)
    w_buf = min(WINDOW, PAST)
    bwin = lay(jnp.where((dist < WINDOW)[:, PAST - w_buf:, None], past[:, PAST - w_buf:], NEG))
    j = jnp.arange(LANE)
    dist = tq[:, None] - j[None, :]
    ok = (dist >= 0) & (j[None, :] < T_SAMPLE)
    bnew = lay(jnp.where(ok[..., None], _bias_lookup(table, dist), NEG))
    c0 = jnp.arange(N_CHUNK) * CMP_STRIDE
    s0 = jnp.arange(LANE) * SEL_BLOCK
    ov = jnp.clip(jnp.minimum(c0[:, None] + CMP_BLOCK, s0[None, :] + SEL_BLOCK)
                  - jnp.maximum(c0[:, None], s0[None, :]), 0, CMP_BLOCK).astype(F32) / CMP_BLOCK
    ov = jnp.where((jnp.arange(N_CHUNK)[:, None] < N_CHUNK - 1) & (j[None, :] < N_SEL_SAMPLE), ov, 0.0)
    expand = (j[:, None] == (jnp.arange(PAST) // SEL_BLOCK)[None, :]).astype(BF16)
    return bcmp, bpast, bwin, bnew, ov.astype(BF16), expand


def _nsa_sample(za, zs, cache_cmp, cache_slc, cache_win, page_table, layer, tables, pe, w1, w2):
    b = za.shape[0]
    n_phys = cache_cmp.shape[1]
    cmp_pages = jnp.transpose(cache_cmp, (0, 1, 3, 4, 5, 2)).reshape(DEPTH * n_phys, 2 * KV_WIDTH, PAGE)
    slc_pages = jnp.transpose(cache_slc, (0, 1, 3, 4, 5, 2)).reshape(DEPTH * n_phys, 2 * KV_WIDTH, PAGE)
    pt = page_table.reshape(-1).astype(jnp.int32) + layer * n_phys
    w_buf = cache_win.shape[2]
    win = jnp.transpose(cache_win, (0, 1, 3, 4, 5, 2)).reshape(DEPTH * b, 2 * KV_WIDTH, w_buf)
    bcmp, bpast, bwin, bnew, ov, expand = tables
    pe2 = jnp.concatenate([pe, pe], axis=-1)
    page_spec = lambda p: pl.BlockSpec((1, 2 * KV_WIDTH, PAGE), lambda bi, pt_: (pt_[bi * N_PAGES + p], 0, 0))
    whole = lambda x: pl.BlockSpec(x.shape, lambda bi, pt_: (0,) * x.ndim)
    eye = jnp.eye(NSA_KV_HEADS, dtype=F32)
    w1h = w1.reshape(2, 2, CMP_STRIDE, HEAD_DIM, CMP_HIDDEN)
    w1_bd = jnp.einsum('khlde,ab->khladbe', w1h, eye).reshape(2, 2, CMP_STRIDE * KV_WIDTH, NSA_KV_HEADS * CMP_HIDDEN)
    w2_bd = jnp.einsum('ked,ab->kaebd', w2, eye).reshape(2, NSA_KV_HEADS * CMP_HIDDEN, KV_WIDTH)
    consts = [pe2, w1_bd.astype(BF16), w2_bd.astype(BF16), bcmp, bpast, bwin, bnew, ov, expand]
    grid_spec = pltpu.PrefetchScalarGridSpec(
        num_scalar_prefetch=1,
        grid=(b,),
        in_specs=[page_spec(p) for p in range(N_PAGES)] * 2
        + [pl.BlockSpec((1, 2 * KV_WIDTH, w_buf), lambda bi, pt_: (layer * b + bi, 0, 0)),
           pl.BlockSpec((1, T_SAMPLE, za.shape[-1]), lambda bi, pt_: (bi, 0, 0)),
           pl.BlockSpec((1, T_SAMPLE, LANE), lambda bi, pt_: (bi, 0, 0))]
        + [whole(x) for x in consts],
        out_specs=[pl.BlockSpec((1, T_SAMPLE, NSA_WIDTH), lambda bi, pt_: (bi, 0, 0)),
                   pl.BlockSpec((1, 2 * KV_WIDTH, w_buf), lambda bi, pt_: (bi, 0, 0))],
        scratch_shapes=[pltpu.VMEM((PAGE, 4 * KV_WIDTH), F32), pltpu.VMEM((T_PAD, NSA_WIDTH), F32),
                        pltpu.VMEM((T_PAD, LANE), F32), pltpu.VMEM((2, N_CHUNK, CMP_STRIDE * KV_WIDTH), BF16),
                        pltpu.VMEM((KV_WIDTH, PAST), BF16), pltpu.VMEM((KV_WIDTH, PAST), BF16),
                        pltpu.VMEM((NSA_HEADS * T_PAD, PAST + PAGE), F32),
                        pltpu.VMEM((NSA_HEADS * T_PAD, PAST + PAGE), BF16),
                        pltpu.VMEM((NSA_HEADS * T_PAD, KV_WIDTH), F32), pltpu.VMEM((2, PAST, KV_WIDTH), F32)],
    )
    return pl.pallas_call(
        _nsa_sample_kernel,
        grid_spec=grid_spec,
        out_shape=[jax.ShapeDtypeStruct((b, T_SAMPLE, NSA_WIDTH), F32),
                   jax.ShapeDtypeStruct((b, 2 * KV_WIDTH, w_buf), F32)],
        compiler_params=pltpu.CompilerParams(dimension_semantics=("arbitrary",), vmem_limit_bytes=VMEM_LIMIT_V7X),
        name="nsa_sample",
    )(pt, *([cmp_pages] * N_PAGES), *([slc_pages] * N_PAGES), win, za, zs, *consts)


I_LANE = 3 * NSA_HEADS
F_LANE = I_LANE + M_HEADS
EXT_PAD = 8


def _mlstm_kernel(zu_ref, zs_ref, c0_ref, n0_ref, m0_ref, cprev_ref, convw_ref, convb_ref, wq_ref, wk_ref, wkt_ref,
                  wv_ref, gb_ref, normg_ref, tril_ref, selrow_ref,
                  o_ref, c_out, n_out, m_out, conv_out,
                  ext_ref, zsp_ref, c_sc, n_sc, m_sc, *, rows, rows_pad, n_valid):
    i = pl.program_id(1)
    L = M_CHUNK

    @pl.when(i == 0)
    def _():
        c_sc[...] = c0_ref[0]
        n_sc[...] = n0_ref[0]
        m_sc[...] = m0_ref[0]
        ext_ref[EXT_PAD - (CONV_W - 1):EXT_PAD, :] = cprev_ref[0]

    if rows < rows_pad:
        ext_ref[EXT_PAD:, :] = jnp.zeros((rows_pad, M_WIDTH), F32)
        zsp_ref[...] = jnp.zeros(zsp_ref.shape, F32)
    ext_ref[EXT_PAD:EXT_PAD + rows, :] = zu_ref[0, :, :M_WIDTH]
    zsp_ref[0:rows, :] = zs_ref[0]

    u = ext_ref[EXT_PAD:EXT_PAD + rows_pad, :]
    conv = convb_ref[...] + convw_ref[CONV_W - 1:CONV_W, :] * u
    for j in range(CONV_W - 1):
        lo = EXT_PAD - (CONV_W - 1) + j
        conv = conv + convw_ref[j:j + 1, :] * ext_ref[lo:lo + rows_pad, :]
    uc = (conv * jax.nn.sigmoid(conv)).astype(BF16)
    ub = u.astype(BF16)

    zb = zsp_ref[...] + gb_ref[...]
    bcum = sum(jnp.dot(tril_ref[...], part, preferred_element_type=F32) for part in _split3(jax.nn.log_sigmoid(zb)))
    bcum = pltpu.roll(bcum, LANE - M_HEADS, 1)
    a_col = zb - bcum
    a_row = sum(lax.dot_general(selrow_ref[...], part, (((1,), (1,)), ((), ())), preferred_element_type=F32)
                for part in _split3(a_col))

    tt = lax.broadcasted_iota(jnp.int32, (L, L), 0)
    ss = lax.broadcasted_iota(jnp.int32, (L, L), 1)
    causal = ss <= tt
    tok_col = lax.broadcasted_iota(jnp.int32, (L, 1), 0)
    tok_row = lax.broadcasted_iota(jnp.int32, (1, L), 1)

    heads = []
    for h in range(M_HEADS):
        hs = slice(h * M_HEAD_DIM, (h + 1) * M_HEAD_DIM)
        heads.append(dict(
            hs=hs,
            q=jnp.dot(uc[:, hs], wq_ref[h].astype(BF16), preferred_element_type=F32) * (M_HEAD_DIM ** -0.5),
            k=jnp.dot(uc[:, hs], wk_ref[h].astype(BF16), preferred_element_type=F32),
            kt=lax.dot_general(wkt_ref[h].astype(BF16), uc[:, hs], (((1,), (1,)), ((), ())),
                               preferred_element_type=F32),
            v=jnp.dot(ub[:, hs], wv_ref[h].astype(BF16), preferred_element_type=F32),
            c=c_sc[h], n=n_sc[h:h + 1, :], m=m_sc[h:h + 1, 0:1]))

    for c in range(rows_pad // L):
        r = slice(c * L, (c + 1) * L)
        for h, hd in enumerate(heads):
            hs, q_all, c_st, n_st, m_st = hd['hs'], hd['q'], hd['c'], hd['n'], hd['m']
            q, k, kt, v = q_all[r].astype(BF16), hd['k'][r], hd['kt'][:, r], hd['v'][r].astype(BF16)
            b_col = bcum[r, I_LANE + h:I_LANE + h + 1]
            a_c = a_col[r, I_LANE + h:I_LANE + h + 1]
            a_r = a_row[h:h + 1, r]
            log_d = jnp.where(causal, b_col + a_r, NEG)
            m_col = jnp.maximum(b_col + m_st, jnp.max(log_d, axis=-1, keepdims=True))
            dw = jnp.exp(log_d - m_col)
            w_inter = jnp.exp(b_col + m_st - m_col)
            s = lax.dot_general(q, k.astype(BF16), (((1,), (1,)), ((), ())), preferred_element_type=F32) * dw
            num = (jnp.dot(s.astype(BF16), v, preferred_element_type=F32)
                   + w_inter * jnp.dot(q, c_st.astype(BF16), preferred_element_type=F32))
            den = (jnp.sum(s, axis=-1, keepdims=True)
                   + w_inter * jnp.sum(q_all[r] * n_st, axis=-1, keepdims=True))
            hh = num / jnp.maximum(jnp.abs(den), jnp.exp(-m_col))
            hn = hh * lax.rsqrt(jnp.mean(hh * hh, axis=-1, keepdims=True) + EPS)
            o_pre = zu_ref[0, :, M_WIDTH + h * M_HEAD_DIM:M_WIDTH + (h + 1) * M_HEAD_DIM]
            if rows < rows_pad:
                o_ref[0, :, hs] = jax.nn.sigmoid(o_pre) * hn[0:rows] * normg_ref[:, hs]
            else:
                o_ref[0, r, hs] = jax.nn.sigmoid(o_pre[r]) * hn * normg_ref[:, hs]
            b_last = b_col[n_valid - 1:n_valid, :]
            log_s = jnp.where(tok_col < n_valid, b_last + a_c, NEG)
            m_new = jnp.maximum(b_last + m_st, jnp.max(log_s, axis=0, keepdims=True))
            ws_col = jnp.exp(log_s - m_new)
            ws_row = jnp.where(tok_row < n_valid, jnp.exp(b_last + a_r - m_new), 0.0)
            wc = jnp.exp(b_last + m_st - m_new)
            hd['c'] = wc * c_st + jnp.dot((kt * ws_row).astype(BF16), v, preferred_element_type=F32)
            hd['n'] = wc * n_st + jnp.sum(k * ws_col, axis=0, keepdims=True)
            hd['m'] = m_new
    for h, hd in enumerate(heads):
        c_sc[h] = hd['c']
        n_sc[h:h + 1, :] = hd['n']
        m_sc[h:h + 1, :] = jnp.broadcast_to(hd['m'], (1, LANE))

    tail = ext_ref[EXT_PAD + rows - (CONV_W - 1):EXT_PAD + rows, :]
    ext_ref[EXT_PAD - (CONV_W - 1):EXT_PAD, :] = tail

    @pl.when(i == pl.num_programs(1) - 1)
    def _():
        c_out[0] = c_sc[...]
        n_out[0] = n_sc[...]
        m_out[0] = m_sc[...]
        conv_out[0] = jnp.zeros(conv_out.shape[1:], F32)
        conv_out[0, 0:CONV_W - 1, :] = tail


def _mlstm(zu, zs, state, conv_prev, conv_w, conv_b, m_qkv, gate_b, norm_g):
    b, t, _ = zu.shape
    rows = min(t, 4 * M_CHUNK)
    rows_pad = -(-rows // M_CHUNK) * M_CHUNK
    n_valid = M_CHUNK if rows == rows_pad else rows
    assert t % rows == 0 and (rows == rows_pad or t == rows)
    c0, n0, m0 = state
    n0p = jnp.pad(n0, ((0, 0), (0, 8 - M_HEADS), (0, 0)))
    m0p = jnp.pad(jnp.broadcast_to(m0[:, :, None], (b, M_HEADS, LANE)), ((0, 0), (0, 8 - M_HEADS), (0, 0)))
    gb = jnp.zeros((1, LANE), F32).at[0, I_LANE:I_LANE + 2 * M_HEADS].set(gate_b.reshape(-1))
    idx = jnp.arange(rows_pad)
    tril = ((idx[:, None] >= idx[None, :]) & (idx[:, None] // M_CHUNK == idx[None, :] // M_CHUNK)).astype(BF16)
    selrow = ((jnp.arange(16)[:, None] + I_LANE == jnp.arange(LANE)[None, :])
              & (jnp.arange(16)[:, None] < M_HEADS)).astype(BF16)
    whole = lambda *shape: pl.BlockSpec(shape, lambda bi, i: (0,) * len(shape))
    per_b = lambda *shape: pl.BlockSpec((1,) + shape, lambda bi, i: (bi,) + (0,) * len(shape))
    out, c_new, n_new, m_new, conv_new = pl.pallas_call(
        functools.partial(_mlstm_kernel, rows=rows, rows_pad=rows_pad, n_valid=n_valid),
        grid=(b, t // rows),
        in_specs=[pl.BlockSpec((1, rows, 2 * M_WIDTH), lambda bi, i: (bi, i, 0)),
                  pl.BlockSpec((1, rows, LANE), lambda bi, i: (bi, i, 0)),
                  per_b(M_HEADS, M_HEAD_DIM, M_HEAD_DIM), per_b(8, M_HEAD_DIM), per_b(8, LANE),
                  per_b(CONV_W - 1, M_WIDTH),
                  whole(CONV_W, M_WIDTH), whole(1, M_WIDTH),
                  whole(M_HEADS, M_HEAD_DIM, M_HEAD_DIM), whole(M_HEADS, M_HEAD_DIM, M_HEAD_DIM),
                  whole(M_HEADS, M_HEAD_DIM, M_HEAD_DIM), whole(M_HEADS, M_HEAD_DIM, M_HEAD_DIM),
                  whole(1, LANE), whole(1, M_WIDTH), whole(rows_pad, rows_pad), whole(16, LANE)],
        out_specs=[pl.BlockSpec((1, rows, M_WIDTH), lambda bi, i: (bi, i, 0)),
                   per_b(M_HEADS, M_HEAD_DIM, M_HEAD_DIM), per_b(8, M_HEAD_DIM), per_b(8, LANE),
                   per_b(8, M_WIDTH)],
        out_shape=[jax.ShapeDtypeStruct((b, t, M_WIDTH), F32),
                   jax.ShapeDtypeStruct((b, M_HEADS, M_HEAD_DIM, M_HEAD_DIM), F32),
                   jax.ShapeDtypeStruct((b, 8, M_HEAD_DIM), F32),
                   jax.ShapeDtypeStruct((b, 8, LANE), F32),
                   jax.ShapeDtypeStruct((b, 8, M_WIDTH), F32)],
        scratch_shapes=[pltpu.VMEM((EXT_PAD + rows_pad, M_WIDTH), F32), pltpu.VMEM((rows_pad, LANE), F32),
                        pltpu.VMEM((M_HEADS, M_HEAD_DIM, M_HEAD_DIM), F32), pltpu.VMEM((8, M_HEAD_DIM), F32),
                        pltpu.VMEM((8, LANE), F32)],
        compiler_params=pltpu.CompilerParams(dimension_semantics=("arbitrary", "arbitrary"),
                                             vmem_limit_bytes=VMEM_LIMIT_V7X),
        name="mlstm",
    )(zu, zs, c0, n0p, m0p, conv_prev, conv_w, conv_b.reshape(1, M_WIDTH), m_qkv[0], m_qkv[1],
      m_qkv[1].transpose(0, 2, 1),
      m_qkv[2], gb, norm_g.reshape(1, M_WIDTH), tril, selrow)
    return out, (c_new, n_new[:, :M_HEADS], m_new[:, :M_HEADS, 0], conv_new[:, :CONV_W - 1])


def rel_bucket(dist):
    d = jnp.maximum(dist, 0)
    exact = REL_BUCKETS // 2
    log_part = exact + (jnp.log(jnp.maximum(d, 1).astype(F32) / exact)
                        / math.log(REL_MAX_DIST / exact) * (REL_BUCKETS - exact)).astype(jnp.int32)
    return jnp.where(d < exact, d, jnp.minimum(log_part, REL_BUCKETS - 1))


def _split_w_in(w_in):
    offs = np.cumsum((0,) + IN_WIDTHS)
    w_a = w_in[:, :offs[7]]
    w_small = jnp.concatenate([w_in[:, offs[7]:offs[8]], w_in[:, offs[10]:offs[11]]], axis=1)
    w_small = jnp.pad(w_small, ((0, 0), (0, LANE - w_small.shape[1])))
    w_u = w_in[:, offs[8]:offs[10]]
    w_g = w_in[:, offs[11]:]
    return w_a, w_small, w_u, w_g


def mixer(x, norm_g, prm, past):
    b, t, _ = x.shape
    x2 = x.reshape(b * t, D_MODEL)
    z_a, z_s, z_u, z_g = _proj(x2, norm_g, [w.astype(BF16) for w in _split_w_in(prm['w_in'])])
    z_a, z_s, z_u = (z.reshape(b, t, -1) for z in (z_a, z_s, z_u))
    new_cmp, new_slc, win_rows = (z_a[..., NSA_WIDTH + 2 * KV_WIDTH * i:NSA_WIDTH + 2 * KV_WIDTH * (i + 1)]
                                  .reshape(b, t, 2, NSA_KV_HEADS, HEAD_DIM) for i in range(3))
    if past is None:
        o_nsa = _nsa_prompt_from_proj(z_a.reshape(b * t, -1), z_s.reshape(b * t, -1), b, t, prm['prompt_tables'],
                                      prm['cmp_pe'], prm['cmp_w1'], prm['cmp_w2'])
        new_win = win_rows[:, t - min(WINDOW, t):]
        conv_prev = jnp.zeros((b, CONV_W - 1, M_WIDTH), F32)
        m_state = (jnp.zeros((b, M_HEADS, M_HEAD_DIM, M_HEAD_DIM), F32),
                   jnp.zeros((b, M_HEADS, M_HEAD_DIM), F32),
                   jnp.zeros((b, M_HEADS), F32))
    else:
        layer = past['layer']
        assert t == T_SAMPLE and past['page_table'].shape[1] == N_PAGES and past['cmp'].shape[2] == PAGE
        o_nsa, win_t = _nsa_sample(z_a, z_s, past['cmp'], past['slc'], past['win'], past['page_table'], layer,
                                   prm['sample_tables'], prm['cmp_pe'], prm['cmp_w1'], prm['cmp_w2'])
        o_nsa = o_nsa.reshape(b * t, NSA_WIDTH)
        new_win = win_t.reshape(b, 2, NSA_KV_HEADS, HEAD_DIM, -1).transpose(0, 4, 1, 2, 3)
        conv_prev = past['conv']
        m_state = (past['C'].astype(F32), past['n'].astype(F32), past['m'].astype(F32))
    o_m, (c_new, n_new, m_new, conv_new) = _mlstm(z_u, z_s, m_state, conv_prev, prm['conv_w'], prm['conv_b'],
                                                  prm['m_qkv'], prm['gate_b'], prm['m_norm'])
    y = _merge_out(x2, o_nsa, o_m.reshape(b * t, M_WIDTH), z_g, prm['w_up_a'].astype(BF16),
                   prm['w_up_b'].astype(BF16), prm['w_out'].astype(BF16)).reshape(b, t, D_MODEL)
    return y, (new_cmp, new_slc, new_win, c_new, n_new, m_new, conv_new)


def _channel_mixer(x, g, l, ffn_w1, ffn_w3, ffn_w2, moe_router, moe_w1, moe_w3, moe_w2):
    b, t, d = x.shape
    x2 = x.reshape(b * t, d)
    i = l // 2
    if l % 2 == 0:
        y = _ffn_dense(x2, g, ffn_w1[i], ffn_w3[i], ffn_w2[i])
    else:
        y = _moe(x2, g, moe_router[i], moe_w1[i], moe_w3[i], moe_w2[i])
    return y.reshape(b, t, d)


def _final_norm(x, g):
    xf = x.astype(F32)
    return xf * lax.rsqrt(jnp.mean(xf * xf, axis=-1, keepdims=True) + EPS) * g


def kernel(x_prompt, x_sample, cache_cmp_kv, cache_slc_kv, cache_win_kv, state_mlstm_C, state_mlstm_n,
           state_mlstm_m, state_mlstm_conv, page_table, rel_bias_table, norm_mix, norm_ffn, norm_final,
           w_in, cmp_pe, cmp_w1, cmp_w2, m_conv_w, m_conv_b, m_qkv, m_gate_bias, m_norm, w_up_a, w_up_b,
           w_out, ffn_w1, ffn_w3, ffn_w2, moe_router, moe_w1, moe_w3, moe_w2):
    xp, xs = x_prompt, x_sample
    prompt_states, sample_states = [], []
    prompt_tables = _prompt_bias_tables(rel_bias_table)
    sample_tables = _sample_bias_tables(rel_bias_table)
    for l in range(DEPTH):
        prm = {'prompt_tables': prompt_tables, 'sample_tables': sample_tables, 'w_in': w_in[l], 'cmp_pe': cmp_pe[l],
               'cmp_w1': cmp_w1[l], 'cmp_w2': cmp_w2[l], 'conv_w': m_conv_w[l], 'conv_b': m_conv_b[l],
               'm_qkv': m_qkv[l], 'gate_b': m_gate_bias[l], 'm_norm': m_norm[l], 'w_up_a': w_up_a[l],
               'w_up_b': w_up_b[l], 'w_out': w_out[l]}
        past = {'cmp': cache_cmp_kv, 'slc': cache_slc_kv, 'win': cache_win_kv, 'layer': l,
                'C': state_mlstm_C[l], 'n': state_mlstm_n[l], 'm': state_mlstm_m[l],
                'conv': state_mlstm_conv[l], 'page_table': page_table}
        xp, st_p = mixer(xp, norm_mix[l], prm, None)
        xs, st_s = mixer(xs, norm_mix[l], prm, past)
        xp = _channel_mixer(xp, norm_ffn[l], l, ffn_w1, ffn_w3, ffn_w2, moe_router, moe_w1, moe_w3, moe_w2)
        xs = _channel_mixer(xs, norm_ffn[l], l, ffn_w1, ffn_w3, ffn_w2, moe_router, moe_w1, moe_w3, moe_w2)
        prompt_states.append(st_p)
        sample_states.append(st_s)
    y_prompt = _final_norm(xp, norm_final)
    y_sample = _final_norm(xs, norm_final)
    ps = [jnp.stack([s[i] for s in prompt_states]) for i in range(7)]
    ss = [jnp.stack([s[i] for s in sample_states]) for i in range(7)]
    return (y_prompt, y_sample, ps[0], ps[1], ps[2], ps[3], ps[4], ps[5], ps[6],
            ss[0], ss[1], ss[2], ss[3], ss[4], ss[5], ss[6])
```
